```python
import math
import jax, jax.numpy as jnp
from jax import lax
import numpy as np

D_MODEL = 2048
BATCH = 8
SEQ = 8192
DEPTH = 4

N_MIXERS = 2
N_ATTN_LAYERS = (DEPTH + 1) // 2
N_LRU_LAYERS = DEPTH // 2

HEAD_DIM = 64
N_HEADS = D_MODEL // HEAD_DIM
N_KV_HEADS = N_HEADS // 8
GROUP = N_HEADS // N_KV_HEADS
WINDOW = 128
BLOCK = 128
Q_W = N_HEADS * HEAD_DIM
KV_W = N_KV_HEADS * HEAD_DIM
ATTN_IN_W = Q_W + 2 * KV_W + Q_W

LRU_W = D_MODEL
LRU_BLOCK_W = 256
N_LRU_BLOCKS = LRU_W // LRU_BLOCK_W
CONV_W = 4
C_RG = 8.0
LRU_IN_W = 2 * LRU_W

NORM_EPS = 1e-6
MASK_VALUE = -1e30

kernel_name = "hybrid_swa_sink_rglru_interleaved"


def rms_norm(x, g):
    x32 = x.astype(jnp.float32)
    y = x32 * lax.rsqrt(jnp.mean(x32 * x32, axis=-1, keepdims=True) + NORM_EPS)
    return (y * g.astype(jnp.float32)).astype(x.dtype)


def sliding_window_attention(h, w_in, w_out, sinks):
    B, S, _ = h.shape
    nb = S // BLOCK
    proj = h @ w_in
    q, k, v, gate = jnp.split(proj, [Q_W, Q_W + KV_W, Q_W + 2 * KV_W], axis=-1)
    q = q.reshape(B, nb, BLOCK, N_KV_HEADS, GROUP, HEAD_DIM)
    k = k.reshape(B, nb, BLOCK, N_KV_HEADS, HEAD_DIM)
    v = v.reshape(B, nb, BLOCK, N_KV_HEADS, HEAD_DIM)
    pad = ((0, 0), (1, 0), (0, 0), (0, 0), (0, 0))
    k_band = jnp.concatenate([jnp.pad(k, pad)[:, :-1], k], axis=2)
    v_band = jnp.concatenate([jnp.pad(v, pad)[:, :-1], v], axis=2)

    scale = 1.0 / math.sqrt(HEAD_DIM)
    scores = jnp.einsum('bnqkgd,bnskd->bnkgqs', q.astype(jnp.float32),
                        k_band.astype(jnp.float32)) * scale
    q_pos = jnp.arange(BLOCK)[:, None] + BLOCK
    k_pos = jnp.arange(2 * BLOCK)[None, :]
    rel = q_pos - k_pos
    in_window = (rel >= 0) & (rel < WINDOW)
    prev_exists = (jnp.arange(nb)[:, None, None] > 0) | (k_pos[None] >= BLOCK)
    mask = (in_window[None] & prev_exists)[:, None, None]
    scores = jnp.where(mask, scores, MASK_VALUE)

    sink = jnp.broadcast_to(
        sinks.astype(jnp.float32).reshape(1, 1, N_KV_HEADS, GROUP, 1, 1),
        scores.shape[:-1] + (1,))
    probs = jax.nn.softmax(jnp.concatenate([scores, sink], axis=-1), axis=-1)[..., :-1]
    out = jnp.einsum('bnkgqs,bnskd->bnqkgd', probs.astype(v.dtype), v_band)
    out = out.reshape(B, S, Q_W)
    return (out * jax.nn.silu(gate)) @ w_out


def _linear_combine(left, right):
    a1, b1 = left
    a2, b2 = right
    return a1 * a2, a2 * b1 + b2


def rglru_block(h, w_in, conv_w, conv_b, w_a, b_a, w_x, b_x, lam, w_out):
    B, S, _ = h.shape
    proj = h @ w_in
    xb, gate = jnp.split(proj, [LRU_W], axis=-1)
    xp = jnp.pad(xb, ((0, 0), (CONV_W - 1, 0), (0, 0)))
    xc = conv_b + sum(conv_w[tap] * xp[:, tap:tap + S] for tap in range(CONV_W))

    xg = xc.reshape(B, S, N_LRU_BLOCKS, LRU_BLOCK_W)
    r = jax.nn.sigmoid(jnp.einsum('bshi,hij->bshj', xg, w_a) + b_a).reshape(B, S, LRU_W)
    i = jax.nn.sigmoid(jnp.einsum('bshi,hij->bshj', xg, w_x) + b_x).reshape(B, S, LRU_W)

    log_a = -C_RG * r.astype(jnp.float32) * jax.nn.softplus(-lam.astype(jnp.float32))
    a = jnp.exp(log_a)
    mult = jnp.sqrt(-jnp.expm1(2.0 * log_a))
    b = mult * (i * xc).astype(jnp.float32)
    _, hs = lax.associative_scan(_linear_combine, (a, b), axis=1)
    y = hs.astype(h.dtype) * jax.nn.silu(gate)
    return y @ w_out


def _fwd_setup_inputs(seed: int = 0) -> dict:
    key = jax.random.key(seed)
    ks = jax.random.split(key, 16)
    f32 = jnp.float32
    nrm = lambda k, shape, fan_in: jax.random.normal(k, shape, f32) * (fan_in ** -0.5)
    x = jax.random.normal(ks[0], (BATCH, SEQ, D_MODEL), f32)
    norm_pre = 1.0 + 0.05 * jax.random.normal(ks[1], (DEPTH, D_MODEL), f32)
    norm_post = 1.0 + 0.05 * jax.random.normal(ks[2], (DEPTH, D_MODEL), f32)

    attn_w_in = nrm(ks[3], (N_ATTN_LAYERS, D_MODEL, ATTN_IN_W), D_MODEL)
    attn_w_out = nrm(ks[4], (N_ATTN_LAYERS, Q_W, D_MODEL), Q_W)
    attn_sinks = 0.5 * jax.random.normal(ks[5], (N_ATTN_LAYERS, N_HEADS), f32)

    lru_w_in = nrm(ks[6], (N_LRU_LAYERS, D_MODEL, LRU_IN_W), D_MODEL)
    lru_conv_w = nrm(ks[7], (N_LRU_LAYERS, CONV_W, LRU_W), CONV_W)
    lru_conv_b = 0.01 * jax.random.normal(ks[8], (N_LRU_LAYERS, LRU_W), f32)
    lru_w_a = nrm(ks[9], (N_LRU_LAYERS, N_LRU_BLOCKS, LRU_BLOCK_W, LRU_BLOCK_W), LRU_BLOCK_W)
    lru_b_a = 0.01 * jax.random.normal(ks[10], (N_LRU_LAYERS, N_LRU_BLOCKS, LRU_BLOCK_W), f32)
    lru_w_x = nrm(ks[11], (N_LRU_LAYERS, N_LRU_BLOCKS, LRU_BLOCK_W, LRU_BLOCK_W), LRU_BLOCK_W)
    lru_b_x = 0.01 * jax.random.normal(ks[12], (N_LRU_LAYERS, N_LRU_BLOCKS, LRU_BLOCK_W), f32)
    u = jax.random.uniform(ks[13], (N_LRU_LAYERS, LRU_W), f32, 0.9, 0.999)
    a0 = u ** (1.0 / C_RG)
    lru_lambda = jnp.log(a0) - jnp.log1p(-a0)
    lru_w_out = nrm(ks[14], (N_LRU_LAYERS, LRU_W, D_MODEL), LRU_W)
    return {
        "x": x, "norm_pre": norm_pre, "norm_post": norm_post,
        "attn_w_in": attn_w_in, "attn_w_out": attn_w_out, "attn_sinks": attn_sinks,
        "lru_w_in": lru_w_in, "lru_conv_w": lru_conv_w, "lru_conv_b": lru_conv_b,
        "lru_w_a": lru_w_a, "lru_b_a": lru_b_a, "lru_w_x": lru_w_x, "lru_b_x": lru_b_x,
        "lru_lambda": lru_lambda, "lru_w_out": lru_w_out,
    }


def _fwd_reference(x, norm_pre, norm_post, attn_w_in, attn_w_out, attn_sinks,
              lru_w_in, lru_conv_w, lru_conv_b, lru_w_a, lru_b_a, lru_w_x, lru_b_x,
              lru_lambda, lru_w_out):
    h = x
    for layer in range(DEPTH):
        u = rms_norm(h, norm_pre[layer])
        j = layer // N_MIXERS
        if layer % N_MIXERS == 0:
            y = sliding_window_attention(u, attn_w_in[j], attn_w_out[j], attn_sinks[j])
        else:
            y = rglru_block(u, lru_w_in[j], lru_conv_w[j], lru_conv_b[j], lru_w_a[j],
                            lru_b_a[j], lru_w_x[j], lru_b_x[j], lru_lambda[j], lru_w_out[j])
        h = h + rms_norm(y, norm_post[layer])
    return h


import jax as _jax
import jax.numpy as _jnp

TWIN_FORMAT = 'train_step'
FWD_PARAMS = ['x', 'norm_pre', 'norm_post', 'attn_w_in', 'attn_w_out', 'attn_sinks', 'lru_w_in', 'lru_conv_w', 'lru_conv_b', 'lru_w_a', 'lru_b_a', 'lru_w_x', 'lru_b_x', 'lru_lambda', 'lru_w_out']
TWIN_WEIGHTS = ['norm_pre', 'norm_post', 'attn_w_in', 'attn_w_out', 'attn_sinks', 'lru_w_in', 'lru_conv_w', 'lru_conv_b', 'lru_w_a', 'lru_b_a', 'lru_w_x', 'lru_b_x', 'lru_lambda', 'lru_w_out']
TWIN_DIFF_INPUT = 'x'
TWIN_INPUTS = ['x', 'norm_pre', 'norm_post', 'attn_w_in', 'attn_w_out', 'attn_sinks', 'lru_w_in', 'lru_conv_w', 'lru_conv_b', 'lru_w_a', 'lru_b_a', 'lru_w_x', 'lru_b_x', 'lru_lambda', 'lru_w_out', 'loss_target', 'm_norm_pre', 'm_norm_post', 'm_attn_w_in', 'm_attn_w_out', 'm_attn_sinks', 'm_lru_w_in', 'm_lru_conv_w', 'm_lru_conv_b', 'm_lru_w_a', 'm_lru_b_a', 'm_lru_w_x', 'm_lru_b_x', 'm_lru_lambda', 'm_lru_w_out', 'v_norm_pre', 'v_norm_post', 'v_attn_w_in', 'v_attn_w_out', 'v_attn_sinks', 'v_lru_w_in', 'v_lru_conv_w', 'v_lru_conv_b', 'v_lru_w_a', 'v_lru_b_a', 'v_lru_w_x', 'v_lru_b_x', 'v_lru_lambda', 'v_lru_w_out']
TWIN_OUTPUTS = ['loss', 'grad_x', 'grad_norm_pre', 'grad_norm_post', 'grad_attn_w_in', 'grad_attn_w_out', 'grad_attn_sinks', 'grad_lru_w_in', 'grad_lru_conv_w', 'grad_lru_conv_b', 'grad_lru_w_a', 'grad_lru_b_a', 'grad_lru_w_x', 'grad_lru_b_x', 'grad_lru_lambda', 'grad_lru_w_out', 'delta_norm_pre', 'delta_norm_post', 'delta_attn_w_in', 'delta_attn_w_out', 'delta_attn_sinks', 'delta_lru_w_in', 'delta_lru_conv_w', 'delta_lru_conv_b', 'delta_lru_w_a', 'delta_lru_b_a', 'delta_lru_w_x', 'delta_lru_b_x', 'delta_lru_lambda', 'delta_lru_w_out', 'new_m_norm_pre', 'new_m_norm_post', 'new_m_attn_w_in', 'new_m_attn_w_out', 'new_m_attn_sinks', 'new_m_lru_w_in', 'new_m_lru_conv_w', 'new_m_lru_conv_b', 'new_m_lru_w_a', 'new_m_lru_b_a', 'new_m_lru_w_x', 'new_m_lru_b_x', 'new_m_lru_lambda', 'new_m_lru_w_out', 'new_v_norm_pre', 'new_v_norm_post', 'new_v_attn_w_in', 'new_v_attn_w_out', 'new_v_attn_sinks', 'new_v_lru_w_in', 'new_v_lru_conv_w', 'new_v_lru_conv_b', 'new_v_lru_w_a', 'new_v_lru_b_a', 'new_v_lru_w_x', 'new_v_lru_b_x', 'new_v_lru_lambda', 'new_v_lru_w_out']
TWIN_LEAF_KINDS = {'loss': 'loss', 'grad_x': 'grad_x', 'grad_norm_pre': 'grad_w', 'grad_norm_post': 'grad_w', 'grad_attn_w_in': 'grad_w', 'grad_attn_w_out': 'grad_w', 'grad_attn_sinks': 'grad_w', 'grad_lru_w_in': 'grad_w', 'grad_lru_conv_w': 'grad_w', 'grad_lru_conv_b': 'grad_w', 'grad_lru_w_a': 'grad_w', 'grad_lru_b_a': 'grad_w', 'grad_lru_w_x': 'grad_w', 'grad_lru_b_x': 'grad_w', 'grad_lru_lambda': 'grad_w', 'grad_lru_w_out': 'grad_w', 'delta_norm_pre': 'delta_w', 'delta_norm_post': 'delta_w', 'delta_attn_w_in': 'delta_w', 'delta_attn_w_out': 'delta_w', 'delta_attn_sinks': 'delta_w', 'delta_lru_w_in': 'delta_w', 'delta_lru_conv_w': 'delta_w', 'delta_lru_conv_b': 'delta_w', 'delta_lru_w_a': 'delta_w', 'delta_lru_b_a': 'delta_w', 'delta_lru_w_x': 'delta_w', 'delta_lru_b_x': 'delta_w', 'delta_lru_lambda': 'delta_w', 'delta_lru_w_out': 'delta_w', 'new_m_norm_pre': 'new_m', 'new_m_norm_post': 'new_m', 'new_m_attn_w_in': 'new_m', 'new_m_attn_w_out': 'new_m', 'new_m_attn_sinks': 'new_m', 'new_m_lru_w_in': 'new_m', 'new_m_lru_conv_w': 'new_m', 'new_m_lru_conv_b': 'new_m', 'new_m_lru_w_a': 'new_m', 'new_m_lru_b_a': 'new_m', 'new_m_lru_w_x': 'new_m', 'new_m_lru_b_x': 'new_m', 'new_m_lru_lambda': 'new_m', 'new_m_lru_w_out': 'new_m', 'new_v_norm_pre': 'new_v', 'new_v_norm_post': 'new_v', 'new_v_attn_w_in': 'new_v', 'new_v_attn_w_out': 'new_v', 'new_v_attn_sinks': 'new_v', 'new_v_lru_w_in': 'new_v', 'new_v_lru_conv_w': 'new_v', 'new_v_lru_conv_b': 'new_v', 'new_v_lru_w_a': 'new_v', 'new_v_lru_b_a': 'new_v', 'new_v_lru_w_x': 'new_v', 'new_v_lru_b_x': 'new_v', 'new_v_lru_lambda': 'new_v', 'new_v_lru_w_out': 'new_v'}


def _forward(args):
    return _fwd_reference(*[args[k] for k in FWD_PARAMS])


def _output_shape():
    def fwd():
        inp = _fwd_setup_inputs(0)
        return _fwd_reference(*[inp[k] for k in FWD_PARAMS])
    out = _jax.eval_shape(fwd)
    return out.shape, out.dtype

N_MICROBATCH = 1
ADAM_LR = 0.001
ADAM_B1 = 0.9
ADAM_B2 = 0.999
ADAM_EPS = 1e-08
ADAM_WD = 0.01
ADAM_STEP = 10
PER_EXAMPLE_BATCH_AXIS = {'x': 0, 'loss_target': 0}
SHARED_INPUTS = []
_WEIGHT_DTYPES = {'norm_pre': _jnp.float32, 'norm_post': _jnp.float32, 'attn_w_in': _jnp.float32, 'attn_w_out': _jnp.float32, 'attn_sinks': _jnp.float32, 'lru_w_in': _jnp.float32, 'lru_conv_w': _jnp.float32, 'lru_conv_b': _jnp.float32, 'lru_w_a': _jnp.float32, 'lru_b_a': _jnp.float32, 'lru_w_x': _jnp.float32, 'lru_b_x': _jnp.float32, 'lru_lambda': _jnp.float32, 'lru_w_out': _jnp.float32}
MOMENT_SCALE = {'norm_pre': 1.021529e+00, 'norm_post': 3.183808e+01, 'attn_w_in': 8.046515e-01, 'attn_w_out': 8.465783e-01, 'attn_sinks': 2.763057e-01, 'lru_w_in': 5.859407e-01, 'lru_conv_w': 7.846034e-01, 'lru_conv_b': 1.781071e+01, 'lru_w_a': 2.254898e-01, 'lru_b_a': 2.251309e-01, 'lru_w_x': 4.201660e-01, 'lru_b_x': 3.097928e-01, 'lru_lambda': 4.681083e-01, 'lru_w_out': 8.251827e-01}


def _to_microbatches(a, axis):
    t = _jnp.moveaxis(a, axis, 0)
    t = t.reshape((N_MICROBATCH, t.shape[0] // N_MICROBATCH) + t.shape[1:])
    return _jnp.moveaxis(t, 1, axis + 1)


def setup_inputs(seed: int = 0) -> dict:
    inp = _fwd_setup_inputs(seed)
    key = _jax.random.fold_in(_jax.random.key(seed), 7919)
    shape, _ = _output_shape()
    out = dict(inp)
    out["loss_target"] = _jax.random.normal(_jax.random.fold_in(key, 0), shape, _jnp.float32)
    for i, name in enumerate(TWIN_WEIGHTS):
        w = inp[name].astype(_jnp.float32)
        if MOMENT_SCALE is None:
            s = _jnp.sqrt(_jnp.mean(_jnp.square(w)) + 1e-30)
        else:
            s = MOMENT_SCALE[name]
        km, kv = _jax.random.split(_jax.random.fold_in(key, i + 1))
        out[name] = w
        out["m_" + name] = s * _jax.random.normal(km, w.shape, _jnp.float32)
        out["v_" + name] = (s * s) * _jax.random.uniform(kv, w.shape, _jnp.float32, 0.5, 1.5)
    if N_MICROBATCH > 1:
        for name, axis in PER_EXAMPLE_BATCH_AXIS.items():
            out[name] = _to_microbatches(out[name], axis)
    return {'x': out['x'], 'norm_pre': out['norm_pre'], 'norm_post': out['norm_post'], 'attn_w_in': out['attn_w_in'], 'attn_w_out': out['attn_w_out'], 'attn_sinks': out['attn_sinks'], 'lru_w_in': out['lru_w_in'], 'lru_conv_w': out['lru_conv_w'], 'lru_conv_b': out['lru_conv_b'], 'lru_w_a': out['lru_w_a'], 'lru_b_a': out['lru_b_a'], 'lru_w_x': out['lru_w_x'], 'lru_b_x': out['lru_b_x'], 'lru_lambda': out['lru_lambda'], 'lru_w_out': out['lru_w_out'], 'loss_target': out['loss_target'], 'm_norm_pre': out['m_norm_pre'], 'm_norm_post': out['m_norm_post'], 'm_attn_w_in': out['m_attn_w_in'], 'm_attn_w_out': out['m_attn_w_out'], 'm_attn_sinks': out['m_attn_sinks'], 'm_lru_w_in': out['m_lru_w_in'], 'm_lru_conv_w': out['m_lru_conv_w'], 'm_lru_conv_b': out['m_lru_conv_b'], 'm_lru_w_a': out['m_lru_w_a'], 'm_lru_b_a': out['m_lru_b_a'], 'm_lru_w_x': out['m_lru_w_x'], 'm_lru_b_x': out['m_lru_b_x'], 'm_lru_lambda': out['m_lru_lambda'], 'm_lru_w_out': out['m_lru_w_out'], 'v_norm_pre': out['v_norm_pre'], 'v_norm_post': out['v_norm_post'], 'v_attn_w_in': out['v_attn_w_in'], 'v_attn_w_out': out['v_attn_w_out'], 'v_attn_sinks': out['v_attn_sinks'], 'v_lru_w_in': out['v_lru_w_in'], 'v_lru_conv_w': out['v_lru_conv_w'], 'v_lru_conv_b': out['v_lru_conv_b'], 'v_lru_w_a': out['v_lru_w_a'], 'v_lru_b_a': out['v_lru_b_a'], 'v_lru_w_x': out['v_lru_w_x'], 'v_lru_b_x': out['v_lru_b_x'], 'v_lru_lambda': out['v_lru_lambda'], 'v_lru_w_out': out['v_lru_w_out']}


def _loss(weights, diff, rest, loss_target):
    with _jax.named_scope("forward"):
        args = {**rest, TWIN_DIFF_INPUT: diff, **{k: w.astype(_WEIGHT_DTYPES[k]) for k, w in weights.items()}}
        y = _forward(args)
    with _jax.named_scope("loss_head"):
        err = _jnp.square(y.astype(_jnp.float32) - loss_target)
        return 0.5 * _jnp.sum(_jnp.mean(err, axis=-1)) if err.ndim else 0.5 * err


def _adamw(w, g, m, v):
    m = ADAM_B1 * m + (1.0 - ADAM_B1) * g
    v = ADAM_B2 * v + (1.0 - ADAM_B2) * _jnp.square(g)
    m_hat = m / (1.0 - ADAM_B1 ** ADAM_STEP)
    v_hat = v / (1.0 - ADAM_B2 ** ADAM_STEP)
    delta = -ADAM_LR * (m_hat / (_jnp.sqrt(v_hat) + ADAM_EPS) + ADAM_WD * w)
    return delta, m, v


def reference(x, norm_pre, norm_post, attn_w_in, attn_w_out, attn_sinks, lru_w_in, lru_conv_w, lru_conv_b, lru_w_a, lru_b_a, lru_w_x, lru_b_x, lru_lambda, lru_w_out, loss_target, m_norm_pre, m_norm_post, m_attn_w_in, m_attn_w_out, m_attn_sinks, m_lru_w_in, m_lru_conv_w, m_lru_conv_b, m_lru_w_a, m_lru_b_a, m_lru_w_x, m_lru_b_x, m_lru_lambda, m_lru_w_out, v_norm_pre, v_norm_post, v_attn_w_in, v_attn_w_out, v_attn_sinks, v_lru_w_in, v_lru_conv_w, v_lru_conv_b, v_lru_w_a, v_lru_b_a, v_lru_w_x, v_lru_b_x, v_lru_lambda, v_lru_w_out):
    given = dict(x=x, norm_pre=norm_pre, norm_post=norm_post, attn_w_in=attn_w_in, attn_w_out=attn_w_out, attn_sinks=attn_sinks, lru_w_in=lru_w_in, lru_conv_w=lru_conv_w, lru_conv_b=lru_conv_b, lru_w_a=lru_w_a, lru_b_a=lru_b_a, lru_w_x=lru_w_x, lru_b_x=lru_b_x, lru_lambda=lru_lambda, lru_w_out=lru_w_out, loss_target=loss_target, m_norm_pre=m_norm_pre, m_norm_post=m_norm_post, m_attn_w_in=m_attn_w_in, m_attn_w_out=m_attn_w_out, m_attn_sinks=m_attn_sinks, m_lru_w_in=m_lru_w_in, m_lru_conv_w=m_lru_conv_w, m_lru_conv_b=m_lru_conv_b, m_lru_w_a=m_lru_w_a, m_lru_b_a=m_lru_b_a, m_lru_w_x=m_lru_w_x, m_lru_b_x=m_lru_b_x, m_lru_lambda=m_lru_lambda, m_lru_w_out=m_lru_w_out, v_norm_pre=v_norm_pre, v_norm_post=v_norm_post, v_attn_w_in=v_attn_w_in, v_attn_w_out=v_attn_w_out, v_attn_sinks=v_attn_sinks, v_lru_w_in=v_lru_w_in, v_lru_conv_w=v_lru_conv_w, v_lru_conv_b=v_lru_conv_b, v_lru_w_a=v_lru_w_a, v_lru_b_a=v_lru_b_a, v_lru_w_x=v_lru_w_x, v_lru_b_x=v_lru_b_x, v_lru_lambda=v_lru_lambda, v_lru_w_out=v_lru_w_out)
    weights = {n: given[n] for n in TWIN_WEIGHTS}
    shared = {n: given[n] for n in SHARED_INPUTS}
    per_example = {n: given[n] for n in ['x']}
    grad_fn = _jax.value_and_grad(_loss, argnums=(0, 1))

    def one_microbatch(ex, loss_target):
        ex = dict(ex)
        diff = ex.pop(TWIN_DIFF_INPUT)
        return grad_fn(weights, diff, {**shared, **ex}, loss_target)

    if N_MICROBATCH == 1:
        loss, (grad_w, grad_x) = one_microbatch(per_example, given["loss_target"])
    else:
        def body(carry, xs):
            loss_sum, grad_sum = carry
            l_k, (gw_k, gx_k) = one_microbatch(xs[0], xs[1])
            with _jax.named_scope("update"):
                return (loss_sum + l_k, _jax.tree.map(_jnp.add, grad_sum, gw_k)), gx_k

        init = (_jnp.zeros((), _jnp.float32), _jax.tree.map(_jnp.zeros_like, weights))
        (loss, grad_w), grad_x = _jax.lax.scan(body, init, (per_example, given["loss_target"]))
    with _jax.named_scope("update"):
        delta_w, new_m, new_v = {}, {}, {}
        for n in TWIN_WEIGHTS:
            delta_w[n], new_m[n], new_v[n] = _adamw(weights[n], grad_w[n], given["m_" + n], given["v_" + n])
    return (loss, grad_x, *[grad_w[n] for n in TWIN_WEIGHTS], *[delta_w[n] for n in TWIN_WEIGHTS],
            *[new_m[n] for n in TWIN_WEIGHTS], *[new_v[n] for n in TWIN_WEIGHTS])
```

```python
import math

import jax
import jax.numpy as jnp
from jax import lax
from jax.experimental import pallas as pl
from jax.experimental.pallas import tpu as pltpu

F32 = jnp.float32
BF16 = jnp.bfloat16

N_DEV = 8
HEAD_DIM = 64
GROUP = 8
WINDOW = 128
LRU_BLOCK = 256
CONV_W = 4
C_RG = 8.0
NORM_EPS = 1e-6
MASK_VALUE = -1e30

ADAM_LR = 0.001
ADAM_B1 = 0.9
ADAM_B2 = 0.999
ADAM_EPS = 1e-08
ADAM_WD = 0.01
ADAM_STEP = 10

ROW_BLOCK = 256
LANES = 128
SUBLANES = 8
PACK_W = 1024
VMEM_LIMIT = 48 * 1024 * 1024
MESH = pl.DeviceIdType.MESH


def _cparams(sem=None):
    return pltpu.CompilerParams(dimension_semantics=sem, vmem_limit_bytes=VMEM_LIMIT)


def _pick(n, target, quantum):
    best = None
    for t in range(quantum, min(n, target) + 1, quantum):
        if n % t == 0:
            best = t
    return n if best is None else best


def _sigmoid(x):
    return 1.0 / (1.0 + jnp.exp(-x))


def _dot(a, b):
    return lax.dot_general(a, b, (((1,), (0,)), ((), ())), preferred_element_type=F32)


def _dot_nt(a, b):
    return lax.dot_general(a, b, (((1,), (1,)), ((), ())), preferred_element_type=F32)


def _dot_tn(a, b):
    return lax.dot_general(a, b, (((0,), (0,)), ((), ())), preferred_element_type=F32)


def _matmul(a, b, *, ta=False, tb=False, out_dtype=F32, name):
    if ta:
        K, M = a.shape
    else:
        M, K = a.shape
    if tb:
        N, K2 = b.shape
    else:
        K2, N = b.shape
    assert K == K2, (a.shape, b.shape, ta, tb)
    tm = _pick(M, 1024, 256)
    tn = _pick(N, 1536, 256)
    tk = _pick(K, 512, 256)
    nk = K // tk
    dot = {(False, False): _dot, (False, True): _dot_nt, (True, False): _dot_tn}[(ta, tb)]

    def body(a_ref, b_ref, o_ref, acc_ref):
        k = pl.program_id(2)

        @pl.when(k == 0)
        def _():
            acc_ref[...] = jnp.zeros_like(acc_ref)

        acc_ref[...] += dot(a_ref[...].astype(BF16), b_ref[...].astype(BF16))

        @pl.when(k == nk - 1)
        def _():
            o_ref[...] = acc_ref[...].astype(o_ref.dtype)

    a_spec = pl.BlockSpec((tk, tm), lambda i, j, k: (k, i)) if ta else pl.BlockSpec((tm, tk), lambda i, j, k: (i, k))
    b_spec = pl.BlockSpec((tn, tk), lambda i, j, k: (j, k)) if tb else pl.BlockSpec((tk, tn), lambda i, j, k: (k, j))
    return pl.pallas_call(
        body, name=name,
        grid=(M // tm, N // tn, nk),
        in_specs=[a_spec, b_spec],
        out_specs=pl.BlockSpec((tm, tn), lambda i, j, k: (i, j)),
        out_shape=jax.ShapeDtypeStruct((M, N), out_dtype),
        scratch_shapes=[pltpu.VMEM((tm, tn), F32)],
        compiler_params=_cparams(("parallel", "parallel", "arbitrary")),
    )(a, b)


def _rms_fwd(h, g, *, name):
    T, D = h.shape
    tm = _pick(T, ROW_BLOCK, SUBLANES)

    def body(h_ref, g_ref, u_ref):
        x = h_ref[...]
        r = lax.rsqrt(jnp.mean(x * x, axis=-1, keepdims=True) + NORM_EPS)
        u_ref[...] = ((x * r) * g_ref[...]).astype(u_ref.dtype)

    return pl.pallas_call(
        body, name=name, grid=(T // tm,),
        in_specs=[pl.BlockSpec((tm, D), lambda i: (i, 0)), pl.BlockSpec((1, D), lambda i: (0, 0))],
        out_specs=pl.BlockSpec((tm, D), lambda i: (i, 0)),
        out_shape=jax.ShapeDtypeStruct((T, D), BF16),
        compiler_params=_cparams(("parallel",)),
    )(h, g)


def _post_fwd(y, g, h, *, name):
    T, D = y.shape
    tm = _pick(T, ROW_BLOCK, SUBLANES)

    def body(y_ref, g_ref, h_ref, o_ref):
        x = y_ref[...]
        r = lax.rsqrt(jnp.mean(x * x, axis=-1, keepdims=True) + NORM_EPS)
        o_ref[...] = h_ref[...] + (x * r) * g_ref[...]

    return pl.pallas_call(
        body, name=name, grid=(T // tm,),
        in_specs=[pl.BlockSpec((tm, D), lambda i: (i, 0)), pl.BlockSpec((1, D), lambda i: (0, 0)),
                  pl.BlockSpec((tm, D), lambda i: (i, 0))],
        out_specs=pl.BlockSpec((tm, D), lambda i: (i, 0)),
        out_shape=jax.ShapeDtypeStruct((T, D), F32),
        compiler_params=_cparams(("parallel",)),
    )(y, g, h)


def _rms_bwd(x, g, dz, res, *, out_dtype, name):
    T, D = x.shape
    tm = _pick(T, ROW_BLOCK, SUBLANES)
    has_res = res is not None

    def body(*refs):
        if has_res:
            x_ref, g_ref, dz_ref, res_ref, dx_ref, dg_ref = refs
        else:
            x_ref, g_ref, dz_ref, dx_ref, dg_ref = refs
        i = pl.program_id(0)

        @pl.when(i == 0)
        def _():
            dg_ref[...] = jnp.zeros_like(dg_ref)

        xv = x_ref[...]
        dzv = dz_ref[...]
        r = lax.rsqrt(jnp.mean(xv * xv, axis=-1, keepdims=True) + NORM_EPS)
        xhat = xv * r
        dg_ref[...] += jnp.sum(dzv * xhat, axis=0, keepdims=True)
        dxh = dzv * g_ref[...]
        dx = r * (dxh - xhat * jnp.mean(dxh * xhat, axis=-1, keepdims=True))
        if has_res:
            dx = dx + res_ref[...]
        dx_ref[...] = dx.astype(dx_ref.dtype)

    row = pl.BlockSpec((tm, D), lambda i: (i, 0))
    vec = pl.BlockSpec((1, D), lambda i: (0, 0))
    ins = [x, g, dz] + ([res] if has_res else [])
    return pl.pallas_call(
        body, name=name, grid=(T // tm,),
        in_specs=[row, vec, row] + ([row] if has_res else []),
        out_specs=[row, vec],
        out_shape=[jax.ShapeDtypeStruct((T, D), out_dtype), jax.ShapeDtypeStruct((1, D), F32)],
        compiler_params=_cparams(("arbitrary",)),
    )(*ins)


def _loss_head(h, target, *, name):
    T, D = h.shape
    tm = _pick(T, ROW_BLOCK, SUBLANES)

    def body(h_ref, t_ref, dh_ref, l_ref):
        i = pl.program_id(0)

        @pl.when(i == 0)
        def _():
            l_ref[...] = jnp.zeros_like(l_ref)

        e = h_ref[...] - t_ref[...]
        dh_ref[...] = e * (1.0 / D)
        row = jnp.sum(e * e, axis=-1, keepdims=True) * (0.5 / D)
        l_ref[...] += jnp.sum(row, axis=0, keepdims=True)

    row = pl.BlockSpec((tm, D), lambda i: (i, 0))
    return pl.pallas_call(
        body, name=name, grid=(T // tm,),
        in_specs=[row, row],
        out_specs=[row, pl.BlockSpec((1, 1), lambda i: (0, 0))],
        out_shape=[jax.ShapeDtypeStruct((T, D), F32), jax.ShapeDtypeStruct((1, 1), F32)],
        compiler_params=_cparams(("arbitrary",)),
    )(h, target)


def _attn_dims(P):
    Q = P * 4 // 9
    KV = Q // GROUP
    assert 2 * Q + 2 * KV == P and KV % LANES == 0
    return Q, KV


def _attn_specs(Q, KV, nb):
    blk = WINDOW
    q_spec = pl.BlockSpec((blk, Q), lambda i: (jnp.minimum(i, nb - 1), 0))
    g_spec = pl.BlockSpec((blk, Q), lambda i: (jnp.minimum(i, nb - 1), 1))
    kvc_spec = pl.BlockSpec((blk, 2 * KV), lambda i: (jnp.minimum(i, nb - 1), Q // KV))
    kvp_spec = pl.BlockSpec((blk, 2 * KV), lambda i: (jnp.maximum(jnp.minimum(i, nb - 1) - 1, 0), Q // KV))
    return q_spec, g_spec, kvc_spec, kvp_spec


def _band_mask(i):
    r = lax.broadcasted_iota(jnp.int32, (WINDOW, 2 * WINDOW), 0)
    c = lax.broadcasted_iota(jnp.int32, (WINDOW, 2 * WINDOW), 1)
    first_key = jnp.where(i > 0, 0, WINDOW)
    return (c > r) & (c <= r + WINDOW) & (c >= first_key)


def _pair_halves(x128, e, lo):
    if e == 0:
        x_lo = jnp.where(lo, x128, 0.0)
        x_hi = pltpu.roll(x_lo, HEAD_DIM, 1)
    else:
        x_hi = jnp.where(lo, 0.0, x128)
        x_lo = pltpu.roll(x_hi, HEAD_DIM, 1)
    return x_lo.astype(BF16), x_hi.astype(BF16)


def _softmax_sink(s, allowed, sink):
    s = jnp.where(allowed, s, MASK_VALUE)
    m = jnp.maximum(jnp.max(s, axis=1, keepdims=True), sink)
    p = jnp.exp(s - m)
    es = jnp.exp(sink - m)
    inv = 1.0 / (jnp.sum(p, axis=1, keepdims=True) + es)
    return p * inv, es * inv


def _attn_fwd(proj, sinks, *, name):
    T, P = proj.shape
    Q, KV = _attn_dims(P)
    nb = T // WINDOW
    npairs = KV // LANES
    scale = 1.0 / math.sqrt(HEAD_DIM)

    def body(sink_ref, q_ref, g_ref, kvc_ref, kvp_ref, out_ref, yp_ref):
        i = pl.program_id(0)
        allowed = _band_mask(i)
        lo = lax.broadcasted_iota(jnp.int32, (2 * WINDOW, LANES), 1) < HEAD_DIM
        for p in range(npairs):
            ks = slice(p * LANES, (p + 1) * LANES)
            vs = slice(KV + p * LANES, KV + (p + 1) * LANES)
            k128 = jnp.concatenate([kvp_ref[:, ks], kvc_ref[:, ks]], axis=0)
            v128 = jnp.concatenate([kvp_ref[:, vs], kvc_ref[:, vs]], axis=0)
            for e in range(2):
                kvh = 2 * p + e
                khalf = _pair_halves(k128, e, lo)
                vhalf = _pair_halves(v128, e, lo)
                for j in range(GROUP // 2):
                    cs = slice(kvh * GROUP * HEAD_DIM + j * LANES, kvh * GROUP * HEAD_DIM + (j + 1) * LANES)
                    q2 = (q_ref[:, cs] * scale).astype(BF16)
                    o2 = None
                    for half in range(2):
                        n = kvh * GROUP + 2 * j + half
                        s = _dot_nt(q2, khalf[half])
                        pn, _ = _softmax_sink(s, allowed, sink_ref[n])
                        o = _dot(pn.astype(BF16), vhalf[half])
                        o2 = o if o2 is None else o2 + o
                    out_ref[:, cs] = o2
                    g2 = g_ref[:, cs]
                    yp_ref[:, cs] = (o2 * (g2 * _sigmoid(g2))).astype(BF16)

    q_spec, g_spec, kvc_spec, kvp_spec = _attn_specs(Q, KV, nb)
    row = pl.BlockSpec((WINDOW, Q), lambda i: (i, 0))
    return pl.pallas_call(
        body, name=name, grid=(nb,),
        in_specs=[pl.BlockSpec(memory_space=pltpu.SMEM), q_spec, g_spec, kvc_spec, kvp_spec],
        out_specs=[row, row],
        out_shape=[jax.ShapeDtypeStruct((T, Q), F32), jax.ShapeDtypeStruct((T, Q), BF16)],
        compiler_params=_cparams(("parallel",)),
    )(sinks, proj, proj, proj, proj)


def _attn_bwd(proj, out, dyp, sinks, *, name):
    T, P = proj.shape
    Q, KV = _attn_dims(P)
    nb = T // WINDOW
    npairs = KV // LANES
    H = Q // HEAD_DIM
    scale = 1.0 / math.sqrt(HEAD_DIM)

    def body(sink_ref, q_ref, g_ref, kvc_ref, kvp_ref, out_ref, dyp_ref, dqg_ref, dkv_ref, dsink_ref, carry_ref):
        i = pl.program_id(0)

        @pl.when(i == 0)
        def _():
            carry_ref[...] = jnp.zeros_like(carry_ref)
            dsink_ref[...] = jnp.zeros_like(dsink_ref)

        @pl.when(i == nb)
        def _():
            dkv_ref[...] = carry_ref[...].astype(dkv_ref.dtype)

        @pl.when(i < nb)
        def _():
            allowed = _band_mask(i)
            lo = lax.broadcasted_iota(jnp.int32, (2 * WINDOW, LANES), 1) < HEAD_DIM
            lo_q = lax.broadcasted_iota(jnp.int32, (WINDOW, LANES), 1) < HEAD_DIM
            for p in range(npairs):
                ks = slice(p * LANES, (p + 1) * LANES)
                vs = slice(KV + p * LANES, KV + (p + 1) * LANES)
                k128 = jnp.concatenate([kvp_ref[:, ks], kvc_ref[:, ks]], axis=0)
                v128 = jnp.concatenate([kvp_ref[:, vs], kvc_ref[:, vs]], axis=0)
                dk_e, dv_e = [], []
                for e in range(2):
                    kvh = 2 * p + e
                    khalf = _pair_halves(k128, e, lo)
                    vhalf = _pair_halves(v128, e, lo)
                    acc_k = jnp.zeros((2 * WINDOW, LANES), F32)
                    acc_v = jnp.zeros((2 * WINDOW, LANES), F32)
                    for j in range(GROUP // 2):
                        c0 = kvh * GROUP * HEAD_DIM + j * LANES
                        cs = slice(c0, c0 + LANES)
                        gs = slice(Q + c0, Q + c0 + LANES)
                        q2 = (q_ref[:, cs] * scale).astype(BF16)
                        g2 = g_ref[:, cs]
                        o2 = out_ref[:, cs]
                        dy2 = dyp_ref[:, cs]
                        sg = _sigmoid(g2)
                        do2 = dy2 * (g2 * sg)
                        dqg_ref[:, gs] = (dy2 * o2 * (sg * (1.0 + g2 * (1.0 - sg)))).astype(dqg_ref.dtype)
                        dod = do2 * o2
                        d_all = jnp.sum(dod, axis=1, keepdims=True)
                        d_lo = jnp.sum(jnp.where(lo_q, dod, 0.0), axis=1, keepdims=True)
                        deltas = (d_lo, d_all - d_lo)
                        do2b = do2.astype(BF16)
                        dq2 = jnp.zeros((WINDOW, LANES), F32)
                        dk_h, dv_h = [], []
                        for half in range(2):
                            n = kvh * GROUP + 2 * j + half
                            s = _dot_nt(q2, khalf[half])
                            pn, psink = _softmax_sink(s, allowed, sink_ref[n])
                            dp = _dot_nt(do2b, vhalf[half])
                            ds = (pn * (dp - deltas[half])).astype(BF16)
                            dq2 = dq2 + _dot(ds, khalf[half])
                            dk_h.append(_dot_tn(ds, q2))
                            dv_h.append(_dot_tn(pn.astype(BF16), do2b))
                            dsn = -jnp.sum(psink * deltas[half], axis=0, keepdims=True)
                            dsink_ref[n:n + 1, :] += jnp.broadcast_to(dsn, (1, LANES))
                        dqg_ref[:, cs] = (dq2 * scale).astype(dqg_ref.dtype)
                        acc_k = acc_k + jnp.where(lo, dk_h[0], dk_h[1])
                        acc_v = acc_v + jnp.where(lo, dv_h[0], dv_h[1])
                    dk_e.append(acc_k + pltpu.roll(acc_k, HEAD_DIM, 1))
                    dv_e.append(acc_v + pltpu.roll(acc_v, HEAD_DIM, 1))
                for sl, de in ((ks, dk_e), (vs, dv_e)):
                    d128 = jnp.where(lo, de[0], de[1])
                    dkv_ref[:, sl] = (carry_ref[:, sl] + d128[:WINDOW]).astype(dkv_ref.dtype)
                    carry_ref[:, sl] = d128[WINDOW:]

    q_spec, g_spec, kvc_spec, kvp_spec = _attn_specs(Q, KV, nb)
    last = lambda i: (jnp.minimum(i, nb - 1), 0)
    row = pl.BlockSpec((WINDOW, Q), last)
    return pl.pallas_call(
        body, name=name, grid=(nb + 1,),
        in_specs=[pl.BlockSpec(memory_space=pltpu.SMEM), q_spec, g_spec, kvc_spec, kvp_spec, row, row],
        out_specs=[pl.BlockSpec((WINDOW, 2 * Q), last),
                   pl.BlockSpec((WINDOW, 2 * KV), lambda i: (jnp.maximum(i - 1, 0), 0)),
                   pl.BlockSpec((H, LANES), lambda i: (0, 0))],
        out_shape=[jax.ShapeDtypeStruct((T, 2 * Q), BF16), jax.ShapeDtypeStruct((T, 2 * KV), BF16),
                   jax.ShapeDtypeStruct((H, LANES), F32)],
        scratch_shapes=[pltpu.VMEM((WINDOW, 2 * KV), F32)],
        compiler_params=_cparams(("arbitrary",)),
    )(sinks, proj, proj, proj, proj, out, dyp)


LRU_CHUNK = 256


def _shift_down(x, halo8, s):
    if s == 0:
        return x
    row8 = lax.broadcasted_iota(jnp.int32, (SUBLANES, 1), 0)
    r = pltpu.roll(x, s, 0)
    top = jnp.where(row8 < s, pltpu.roll(halo8, s, 0), r[:SUBLANES])
    return jnp.concatenate([top, r[SUBLANES:]], axis=0)


def _shift_up(x, halo8, s):
    if s == 0:
        return x
    n = x.shape[0]
    row8 = lax.broadcasted_iota(jnp.int32, (SUBLANES, 1), 0)
    r = pltpu.roll(x, n - s, 0)
    bot = jnp.where(row8 >= SUBLANES - s, pltpu.roll(halo8, SUBLANES - s, 0), r[n - SUBLANES:])
    return jnp.concatenate([r[:n - SUBLANES], bot], axis=0)


def _scan_fwd(a, b):
    n = a.shape[0]
    row = lax.broadcasted_iota(jnp.int32, (n, 1), 0)
    s = 1
    while s < n:
        keep = row >= s
        ar = jnp.where(keep, pltpu.roll(a, s, 0), 1.0)
        br = jnp.where(keep, pltpu.roll(b, s, 0), 0.0)
        b = a * br + b
        a = a * ar
        s *= 2
    return a, b


def _scan_rev(al, b):
    n = al.shape[0]
    row = lax.broadcasted_iota(jnp.int32, (n, 1), 0)
    s = 1
    while s < n:
        keep = row < n - s
        ar = jnp.where(keep, pltpu.roll(al, n - s, 0), 1.0)
        br = jnp.where(keep, pltpu.roll(b, n - s, 0), 0.0)
        b = b + al * br
        al = al * ar
        s *= 2
    return al, b


def _log1p_pos(z):
    return jnp.where(z < 0.01, z * (1.0 - z * (0.5 - z * (1.0 / 3.0))), jnp.log(1.0 + z))


def _neg_expm1(x):
    series = -x * (1.0 + x * (0.5 + x * (1.0 / 6.0 + x * (1.0 / 24.0 + x * (1.0 / 120.0)))))
    return jnp.where(x > -0.05, series, 1.0 - jnp.exp(x))


def _softplus_neg(lam):
    return jnp.maximum(-lam, 0.0) + _log1p_pos(jnp.exp(-jnp.abs(lam)))


def _lru_gates(xb, halo, wa, wx, ba, bx, cw_ref, cb, lam):
    xs = [_shift_down(xb, halo, s) for s in range(CONV_W)]
    xc = cb + xs[3] * cw_ref[0:1, :] + xs[2] * cw_ref[1:2, :] + xs[1] * cw_ref[2:3, :] + xs[0] * cw_ref[3:4, :]
    xcb = xc.astype(BF16)
    r = _sigmoid(_dot(xcb, wa) + ba)
    ig = _sigmoid(_dot(xcb, wx) + bx)
    sp = _softplus_neg(lam)
    log_a = (-C_RG * r) * sp
    a = jnp.exp(log_a)
    mult = jnp.sqrt(_neg_expm1(2.0 * log_a))
    return xs, xc, xcb, r, ig, sp, a, mult


def _lru_specs(nh, nt, tc, rev):
    tix = (lambda t: nt - 1 - t) if rev else (lambda t: t)
    per8 = tc // SUBLANES
    chunk = lambda off: pl.BlockSpec((tc, LRU_BLOCK), lambda h, t: (tix(t), h + off))
    prev8 = lambda off: pl.BlockSpec((SUBLANES, LRU_BLOCK), lambda h, t: (jnp.maximum(tix(t) * per8 - 1, 0), h + off))
    wblk = pl.BlockSpec((None, LRU_BLOCK, LRU_BLOCK), lambda h, t: (h, 0, 0))
    vec = pl.BlockSpec((1, LRU_BLOCK), lambda h, t: (0, h))
    cwb = pl.BlockSpec((CONV_W, LRU_BLOCK), lambda h, t: (0, h))
    return tix, chunk, prev8, wblk, vec, cwb


def _lru_fwd(proj, wa, wx, ba, bx, cw, cb, lam, *, name):
    T, W2 = proj.shape
    W = W2 // 2
    nh = W // LRU_BLOCK
    tc = _pick(T, LRU_CHUNK, SUBLANES)
    nt = T // tc

    def body(xb_ref, xh_ref, gt_ref, wa_ref, wx_ref, ba_ref, bx_ref, cw_ref, cb_ref, lam_ref, hs_ref, yp_ref, carry_ref):
        t = pl.program_id(1)

        @pl.when(t == 0)
        def _():
            carry_ref[...] = jnp.zeros_like(carry_ref)

        halo = jnp.where(t > 0, xh_ref[...], 0.0)
        _, xc, _, _, ig, _, a, mult = _lru_gates(xb_ref[...], halo, wa_ref[...], wx_ref[...], ba_ref[...], bx_ref[...],
                                               cw_ref, cb_ref[...], lam_ref[...])
        pa, hb = _scan_fwd(a, mult * (ig * xc))
        hs = pa * carry_ref[SUBLANES - 1:SUBLANES, :] + hb
        hs_ref[...] = hs
        carry_ref[...] = hs[tc - SUBLANES:]
        g = gt_ref[...]
        yp_ref[...] = (hs * (g * _sigmoid(g))).astype(BF16)

    _, chunk, prev8, wblk, vec, cwb = _lru_specs(nh, nt, tc, False)
    return pl.pallas_call(
        body, name=name, grid=(nh, nt),
        in_specs=[chunk(0), prev8(0), chunk(nh), wblk, wblk, vec, vec, cwb, vec, vec],
        out_specs=[chunk(0), chunk(0)],
        out_shape=[jax.ShapeDtypeStruct((T, W), F32), jax.ShapeDtypeStruct((T, W), BF16)],
        scratch_shapes=[pltpu.VMEM((SUBLANES, LRU_BLOCK), F32)],
        compiler_params=_cparams(("parallel", "arbitrary")),
    )(proj, proj, proj, wa, wx, ba, bx, cw, cb, lam)


def _lru_bwd(proj, hs, dyp, wa, wx, ba, bx, cw, cb, lam, *, name):
    T, W2 = proj.shape
    W = W2 // 2
    nh = W // LRU_BLOCK
    tc = _pick(T, LRU_CHUNK, SUBLANES)
    nt = T // tc

    def body(xb_ref, xh_ref, gt_ref, hs_ref, hh_ref, dyp_ref, wa_ref, wx_ref, ba_ref, bx_ref, cw_ref, cb_ref, lam_ref,
             dx_ref, dg_ref, dwa_ref, dwx_ref, dba_ref, dbx_ref, dcw_ref, dcb_ref, dlam_ref,
             ca_ref, cl_ref, cx_ref):
        t = pl.program_id(1)
        first = t == nt - 1

        @pl.when(t == 0)
        def _():
            for ref in (ca_ref, cl_ref, cx_ref, dwa_ref, dwx_ref, dba_ref, dbx_ref, dcw_ref, dcb_ref, dlam_ref):
                ref[...] = jnp.zeros_like(ref)

        xb = xb_ref[...]
        halo = jnp.where(first, 0.0, xh_ref[...])
        wa = wa_ref[...]
        wx = wx_ref[...]
        lam = lam_ref[...]
        xs, xc, xcb, r, ig, sp, a, mult = _lru_gates(xb, halo, wa, wx, ba_ref[...], bx_ref[...], cw_ref, cb_ref[...], lam)
        hsv = hs_ref[...]
        g = gt_ref[...]
        dy = dyp_ref[...]
        sg = _sigmoid(g)
        dg_ref[...] = (dy * hsv * (sg * (1.0 + g * (1.0 - sg)))).astype(dg_ref.dtype)
        dhs = dy * (g * sg)

        al = _shift_up(a, ca_ref[...], 1)
        pal, lb = _scan_rev(al, dhs)
        lmb = lb + pal * cl_ref[0:1, :]
        hprev = _shift_down(hsv, jnp.where(first, 0.0, hh_ref[...]), 1)
        da = lmb * hprev
        ixc = ig * xc
        dmult = lmb * ixc
        dlog_a = da * a - dmult * (a * a) / mult
        dr = dlog_a * (-C_RG * sp)
        dlam_ref[...] += jnp.sum(dlog_a * r, axis=0, keepdims=True) * (C_RG * _sigmoid(-lam))
        dpa = dr * (r * (1.0 - r))
        dpx = (lmb * mult * xc) * (ig * (1.0 - ig))
        dpab = dpa.astype(BF16)
        dpxb = dpx.astype(BF16)
        dwa_ref[...] += _dot_tn(xcb, dpab)
        dwx_ref[...] += _dot_tn(xcb, dpxb)
        dba_ref[...] += jnp.sum(dpa, axis=0, keepdims=True)
        dbx_ref[...] += jnp.sum(dpx, axis=0, keepdims=True)
        dxc = lmb * mult * ig + _dot_nt(dpab, wa) + _dot_nt(dpxb, wx)
        dcb_ref[...] += jnp.sum(dxc, axis=0, keepdims=True)
        for s in range(CONV_W):
            dcw_ref[CONV_W - 1 - s:CONV_W - s, :] += jnp.sum(dxc * xs[s], axis=0, keepdims=True)
        cxv = cx_ref[...]
        dxb = dxc * cw_ref[3:4, :]
        for s in range(1, CONV_W):
            dxb = dxb + _shift_up(dxc, cxv, s) * cw_ref[3 - s:4 - s, :]
        dx_ref[...] = dxb.astype(dx_ref.dtype)
        ca_ref[...] = a[:SUBLANES]
        cl_ref[...] = lmb[:SUBLANES]
        cx_ref[...] = dxc[:SUBLANES]

    tix, chunk, prev8, wblk, vec, cwb = _lru_specs(nh, nt, tc, True)
    hchunk = pl.BlockSpec((tc, LRU_BLOCK), lambda h, t: (tix(t), h))
    carry = pltpu.VMEM((SUBLANES, LRU_BLOCK), F32)
    return pl.pallas_call(
        body, name=name, grid=(nh, nt),
        in_specs=[chunk(0), prev8(0), chunk(nh), hchunk, prev8(0), hchunk, wblk, wblk, vec, vec, cwb, vec, vec],
        out_specs=[hchunk, hchunk, wblk, wblk, vec, vec, cwb, vec, vec],
        out_shape=[jax.ShapeDtypeStruct((T, W), BF16), jax.ShapeDtypeStruct((T, W), BF16),
                   jax.ShapeDtypeStruct((nh, LRU_BLOCK, LRU_BLOCK), F32), jax.ShapeDtypeStruct((nh, LRU_BLOCK, LRU_BLOCK), F32),
                   jax.ShapeDtypeStruct((1, W), F32), jax.ShapeDtypeStruct((1, W), F32),
                   jax.ShapeDtypeStruct((CONV_W, W), F32), jax.ShapeDtypeStruct((1, W), F32), jax.ShapeDtypeStruct((1, W), F32)],
        scratch_shapes=[carry, carry, carry],
        compiler_params=_cparams(("parallel", "arbitrary")),
    )(proj, proj, proj, hs, hs, dyp, wa, wx, ba, bx, cw, cb, lam)


def _position():
    return lax.axis_index("x"), lax.axis_index("y"), lax.axis_index("c")


def _all_gather(arrs, *, name):
    n = len(arrs)

    def body(*refs):
        ins, outs = refs[:n], refs[n:2 * n]
        send_sems, recv_sems, local_sems = refs[2 * n:]
        x, y, c = _position()
        me, sibling = (x, y, c), (x, y, 1 - c)
        chips = [(1 - x, y), (x, 1 - y), (1 - x, 1 - y)]

        def copy(a, k, block, to, src=None):
            px, py, pc = block
            rows = outs[a].at[4 * px + 2 * py + pc]
            return pltpu.make_async_remote_copy(
                src_ref=rows if src is None else src, dst_ref=rows,
                send_sem=send_sems.at[a, k], recv_sem=recv_sems.at[a, k],
                device_id=to, device_id_type=MESH)

        mine, first, passed = [], [], []
        for a in range(n):
            own = pltpu.make_async_copy(ins[a], outs[a].at[4 * x + 2 * y + c], local_sems.at[a])
            own.start()
            mine.append(own)
            cps = [copy(a, 0, me, sibling, src=ins[a])]
            cps += [copy(a, 1 + j, me, (*chip, c), src=ins[a]) for j, chip in enumerate(chips)]
            for cp in cps:
                cp.start()
            first += cps
        for a in range(n):
            for j, chip in enumerate(chips):
                copy(a, 1 + j, (*chip, c), me).wait_recv()
                fwd = copy(a, 4 + j, (*chip, c), sibling)
                fwd.start()
                passed.append(fwd)
        for a in range(n):
            copy(a, 0, sibling, me).wait_recv()
            for j, chip in enumerate(chips):
                copy(a, 4 + j, (*chip, 1 - c), me).wait_recv()
        for cp in first + passed:
            cp.wait_send()
        for own in mine:
            own.wait()

    any_spec = pl.BlockSpec(memory_space=pl.ANY)
    return pl.pallas_call(
        body, name=name,
        in_specs=[any_spec] * n, out_specs=[any_spec] * n,
        out_shape=[jax.ShapeDtypeStruct((N_DEV,) + a.shape, a.dtype) for a in arrs],
        scratch_shapes=[pltpu.SemaphoreType.DMA((n, 7)), pltpu.SemaphoreType.DMA((n, 7)), pltpu.SemaphoreType.DMA((n,))],
    )(*arrs)


def _exchange(arrs, *, name):
    n = len(arrs)

    def body(*refs):
        ins, outs = refs[:n], refs[n:2 * n]
        send_sems, recv_sems, local_sems = refs[2 * n:]
        x, y, c = _position()
        me = 4 * x + 2 * y + c
        copies = []
        for a in range(n):
            own = pltpu.make_async_copy(ins[a].at[me], outs[a].at[me], local_sems.at[a])
            own.start()
            copies.append(own)
        for k in range(1, N_DEV):
            px = x ^ ((k >> 2) & 1)
            py = y ^ ((k >> 1) & 1)
            pc = c ^ (k & 1)
            for a in range(n):
                cp = pltpu.make_async_remote_copy(
                    src_ref=ins[a].at[4 * px + 2 * py + pc], dst_ref=outs[a].at[me],
                    send_sem=send_sems.at[a, k - 1], recv_sem=recv_sems.at[a, k - 1],
                    device_id=(px, py, pc), device_id_type=MESH)
                cp.start()
                copies.append(cp)
        for cp in copies:
            cp.wait()

    any_spec = pl.BlockSpec(memory_space=pl.ANY)
    return pl.pallas_call(
        body, name=name,
        in_specs=[any_spec] * n, out_specs=[any_spec] * n,
        out_shape=[jax.ShapeDtypeStruct(a.shape, a.dtype) for a in arrs],
        scratch_shapes=[pltpu.SemaphoreType.DMA((n, 7)), pltpu.SemaphoreType.DMA((n, 7)), pltpu.SemaphoreType.DMA((n,))],
    )(*arrs)


def _adamw(parts, w, m, v, *, name):
    _, R, C = parts.shape
    tr = _pick(R, 256, 16)
    c1 = 1.0 / (1.0 - ADAM_B1 ** ADAM_STEP)
    c2 = 1.0 / (1.0 - ADAM_B2 ** ADAM_STEP)

    def body(p_ref, w_ref, m_ref, v_ref, g_ref, d_ref, nm_ref, nv_ref):
        g = p_ref[0].astype(F32)
        for s in range(1, N_DEV):
            g = g + p_ref[s].astype(F32)
        nm = ADAM_B1 * m_ref[...] + (1.0 - ADAM_B1) * g
        nv = ADAM_B2 * v_ref[...] + (1.0 - ADAM_B2) * (g * g)
        g_ref[...] = g
        nm_ref[...] = nm
        nv_ref[...] = nv
        d_ref[...] = -ADAM_LR * ((nm * c1) / (jnp.sqrt(nv * c2) + ADAM_EPS) + ADAM_WD * w_ref[...])

    blk = pl.BlockSpec((tr, C), lambda i: (i, 0))
    return pl.pallas_call(
        body, name=name, grid=(R // tr,),
        in_specs=[pl.BlockSpec((N_DEV, tr, C), lambda i: (0, i, 0)), blk, blk, blk],
        out_specs=[blk] * 4,
        out_shape=[jax.ShapeDtypeStruct((R, C), F32)] * 4,
        compiler_params=_cparams(("parallel",)),
    )(parts, w, m, v)


def _pack(flat_parts, row_multiple, dtype):
    lead = flat_parts[0].shape[:-1]
    total = sum(p.shape[-1] for p in flat_parts)
    quantum = PACK_W * row_multiple
    padded = -(-total // quantum) * quantum
    parts = [p.astype(dtype) for p in flat_parts]
    if padded > total:
        parts.append(jnp.zeros(lead + (padded - total,), dtype))
    return jnp.concatenate(parts, axis=-1).reshape(lead + (padded // PACK_W, PACK_W))


def _unpack(buf, shapes):
    lead = buf.shape[:-2]
    flat = buf.reshape(lead + (-1,))
    out, off = [], 0
    for shp in shapes:
        n = math.prod(shp)
        out.append(flat[..., off:off + n].reshape(lead + tuple(shp)))
        off += n
    return out


def _to_full(seg, ax):
    shard = seg.shape[1:]
    full = shard[:ax] + (N_DEV * shard[ax],) + shard[ax + 1:]
    return jnp.moveaxis(seg, 0, ax).reshape(full)


def _to_shards(full, ax):
    shp = full.shape
    split = shp[:ax] + (N_DEV, shp[ax] // N_DEV) + shp[ax + 1:]
    return jnp.moveaxis(full.reshape(split), ax, 0).reshape(N_DEV, -1)


BIG = (("attn_w_in", 2), ("attn_w_out", 1), ("lru_w_in", 2), ("lru_w_a", 2), ("lru_w_x", 2), ("lru_w_out", 1))
SMALL = (("lru_conv_w", 2), ("lru_conv_b", 1), ("lru_b_a", 2), ("lru_b_x", 2), ("lru_lambda", 1))
REPL = ("norm_pre", "norm_post", "attn_sinks")
ORDER = ("norm_pre", "norm_post", "attn_w_in", "attn_w_out", "attn_sinks", "lru_w_in", "lru_conv_w", "lru_conv_b",
         "lru_w_a", "lru_b_a", "lru_w_x", "lru_b_x", "lru_lambda", "lru_w_out")


def kernel(x, norm_pre, norm_post, attn_w_in, attn_w_out, attn_sinks, lru_w_in, lru_conv_w, lru_conv_b, lru_w_a, lru_b_a, lru_w_x, lru_b_x, lru_lambda, lru_w_out, loss_target, m_norm_pre, m_norm_post, m_attn_w_in, m_attn_w_out, m_attn_sinks, m_lru_w_in, m_lru_conv_w, m_lru_conv_b, m_lru_w_a, m_lru_b_a, m_lru_w_x, m_lru_b_x, m_lru_lambda, m_lru_w_out, v_norm_pre, v_norm_post, v_attn_w_in, v_attn_w_out, v_attn_sinks, v_lru_w_in, v_lru_conv_w, v_lru_conv_b, v_lru_w_a, v_lru_b_a, v_lru_w_x, v_lru_b_x, v_lru_lambda, v_lru_w_out):
    W = dict(norm_pre=norm_pre, norm_post=norm_post, attn_w_in=attn_w_in, attn_w_out=attn_w_out, attn_sinks=attn_sinks,
             lru_w_in=lru_w_in, lru_conv_w=lru_conv_w, lru_conv_b=lru_conv_b, lru_w_a=lru_w_a, lru_b_a=lru_b_a,
             lru_w_x=lru_w_x, lru_b_x=lru_b_x, lru_lambda=lru_lambda, lru_w_out=lru_w_out)
    M = dict(norm_pre=m_norm_pre, norm_post=m_norm_post, attn_w_in=m_attn_w_in, attn_w_out=m_attn_w_out,
             attn_sinks=m_attn_sinks, lru_w_in=m_lru_w_in, lru_conv_w=m_lru_conv_w, lru_conv_b=m_lru_conv_b,
             lru_w_a=m_lru_w_a, lru_b_a=m_lru_b_a, lru_w_x=m_lru_w_x, lru_b_x=m_lru_b_x, lru_lambda=m_lru_lambda,
             lru_w_out=m_lru_w_out)
    V = dict(norm_pre=v_norm_pre, norm_post=v_norm_post, attn_w_in=v_attn_w_in, attn_w_out=v_attn_w_out,
             attn_sinks=v_attn_sinks, lru_w_in=v_lru_w_in, lru_conv_w=v_lru_conv_w, lru_conv_b=v_lru_conv_b,
             lru_w_a=v_lru_w_a, lru_b_a=v_lru_b_a, lru_w_x=v_lru_w_x, lru_b_x=v_lru_b_x, lru_lambda=v_lru_lambda,
             lru_w_out=v_lru_w_out)

    h0 = x[0]
    target = loss_target[0]
    T, D = h0.shape
    depth = norm_pre.shape[0]
    n_attn = attn_w_in.shape[0]
    Q = attn_w_out.shape[1] * N_DEV
    KV = Q // GROUP
    LW = lru_w_out.shape[1] * N_DEV
    nh = LW // LRU_BLOCK

    big_names = [n for n, _ in BIG]
    small_names = [n for n, _ in SMALL]
    big_shapes = [W[n].shape for n in big_names]
    small_shapes = [W[n].shape for n in small_names]
    repl_shapes = [W[n].shape for n in REPL]

    flat = lambda a: a.reshape(-1)
    gat_big, gat_small = _all_gather(
        [_pack([flat(W[n]) for n in big_names], 256, BF16), _pack([flat(W[n]) for n in small_names], SUBLANES, F32)],
        name="gather_weights")
    full = {}
    for (n, ax), seg in zip(BIG, _unpack(gat_big, big_shapes)):
        full[n] = _to_full(seg, ax)
    for (n, ax), seg in zip(SMALL, _unpack(gat_small, small_shapes)):
        full[n] = _to_full(seg, ax)
    w_in_a = full["attn_w_in"]
    w_in_a = jnp.concatenate([w_in_a[..., :Q], w_in_a[..., Q + 2 * KV:], w_in_a[..., Q:Q + 2 * KV]], axis=-1)
    cw_f = full["lru_conv_w"]
    cb_f = full["lru_conv_b"][:, None, :]
    ba_f = full["lru_b_a"].reshape(-1, 1, LW)
    bx_f = full["lru_b_x"].reshape(-1, 1, LW)
    lam_f = full["lru_lambda"][:, None, :]

    h = h0
    saved = []
    for layer in range(depth):
        j = layer // 2
        u = _rms_fwd(h, norm_pre[layer:layer + 1], name="rms_fwd")
        if layer % 2 == 0:
            proj = _matmul(u, w_in_a[j], name="attn_in")
            mix, ypre = _attn_fwd(proj, attn_sinks[j], name="attn_fwd")
            y = _matmul(ypre, full["attn_w_out"][j], name="attn_out")
        else:
            proj = _matmul(u, full["lru_w_in"][j], name="lru_in")
            mix, ypre = _lru_fwd(proj, full["lru_w_a"][j], full["lru_w_x"][j], ba_f[j], bx_f[j], cw_f[j], cb_f[j],
                                 lam_f[j], name="lru_fwd")
            y = _matmul(ypre, full["lru_w_out"][j], name="lru_out")
        saved.append((h, u, proj, mix, ypre, y))
        h = _post_fwd(y, norm_post[layer:layer + 1], h, name="post_fwd")

    dh, loss_part = _loss_head(h, target, name="loss_head")

    g_pre = [None] * depth
    g_post = [None] * depth
    grads = {n: [None] * W[n].shape[0] for n in ORDER if n not in ("norm_pre", "norm_post")}
    for layer in reversed(range(depth)):
        j = layer // 2
        h_in, u, proj, mix, ypre, y = saved[layer]
        dy, g_post[layer] = _rms_bwd(y, norm_post[layer:layer + 1], dh, None, out_dtype=BF16, name="post_bwd")
        if layer % 2 == 0:
            w_out, w_in = full["attn_w_out"][j], w_in_a[j]
            dyp = _matmul(dy, w_out, tb=True, name="attn_out_dx")
            grads["attn_w_out"][j] = _matmul(ypre, dy, ta=True, name="attn_out_dw")
            dqg, dkv, dsink = _attn_bwd(proj, mix, dyp, attn_sinks[j], name="attn_bwd")
            grads["attn_sinks"][j] = dsink[:, 0]
            dproj = jnp.concatenate([dqg, dkv], axis=1)
            du = _matmul(dproj, w_in, tb=True, name="attn_in_dx")
            dw = _matmul(u, dproj, ta=True, name="attn_in_dw")
            grads["attn_w_in"][j] = jnp.concatenate([dw[:, :Q], dw[:, 2 * Q:], dw[:, Q:2 * Q]], axis=1)
        else:
            w_out, w_in = full["lru_w_out"][j], full["lru_w_in"][j]
            dyp = _matmul(dy, w_out, tb=True, name="lru_out_dx")
            grads["lru_w_out"][j] = _matmul(ypre, dy, ta=True, name="lru_out_dw")
            dxb, dgt, dwa, dwx, dba, dbx, dcw, dcb, dlam = _lru_bwd(
                proj, mix, dyp, full["lru_w_a"][j], full["lru_w_x"][j], ba_f[j], bx_f[j], cw_f[j], cb_f[j], lam_f[j],
                name="lru_bwd")
            grads["lru_w_a"][j], grads["lru_w_x"][j] = dwa, dwx
            grads["lru_b_a"][j], grads["lru_b_x"][j] = dba.reshape(nh, LRU_BLOCK), dbx.reshape(nh, LRU_BLOCK)
            grads["lru_conv_w"][j], grads["lru_conv_b"][j], grads["lru_lambda"][j] = dcw, dcb[0], dlam[0]
            dproj = jnp.concatenate([dxb, dgt], axis=1)
            du = _matmul(dproj, w_in, tb=True, name="lru_in_dx")
            grads["lru_w_in"][j] = _matmul(u, dproj, ta=True, name="lru_in_dw")
        dh, g_pre[layer] = _rms_bwd(h_in, norm_pre[layer:layer + 1], du, dh, out_dtype=F32, name="pre_bwd")

    gfull = {n: jnp.stack(g) for n, g in grads.items()}
    gfull["norm_pre"] = jnp.concatenate(g_pre, axis=0)
    gfull["norm_post"] = jnp.concatenate(g_post, axis=0)

    send_big = _pack([_to_shards(gfull[n], ax) for n, ax in BIG], 256, BF16)
    repl_part = [jnp.broadcast_to(gfull[n].reshape(1, -1), (N_DEV, gfull[n].size)) for n in REPL]
    loss_slot = jnp.broadcast_to(loss_part.reshape(1, 1), (N_DEV, 1))
    send_small = _pack([_to_shards(gfull[n], ax) for n, ax in SMALL] + repl_part + [loss_slot], SUBLANES, F32)
    recv_big, recv_small = _exchange([send_big, send_small], name="exchange_grads")

    zero1 = jnp.zeros((1,), F32)
    outs = {}
    res_big = _adamw(recv_big, *[_pack([flat(S[n]) for n in big_names], 256, F32) for S in (W, M, V)], name="adamw_big")
    res_small = _adamw(recv_small,
                       *[_pack([flat(S[n]) for n in small_names] + [flat(S[n]) for n in REPL] + [zero1], SUBLANES, F32)
                         for S in (W, M, V)], name="adamw_small")
    for kind, rb, rs in zip(("grad", "delta", "new_m", "new_v"), res_big, res_small):
        for n, a in zip(big_names, _unpack(rb, big_shapes)):
            outs[kind, n] = a
        for n, a in zip(small_names + list(REPL) + ["loss"], _unpack(rs, small_shapes + repl_shapes + [(1,)])):
            outs[kind, n] = a
    loss = outs["grad", "loss"][0]
    result = [loss, dh[None]]
    for kind in ("grad", "delta", "new_m", "new_v"):
        result += [outs[kind, n] for n in ORDER]
    return tuple(result)
```

```python
import math

import jax
import jax.numpy as jnp
from jax import lax
from jax.experimental import pallas as pl
from jax.experimental.pallas import tpu as pltpu

F32 = jnp.float32
BF16 = jnp.bfloat16

N_DEV = 8
HEAD_DIM = 64
GROUP = 8
WINDOW = 128
LRU_BLOCK = 256
CONV_W = 4
C_RG = 8.0
NORM_EPS = 1e-6
MASK_VALUE = -1e30

ADAM_LR = 0.001
ADAM_B1 = 0.9
ADAM_B2 = 0.999
ADAM_EPS = 1e-08
ADAM_WD = 0.01
ADAM_STEP = 10

ROW_BLOCK = 256
LANES = 128
SUBLANES = 8
PACK_W = 1024
VMEM_LIMIT = 48 * 1024 * 1024
MESH = pl.DeviceIdType.MESH


def _cparams(sem=None):
    return pltpu.CompilerParams(dimension_semantics=sem, vmem_limit_bytes=VMEM_LIMIT)


def _pick(n, target, quantum):
    best = None
    for t in range(quantum, min(n, target) + 1, quantum):
        if n % t == 0:
            best = t
    return n if best is None else best


def _sigmoid(x):
    return 1.0 / (1.0 + jnp.exp(-x))


def _dot(a, b):
    return lax.dot_general(a, b, (((1,), (0,)), ((), ())), preferred_element_type=F32)


def _dot_nt(a, b):
    return lax.dot_general(a, b, (((1,), (1,)), ((), ())), preferred_element_type=F32)


def _dot_tn(a, b):
    return lax.dot_general(a, b, (((0,), (0,)), ((), ())), preferred_element_type=F32)


def _matmul(a, b, *, ta=False, tb=False, out_dtype=F32, by_owner=False, name):
    if ta:
        K, M = a.shape
    else:
        M, K = a.shape
    if tb:
        N, K2 = b.shape
    else:
        K2, N = b.shape
    assert K == K2, (a.shape, b.shape, ta, tb)
    tm = _pick(M, 1024, 256)
    tn = N // N_DEV if by_owner else _pick(N, 1536, 256)
    tk = _pick(K, 512, 256)
    assert tn % LANES == 0
    nk = K // tk
    dot = {(False, False): _dot, (False, True): _dot_nt, (True, False): _dot_tn}[(ta, tb)]

    def body(a_ref, b_ref, o_ref, acc_ref):
        k = pl.program_id(2)

        @pl.when(k == 0)
        def _():
            acc_ref[...] = jnp.zeros_like(acc_ref)

        acc_ref[...] += dot(a_ref[...].astype(BF16), b_ref[...].astype(BF16))

        @pl.when(k == nk - 1)
        def _():
            o_ref[...] = acc_ref[...].astype(o_ref.dtype)

    a_spec = pl.BlockSpec((tk, tm), lambda i, j, k: (k, i)) if ta else pl.BlockSpec((tm, tk), lambda i, j, k: (i, k))
    b_spec = pl.BlockSpec((tn, tk), lambda i, j, k: (j, k)) if tb else pl.BlockSpec((tk, tn), lambda i, j, k: (k, j))
    if by_owner:
        o_spec = pl.BlockSpec((None, tm, tn), lambda i, j, k: (j, i, 0))
        o_shape = jax.ShapeDtypeStruct((N_DEV, M, tn), out_dtype)
    else:
        o_spec = pl.BlockSpec((tm, tn), lambda i, j, k: (i, j))
        o_shape = jax.ShapeDtypeStruct((M, N), out_dtype)
    return pl.pallas_call(
        body, name=name,
        grid=(M // tm, N // tn, nk),
        in_specs=[a_spec, b_spec],
        out_specs=o_spec,
        out_shape=o_shape,
        scratch_shapes=[pltpu.VMEM((tm, tn), F32)],
        compiler_params=_cparams(("parallel", "parallel", "arbitrary")),
    )(a, b)


def _rms_fwd(h, g, *, name):
    T, D = h.shape
    tm = _pick(T, ROW_BLOCK, SUBLANES)

    def body(h_ref, g_ref, u_ref):
        x = h_ref[...]
        r = lax.rsqrt(jnp.mean(x * x, axis=-1, keepdims=True) + NORM_EPS)
        u_ref[...] = ((x * r) * g_ref[...]).astype(u_ref.dtype)

    return pl.pallas_call(
        body, name=name, grid=(T // tm,),
        in_specs=[pl.BlockSpec((tm, D), lambda i: (i, 0)), pl.BlockSpec((1, D), lambda i: (0, 0))],
        out_specs=pl.BlockSpec((tm, D), lambda i: (i, 0)),
        out_shape=jax.ShapeDtypeStruct((T, D), BF16),
        compiler_params=_cparams(("parallel",)),
    )(h, g)


def _post_fwd(y, g, h, *, name):
    T, D = y.shape
    tm = _pick(T, ROW_BLOCK, SUBLANES)

    def body(y_ref, g_ref, h_ref, o_ref):
        x = y_ref[...]
        r = lax.rsqrt(jnp.mean(x * x, axis=-1, keepdims=True) + NORM_EPS)
        o_ref[...] = h_ref[...] + (x * r) * g_ref[...]

    return pl.pallas_call(
        body, name=name, grid=(T // tm,),
        in_specs=[pl.BlockSpec((tm, D), lambda i: (i, 0)), pl.BlockSpec((1, D), lambda i: (0, 0)),
                  pl.BlockSpec((tm, D), lambda i: (i, 0))],
        out_specs=pl.BlockSpec((tm, D), lambda i: (i, 0)),
        out_shape=jax.ShapeDtypeStruct((T, D), F32),
        compiler_params=_cparams(("parallel",)),
    )(y, g, h)


def _rms_bwd(x, g, dz, res, *, out_dtype, name):
    T, D = x.shape
    tm = _pick(T, ROW_BLOCK, SUBLANES)
    has_res = res is not None

    def body(*refs):
        if has_res:
            x_ref, g_ref, dz_ref, res_ref, dx_ref, dg_ref = refs
        else:
            x_ref, g_ref, dz_ref, dx_ref, dg_ref = refs
        i = pl.program_id(0)

        @pl.when(i == 0)
        def _():
            dg_ref[...] = jnp.zeros_like(dg_ref)

        xv = x_ref[...]
        dzv = dz_ref[...]
        r = lax.rsqrt(jnp.mean(xv * xv, axis=-1, keepdims=True) + NORM_EPS)
        xhat = xv * r
        dg_ref[...] += jnp.sum(dzv * xhat, axis=0, keepdims=True)
        dxh = dzv * g_ref[...]
        dx = r * (dxh - xhat * jnp.mean(dxh * xhat, axis=-1, keepdims=True))
        if has_res:
            dx = dx + res_ref[...]
        dx_ref[...] = dx.astype(dx_ref.dtype)

    row = pl.BlockSpec((tm, D), lambda i: (i, 0))
    vec = pl.BlockSpec((1, D), lambda i: (0, 0))
    ins = [x, g, dz] + ([res] if has_res else [])
    return pl.pallas_call(
        body, name=name, grid=(T // tm,),
        in_specs=[row, vec, row] + ([row] if has_res else []),
        out_specs=[row, vec],
        out_shape=[jax.ShapeDtypeStruct((T, D), out_dtype), jax.ShapeDtypeStruct((1, D), F32)],
        compiler_params=_cparams(("arbitrary",)),
    )(*ins)


def _loss_head(h, target, *, name):
    T, D = h.shape
    tm = _pick(T, ROW_BLOCK, SUBLANES)

    def body(h_ref, t_ref, dh_ref, l_ref):
        i = pl.program_id(0)

        @pl.when(i == 0)
        def _():
            l_ref[...] = jnp.zeros_like(l_ref)

        e = h_ref[...] - t_ref[...]
        dh_ref[...] = e * (1.0 / D)
        row = jnp.sum(e * e, axis=-1, keepdims=True) * (0.5 / D)
        l_ref[...] += jnp.sum(row, axis=0, keepdims=True)

    row = pl.BlockSpec((tm, D), lambda i: (i, 0))
    return pl.pallas_call(
        body, name=name, grid=(T // tm,),
        in_specs=[row, row],
        out_specs=[row, pl.BlockSpec((1, 1), lambda i: (0, 0))],
        out_shape=[jax.ShapeDtypeStruct((T, D), F32), jax.ShapeDtypeStruct((1, 1), F32)],
        compiler_params=_cparams(("arbitrary",)),
    )(h, target)


def _attn_dims(P):
    Q = P * 4 // 9
    KV = Q // GROUP
    assert 2 * Q + 2 * KV == P and KV % LANES == 0
    return Q, KV


def _attn_specs(Q, KV, nb):
    blk = WINDOW
    q_spec = pl.BlockSpec((blk, Q), lambda i: (jnp.minimum(i, nb - 1), 0))
    g_spec = pl.BlockSpec((blk, Q), lambda i: (jnp.minimum(i, nb - 1), 1))
    kvc_spec = pl.BlockSpec((blk, 2 * KV), lambda i: (jnp.minimum(i, nb - 1), Q // KV))
    kvp_spec = pl.BlockSpec((blk, 2 * KV), lambda i: (jnp.maximum(jnp.minimum(i, nb - 1) - 1, 0), Q // KV))
    return q_spec, g_spec, kvc_spec, kvp_spec


PAIRS = GROUP // 2
STACK = PAIRS * WINDOW


def _band_mask(i):
    r = lax.broadcasted_iota(jnp.int32, (STACK, 2 * WINDOW), 0) & (WINDOW - 1)
    c = lax.broadcasted_iota(jnp.int32, (STACK, 2 * WINDOW), 1)
    first_key = jnp.where(i > 0, 0, WINDOW)
    return (c > r) & (c <= r + WINDOW) & (c >= first_key)


def _group_cols(kvh):
    c0 = kvh * GROUP * HEAD_DIM
    return [slice(c0 + j * LANES, c0 + (j + 1) * LANES) for j in range(PAIRS)]


def _stack(ref, cols, scale=None):
    parts = [ref[:, cs] for cs in cols]
    if scale is not None:
        parts = [p * scale for p in parts]
    return jnp.concatenate(parts, axis=0)


def _group_sinks(sink_ref, kvh, half):
    return jnp.concatenate([jnp.full((WINDOW, 1), sink_ref[kvh * GROUP + 2 * j + half], F32) for j in range(PAIRS)], axis=0)


def _pair_halves(x128, e, lo):
    if e == 0:
        x_lo = jnp.where(lo, x128, 0.0)
        x_hi = pltpu.roll(x_lo, HEAD_DIM, 1)
    else:
        x_hi = jnp.where(lo, 0.0, x128)
        x_lo = pltpu.roll(x_hi, HEAD_DIM, 1)
    return x_lo.astype(BF16), x_hi.astype(BF16)


def _softmax_sink(s, allowed, sink):
    s = jnp.where(allowed, s, MASK_VALUE)
    m = jnp.maximum(jnp.max(s, axis=1, keepdims=True), sink)
    p = jnp.exp(s - m)
    es = jnp.exp(sink - m)
    inv = 1.0 / (jnp.sum(p, axis=1, keepdims=True) + es)
    return p * inv, es * inv


def _attn_fwd(proj, sinks, *, name):
    T, P = proj.shape
    Q, KV = _attn_dims(P)
    nb = T // WINDOW
    npairs = KV // LANES
    scale = 1.0 / math.sqrt(HEAD_DIM)

    def body(sink_ref, q_ref, g_ref, kvc_ref, kvp_ref, out_ref, yp_ref):
        i = pl.program_id(0)
        allowed = _band_mask(i)
        lo = lax.broadcasted_iota(jnp.int32, (2 * WINDOW, LANES), 1) < HEAD_DIM
        for p in range(npairs):
            ks = slice(p * LANES, (p + 1) * LANES)
            vs = slice(KV + p * LANES, KV + (p + 1) * LANES)
            k128 = jnp.concatenate([kvp_ref[:, ks], kvc_ref[:, ks]], axis=0)
            v128 = jnp.concatenate([kvp_ref[:, vs], kvc_ref[:, vs]], axis=0)
            for e in range(2):
                kvh = 2 * p + e
                khalf = _pair_halves(k128, e, lo)
                vhalf = _pair_halves(v128, e, lo)
                cols = _group_cols(kvh)
                q4 = _stack(q_ref, cols, scale).astype(BF16)
                o4 = None
                for half in range(2):
                    s = _dot_nt(q4, khalf[half])
                    pn, _ = _softmax_sink(s, allowed, _group_sinks(sink_ref, kvh, half))
                    o = _dot(pn.astype(BF16), vhalf[half])
                    o4 = o if o4 is None else o4 + o
                g4 = _stack(g_ref, cols)
                y4 = (o4 * (g4 * _sigmoid(g4))).astype(BF16)
                for j, cs in enumerate(cols):
                    out_ref[:, cs] = o4[j * WINDOW:(j + 1) * WINDOW]
                    yp_ref[:, cs] = y4[j * WINDOW:(j + 1) * WINDOW]

    q_spec, g_spec, kvc_spec, kvp_spec = _attn_specs(Q, KV, nb)
    row = pl.BlockSpec((WINDOW, Q), lambda i: (i, 0))
    return pl.pallas_call(
        body, name=name, grid=(nb,),
        in_specs=[pl.BlockSpec(memory_space=pltpu.SMEM), q_spec, g_spec, kvc_spec, kvp_spec],
        out_specs=[row, row],
        out_shape=[jax.ShapeDtypeStruct((T, Q), F32), jax.ShapeDtypeStruct((T, Q), BF16)],
        compiler_params=_cparams(("parallel",)),
    )(sinks, proj, proj, proj, proj)


def _attn_bwd(proj, out, dyp, sinks, *, name):
    T, P = proj.shape
    Q, KV = _attn_dims(P)
    nb = T // WINDOW
    npairs = KV // LANES
    H = Q // HEAD_DIM
    scale = 1.0 / math.sqrt(HEAD_DIM)

    def body(sink_ref, q_ref, g_ref, kvc_ref, kvp_ref, out_ref, dyp_ref, dqg_ref, dkv_ref, dsink_ref, carry_ref):
        i = pl.program_id(0)

        @pl.when(i == 0)
        def _():
            carry_ref[...] = jnp.zeros_like(carry_ref)
            dsink_ref[...] = jnp.zeros_like(dsink_ref)

        @pl.when(i == nb)
        def _():
            dkv_ref[...] = carry_ref[...].astype(dkv_ref.dtype)

        @pl.when(i < nb)
        def _():
            allowed = _band_mask(i)
            lo = lax.broadcasted_iota(jnp.int32, (2 * WINDOW, LANES), 1) < HEAD_DIM
            lo_q = lax.broadcasted_iota(jnp.int32, (STACK, LANES), 1) < HEAD_DIM
            for p in range(npairs):
                ks = slice(p * LANES, (p + 1) * LANES)
                vs = slice(KV + p * LANES, KV + (p + 1) * LANES)
                k128 = jnp.concatenate([kvp_ref[:, ks], kvc_ref[:, ks]], axis=0)
                v128 = jnp.concatenate([kvp_ref[:, vs], kvc_ref[:, vs]], axis=0)
                dk_e, dv_e = [], []
                for e in range(2):
                    kvh = 2 * p + e
                    khalf = _pair_halves(k128, e, lo)
                    vhalf = _pair_halves(v128, e, lo)
                    cols = _group_cols(kvh)
                    q4 = _stack(q_ref, cols, scale).astype(BF16)
                    g4 = _stack(g_ref, cols)
                    o4 = _stack(out_ref, cols)
                    dy4 = _stack(dyp_ref, cols)
                    sg = _sigmoid(g4)
                    do4 = dy4 * (g4 * sg)
                    dg4 = (dy4 * o4 * (sg * (1.0 + g4 * (1.0 - sg)))).astype(dqg_ref.dtype)
                    dod = do4 * o4
                    d_all = jnp.sum(dod, axis=1, keepdims=True)
                    d_lo = jnp.sum(jnp.where(lo_q, dod, 0.0), axis=1, keepdims=True)
                    deltas = (d_lo, d_all - d_lo)
                    do4b = do4.astype(BF16)
                    dq4 = None
                    dk_h, dv_h = [], []
                    for half in range(2):
                        s = _dot_nt(q4, khalf[half])
                        pn, psink = _softmax_sink(s, allowed, _group_sinks(sink_ref, kvh, half))
                        dp = _dot_nt(do4b, vhalf[half])
                        ds = (pn * (dp - deltas[half])).astype(BF16)
                        dq = _dot(ds, khalf[half])
                        dq4 = dq if dq4 is None else dq4 + dq
                        dk_h.append(_dot_tn(ds, q4))
                        dv_h.append(_dot_tn(pn.astype(BF16), do4b))
                        pd = psink * deltas[half]
                        for j in range(PAIRS):
                            n = kvh * GROUP + 2 * j + half
                            dsn = -jnp.sum(pd[j * WINDOW:(j + 1) * WINDOW], axis=0, keepdims=True)
                            dsink_ref[n:n + 1, :] += jnp.broadcast_to(dsn, (1, LANES))
                    dq4 = (dq4 * scale).astype(dqg_ref.dtype)
                    for j, cs in enumerate(cols):
                        dqg_ref[:, cs] = dq4[j * WINDOW:(j + 1) * WINDOW]
                        dqg_ref[:, slice(Q + cs.start, Q + cs.stop)] = dg4[j * WINDOW:(j + 1) * WINDOW]
                    acc_k = jnp.where(lo, dk_h[0], dk_h[1])
                    acc_v = jnp.where(lo, dv_h[0], dv_h[1])
                    dk_e.append(acc_k + pltpu.roll(acc_k, HEAD_DIM, 1))
                    dv_e.append(acc_v + pltpu.roll(acc_v, HEAD_DIM, 1))
                for sl, de in ((ks, dk_e), (vs, dv_e)):
                    d128 = jnp.where(lo, de[0], de[1])
                    dkv_ref[:, sl] = (carry_ref[:, sl] + d128[:WINDOW]).astype(dkv_ref.dtype)
                    carry_ref[:, sl] = d128[WINDOW:]

    q_spec, g_spec, kvc_spec, kvp_spec = _attn_specs(Q, KV, nb)
    last = lambda i: (jnp.minimum(i, nb - 1), 0)
    row = pl.BlockSpec((WINDOW, Q), last)
    return pl.pallas_call(
        body, name=name, grid=(nb + 1,),
        in_specs=[pl.BlockSpec(memory_space=pltpu.SMEM), q_spec, g_spec, kvc_spec, kvp_spec, row, row],
        out_specs=[pl.BlockSpec((WINDOW, 2 * Q), last),
                   pl.BlockSpec((WINDOW, 2 * KV), lambda i: (jnp.maximum(i - 1, 0), 0)),
                   pl.BlockSpec((H, LANES), lambda i: (0, 0))],
        out_shape=[jax.ShapeDtypeStruct((T, 2 * Q), BF16), jax.ShapeDtypeStruct((T, 2 * KV), BF16),
                   jax.ShapeDtypeStruct((H, LANES), F32)],
        scratch_shapes=[pltpu.VMEM((WINDOW, 2 * KV), F32)],
        compiler_params=_cparams(("arbitrary",)),
    )(sinks, proj, proj, proj, proj, out, dyp)


LRU_CHUNK = 256


def _shift_down(x, halo8, s):
    if s == 0:
        return x
    row8 = lax.broadcasted_iota(jnp.int32, (SUBLANES, 1), 0)
    r = pltpu.roll(x, s, 0)
    top = jnp.where(row8 < s, pltpu.roll(halo8, s, 0), r[:SUBLANES])
    return jnp.concatenate([top, r[SUBLANES:]], axis=0)


def _shift_up(x, halo8, s):
    if s == 0:
        return x
    n = x.shape[0]
    row8 = lax.broadcasted_iota(jnp.int32, (SUBLANES, 1), 0)
    r = pltpu.roll(x, n - s, 0)
    bot = jnp.where(row8 >= SUBLANES - s, pltpu.roll(halo8, SUBLANES - s, 0), r[n - SUBLANES:])
    return jnp.concatenate([r[:n - SUBLANES], bot], axis=0)


def _scan_fwd(a, b):
    n = a.shape[0]
    row = lax.broadcasted_iota(jnp.int32, (n, 1), 0)
    s = 1
    while s < n:
        keep = row >= s
        ar = jnp.where(keep, pltpu.roll(a, s, 0), 1.0)
        br = jnp.where(keep, pltpu.roll(b, s, 0), 0.0)
        b = a * br + b
        a = a * ar
        s *= 2
    return a, b


def _scan_rev(al, b):
    n = al.shape[0]
    row = lax.broadcasted_iota(jnp.int32, (n, 1), 0)
    s = 1
    while s < n:
        keep = row < n - s
        ar = jnp.where(keep, pltpu.roll(al, n - s, 0), 1.0)
        br = jnp.where(keep, pltpu.roll(b, n - s, 0), 0.0)
        b = b + al * br
        al = al * ar
        s *= 2
    return al, b


def _log1p_pos(z):
    return jnp.where(z < 0.01, z * (1.0 - z * (0.5 - z * (1.0 / 3.0))), jnp.log(1.0 + z))


def _neg_expm1(x):
    series = -x * (1.0 + x * (0.5 + x * (1.0 / 6.0 + x * (1.0 / 24.0 + x * (1.0 / 120.0)))))
    return jnp.where(x > -0.05, series, 1.0 - jnp.exp(x))


def _softplus_neg(lam):
    return jnp.maximum(-lam, 0.0) + _log1p_pos(jnp.exp(-jnp.abs(lam)))


def _lru_gates(xb, halo, wa, wx, ba, bx, cw_ref, cb, lam):
    xs = [_shift_down(xb, halo, s) for s in range(CONV_W)]
    xc = cb + xs[3] * cw_ref[0:1, :] + xs[2] * cw_ref[1:2, :] + xs[1] * cw_ref[2:3, :] + xs[0] * cw_ref[3:4, :]
    xcb = xc.astype(BF16)
    r = _sigmoid(_dot(xcb, wa) + ba)
    ig = _sigmoid(_dot(xcb, wx) + bx)
    sp = _softplus_neg(lam)
    log_a = (-C_RG * r) * sp
    a = jnp.exp(log_a)
    mult = jnp.sqrt(_neg_expm1(2.0 * log_a))
    return xs, xc, xcb, r, ig, sp, a, mult


def _lru_specs(nh, nt, tc, rev):
    tix = (lambda t: nt - 1 - t) if rev else (lambda t: t)
    per8 = tc // SUBLANES
    chunk = lambda off: pl.BlockSpec((tc, LRU_BLOCK), lambda h, t: (tix(t), h + off))
    prev8 = lambda off: pl.BlockSpec((SUBLANES, LRU_BLOCK), lambda h, t: (jnp.maximum(tix(t) * per8 - 1, 0), h + off))
    wblk = pl.BlockSpec((None, LRU_BLOCK, LRU_BLOCK), lambda h, t: (h, 0, 0))
    vec = pl.BlockSpec((1, LRU_BLOCK), lambda h, t: (0, h))
    cwb = pl.BlockSpec((CONV_W, LRU_BLOCK), lambda h, t: (0, h))
    return tix, chunk, prev8, wblk, vec, cwb


def _lru_fwd(proj, wa, wx, ba, bx, cw, cb, lam, *, name):
    T, W2 = proj.shape
    W = W2 // 2
    nh = W // LRU_BLOCK
    tc = _pick(T, LRU_CHUNK, SUBLANES)
    nt = T // tc

    def body(xb_ref, xh_ref, gt_ref, wa_ref, wx_ref, ba_ref, bx_ref, cw_ref, cb_ref, lam_ref, hs_ref, yp_ref, carry_ref):
        t = pl.program_id(1)

        @pl.when(t == 0)
        def _():
            carry_ref[...] = jnp.zeros_like(carry_ref)

        halo = jnp.where(t > 0, xh_ref[...], 0.0)
        _, xc, _, _, ig, _, a, mult = _lru_gates(xb_ref[...], halo, wa_ref[...], wx_ref[...], ba_ref[...], bx_ref[...],
                                               cw_ref, cb_ref[...], lam_ref[...])
        pa, hb = _scan_fwd(a, mult * (ig * xc))
        hs = pa * carry_ref[SUBLANES - 1:SUBLANES, :] + hb
        hs_ref[...] = hs
        carry_ref[...] = hs[tc - SUBLANES:]
        g = gt_ref[...]
        yp_ref[...] = (hs * (g * _sigmoid(g))).astype(BF16)

    _, chunk, prev8, wblk, vec, cwb = _lru_specs(nh, nt, tc, False)
    return pl.pallas_call(
        body, name=name, grid=(nh, nt),
        in_specs=[chunk(0), prev8(0), chunk(nh), wblk, wblk, vec, vec, cwb, vec, vec],
        out_specs=[chunk(0), chunk(0)],
        out_shape=[jax.ShapeDtypeStruct((T, W), F32), jax.ShapeDtypeStruct((T, W), BF16)],
        scratch_shapes=[pltpu.VMEM((SUBLANES, LRU_BLOCK), F32)],
        compiler_params=_cparams(("parallel", "arbitrary")),
    )(proj, proj, proj, wa, wx, ba, bx, cw, cb, lam)


def _lru_bwd(proj, hs, dyp, wa, wx, ba, bx, cw, cb, lam, *, name):
    T, W2 = proj.shape
    W = W2 // 2
    nh = W // LRU_BLOCK
    tc = _pick(T, LRU_CHUNK, SUBLANES)
    nt = T // tc

    def body(xb_ref, xh_ref, gt_ref, hs_ref, hh_ref, dyp_ref, wa_ref, wx_ref, ba_ref, bx_ref, cw_ref, cb_ref, lam_ref,
             dx_ref, dg_ref, dwa_ref, dwx_ref, dba_ref, dbx_ref, dcw_ref, dcb_ref, dlam_ref,
             ca_ref, cl_ref, cx_ref):
        t = pl.program_id(1)
        first = t == nt - 1

        @pl.when(t == 0)
        def _():
            for ref in (ca_ref, cl_ref, cx_ref, dwa_ref, dwx_ref, dba_ref, dbx_ref, dcw_ref, dcb_ref, dlam_ref):
                ref[...] = jnp.zeros_like(ref)

        xb = xb_ref[...]
        halo = jnp.where(first, 0.0, xh_ref[...])
        wa = wa_ref[...]
        wx = wx_ref[...]
        lam = lam_ref[...]
        xs, xc, xcb, r, ig, sp, a, mult = _lru_gates(xb, halo, wa, wx, ba_ref[...], bx_ref[...], cw_ref, cb_ref[...], lam)
        hsv = hs_ref[...]
        g = gt_ref[...]
        dy = dyp_ref[...]
        sg = _sigmoid(g)
        dg_ref[...] = (dy * hsv * (sg * (1.0 + g * (1.0 - sg)))).astype(dg_ref.dtype)
        dhs = dy * (g * sg)

        al = _shift_up(a, ca_ref[...], 1)
        pal, lb = _scan_rev(al, dhs)
        lmb = lb + pal * cl_ref[0:1, :]
        hprev = _shift_down(hsv, jnp.where(first, 0.0, hh_ref[...]), 1)
        da = lmb * hprev
        ixc = ig * xc
        dmult = lmb * ixc
        dlog_a = da * a - dmult * (a * a) / mult
        dr = dlog_a * (-C_RG * sp)
        dlam_ref[...] += jnp.sum(dlog_a * r, axis=0, keepdims=True) * (C_RG * _sigmoid(-lam))
        dpa = dr * (r * (1.0 - r))
        dpx = (lmb * mult * xc) * (ig * (1.0 - ig))
        dpab = dpa.astype(BF16)
        dpxb = dpx.astype(BF16)
        dwa_ref[...] += _dot_tn(xcb, dpab)
        dwx_ref[...] += _dot_tn(xcb, dpxb)
        dba_ref[...] += jnp.sum(dpa, axis=0, keepdims=True)
        dbx_ref[...] += jnp.sum(dpx, axis=0, keepdims=True)
        dxc = lmb * mult * ig + _dot_nt(dpab, wa) + _dot_nt(dpxb, wx)
        dcb_ref[...] += jnp.sum(dxc, axis=0, keepdims=True)
        for s in range(CONV_W):
            dcw_ref[CONV_W - 1 - s:CONV_W - s, :] += jnp.sum(dxc * xs[s], axis=0, keepdims=True)
        cxv = cx_ref[...]
        dxb = dxc * cw_ref[3:4, :]
        for s in range(1, CONV_W):
            dxb = dxb + _shift_up(dxc, cxv, s) * cw_ref[3 - s:4 - s, :]
        dx_ref[...] = dxb.astype(dx_ref.dtype)
        ca_ref[...] = a[:SUBLANES]
        cl_ref[...] = lmb[:SUBLANES]
        cx_ref[...] = dxc[:SUBLANES]

    tix, chunk, prev8, wblk, vec, cwb = _lru_specs(nh, nt, tc, True)
    hchunk = pl.BlockSpec((tc, LRU_BLOCK), lambda h, t: (tix(t), h))
    carry = pltpu.VMEM((SUBLANES, LRU_BLOCK), F32)
    return pl.pallas_call(
        body, name=name, grid=(nh, nt),
        in_specs=[chunk(0), prev8(0), chunk(nh), hchunk, prev8(0), hchunk, wblk, wblk, vec, vec, cwb, vec, vec],
        out_specs=[hchunk, hchunk, wblk, wblk, vec, vec, cwb, vec, vec],
        out_shape=[jax.ShapeDtypeStruct((T, W), BF16), jax.ShapeDtypeStruct((T, W), BF16),
                   jax.ShapeDtypeStruct((nh, LRU_BLOCK, LRU_BLOCK), F32), jax.ShapeDtypeStruct((nh, LRU_BLOCK, LRU_BLOCK), F32),
                   jax.ShapeDtypeStruct((1, W), F32), jax.ShapeDtypeStruct((1, W), F32),
                   jax.ShapeDtypeStruct((CONV_W, W), F32), jax.ShapeDtypeStruct((1, W), F32), jax.ShapeDtypeStruct((1, W), F32)],
        scratch_shapes=[carry, carry, carry],
        compiler_params=_cparams(("parallel", "arbitrary")),
    )(proj, proj, proj, hs, hs, dyp, wa, wx, ba, bx, cw, cb, lam)


def _position():
    return lax.axis_index("x"), lax.axis_index("y"), lax.axis_index("c")


def _all_gather(arrs, axes, *, name):
    n = len(arrs)

    def body(*refs):
        ins, outs = refs[:n], refs[n:2 * n]
        send_sems, recv_sems, local_sems = refs[2 * n:]
        x, y, c = _position()
        me, sibling = (x, y, c), (x, y, 1 - c)
        chips = [(1 - x, y), (x, 1 - y), (1 - x, 1 - y)]

        def slot(a, pos):
            return outs[a].at[(slice(None),) * axes[a] + (pos,)]

        def copy(a, k, block, to, src=None):
            px, py, pc = block
            rows = slot(a, 4 * px + 2 * py + pc)
            return pltpu.make_async_remote_copy(
                src_ref=rows if src is None else src, dst_ref=rows,
                send_sem=send_sems.at[a, k], recv_sem=recv_sems.at[a, k],
                device_id=to, device_id_type=MESH)

        mine, first, passed = [], [], []
        for a in range(n):
            own = pltpu.make_async_copy(ins[a], slot(a, 4 * x + 2 * y + c), local_sems.at[a])
            own.start()
            mine.append(own)
            cps = [copy(a, 0, me, sibling, src=ins[a])]
            cps += [copy(a, 1 + j, me, (*chip, c), src=ins[a]) for j, chip in enumerate(chips)]
            for cp in cps:
                cp.start()
            first += cps
        for a in range(n):
            for j, chip in enumerate(chips):
                copy(a, 1 + j, (*chip, c), me).wait_recv()
                fwd = copy(a, 4 + j, (*chip, c), sibling)
                fwd.start()
                passed.append(fwd)
        for a in range(n):
            copy(a, 0, sibling, me).wait_recv()
            for j, chip in enumerate(chips):
                copy(a, 4 + j, (*chip, 1 - c), me).wait_recv()
        for cp in first + passed:
            cp.wait_send()
        for own in mine:
            own.wait()

    any_spec = pl.BlockSpec(memory_space=pl.ANY)
    return pl.pallas_call(
        body, name=name,
        in_specs=[any_spec] * n, out_specs=[any_spec] * n,
        out_shape=[jax.ShapeDtypeStruct(a.shape[:ax] + (N_DEV,) + a.shape[ax:], a.dtype) for a, ax in zip(arrs, axes)],
        scratch_shapes=[pltpu.SemaphoreType.DMA((n, 7)), pltpu.SemaphoreType.DMA((n, 7)), pltpu.SemaphoreType.DMA((n,))],
    )(*arrs)


def _exchange(items, out_shapes, *, name):
    n = len(items)
    n_out = len(out_shapes)

    def body(*refs):
        ins, outs = refs[:n], refs[n:n + n_out]
        send_sems, recv_sems, local_sems = refs[n + n_out:]
        x, y, c = _position()
        me = 4 * x + 2 * y + c

        def src(a, pos):
            if items[a][1]:
                rows = ins[a].shape[1] // N_DEV
                return ins[a].at[:, pl.ds(pl.multiple_of(pos * rows, rows), rows)]
            return ins[a].at[pos]

        def dst(a):
            return outs[items[a][2]].at[me, items[a][3]]

        copies = []
        for a in range(n):
            own = pltpu.make_async_copy(src(a, me), dst(a), local_sems.at[a])
            own.start()
            copies.append(own)
        for k in range(1, N_DEV):
            px = x ^ ((k >> 2) & 1)
            py = y ^ ((k >> 1) & 1)
            pc = c ^ (k & 1)
            for a in range(n):
                cp = pltpu.make_async_remote_copy(
                    src_ref=src(a, 4 * px + 2 * py + pc), dst_ref=dst(a),
                    send_sem=send_sems.at[a, k - 1], recv_sem=recv_sems.at[a, k - 1],
                    device_id=(px, py, pc), device_id_type=MESH)
                cp.start()
                copies.append(cp)
        for cp in copies:
            cp.wait()

    any_spec = pl.BlockSpec(memory_space=pl.ANY)
    return pl.pallas_call(
        body, name=name,
        in_specs=[any_spec] * n, out_specs=[any_spec] * n_out,
        out_shape=out_shapes,
        scratch_shapes=[pltpu.SemaphoreType.DMA((n, 7)), pltpu.SemaphoreType.DMA((n, 7)), pltpu.SemaphoreType.DMA((n,))],
    )(*[it[0] for it in items])


def _adamw(parts, w, m, v, *, name):
    _, L, R, C = parts.shape
    tr = _pick(R, 256, 16)
    c1 = 1.0 / (1.0 - ADAM_B1 ** ADAM_STEP)
    c2 = 1.0 / (1.0 - ADAM_B2 ** ADAM_STEP)

    def body(p_ref, w_ref, m_ref, v_ref, g_ref, d_ref, nm_ref, nv_ref):
        g = p_ref[0].astype(F32)
        for s in range(1, N_DEV):
            g = g + p_ref[s].astype(F32)
        nm = ADAM_B1 * m_ref[...] + (1.0 - ADAM_B1) * g
        nv = ADAM_B2 * v_ref[...] + (1.0 - ADAM_B2) * (g * g)
        g_ref[...] = g
        nm_ref[...] = nm
        nv_ref[...] = nv
        d_ref[...] = -ADAM_LR * ((nm * c1) / (jnp.sqrt(nv * c2) + ADAM_EPS) + ADAM_WD * w_ref[...])

    blk = pl.BlockSpec((None, tr, C), lambda l, i: (l, i, 0))
    return pl.pallas_call(
        body, name=name, grid=(L, R // tr),
        in_specs=[pl.BlockSpec((N_DEV, None, tr, C), lambda l, i: (0, l, i, 0)), blk, blk, blk],
        out_specs=[blk] * 4,
        out_shape=[jax.ShapeDtypeStruct((L, R, C), F32)] * 4,
        compiler_params=_cparams(("parallel", "parallel")),
    )(parts, w, m, v)


def _pack(flat_parts, row_multiple, dtype):
    lead = flat_parts[0].shape[:-1]
    total = sum(p.shape[-1] for p in flat_parts)
    quantum = PACK_W * row_multiple
    padded = -(-total // quantum) * quantum
    parts = [p.astype(dtype) for p in flat_parts]
    if padded > total:
        parts.append(jnp.zeros(lead + (padded - total,), dtype))
    return jnp.concatenate(parts, axis=-1).reshape(lead + (padded // PACK_W, PACK_W))


def _unpack(buf, shapes):
    lead = buf.shape[:-2]
    flat = buf.reshape(lead + (-1,))
    out, off = [], 0
    for shp in shapes:
        n = math.prod(shp)
        out.append(flat[..., off:off + n].reshape(lead + tuple(shp)))
        off += n
    return out


def _to_full(seg, ax):
    shard = seg.shape[1:]
    full = shard[:ax] + (N_DEV * shard[ax],) + shard[ax + 1:]
    return jnp.moveaxis(seg, 0, ax).reshape(full)


def _to_shards(full, ax):
    shp = full.shape
    split = shp[:ax] + (N_DEV, shp[ax] // N_DEV) + shp[ax + 1:]
    return jnp.moveaxis(full.reshape(split), ax, 0).reshape(N_DEV, -1)


BIG = (("attn_w_in", 1), ("attn_w_out", 1), ("lru_w_in", 1), ("lru_w_a", 2), ("lru_w_x", 2), ("lru_w_out", 1))
SMALL = (("lru_conv_w", 2), ("lru_conv_b", 1), ("lru_b_a", 2), ("lru_b_x", 2), ("lru_lambda", 1))
REPL = ("norm_pre", "norm_post", "attn_sinks")
ORDER = ("norm_pre", "norm_post", "attn_w_in", "attn_w_out", "attn_sinks", "lru_w_in", "lru_conv_w", "lru_conv_b",
         "lru_w_a", "lru_b_a", "lru_w_x", "lru_b_x", "lru_lambda", "lru_w_out")


def kernel(x, norm_pre, norm_post, attn_w_in, attn_w_out, attn_sinks, lru_w_in, lru_conv_w, lru_conv_b, lru_w_a, lru_b_a, lru_w_x, lru_b_x, lru_lambda, lru_w_out, loss_target, m_norm_pre, m_norm_post, m_attn_w_in, m_attn_w_out, m_attn_sinks, m_lru_w_in, m_lru_conv_w, m_lru_conv_b, m_lru_w_a, m_lru_b_a, m_lru_w_x, m_lru_b_x, m_lru_lambda, m_lru_w_out, v_norm_pre, v_norm_post, v_attn_w_in, v_attn_w_out, v_attn_sinks, v_lru_w_in, v_lru_conv_w, v_lru_conv_b, v_lru_w_a, v_lru_b_a, v_lru_w_x, v_lru_b_x, v_lru_lambda, v_lru_w_out):
    W = dict(norm_pre=norm_pre, norm_post=norm_post, attn_w_in=attn_w_in, attn_w_out=attn_w_out, attn_sinks=attn_sinks,
             lru_w_in=lru_w_in, lru_conv_w=lru_conv_w, lru_conv_b=lru_conv_b, lru_w_a=lru_w_a, lru_b_a=lru_b_a,
             lru_w_x=lru_w_x, lru_b_x=lru_b_x, lru_lambda=lru_lambda, lru_w_out=lru_w_out)
    M = dict(norm_pre=m_norm_pre, norm_post=m_norm_post, attn_w_in=m_attn_w_in, attn_w_out=m_attn_w_out,
             attn_sinks=m_attn_sinks, lru_w_in=m_lru_w_in, lru_conv_w=m_lru_conv_w, lru_conv_b=m_lru_conv_b,
             lru_w_a=m_lru_w_a, lru_b_a=m_lru_b_a, lru_w_x=m_lru_w_x, lru_b_x=m_lru_b_x, lru_lambda=m_lru_lambda,
             lru_w_out=m_lru_w_out)
    V = dict(norm_pre=v_norm_pre, norm_post=v_norm_post, attn_w_in=v_attn_w_in, attn_w_out=v_attn_w_out,
             attn_sinks=v_attn_sinks, lru_w_in=v_lru_w_in, lru_conv_w=v_lru_conv_w, lru_conv_b=v_lru_conv_b,
             lru_w_a=v_lru_w_a, lru_b_a=v_lru_b_a, lru_w_x=v_lru_w_x, lru_b_x=v_lru_b_x, lru_lambda=v_lru_lambda,
             lru_w_out=v_lru_w_out)

    h0 = x[0]
    target = loss_target[0]
    T, D = h0.shape
    depth = norm_pre.shape[0]
    n_attn = attn_w_in.shape[0]
    Q = attn_w_out.shape[1] * N_DEV
    KV = Q // GROUP
    LW = lru_w_out.shape[1] * N_DEV
    nh = LW // LRU_BLOCK

    n_lru = lru_w_in.shape[0]
    big_names = [n for n, _ in BIG]
    small_names = [n for n, _ in SMALL]
    small_shapes = [W[n].shape for n in small_names]
    repl_shapes = [W[n].shape for n in REPL]

    flat = lambda a: a.reshape(-1)
    gathered = _all_gather(
        [W[n].astype(BF16) for n in big_names] + [_pack([flat(W[n]) for n in small_names], SUBLANES, F32)],
        [ax for _, ax in BIG] + [0], name="gather_weights")
    g_ain, g_aout, g_lin, g_wa, g_wx, g_lout, g_small = gathered
    full = {}
    for (n, ax), seg in zip(SMALL, _unpack(g_small, small_shapes)):
        full[n] = _to_full(seg, ax)
    w_in_a = jnp.moveaxis(g_ain, 1, 2).reshape(n_attn, D, -1)
    w_in_a = jnp.concatenate([w_in_a[..., :Q], w_in_a[..., Q + 2 * KV:], w_in_a[..., Q:Q + 2 * KV]], axis=-1)
    full["attn_w_out"] = g_aout.reshape(n_attn, Q, D)
    full["lru_w_in"] = jnp.moveaxis(g_lin, 1, 2).reshape(n_lru, D, 2 * LW)
    full["lru_w_a"] = g_wa.reshape(n_lru, nh, LRU_BLOCK, LRU_BLOCK)
    full["lru_w_x"] = g_wx.reshape(n_lru, nh, LRU_BLOCK, LRU_BLOCK)
    full["lru_w_out"] = g_lout.reshape(n_lru, LW, D)
    cw_f = full["lru_conv_w"]
    cb_f = full["lru_conv_b"][:, None, :]
    ba_f = full["lru_b_a"].reshape(-1, 1, LW)
    bx_f = full["lru_b_x"].reshape(-1, 1, LW)
    lam_f = full["lru_lambda"][:, None, :]

    h = h0
    saved = []
    for layer in range(depth):
        j = layer // 2
        u = _rms_fwd(h, norm_pre[layer:layer + 1], name="rms_fwd")
        if layer % 2 == 0:
            proj = _matmul(u, w_in_a[j], name="attn_in")
            mix, ypre = _attn_fwd(proj, attn_sinks[j], name="attn_fwd")
            y = _matmul(ypre, full["attn_w_out"][j], name="attn_out")
        else:
            proj = _matmul(u, full["lru_w_in"][j], name="lru_in")
            mix, ypre = _lru_fwd(proj, full["lru_w_a"][j], full["lru_w_x"][j], ba_f[j], bx_f[j], cw_f[j], cb_f[j],
                                 lam_f[j], name="lru_fwd")
            y = _matmul(ypre, full["lru_w_out"][j], name="lru_out")
        saved.append((h, u, proj, mix, ypre, y))
        h = _post_fwd(y, norm_post[layer:layer + 1], h, name="post_fwd")

    dh, loss_part = _loss_head(h, target, name="loss_head")

    g_pre = [None] * depth
    g_post = [None] * depth
    small_vec = [n for n in small_names] + ["attn_sinks"]
    grads = {n: [None] * W[n].shape[0] for n in small_vec}
    items = []
    for layer in reversed(range(depth)):
        j = layer // 2
        h_in, u, proj, mix, ypre, y = saved[layer]
        dy, g_post[layer] = _rms_bwd(y, norm_post[layer:layer + 1], dh, None, out_dtype=BF16, name="post_bwd")
        if layer % 2 == 0:
            w_out, w_in = full["attn_w_out"][j], w_in_a[j]
            dyp = _matmul(dy, w_out, tb=True, name="attn_out_dx")
            dw_out = _matmul(ypre, dy, ta=True, out_dtype=BF16, name="attn_out_dw")
            items.append((dw_out.reshape(N_DEV, Q // N_DEV, D), False, 1, j))
            dqg, dkv, dsink = _attn_bwd(proj, mix, dyp, attn_sinks[j], name="attn_bwd")
            grads["attn_sinks"][j] = dsink[:, 0]
            dproj = jnp.concatenate([dqg, dkv], axis=1)
            du = _matmul(dproj, w_in, tb=True, name="attn_in_dx")
            dw = _matmul(u, dproj, ta=True, out_dtype=BF16, name="attn_in_dw")
            dw = jnp.concatenate([dw[:, :Q], dw[:, 2 * Q:], dw[:, Q:2 * Q]], axis=1)
            items.append((jnp.moveaxis(dw.reshape(D, N_DEV, -1), 1, 0), False, 0, j))
        else:
            w_out, w_in = full["lru_w_out"][j], full["lru_w_in"][j]
            dyp = _matmul(dy, w_out, tb=True, name="lru_out_dx")
            dw_out = _matmul(ypre, dy, ta=True, out_dtype=BF16, name="lru_out_dw")
            items.append((dw_out.reshape(N_DEV, LW // N_DEV, D), False, 5, j))
            dxb, dgt, dwa, dwx, dba, dbx, dcw, dcb, dlam = _lru_bwd(
                proj, mix, dyp, full["lru_w_a"][j], full["lru_w_x"][j], ba_f[j], bx_f[j], cw_f[j], cb_f[j], lam_f[j],
                name="lru_bwd")
            items.append((dwa, True, 3, j))
            items.append((dwx, True, 4, j))
            grads["lru_b_a"][j], grads["lru_b_x"][j] = dba.reshape(nh, LRU_BLOCK), dbx.reshape(nh, LRU_BLOCK)
            grads["lru_conv_w"][j], grads["lru_conv_b"][j], grads["lru_lambda"][j] = dcw, dcb[0], dlam[0]
            dproj = jnp.concatenate([dxb, dgt], axis=1)
            du = _matmul(dproj, w_in, tb=True, name="lru_in_dx")
            items.append((_matmul(u, dproj, ta=True, out_dtype=BF16, by_owner=True, name="lru_in_dw"), False, 2, j))
        dh, g_pre[layer] = _rms_bwd(h_in, norm_pre[layer:layer + 1], du, dh, out_dtype=F32, name="pre_bwd")

    gfull = {n: jnp.stack(g) for n, g in grads.items()}
    gfull["norm_pre"] = jnp.concatenate(g_pre, axis=0)
    gfull["norm_post"] = jnp.concatenate(g_post, axis=0)

    repl_part = [jnp.broadcast_to(gfull[n].reshape(1, -1), (N_DEV, gfull[n].size)) for n in REPL]
    loss_slot = jnp.broadcast_to(loss_part.reshape(1, 1), (N_DEV, 1))
    send_small = _pack([_to_shards(gfull[n], ax) for n, ax in SMALL] + repl_part + [loss_slot], SUBLANES, F32)
    items.append((send_small, False, len(BIG), 0))
    recv_shapes = [jax.ShapeDtypeStruct((N_DEV,) + W[n].shape, F32 if n in ("lru_w_a", "lru_w_x") else BF16) for n in big_names]
    recv_shapes.append(jax.ShapeDtypeStruct((N_DEV, 1) + send_small.shape[1:], F32))
    recv = _exchange(items, recv_shapes, name="exchange_grads")

    zero1 = jnp.zeros((1,), F32)
    outs = {}
    for n, parts in zip(big_names, recv):
        shp = W[n].shape
        as3 = lambda a: a.reshape((shp[0], -1, shp[-1]))
        res = _adamw(parts.reshape((N_DEV, shp[0], -1, shp[-1])), as3(W[n]), as3(M[n]), as3(V[n]), name="adamw_" + n)
        for kind, a in zip(("grad", "delta", "new_m", "new_v"), res):
            outs[kind, n] = a.reshape(shp)
    res_small = _adamw(recv[-1],
                       *[_pack([flat(S[n]) for n in small_names] + [flat(S[n]) for n in REPL] + [zero1], SUBLANES, F32)[None]
                         for S in (W, M, V)], name="adamw_small")
    for kind, rs in zip(("grad", "delta", "new_m", "new_v"), res_small):
        for n, a in zip(small_names + list(REPL) + ["loss"], _unpack(rs[0], small_shapes + repl_shapes + [(1,)])):
            outs[kind, n] = a
    loss = outs["grad", "loss"][0]
    result = [loss, dh[None]]
    for kind in ("grad", "delta", "new_m", "new_v"):
        result += [outs[kind, n] for n in ORDER]
    return tuple(result)
```

```python
import math

import jax
import jax.numpy as jnp
from jax import lax
from jax.experimental import pallas as pl
from jax.experimental.pallas import tpu as pltpu

F32 = jnp.float32
BF16 = jnp.bfloat16

N_DEV = 8
HEAD_DIM = 64
GROUP = 8
WINDOW = 128
LRU_BLOCK = 256
CONV_W = 4
C_RG = 8.0
NORM_EPS = 1e-6
MASK_VALUE = -1e30

ADAM_LR = 0.001
ADAM_B1 = 0.9
ADAM_B2 = 0.999
ADAM_EPS = 1e-08
ADAM_WD = 0.01
ADAM_STEP = 10

ROW_BLOCK = 256
LANES = 128
SUBLANES = 8
PACK_W = 1024
VMEM_LIMIT = 56 * 1024 * 1024
MATMUL_VMEM = 36 * 1024 * 1024
MESH = pl.DeviceIdType.MESH


def _cparams(sem=None):
    return pltpu.CompilerParams(dimension_semantics=sem, vmem_limit_bytes=VMEM_LIMIT)


def _pick(n, target, quantum):
    best = None
    for t in range(quantum, min(n, target) + 1, quantum):
        if n % t == 0:
            best = t
    return n if best is None else best


def _sigmoid(x):
    return 1.0 / (1.0 + jnp.exp(-x))


def _dot(a, b):
    return lax.dot_general(a, b, (((1,), (0,)), ((), ())), preferred_element_type=F32)


def _dot_nt(a, b):
    return lax.dot_general(a, b, (((1,), (1,)), ((), ())), preferred_element_type=F32)


def _dot_tn(a, b):
    return lax.dot_general(a, b, (((0,), (0,)), ((), ())), preferred_element_type=F32)


def _matmul(a, b, *, ta=False, tb=False, out_dtype=F32, by_owner=False, name):
    if ta:
        K, M = a.shape
    else:
        M, K = a.shape
    if tb:
        N, K2 = b.shape
    else:
        K2, N = b.shape
    assert K == K2, (a.shape, b.shape, ta, tb)
    tm = _pick(M, 1024, 256)
    tn = N // N_DEV if by_owner else _pick(N, 1536, 256)
    assert tn % LANES == 0
    osz = jnp.dtype(out_dtype).itemsize

    def vmem_bytes(tk):
        acc = 0 if tk == K else tm * tn * 4
        return 2 * (tm * tk * a.dtype.itemsize + tk * tn * b.dtype.itemsize + tm * tn * osz) + acc

    tk = max([t for t in range(256, K + 1, 256) if K % t == 0 and vmem_bytes(t) <= MATMUL_VMEM] or [_pick(K, 512, 256)])
    nk = K // tk
    dot = {(False, False): _dot, (False, True): _dot_nt, (True, False): _dot_tn}[(ta, tb)]

    if nk == 1:
        def body(a_ref, b_ref, o_ref):
            o_ref[...] = dot(a_ref[...].astype(BF16), b_ref[...].astype(BF16)).astype(o_ref.dtype)
        scratch = []
    else:
        def body(a_ref, b_ref, o_ref, acc_ref):
            k = pl.program_id(2)

            @pl.when(k == 0)
            def _():
                acc_ref[...] = jnp.zeros_like(acc_ref)

            acc_ref[...] += dot(a_ref[...].astype(BF16), b_ref[...].astype(BF16))

            @pl.when(k == nk - 1)
            def _():
                o_ref[...] = acc_ref[...].astype(o_ref.dtype)
        scratch = [pltpu.VMEM((tm, tn), F32)]

    a_spec = pl.BlockSpec((tk, tm), lambda j, i, k: (k, i)) if ta else pl.BlockSpec((tm, tk), lambda j, i, k: (i, k))
    b_spec = pl.BlockSpec((tn, tk), lambda j, i, k: (j, k)) if tb else pl.BlockSpec((tk, tn), lambda j, i, k: (k, j))
    if by_owner:
        o_spec = pl.BlockSpec((None, tm, tn), lambda j, i, k: (j, i, 0))
        o_shape = jax.ShapeDtypeStruct((N_DEV, M, tn), out_dtype)
    else:
        o_spec = pl.BlockSpec((tm, tn), lambda j, i, k: (i, j))
        o_shape = jax.ShapeDtypeStruct((M, N), out_dtype)
    return pl.pallas_call(
        body, name=name,
        grid=(N // tn, M // tm, nk),
        in_specs=[a_spec, b_spec],
        out_specs=o_spec,
        out_shape=o_shape,
        scratch_shapes=scratch,
        compiler_params=_cparams(("parallel", "parallel", "arbitrary")),
    )(a, b)


def _rms_fwd(h, g, *, name):
    T, D = h.shape
    tm = _pick(T, ROW_BLOCK, SUBLANES)

    def body(h_ref, g_ref, u_ref):
        x = h_ref[...]
        r = lax.rsqrt(jnp.mean(x * x, axis=-1, keepdims=True) + NORM_EPS)
        u_ref[...] = ((x * r) * g_ref[...]).astype(u_ref.dtype)

    return pl.pallas_call(
        body, name=name, grid=(T // tm,),
        in_specs=[pl.BlockSpec((tm, D), lambda i: (i, 0)), pl.BlockSpec((1, D), lambda i: (0, 0))],
        out_specs=pl.BlockSpec((tm, D), lambda i: (i, 0)),
        out_shape=jax.ShapeDtypeStruct((T, D), BF16),
        compiler_params=_cparams(("parallel",)),
    )(h, g)


def _post_fwd(y, g, h, g_next, *, name):
    T, D = y.shape
    tm = _pick(T, ROW_BLOCK, SUBLANES)

    has_next = g_next is not None

    def body(*refs):
        y_ref, g_ref, h_ref = refs[:3]
        x = y_ref[...]
        r = lax.rsqrt(jnp.mean(x * x, axis=-1, keepdims=True) + NORM_EPS)
        ho = h_ref[...] + (x * r) * g_ref[...]
        if has_next:
            gn_ref, o_ref, u_ref = refs[3:]
            rn = lax.rsqrt(jnp.mean(ho * ho, axis=-1, keepdims=True) + NORM_EPS)
            u_ref[...] = ((ho * rn) * gn_ref[...]).astype(u_ref.dtype)
        else:
            o_ref, = refs[3:]
        o_ref[...] = ho

    row = pl.BlockSpec((tm, D), lambda i: (i, 0))
    vec = pl.BlockSpec((1, D), lambda i: (0, 0))
    res = pl.pallas_call(
        body, name=name, grid=(T // tm,),
        in_specs=[row, vec, row] + ([vec] if has_next else []),
        out_specs=[row] + ([row] if has_next else []),
        out_shape=[jax.ShapeDtypeStruct((T, D), F32)] + ([jax.ShapeDtypeStruct((T, D), BF16)] if has_next else []),
        compiler_params=_cparams(("parallel",)),
    )(*([y, g, h] + ([g_next] if has_next else [])))
    return (res[0], res[1]) if has_next else (res[0], None)


def _rms_bwd(x, g, dz, res, *, out_dtype, name):
    T, D = x.shape
    tm = _pick(T, ROW_BLOCK, SUBLANES)
    has_res = res is not None

    def body(*refs):
        if has_res:
            x_ref, g_ref, dz_ref, res_ref, dx_ref, dg_ref = refs
        else:
            x_ref, g_ref, dz_ref, dx_ref, dg_ref = refs
        i = pl.program_id(0)

        @pl.when(i == 0)
        def _():
            dg_ref[...] = jnp.zeros_like(dg_ref)

        xv = x_ref[...]
        dzv = dz_ref[...]
        r = lax.rsqrt(jnp.mean(xv * xv, axis=-1, keepdims=True) + NORM_EPS)
        xhat = xv * r
        dg_ref[...] += jnp.sum(dzv * xhat, axis=0, keepdims=True)
        dxh = dzv * g_ref[...]
        dx = r * (dxh - xhat * jnp.mean(dxh * xhat, axis=-1, keepdims=True))
        if has_res:
            dx = dx + res_ref[...]
        dx_ref[...] = dx.astype(dx_ref.dtype)

    row = pl.BlockSpec((tm, D), lambda i: (i, 0))
    vec = pl.BlockSpec((1, D), lambda i: (0, 0))
    ins = [x, g, dz] + ([res] if has_res else [])
    return pl.pallas_call(
        body, name=name, grid=(T // tm,),
        in_specs=[row, vec, row] + ([row] if has_res else []),
        out_specs=[row, vec],
        out_shape=[jax.ShapeDtypeStruct((T, D), out_dtype), jax.ShapeDtypeStruct((1, D), F32)],
        compiler_params=_cparams(("arbitrary",)),
    )(*ins)


def _loss_head(h, target, *, name):
    T, D = h.shape
    tm = _pick(T, ROW_BLOCK, SUBLANES)

    def body(h_ref, t_ref, dh_ref, l_ref):
        i = pl.program_id(0)

        @pl.when(i == 0)
        def _():
            l_ref[...] = jnp.zeros_like(l_ref)

        e = h_ref[...] - t_ref[...]
        dh_ref[...] = e * (1.0 / D)
        row = jnp.sum(e * e, axis=-1, keepdims=True) * (0.5 / D)
        l_ref[...] += jnp.sum(row, axis=0, keepdims=True)

    row = pl.BlockSpec((tm, D), lambda i: (i, 0))
    return pl.pallas_call(
        body, name=name, grid=(T // tm,),
        in_specs=[row, row],
        out_specs=[row, pl.BlockSpec((1, 1), lambda i: (0, 0))],
        out_shape=[jax.ShapeDtypeStruct((T, D), F32), jax.ShapeDtypeStruct((1, 1), F32)],
        compiler_params=_cparams(("arbitrary",)),
    )(h, target)


def _attn_dims(P):
    Q = P * 4 // 9
    KV = Q // GROUP
    assert 2 * Q + 2 * KV == P and KV % LANES == 0
    return Q, KV


def _attn_specs(Q, KV, nb):
    blk = WINDOW
    q_spec = pl.BlockSpec((blk, Q), lambda i: (jnp.minimum(i, nb - 1), 0))
    g_spec = pl.BlockSpec((blk, Q), lambda i: (jnp.minimum(i, nb - 1), 1))
    kvc_spec = pl.BlockSpec((blk, 2 * KV), lambda i: (jnp.minimum(i, nb - 1), Q // KV))
    kvp_spec = pl.BlockSpec((blk, 2 * KV), lambda i: (jnp.maximum(jnp.minimum(i, nb - 1) - 1, 0), Q // KV))
    return q_spec, g_spec, kvc_spec, kvp_spec


PAIRS = GROUP // 2
STACK = PAIRS * WINDOW


def _band_mask(i):
    r = lax.broadcasted_iota(jnp.int32, (STACK, 2 * WINDOW), 0) & (WINDOW - 1)
    c = lax.broadcasted_iota(jnp.int32, (STACK, 2 * WINDOW), 1)
    first_key = jnp.where(i > 0, 0, WINDOW)
    return (c > r) & (c <= r + WINDOW) & (c >= first_key)


def _group_cols(kvh):
    c0 = kvh * GROUP * HEAD_DIM
    return [slice(c0 + j * LANES, c0 + (j + 1) * LANES) for j in range(PAIRS)]


def _stack(ref, cols, scale=None):
    x = jnp.concatenate([ref[:, cs] for cs in cols], axis=0).astype(F32)
    return x if scale is None else x * scale


def _group_sinks(sink_ref, kvh, half):
    return jnp.concatenate([jnp.full((WINDOW, 1), sink_ref[kvh * GROUP + 2 * j + half], F32) for j in range(PAIRS)], axis=0)


def _pair_halves(x128, e, lo):
    if e == 0:
        x_lo = jnp.where(lo, x128, 0.0)
        x_hi = pltpu.roll(x_lo, HEAD_DIM, 1)
    else:
        x_hi = jnp.where(lo, 0.0, x128)
        x_lo = pltpu.roll(x_hi, HEAD_DIM, 1)
    return x_lo.astype(BF16), x_hi.astype(BF16)


def _softmax_sink(s, allowed, sink):
    s = jnp.where(allowed, s, MASK_VALUE)
    m = jnp.maximum(jnp.max(s, axis=1, keepdims=True), sink)
    p = jnp.exp(s - m)
    es = jnp.exp(sink - m)
    inv = 1.0 / (jnp.sum(p, axis=1, keepdims=True) + es)
    return p * inv, es * inv


def _attn_fwd(proj, sinks, *, name):
    T, P = proj.shape
    Q, KV = _attn_dims(P)
    nb = T // WINDOW
    npairs = KV // LANES
    scale = 1.0 / math.sqrt(HEAD_DIM)

    def body(sink_ref, q_ref, g_ref, kvc_ref, kvp_ref, out_ref, yp_ref):
        i = pl.program_id(0)
        allowed = _band_mask(i)
        lo = lax.broadcasted_iota(jnp.int32, (2 * WINDOW, LANES), 1) < HEAD_DIM
        for p in range(npairs):
            ks = slice(p * LANES, (p + 1) * LANES)
            vs = slice(KV + p * LANES, KV + (p + 1) * LANES)
            k128 = jnp.concatenate([kvp_ref[:, ks], kvc_ref[:, ks]], axis=0).astype(F32)
            v128 = jnp.concatenate([kvp_ref[:, vs], kvc_ref[:, vs]], axis=0).astype(F32)
            for e in range(2):
                kvh = 2 * p + e
                khalf = _pair_halves(k128, e, lo)
                vhalf = _pair_halves(v128, e, lo)
                cols = _group_cols(kvh)
                q4 = _stack(q_ref, cols, scale).astype(BF16)
                o4 = None
                for half in range(2):
                    s = _dot_nt(q4, khalf[half])
                    pn, _ = _softmax_sink(s, allowed, _group_sinks(sink_ref, kvh, half))
                    o = _dot(pn.astype(BF16), vhalf[half])
                    o4 = o if o4 is None else o4 + o
                g4 = _stack(g_ref, cols)
                y4 = (o4 * (g4 * _sigmoid(g4))).astype(BF16)
                for j, cs in enumerate(cols):
                    out_ref[:, cs] = o4[j * WINDOW:(j + 1) * WINDOW]
                    yp_ref[:, cs] = y4[j * WINDOW:(j + 1) * WINDOW]

    q_spec, g_spec, kvc_spec, kvp_spec = _attn_specs(Q, KV, nb)
    row = pl.BlockSpec((WINDOW, Q), lambda i: (i, 0))
    return pl.pallas_call(
        body, name=name, grid=(nb,),
        in_specs=[pl.BlockSpec(memory_space=pltpu.SMEM), q_spec, g_spec, kvc_spec, kvp_spec],
        out_specs=[row, row],
        out_shape=[jax.ShapeDtypeStruct((T, Q), F32), jax.ShapeDtypeStruct((T, Q), BF16)],
        compiler_params=_cparams(("parallel",)),
    )(sinks, proj, proj, proj, proj)


def _attn_bwd(proj, out, dyp, sinks, *, name):
    T, P = proj.shape
    Q, KV = _attn_dims(P)
    nb = T // WINDOW
    npairs = KV // LANES
    H = Q // HEAD_DIM
    scale = 1.0 / math.sqrt(HEAD_DIM)

    def body(sink_ref, q_ref, g_ref, kvc_ref, kvp_ref, out_ref, dyp_ref, dqg_ref, dkv_ref, dsink_ref, carry_ref):
        i = pl.program_id(0)

        @pl.when(i == 0)
        def _():
            carry_ref[...] = jnp.zeros_like(carry_ref)
            dsink_ref[...] = jnp.zeros_like(dsink_ref)

        @pl.when(i == nb)
        def _():
            dkv_ref[...] = carry_ref[...].astype(dkv_ref.dtype)

        @pl.when(i < nb)
        def _():
            allowed = _band_mask(i)
            lo = lax.broadcasted_iota(jnp.int32, (2 * WINDOW, LANES), 1) < HEAD_DIM
            lo_q = lax.broadcasted_iota(jnp.int32, (STACK, LANES), 1) < HEAD_DIM
            for p in range(npairs):
                ks = slice(p * LANES, (p + 1) * LANES)
                vs = slice(KV + p * LANES, KV + (p + 1) * LANES)
                k128 = jnp.concatenate([kvp_ref[:, ks], kvc_ref[:, ks]], axis=0).astype(F32)
                v128 = jnp.concatenate([kvp_ref[:, vs], kvc_ref[:, vs]], axis=0).astype(F32)
                dk_e, dv_e = [], []
                for e in range(2):
                    kvh = 2 * p + e
                    khalf = _pair_halves(k128, e, lo)
                    vhalf = _pair_halves(v128, e, lo)
                    cols = _group_cols(kvh)
                    q4 = _stack(q_ref, cols, scale).astype(BF16)
                    g4 = _stack(g_ref, cols)
                    o4 = _stack(out_ref, cols)
                    dy4 = _stack(dyp_ref, cols)
                    sg = _sigmoid(g4)
                    do4 = dy4 * (g4 * sg)
                    dg4 = (dy4 * o4 * (sg * (1.0 + g4 * (1.0 - sg)))).astype(dqg_ref.dtype)
                    dod = do4 * o4
                    d_all = jnp.sum(dod, axis=1, keepdims=True)
                    d_lo = jnp.sum(jnp.where(lo_q, dod, 0.0), axis=1, keepdims=True)
                    deltas = (d_lo, d_all - d_lo)
                    do4b = do4.astype(BF16)
                    dq4 = None
                    dk_h, dv_h = [], []
                    for half in range(2):
                        s = _dot_nt(q4, khalf[half])
                        pn, psink = _softmax_sink(s, allowed, _group_sinks(sink_ref, kvh, half))
                        dp = _dot_nt(do4b, vhalf[half])
                        ds = (pn * (dp - deltas[half])).astype(BF16)
                        dq = _dot(ds, khalf[half])
                        dq4 = dq if dq4 is None else dq4 + dq
                        dk_h.append(_dot_tn(ds, q4))
                        dv_h.append(_dot_tn(pn.astype(BF16), do4b))
                        pd = psink * deltas[half]
                        for j in range(PAIRS):
                            n = kvh * GROUP + 2 * j + half
                            dsn = -jnp.sum(pd[j * WINDOW:(j + 1) * WINDOW], axis=0, keepdims=True)
                            dsink_ref[n:n + 1, :] += jnp.broadcast_to(dsn, (1, LANES))
                    dq4 = (dq4 * scale).astype(dqg_ref.dtype)
                    for j, cs in enumerate(cols):
                        dqg_ref[:, cs] = dq4[j * WINDOW:(j + 1) * WINDOW]
                        dqg_ref[:, slice(Q + cs.start, Q + cs.stop)] = dg4[j * WINDOW:(j + 1) * WINDOW]
                    acc_k = jnp.where(lo, dk_h[0], dk_h[1])
                    acc_v = jnp.where(lo, dv_h[0], dv_h[1])
                    dk_e.append(acc_k + pltpu.roll(acc_k, HEAD_DIM, 1))
                    dv_e.append(acc_v + pltpu.roll(acc_v, HEAD_DIM, 1))
                for sl, de in ((ks, dk_e), (vs, dv_e)):
                    d128 = jnp.where(lo, de[0], de[1])
                    dkv_ref[:, sl] = (carry_ref[:, sl] + d128[:WINDOW]).astype(dkv_ref.dtype)
                    carry_ref[:, sl] = d128[WINDOW:]

    q_spec, g_spec, kvc_spec, kvp_spec = _attn_specs(Q, KV, nb)
    last = lambda i: (jnp.minimum(i, nb - 1), 0)
    row = pl.BlockSpec((WINDOW, Q), last)
    return pl.pallas_call(
        body, name=name, grid=(nb + 1,),
        in_specs=[pl.BlockSpec(memory_space=pltpu.SMEM), q_spec, g_spec, kvc_spec, kvp_spec, row, row],
        out_specs=[pl.BlockSpec((WINDOW, 2 * Q), last),
                   pl.BlockSpec((WINDOW, 2 * KV), lambda i: (jnp.maximum(i - 1, 0), 0)),
                   pl.BlockSpec((H, LANES), lambda i: (0, 0))],
        out_shape=[jax.ShapeDtypeStruct((T, 2 * Q), BF16), jax.ShapeDtypeStruct((T, 2 * KV), BF16),
                   jax.ShapeDtypeStruct((H, LANES), F32)],
        scratch_shapes=[pltpu.VMEM((WINDOW, 2 * KV), F32)],
        compiler_params=_cparams(("arbitrary",)),
    )(sinks, proj, proj, proj, proj, out, dyp)


LRU_CHUNK = 256


def _shift_down(x, halo8, s):
    if s == 0:
        return x
    row8 = lax.broadcasted_iota(jnp.int32, (SUBLANES, 1), 0)
    r = pltpu.roll(x, s, 0)
    top = jnp.where(row8 < s, pltpu.roll(halo8, s, 0), r[:SUBLANES])
    return jnp.concatenate([top, r[SUBLANES:]], axis=0)


def _shift_up(x, halo8, s):
    if s == 0:
        return x
    n = x.shape[0]
    row8 = lax.broadcasted_iota(jnp.int32, (SUBLANES, 1), 0)
    r = pltpu.roll(x, n - s, 0)
    bot = jnp.where(row8 >= SUBLANES - s, pltpu.roll(halo8, SUBLANES - s, 0), r[n - SUBLANES:])
    return jnp.concatenate([r[:n - SUBLANES], bot], axis=0)


def _scan_fwd(a, b, c0):
    n = a.shape[0]
    row = lax.broadcasted_iota(jnp.int32, (n, 1), 0) & (SUBLANES - 1)
    s = 1
    while s < SUBLANES:
        keep = row >= s
        ar = jnp.where(keep, pltpu.roll(a, s, 0), 1.0)
        br = jnp.where(keep, pltpu.roll(b, s, 0), 0.0)
        b = a * br + b
        a = a * ar
        s *= 2
    out, c = [], c0
    for i in range(n // SUBLANES):
        rows = slice(i * SUBLANES, (i + 1) * SUBLANES)
        h = a[rows] * c + b[rows]
        out.append(h)
        c = h[SUBLANES - 1:]
    return jnp.concatenate(out, axis=0)


def _scan_rev(al, b, c0):
    n = al.shape[0]
    row = lax.broadcasted_iota(jnp.int32, (n, 1), 0) & (SUBLANES - 1)
    s = 1
    while s < SUBLANES:
        keep = row < SUBLANES - s
        ar = jnp.where(keep, pltpu.roll(al, n - s, 0), 1.0)
        br = jnp.where(keep, pltpu.roll(b, n - s, 0), 0.0)
        b = b + al * br
        al = al * ar
        s *= 2
    out, c = [], c0
    for i in reversed(range(n // SUBLANES)):
        rows = slice(i * SUBLANES, (i + 1) * SUBLANES)
        l = b[rows] + al[rows] * c
        out.append(l)
        c = l[:1]
    return jnp.concatenate(out[::-1], axis=0)


def _log1p_pos(z):
    return jnp.where(z < 0.01, z * (1.0 - z * (0.5 - z * (1.0 / 3.0))), jnp.log(1.0 + z))


def _neg_expm1(x):
    series = -x * (1.0 + x * (0.5 + x * (1.0 / 6.0 + x * (1.0 / 24.0 + x * (1.0 / 120.0)))))
    return jnp.where(x > -0.05, series, 1.0 - jnp.exp(x))


def _softplus_neg(lam):
    return jnp.maximum(-lam, 0.0) + _log1p_pos(jnp.exp(-jnp.abs(lam)))


def _lru_gates(xb, halo, wa, wx, ba, bx, cw_ref, cb, lam):
    xs = [_shift_down(xb, halo, s) for s in range(CONV_W)]
    xc = cb + xs[3] * cw_ref[0:1, :] + xs[2] * cw_ref[1:2, :] + xs[1] * cw_ref[2:3, :] + xs[0] * cw_ref[3:4, :]
    xcb = xc.astype(BF16)
    r = _sigmoid(_dot(xcb, wa) + ba)
    ig = _sigmoid(_dot(xcb, wx) + bx)
    sp = _softplus_neg(lam)
    log_a = (-C_RG * r) * sp
    a = jnp.exp(log_a)
    mult = jnp.sqrt(_neg_expm1(2.0 * log_a))
    return xs, xc, xcb, r, ig, sp, a, mult


def _tile_rows(dtype):
    return SUBLANES * 4 // jnp.dtype(dtype).itemsize


def _last_rows(ref):
    return ref[...].astype(F32)[ref.shape[0] - SUBLANES:]


def _lru_specs(nh, nt, tc, rev):
    tix = (lambda t: nt - 1 - t) if rev else (lambda t: t)
    chunk = lambda off: pl.BlockSpec((tc, LRU_BLOCK), lambda h, t: (tix(t), h + off))
    prev8 = lambda off, rows: pl.BlockSpec((rows, LRU_BLOCK),
                                           lambda h, t: (jnp.maximum(tix(t) * (tc // rows) - 1, 0), h + off))
    wblk = pl.BlockSpec((None, LRU_BLOCK, LRU_BLOCK), lambda h, t: (h, 0, 0))
    vec = pl.BlockSpec((1, LRU_BLOCK), lambda h, t: (0, h))
    cwb = pl.BlockSpec((CONV_W, LRU_BLOCK), lambda h, t: (0, h))
    return tix, chunk, prev8, wblk, vec, cwb


def _lru_fwd(proj, wa, wx, ba, bx, cw, cb, lam, *, name):
    T, W2 = proj.shape
    W = W2 // 2
    nh = W // LRU_BLOCK
    tc = _pick(T, LRU_CHUNK, SUBLANES)
    nt = T // tc

    def body(xb_ref, xh_ref, gt_ref, wa_ref, wx_ref, ba_ref, bx_ref, cw_ref, cb_ref, lam_ref, hs_ref, yp_ref, carry_ref):
        t = pl.program_id(1)

        @pl.when(t == 0)
        def _():
            carry_ref[...] = jnp.zeros_like(carry_ref)

        halo = jnp.where(t > 0, _last_rows(xh_ref), 0.0)
        _, xc, _, _, ig, _, a, mult = _lru_gates(xb_ref[...].astype(F32), halo, wa_ref[...], wx_ref[...], ba_ref[...], bx_ref[...],
                                               cw_ref, cb_ref[...], lam_ref[...])
        hs = _scan_fwd(a, mult * (ig * xc), carry_ref[SUBLANES - 1:SUBLANES, :])
        hs_ref[...] = hs
        carry_ref[...] = hs[tc - SUBLANES:]
        g = gt_ref[...].astype(F32)
        yp_ref[...] = (hs * (g * _sigmoid(g))).astype(BF16)

    _, chunk, prev8, wblk, vec, cwb = _lru_specs(nh, nt, tc, False)
    return pl.pallas_call(
        body, name=name, grid=(nh, nt),
        in_specs=[chunk(0), prev8(0, _tile_rows(proj.dtype)), chunk(nh), wblk, wblk, vec, vec, cwb, vec, vec],
        out_specs=[chunk(0), chunk(0)],
        out_shape=[jax.ShapeDtypeStruct((T, W), F32), jax.ShapeDtypeStruct((T, W), BF16)],
        scratch_shapes=[pltpu.VMEM((SUBLANES, LRU_BLOCK), F32)],
        compiler_params=_cparams(("parallel", "arbitrary")),
    )(proj, proj, proj, wa, wx, ba, bx, cw, cb, lam)


def _lru_bwd(proj, hs, dyp, wa, wx, ba, bx, cw, cb, lam, *, name):
    T, W2 = proj.shape
    W = W2 // 2
    nh = W // LRU_BLOCK
    tc = _pick(T, LRU_CHUNK, SUBLANES)
    nt = T // tc

    def body(xb_ref, xh_ref, gt_ref, hs_ref, hh_ref, dyp_ref, wa_ref, wx_ref, ba_ref, bx_ref, cw_ref, cb_ref, lam_ref,
             dx_ref, dg_ref, dwa_ref, dwx_ref, dba_ref, dbx_ref, dcw_ref, dcb_ref, dlam_ref,
             ca_ref, cl_ref, cx_ref):
        t = pl.program_id(1)
        first = t == nt - 1

        @pl.when(t == 0)
        def _():
            for ref in (ca_ref, cl_ref, cx_ref, dwa_ref, dwx_ref, dba_ref, dbx_ref, dcw_ref, dcb_ref, dlam_ref):
                ref[...] = jnp.zeros_like(ref)

        xb = xb_ref[...].astype(F32)
        halo = jnp.where(first, 0.0, _last_rows(xh_ref))
        wa = wa_ref[...]
        wx = wx_ref[...]
        lam = lam_ref[...]
        xs, xc, xcb, r, ig, sp, a, mult = _lru_gates(xb, halo, wa, wx, ba_ref[...], bx_ref[...], cw_ref, cb_ref[...], lam)
        hsv = hs_ref[...]
        g = gt_ref[...].astype(F32)
        dy = dyp_ref[...].astype(F32)
        sg = _sigmoid(g)
        dg_ref[...] = (dy * hsv * (sg * (1.0 + g * (1.0 - sg)))).astype(dg_ref.dtype)
        dhs = dy * (g * sg)

        al = _shift_up(a, ca_ref[...], 1)
        lmb = _scan_rev(al, dhs, cl_ref[0:1, :])
        hprev = _shift_down(hsv, jnp.where(first, 0.0, _last_rows(hh_ref)), 1)
        da = lmb * hprev
        ixc = ig * xc
        dmult = lmb * ixc
        dlog_a = da * a - dmult * (a * a) / mult
        dr = dlog_a * (-C_RG * sp)
        dlam_ref[...] += jnp.sum(dlog_a * r, axis=0, keepdims=True) * (C_RG * _sigmoid(-lam))
        dpa = dr * (r * (1.0 - r))
        dpx = (lmb * mult * xc) * (ig * (1.0 - ig))
        dpab = dpa.astype(BF16)
        dpxb = dpx.astype(BF16)
        dwa_ref[...] += _dot_tn(xcb, dpab)
        dwx_ref[...] += _dot_tn(xcb, dpxb)
        dba_ref[...] += jnp.sum(dpa, axis=0, keepdims=True)
        dbx_ref[...] += jnp.sum(dpx, axis=0, keepdims=True)
        dxc = lmb * mult * ig + _dot_nt(dpab, wa) + _dot_nt(dpxb, wx)
        dcb_ref[...] += jnp.sum(dxc, axis=0, keepdims=True)
        for s in range(CONV_W):
            dcw_ref[CONV_W - 1 - s:CONV_W - s, :] += jnp.sum(dxc * xs[s], axis=0, keepdims=True)
        cxv = cx_ref[...]
        dxb = dxc * cw_ref[3:4, :]
        for s in range(1, CONV_W):
            dxb = dxb + _shift_up(dxc, cxv, s) * cw_ref[3 - s:4 - s, :]
        dx_ref[...] = dxb.astype(dx_ref.dtype)
        ca_ref[...] = a[:SUBLANES]
        cl_ref[...] = lmb[:SUBLANES]
        cx_ref[...] = dxc[:SUBLANES]

    tix, chunk, prev8, wblk, vec, cwb = _lru_specs(nh, nt, tc, True)
    hchunk = pl.BlockSpec((tc, LRU_BLOCK), lambda h, t: (tix(t), h))
    carry = pltpu.VMEM((SUBLANES, LRU_BLOCK), F32)
    return pl.pallas_call(
        body, name=name, grid=(nh, nt),
        in_specs=[chunk(0), prev8(0, _tile_rows(proj.dtype)), chunk(nh), hchunk, prev8(0, _tile_rows(hs.dtype)), hchunk,
                  wblk, wblk, vec, vec, cwb, vec, vec],
        out_specs=[hchunk, hchunk, wblk, wblk, vec, vec, cwb, vec, vec],
        out_shape=[jax.ShapeDtypeStruct((T, W), BF16), jax.ShapeDtypeStruct((T, W), BF16),
                   jax.ShapeDtypeStruct((nh, LRU_BLOCK, LRU_BLOCK), F32), jax.ShapeDtypeStruct((nh, LRU_BLOCK, LRU_BLOCK), F32),
                   jax.ShapeDtypeStruct((1, W), F32), jax.ShapeDtypeStruct((1, W), F32),
                   jax.ShapeDtypeStruct((CONV_W, W), F32), jax.ShapeDtypeStruct((1, W), F32), jax.ShapeDtypeStruct((1, W), F32)],
        scratch_shapes=[carry, carry, carry],
        compiler_params=_cparams(("parallel", "arbitrary")),
    )(proj, proj, proj, hs, hs, dyp, wa, wx, ba, bx, cw, cb, lam)


def _position():
    return lax.axis_index("x"), lax.axis_index("y"), lax.axis_index("c")


def _all_gather(arrs, axes, *, name):
    n = len(arrs)

    def body(*refs):
        ins, outs = refs[:n], refs[n:2 * n]
        send_sems, recv_sems, local_sems = refs[2 * n:]
        x, y, c = _position()
        me, sibling = (x, y, c), (x, y, 1 - c)
        chips = [(1 - x, y), (x, 1 - y), (1 - x, 1 - y)]

        def slot(a, pos):
            return outs[a].at[(slice(None),) * axes[a] + (pos,)]

        def copy(a, k, block, to, src=None):
            px, py, pc = block
            rows = slot(a, 4 * px + 2 * py + pc)
            return pltpu.make_async_remote_copy(
                src_ref=rows if src is None else src, dst_ref=rows,
                send_sem=send_sems.at[a, k], recv_sem=recv_sems.at[a, k],
                device_id=to, device_id_type=MESH)

        mine, first, passed = [], [], []
        for a in range(n):
            own = pltpu.make_async_copy(ins[a], slot(a, 4 * x + 2 * y + c), local_sems.at[a])
            own.start()
            mine.append(own)
            cps = [copy(a, 0, me, sibling, src=ins[a])]
            cps += [copy(a, 1 + j, me, (*chip, c), src=ins[a]) for j, chip in enumerate(chips)]
            for cp in cps:
                cp.start()
            first += cps
        for a in range(n):
            for j, chip in enumerate(chips):
                copy(a, 1 + j, (*chip, c), me).wait_recv()
                fwd = copy(a, 4 + j, (*chip, c), sibling)
                fwd.start()
                passed.append(fwd)
        for a in range(n):
            copy(a, 0, sibling, me).wait_recv()
            for j, chip in enumerate(chips):
                copy(a, 4 + j, (*chip, 1 - c), me).wait_recv()
        for cp in first + passed:
            cp.wait_send()
        for own in mine:
            own.wait()

    any_spec = pl.BlockSpec(memory_space=pl.ANY)
    return pl.pallas_call(
        body, name=name,
        in_specs=[any_spec] * n, out_specs=[any_spec] * n,
        out_shape=[jax.ShapeDtypeStruct(a.shape[:ax] + (N_DEV,) + a.shape[ax:], a.dtype) for a, ax in zip(arrs, axes)],
        scratch_shapes=[pltpu.SemaphoreType.DMA((n, 7)), pltpu.SemaphoreType.DMA((n, 7)), pltpu.SemaphoreType.DMA((n,))],
    )(*arrs)


def _exchange(items, out_shapes, *, name):
    n = len(items)
    n_out = len(out_shapes)

    def body(*refs):
        ins, outs = refs[:n], refs[n:n + n_out]
        send_sems, recv_sems, local_sems = refs[n + n_out:]
        x, y, c = _position()
        me = 4 * x + 2 * y + c

        def src(a, pos):
            if items[a][1]:
                rows = ins[a].shape[1] // N_DEV
                return ins[a].at[:, pl.ds(pl.multiple_of(pos * rows, rows), rows)]
            return ins[a].at[pos]

        def dst(a):
            return outs[items[a][2]].at[me, items[a][3]]

        copies = []
        for a in range(n):
            own = pltpu.make_async_copy(src(a, me), dst(a), local_sems.at[a])
            own.start()
            copies.append(own)
        for k in range(1, N_DEV):
            px = x ^ ((k >> 2) & 1)
            py = y ^ ((k >> 1) & 1)
            pc = c ^ (k & 1)
            for a in range(n):
                cp = pltpu.make_async_remote_copy(
                    src_ref=src(a, 4 * px + 2 * py + pc), dst_ref=dst(a),
                    send_sem=send_sems.at[a, k - 1], recv_sem=recv_sems.at[a, k - 1],
                    device_id=(px, py, pc), device_id_type=MESH)
                cp.start()
                copies.append(cp)
        for cp in copies:
            cp.wait()

    any_spec = pl.BlockSpec(memory_space=pl.ANY)
    return pl.pallas_call(
        body, name=name,
        in_specs=[any_spec] * n, out_specs=[any_spec] * n_out,
        out_shape=out_shapes,
        scratch_shapes=[pltpu.SemaphoreType.DMA((n, 7)), pltpu.SemaphoreType.DMA((n, 7)), pltpu.SemaphoreType.DMA((n,))],
    )(*[it[0] for it in items])


def _adamw(parts, w, m, v, *, name):
    _, L, R, C = parts.shape
    tr = _pick(R, 256, 16)
    c1 = 1.0 / (1.0 - ADAM_B1 ** ADAM_STEP)
    c2 = 1.0 / (1.0 - ADAM_B2 ** ADAM_STEP)

    def body(p_ref, w_ref, m_ref, v_ref, g_ref, d_ref, nm_ref, nv_ref):
        g = p_ref[0].astype(F32)
        for s in range(1, N_DEV):
            g = g + p_ref[s].astype(F32)
        nm = ADAM_B1 * m_ref[...] + (1.0 - ADAM_B1) * g
        nv = ADAM_B2 * v_ref[...] + (1.0 - ADAM_B2) * (g * g)
        g_ref[...] = g
        nm_ref[...] = nm
        nv_ref[...] = nv
        d_ref[...] = -ADAM_LR * ((nm * c1) / (jnp.sqrt(nv * c2) + ADAM_EPS) + ADAM_WD * w_ref[...])

    blk = pl.BlockSpec((None, tr, C), lambda l, i: (l, i, 0))
    return pl.pallas_call(
        body, name=name, grid=(L, R // tr),
        in_specs=[pl.BlockSpec((N_DEV, None, tr, C), lambda l, i: (0, l, i, 0)), blk, blk, blk],
        out_specs=[blk] * 4,
        out_shape=[jax.ShapeDtypeStruct((L, R, C), F32)] * 4,
        compiler_params=_cparams(("parallel", "parallel")),
    )(parts, w, m, v)


def _pack(flat_parts, row_multiple, dtype):
    lead = flat_parts[0].shape[:-1]
    total = sum(p.shape[-1] for p in flat_parts)
    quantum = PACK_W * row_multiple
    padded = -(-total // quantum) * quantum
    parts = [p.astype(dtype) for p in flat_parts]
    if padded > total:
        parts.append(jnp.zeros(lead + (padded - total,), dtype))
    return jnp.concatenate(parts, axis=-1).reshape(lead + (padded // PACK_W, PACK_W))


def _unpack(buf, shapes):
    lead = buf.shape[:-2]
    flat = buf.reshape(lead + (-1,))
    out, off = [], 0
    for shp in shapes:
        n = math.prod(shp)
        out.append(flat[..., off:off + n].reshape(lead + tuple(shp)))
        off += n
    return out


def _to_full(seg, ax):
    shard = seg.shape[1:]
    full = shard[:ax] + (N_DEV * shard[ax],) + shard[ax + 1:]
    return jnp.moveaxis(seg, 0, ax).reshape(full)


def _to_shards(full, ax):
    shp = full.shape
    split = shp[:ax] + (N_DEV, shp[ax] // N_DEV) + shp[ax + 1:]
    return jnp.moveaxis(full.reshape(split), ax, 0).reshape(N_DEV, -1)


BIG = (("attn_w_in", 1), ("attn_w_out", 1), ("lru_w_in", 1), ("lru_w_a", 2), ("lru_w_x", 2), ("lru_w_out", 1))
SMALL = (("lru_conv_w", 2), ("lru_conv_b", 1), ("lru_b_a", 2), ("lru_b_x", 2), ("lru_lambda", 1))
REPL = ("norm_pre", "norm_post", "attn_sinks")
ORDER = ("norm_pre", "norm_post", "attn_w_in", "attn_w_out", "attn_sinks", "lru_w_in", "lru_conv_w", "lru_conv_b",
         "lru_w_a", "lru_b_a", "lru_w_x", "lru_b_x", "lru_lambda", "lru_w_out")


def kernel(x, norm_pre, norm_post, attn_w_in, attn_w_out, attn_sinks, lru_w_in, lru_conv_w, lru_conv_b, lru_w_a, lru_b_a, lru_w_x, lru_b_x, lru_lambda, lru_w_out, loss_target, m_norm_pre, m_norm_post, m_attn_w_in, m_attn_w_out, m_attn_sinks, m_lru_w_in, m_lru_conv_w, m_lru_conv_b, m_lru_w_a, m_lru_b_a, m_lru_w_x, m_lru_b_x, m_lru_lambda, m_lru_w_out, v_norm_pre, v_norm_post, v_attn_w_in, v_attn_w_out, v_attn_sinks, v_lru_w_in, v_lru_conv_w, v_lru_conv_b, v_lru_w_a, v_lru_b_a, v_lru_w_x, v_lru_b_x, v_lru_lambda, v_lru_w_out):
    W = dict(norm_pre=norm_pre, norm_post=norm_post, attn_w_in=attn_w_in, attn_w_out=attn_w_out, attn_sinks=attn_sinks,
             lru_w_in=lru_w_in, lru_conv_w=lru_conv_w, lru_conv_b=lru_conv_b, lru_w_a=lru_w_a, lru_b_a=lru_b_a,
             lru_w_x=lru_w_x, lru_b_x=lru_b_x, lru_lambda=lru_lambda, lru_w_out=lru_w_out)
    M = dict(norm_pre=m_norm_pre, norm_post=m_norm_post, attn_w_in=m_attn_w_in, attn_w_out=m_attn_w_out,
             attn_sinks=m_attn_sinks, lru_w_in=m_lru_w_in, lru_conv_w=m_lru_conv_w, lru_conv_b=m_lru_conv_b,
             lru_w_a=m_lru_w_a, lru_b_a=m_lru_b_a, lru_w_x=m_lru_w_x, lru_b_x=m_lru_b_x, lru_lambda=m_lru_lambda,
             lru_w_out=m_lru_w_out)
    V = dict(norm_pre=v_norm_pre, norm_post=v_norm_post, attn_w_in=v_attn_w_in, attn_w_out=v_attn_w_out,
             attn_sinks=v_attn_sinks, lru_w_in=v_lru_w_in, lru_conv_w=v_lru_conv_w, lru_conv_b=v_lru_conv_b,
             lru_w_a=v_lru_w_a, lru_b_a=v_lru_b_a, lru_w_x=v_lru_w_x, lru_b_x=v_lru_b_x, lru_lambda=v_lru_lambda,
             lru_w_out=v_lru_w_out)

    h0 = x[0]
    target = loss_target[0]
    T, D = h0.shape
    depth = norm_pre.shape[0]
    n_attn = attn_w_in.shape[0]
    Q = attn_w_out.shape[1] * N_DEV
    KV = Q // GROUP
    LW = lru_w_out.shape[1] * N_DEV
    nh = LW // LRU_BLOCK

    n_lru = lru_w_in.shape[0]
    big_names = [n for n, _ in BIG]
    small_names = [n for n, _ in SMALL]
    small_shapes = [W[n].shape for n in small_names]
    repl_shapes = [W[n].shape for n in REPL]

    flat = lambda a: a.reshape(-1)
    gathered = _all_gather(
        [W[n].astype(BF16) for n in big_names] + [_pack([flat(W[n]) for n in small_names], SUBLANES, F32)],
        [ax for _, ax in BIG] + [0], name="gather_weights")
    g_ain, g_aout, g_lin, g_wa, g_wx, g_lout, g_small = gathered
    full = {}
    for (n, ax), seg in zip(SMALL, _unpack(g_small, small_shapes)):
        full[n] = _to_full(seg, ax)
    w_in_a = jnp.moveaxis(g_ain, 1, 2).reshape(n_attn, D, -1)
    w_in_a = jnp.concatenate([w_in_a[..., :Q], w_in_a[..., Q + 2 * KV:], w_in_a[..., Q:Q + 2 * KV]], axis=-1)
    full["attn_w_out"] = g_aout.reshape(n_attn, Q, D)
    full["lru_w_in"] = jnp.moveaxis(g_lin, 1, 2).reshape(n_lru, D, 2 * LW)
    full["lru_w_a"] = g_wa.reshape(n_lru, nh, LRU_BLOCK, LRU_BLOCK)
    full["lru_w_x"] = g_wx.reshape(n_lru, nh, LRU_BLOCK, LRU_BLOCK)
    full["lru_w_out"] = g_lout.reshape(n_lru, LW, D)
    cw_f = full["lru_conv_w"]
    cb_f = full["lru_conv_b"][:, None, :]
    ba_f = full["lru_b_a"].reshape(-1, 1, LW)
    bx_f = full["lru_b_x"].reshape(-1, 1, LW)
    lam_f = full["lru_lambda"][:, None, :]

    h = h0
    saved = []
    for layer in range(depth):
        j = layer // 2
        if layer == 0:
            u = _rms_fwd(h, norm_pre[0:1], name="rms_fwd")
        if layer % 2 == 0:
            proj = _matmul(u, w_in_a[j], out_dtype=BF16, name="attn_in")
            mix, ypre = _attn_fwd(proj, attn_sinks[j], name="attn_fwd")
            y = _matmul(ypre, full["attn_w_out"][j], name="attn_out")
        else:
            proj = _matmul(u, full["lru_w_in"][j], out_dtype=BF16, name="lru_in")
            mix, ypre = _lru_fwd(proj, full["lru_w_a"][j], full["lru_w_x"][j], ba_f[j], bx_f[j], cw_f[j], cb_f[j],
                                 lam_f[j], name="lru_fwd")
            y = _matmul(ypre, full["lru_w_out"][j], name="lru_out")
        saved.append((h, u, proj, mix, ypre, y))
        g_next = norm_pre[layer + 1:layer + 2] if layer + 1 < depth else None
        h, u = _post_fwd(y, norm_post[layer:layer + 1], h, g_next, name="post_fwd")

    dh, loss_part = _loss_head(h, target, name="loss_head")

    g_pre = [None] * depth
    g_post = [None] * depth
    small_vec = [n for n in small_names] + ["attn_sinks"]
    grads = {n: [None] * W[n].shape[0] for n in small_vec}
    items = []
    for layer in reversed(range(depth)):
        j = layer // 2
        h_in, u, proj, mix, ypre, y = saved[layer]
        dy, g_post[layer] = _rms_bwd(y, norm_post[layer:layer + 1], dh, None, out_dtype=BF16, name="post_bwd")
        if layer % 2 == 0:
            w_out, w_in = full["attn_w_out"][j], w_in_a[j]
            dyp = _matmul(dy, w_out, tb=True, out_dtype=BF16, name="attn_out_dx")
            dw_out = _matmul(ypre, dy, ta=True, out_dtype=BF16, name="attn_out_dw")
            items.append((dw_out.reshape(N_DEV, Q // N_DEV, D), False, 1, j))
            dqg, dkv, dsink = _attn_bwd(proj, mix, dyp, attn_sinks[j], name="attn_bwd")
            grads["attn_sinks"][j] = dsink[:, 0]
            dproj = jnp.concatenate([dqg, dkv], axis=1)
            du = _matmul(dproj, w_in, tb=True, name="attn_in_dx")
            dw = _matmul(u, dproj, ta=True, out_dtype=BF16, name="attn_in_dw")
            dw = jnp.concatenate([dw[:, :Q], dw[:, 2 * Q:], dw[:, Q:2 * Q]], axis=1)
            items.append((jnp.moveaxis(dw.reshape(D, N_DEV, -1), 1, 0), False, 0, j))
        else:
            w_out, w_in = full["lru_w_out"][j], full["lru_w_in"][j]
            dyp = _matmul(dy, w_out, tb=True, out_dtype=BF16, name="lru_out_dx")
            dw_out = _matmul(ypre, dy, ta=True, out_dtype=BF16, name="lru_out_dw")
            items.append((dw_out.reshape(N_DEV, LW // N_DEV, D), False, 5, j))
            dxb, dgt, dwa, dwx, dba, dbx, dcw, dcb, dlam = _lru_bwd(
                proj, mix, dyp, full["lru_w_a"][j], full["lru_w_x"][j], ba_f[j], bx_f[j], cw_f[j], cb_f[j], lam_f[j],
                name="lru_bwd")
            items.append((dwa, True, 3, j))
            items.append((dwx, True, 4, j))
            grads["lru_b_a"][j], grads["lru_b_x"][j] = dba.reshape(nh, LRU_BLOCK), dbx.reshape(nh, LRU_BLOCK)
            grads["lru_conv_w"][j], grads["lru_conv_b"][j], grads["lru_lambda"][j] = dcw, dcb[0], dlam[0]
            dproj = jnp.concatenate([dxb, dgt], axis=1)
            du = _matmul(dproj, w_in, tb=True, name="lru_in_dx")
            items.append((_matmul(u, dproj, ta=True, out_dtype=BF16, by_owner=True, name="lru_in_dw"), False, 2, j))
        dh, g_pre[layer] = _rms_bwd(h_in, norm_pre[layer:layer + 1], du, dh, out_dtype=F32, name="pre_bwd")

    gfull = {n: jnp.stack(g) for n, g in grads.items()}
    gfull["norm_pre"] = jnp.concatenate(g_pre, axis=0)
    gfull["norm_post"] = jnp.concatenate(g_post, axis=0)

    repl_part = [jnp.broadcast_to(gfull[n].reshape(1, -1), (N_DEV, gfull[n].size)) for n in REPL]
    loss_slot = jnp.broadcast_to(loss_part.reshape(1, 1), (N_DEV, 1))
    send_small = _pack([_to_shards(gfull[n], ax) for n, ax in SMALL] + repl_part + [loss_slot], SUBLANES, F32)
    items.append((send_small, False, len(BIG), 0))
    recv_shapes = [jax.ShapeDtypeStruct((N_DEV,) + W[n].shape, F32 if n in ("lru_w_a", "lru_w_x") else BF16) for n in big_names]
    recv_shapes.append(jax.ShapeDtypeStruct((N_DEV, 1) + send_small.shape[1:], F32))
    recv = _exchange(items, recv_shapes, name="exchange_grads")

    zero1 = jnp.zeros((1,), F32)
    outs = {}
    for n, parts in zip(big_names, recv):
        shp = W[n].shape
        as3 = lambda a: a.reshape((shp[0], -1, shp[-1]))
        res = _adamw(parts.reshape((N_DEV, shp[0], -1, shp[-1])), as3(W[n]), as3(M[n]), as3(V[n]), name="adamw_" + n)
        for kind, a in zip(("grad", "delta", "new_m", "new_v"), res):
            outs[kind, n] = a.reshape(shp)
    res_small = _adamw(recv[-1],
                       *[_pack([flat(S[n]) for n in small_names] + [flat(S[n]) for n in REPL] + [zero1], SUBLANES, F32)[None]
                         for S in (W, M, V)], name="adamw_small")
    for kind, rs in zip(("grad", "delta", "new_m", "new_v"), res_small):
        for n, a in zip(small_names + list(REPL) + ["loss"], _unpack(rs[0], small_shapes + repl_shapes + [(1,)])):
            outs[kind, n] = a
    loss = outs["grad", "loss"][0]
    result = [loss, dh[None]]
    for kind in ("grad", "delta", "new_m", "new_v"):
        result += [outs[kind, n] for n in ORDER]
    return tuple(result)
```

```python
import math
from typing import Callable, NamedTuple

import jax
import jax.numpy as jnp
from jax import lax
from jax.experimental import pallas as pl
from jax.experimental.pallas import tpu as pltpu

F32 = jnp.float32
BF16 = jnp.bfloat16

N_DEV = 8
HEAD_DIM = 64
GROUP = 8
WINDOW = 128
LRU_BLOCK = 256
CONV_W = 4
C_RG = 8.0
NORM_EPS = 1e-6
MASK_VALUE = -1e30

ADAM_LR = 0.001
ADAM_B1 = 0.9
ADAM_B2 = 0.999
ADAM_EPS = 1e-08
ADAM_WD = 0.01
ADAM_STEP = 10

ROW_BLOCK = 256
LANES = 128
SUBLANES = 8
PACK_W = 1024
VMEM_LIMIT = 56 * 1024 * 1024
MATMUL_VMEM = 36 * 1024 * 1024
ADAMW_VMEM = 24 * 1024 * 1024
MESH = pl.DeviceIdType.MESH


def _cparams(sem=None):
    return pltpu.CompilerParams(dimension_semantics=sem, vmem_limit_bytes=VMEM_LIMIT)


class Comm(NamedTuple):
    arrays: list
    out_shapes: list
    sems: list
    start: Callable
    mid: Callable
    finish: Callable


def _call(body, *, name, grid, in_specs, out_specs, out_shape, scratch_shapes, semantics, args, comm=None):
    if comm is None:
        res = pl.pallas_call(body, name=name, grid=grid, in_specs=in_specs, out_specs=out_specs, out_shape=out_shape,
                             scratch_shapes=scratch_shapes, compiler_params=_cparams(semantics))(*args)
        return list(res), []
    n_in, n_out, n_scr = len(in_specs), len(out_specs), len(scratch_shapes)
    ci, co = len(comm.arrays), len(comm.out_shapes)
    steps = math.prod(grid)

    def hosted(*refs):
        ins, cins = refs[:n_in], refs[n_in:n_in + ci]
        o0 = n_in + ci
        outs, couts = refs[o0:o0 + n_out], refs[o0 + n_out:o0 + n_out + co]
        s0 = o0 + n_out + co
        scr, sems = refs[s0:s0 + n_scr], refs[s0 + n_scr:]
        step = 0
        for ax, g in enumerate(grid):
            step = step * g + pl.program_id(ax)

        @pl.when(step == 0)
        def _():
            comm.start(cins, couts, *sems)

        body(*ins, *outs, *scr)

        @pl.when(step == steps // 2)
        def _():
            comm.mid(cins, couts, *sems)

        @pl.when(step == steps - 1)
        def _():
            comm.finish(cins, couts, *sems)

    any_spec = pl.BlockSpec(memory_space=pl.ANY)
    res = pl.pallas_call(
        hosted, name=name, grid=grid,
        in_specs=list(in_specs) + [any_spec] * ci, out_specs=list(out_specs) + [any_spec] * co,
        out_shape=list(out_shape) + list(comm.out_shapes), scratch_shapes=list(scratch_shapes) + list(comm.sems),
        compiler_params=_cparams(("arbitrary",) * len(grid)),
    )(*args, *comm.arrays)
    return list(res[:n_out]), list(res[n_out:])


def _pick(n, target, quantum):
    best = None
    for t in range(quantum, min(n, target) + 1, quantum):
        if n % t == 0:
            best = t
    return n if best is None else best


def _sigmoid(x):
    return 1.0 / (1.0 + jnp.exp(-x))


def _dot(a, b):
    return lax.dot_general(a, b, (((1,), (0,)), ((), ())), preferred_element_type=F32)


def _dot_nt(a, b):
    return lax.dot_general(a, b, (((1,), (1,)), ((), ())), preferred_element_type=F32)


def _dot_tn(a, b):
    return lax.dot_general(a, b, (((0,), (0,)), ((), ())), preferred_element_type=F32)


def _matmul(a, b, *, ta=False, tb=False, out_dtype=F32, by_owner=False, comm=None, name):
    if ta:
        K, M = a.shape
    else:
        M, K = a.shape
    if tb:
        N, K2 = b.shape
    else:
        K2, N = b.shape
    assert K == K2, (a.shape, b.shape, ta, tb)
    tm = _pick(M, 1024, 256)
    tn = N // N_DEV if by_owner else _pick(N, 1536, 256)
    assert tn % LANES == 0
    osz = jnp.dtype(out_dtype).itemsize

    def vmem_bytes(tk):
        acc = 0 if tk == K else tm * tn * 4
        return 2 * (tm * tk * a.dtype.itemsize + tk * tn * b.dtype.itemsize + tm * tn * osz) + acc

    tk = max([t for t in range(256, K + 1, 256) if K % t == 0 and vmem_bytes(t) <= MATMUL_VMEM] or [_pick(K, 512, 256)])
    nk = K // tk
    dot = {(False, False): _dot, (False, True): _dot_nt, (True, False): _dot_tn}[(ta, tb)]

    if nk == 1:
        def body(a_ref, b_ref, o_ref):
            o_ref[...] = dot(a_ref[...].astype(BF16), b_ref[...].astype(BF16)).astype(o_ref.dtype)
        scratch = []
    else:
        def body(a_ref, b_ref, o_ref, acc_ref):
            k = pl.program_id(2)

            @pl.when(k == 0)
            def _():
                acc_ref[...] = jnp.zeros_like(acc_ref)

            acc_ref[...] += dot(a_ref[...].astype(BF16), b_ref[...].astype(BF16))

            @pl.when(k == nk - 1)
            def _():
                o_ref[...] = acc_ref[...].astype(o_ref.dtype)
        scratch = [pltpu.VMEM((tm, tn), F32)]

    a_spec = pl.BlockSpec((tk, tm), lambda j, i, k: (k, i)) if ta else pl.BlockSpec((tm, tk), lambda j, i, k: (i, k))
    b_spec = pl.BlockSpec((tn, tk), lambda j, i, k: (j, k)) if tb else pl.BlockSpec((tk, tn), lambda j, i, k: (k, j))
    if by_owner:
        o_spec = pl.BlockSpec((None, tm, tn), lambda j, i, k: (j, i, 0))
        o_shape = jax.ShapeDtypeStruct((N_DEV, M, tn), out_dtype)
    else:
        o_spec = pl.BlockSpec((tm, tn), lambda j, i, k: (i, j))
        o_shape = jax.ShapeDtypeStruct((M, N), out_dtype)
    res, extra = _call(body, name=name, grid=(N // tn, M // tm, nk), in_specs=[a_spec, b_spec], out_specs=[o_spec],
                       out_shape=[o_shape], scratch_shapes=scratch, semantics=("parallel", "parallel", "arbitrary"),
                       args=(a, b), comm=comm)
    return res[0] if comm is None else (res[0], extra)


def _rms_fwd(h, g, *, name):
    T, D = h.shape
    tm = _pick(T, ROW_BLOCK, SUBLANES)

    def body(h_ref, g_ref, u_ref):
        x = h_ref[...]
        r = lax.rsqrt(jnp.mean(x * x, axis=-1, keepdims=True) + NORM_EPS)
        u_ref[...] = ((x * r) * g_ref[...]).astype(u_ref.dtype)

    return pl.pallas_call(
        body, name=name, grid=(T // tm,),
        in_specs=[pl.BlockSpec((tm, D), lambda i: (i, 0)), pl.BlockSpec((1, D), lambda i: (0, 0))],
        out_specs=pl.BlockSpec((tm, D), lambda i: (i, 0)),
        out_shape=jax.ShapeDtypeStruct((T, D), BF16),
        compiler_params=_cparams(("parallel",)),
    )(h, g)


def _post_fwd(y, g, h, g_next, *, name):
    T, D = y.shape
    tm = _pick(T, ROW_BLOCK, SUBLANES)

    has_next = g_next is not None

    def body(*refs):
        y_ref, g_ref, h_ref = refs[:3]
        x = y_ref[...]
        r = lax.rsqrt(jnp.mean(x * x, axis=-1, keepdims=True) + NORM_EPS)
        ho = h_ref[...] + (x * r) * g_ref[...]
        if has_next:
            gn_ref, o_ref, u_ref = refs[3:]
            rn = lax.rsqrt(jnp.mean(ho * ho, axis=-1, keepdims=True) + NORM_EPS)
            u_ref[...] = ((ho * rn) * gn_ref[...]).astype(u_ref.dtype)
        else:
            o_ref, = refs[3:]
        o_ref[...] = ho

    row = pl.BlockSpec((tm, D), lambda i: (i, 0))
    vec = pl.BlockSpec((1, D), lambda i: (0, 0))
    res = pl.pallas_call(
        body, name=name, grid=(T // tm,),
        in_specs=[row, vec, row] + ([vec] if has_next else []),
        out_specs=[row] + ([row] if has_next else []),
        out_shape=[jax.ShapeDtypeStruct((T, D), F32)] + ([jax.ShapeDtypeStruct((T, D), BF16)] if has_next else []),
        compiler_params=_cparams(("parallel",)),
    )(*([y, g, h] + ([g_next] if has_next else [])))
    return (res[0], res[1]) if has_next else (res[0], None)


def _rms_bwd(x, g, dz, res, *, out_dtype, name):
    T, D = x.shape
    tm = _pick(T, ROW_BLOCK, SUBLANES)
    has_res = res is not None

    def body(*refs):
        if has_res:
            x_ref, g_ref, dz_ref, res_ref, dx_ref, dg_ref = refs
        else:
            x_ref, g_ref, dz_ref, dx_ref, dg_ref = refs
        i = pl.program_id(0)

        @pl.when(i == 0)
        def _():
            dg_ref[...] = jnp.zeros_like(dg_ref)

        xv = x_ref[...]
        dzv = dz_ref[...]
        r = lax.rsqrt(jnp.mean(xv * xv, axis=-1, keepdims=True) + NORM_EPS)
        xhat = xv * r
        dg_ref[...] += jnp.sum(dzv * xhat, axis=0, keepdims=True)
        dxh = dzv * g_ref[...]
        dx = r * (dxh - xhat * jnp.mean(dxh * xhat, axis=-1, keepdims=True))
        if has_res:
            dx = dx + res_ref[...]
        dx_ref[...] = dx.astype(dx_ref.dtype)

    row = pl.BlockSpec((tm, D), lambda i: (i, 0))
    vec = pl.BlockSpec((1, D), lambda i: (0, 0))
    ins = [x, g, dz] + ([res] if has_res else [])
    return pl.pallas_call(
        body, name=name, grid=(T // tm,),
        in_specs=[row, vec, row] + ([row] if has_res else []),
        out_specs=[row, vec],
        out_shape=[jax.ShapeDtypeStruct((T, D), out_dtype), jax.ShapeDtypeStruct((1, D), F32)],
        compiler_params=_cparams(("arbitrary",)),
    )(*ins)


def _loss_head(h, target, *, name):
    T, D = h.shape
    tm = _pick(T, ROW_BLOCK, SUBLANES)

    def body(h_ref, t_ref, dh_ref, l_ref):
        i = pl.program_id(0)

        @pl.when(i == 0)
        def _():
            l_ref[...] = jnp.zeros_like(l_ref)

        e = h_ref[...] - t_ref[...]
        dh_ref[...] = e * (1.0 / D)
        row = jnp.sum(e * e, axis=-1, keepdims=True) * (0.5 / D)
        l_ref[...] += jnp.sum(row, axis=0, keepdims=True)

    row = pl.BlockSpec((tm, D), lambda i: (i, 0))
    return pl.pallas_call(
        body, name=name, grid=(T // tm,),
        in_specs=[row, row],
        out_specs=[row, pl.BlockSpec((1, 1), lambda i: (0, 0))],
        out_shape=[jax.ShapeDtypeStruct((T, D), F32), jax.ShapeDtypeStruct((1, 1), F32)],
        compiler_params=_cparams(("arbitrary",)),
    )(h, target)


def _attn_dims(P):
    Q = P * 4 // 9
    KV = Q // GROUP
    assert 2 * Q + 2 * KV == P and KV % LANES == 0
    return Q, KV


def _attn_specs(Q, KV, nb):
    blk = WINDOW
    q_spec = pl.BlockSpec((blk, Q), lambda i: (jnp.minimum(i, nb - 1), 0))
    g_spec = pl.BlockSpec((blk, Q), lambda i: (jnp.minimum(i, nb - 1), 1))
    kvc_spec = pl.BlockSpec((blk, 2 * KV), lambda i: (jnp.minimum(i, nb - 1), Q // KV))
    kvp_spec = pl.BlockSpec((blk, 2 * KV), lambda i: (jnp.maximum(jnp.minimum(i, nb - 1) - 1, 0), Q // KV))
    return q_spec, g_spec, kvc_spec, kvp_spec


PAIRS = GROUP // 2
STACK = PAIRS * WINDOW


def _band_mask(i):
    r = lax.broadcasted_iota(jnp.int32, (STACK, 2 * WINDOW), 0) & (WINDOW - 1)
    c = lax.broadcasted_iota(jnp.int32, (STACK, 2 * WINDOW), 1)
    first_key = jnp.where(i > 0, 0, WINDOW)
    return (c > r) & (c <= r + WINDOW) & (c >= first_key)


def _group_cols(kvh):
    c0 = kvh * GROUP * HEAD_DIM
    return [slice(c0 + j * LANES, c0 + (j + 1) * LANES) for j in range(PAIRS)]


def _stack(ref, cols, scale=None):
    x = jnp.concatenate([ref[:, cs] for cs in cols], axis=0).astype(F32)
    return x if scale is None else x * scale


def _group_sinks(sink_ref, kvh, half):
    return jnp.concatenate([jnp.full((WINDOW, 1), sink_ref[kvh * GROUP + 2 * j + half], F32) for j in range(PAIRS)], axis=0)


def _pair_halves(x128, e, lo):
    if e == 0:
        x_lo = jnp.where(lo, x128, 0.0)
        x_hi = pltpu.roll(x_lo, HEAD_DIM, 1)
    else:
        x_hi = jnp.where(lo, 0.0, x128)
        x_lo = pltpu.roll(x_hi, HEAD_DIM, 1)
    return x_lo.astype(BF16), x_hi.astype(BF16)


def _softmax_sink(s, allowed, sink):
    s = jnp.where(allowed, s, MASK_VALUE)
    m = jnp.maximum(jnp.max(s, axis=1, keepdims=True), sink)
    p = jnp.exp(s - m)
    es = jnp.exp(sink - m)
    inv = 1.0 / (jnp.sum(p, axis=1, keepdims=True) + es)
    return p * inv, es * inv


def _attn_fwd(proj, sinks, *, comm=None, name):
    T, P = proj.shape
    Q, KV = _attn_dims(P)
    nb = T // WINDOW
    npairs = KV // LANES
    scale = 1.0 / math.sqrt(HEAD_DIM)

    def body(sink_ref, q_ref, g_ref, kvc_ref, kvp_ref, out_ref, yp_ref):
        i = pl.program_id(0)
        allowed = _band_mask(i)
        lo = lax.broadcasted_iota(jnp.int32, (2 * WINDOW, LANES), 1) < HEAD_DIM
        for p in range(npairs):
            ks = slice(p * LANES, (p + 1) * LANES)
            vs = slice(KV + p * LANES, KV + (p + 1) * LANES)
            k128 = jnp.concatenate([kvp_ref[:, ks], kvc_ref[:, ks]], axis=0).astype(F32)
            v128 = jnp.concatenate([kvp_ref[:, vs], kvc_ref[:, vs]], axis=0).astype(F32)
            for e in range(2):
                kvh = 2 * p + e
                khalf = _pair_halves(k128, e, lo)
                vhalf = _pair_halves(v128, e, lo)
                cols = _group_cols(kvh)
                q4 = _stack(q_ref, cols, scale).astype(BF16)
                o4 = None
                for half in range(2):
                    s = _dot_nt(q4, khalf[half])
                    pn, _ = _softmax_sink(s, allowed, _group_sinks(sink_ref, kvh, half))
                    o = _dot(pn.astype(BF16), vhalf[half])
                    o4 = o if o4 is None else o4 + o
                g4 = _stack(g_ref, cols)
                y4 = (o4 * (g4 * _sigmoid(g4))).astype(BF16)
                for j, cs in enumerate(cols):
                    out_ref[:, cs] = o4[j * WINDOW:(j + 1) * WINDOW]
                    yp_ref[:, cs] = y4[j * WINDOW:(j + 1) * WINDOW]

    q_spec, g_spec, kvc_spec, kvp_spec = _attn_specs(Q, KV, nb)
    row = pl.BlockSpec((WINDOW, Q), lambda i: (i, 0))
    return _call(
        body, name=name, grid=(nb,),
        in_specs=[pl.BlockSpec(memory_space=pltpu.SMEM), q_spec, g_spec, kvc_spec, kvp_spec],
        out_specs=[row, row],
        out_shape=[jax.ShapeDtypeStruct((T, Q), F32), jax.ShapeDtypeStruct((T, Q), BF16)],
        scratch_shapes=[], semantics=("parallel",), args=(sinks, proj, proj, proj, proj), comm=comm)


def _attn_bwd(proj, out, dyp, sinks, *, comm=None, name):
    T, P = proj.shape
    Q, KV = _attn_dims(P)
    nb = T // WINDOW
    npairs = KV // LANES
    H = Q // HEAD_DIM
    scale = 1.0 / math.sqrt(HEAD_DIM)

    def body(sink_ref, q_ref, g_ref, kvc_ref, kvp_ref, out_ref, dyp_ref, dqg_ref, dkv_ref, dsink_ref, carry_ref):
        i = pl.program_id(0)

        @pl.when(i == 0)
        def _():
            carry_ref[...] = jnp.zeros_like(carry_ref)
            dsink_ref[...] = jnp.zeros_like(dsink_ref)

        @pl.when(i == nb)
        def _():
            dkv_ref[...] = carry_ref[...].astype(dkv_ref.dtype)

        @pl.when(i < nb)
        def _():
            allowed = _band_mask(i)
            lo = lax.broadcasted_iota(jnp.int32, (2 * WINDOW, LANES), 1) < HEAD_DIM
            lo_q = lax.broadcasted_iota(jnp.int32, (STACK, LANES), 1) < HEAD_DIM
            for p in range(npairs):
                ks = slice(p * LANES, (p + 1) * LANES)
                vs = slice(KV + p * LANES, KV + (p + 1) * LANES)
                k128 = jnp.concatenate([kvp_ref[:, ks], kvc_ref[:, ks]], axis=0).astype(F32)
                v128 = jnp.concatenate([kvp_ref[:, vs], kvc_ref[:, vs]], axis=0).astype(F32)
                dk_e, dv_e = [], []
                for e in range(2):
                    kvh = 2 * p + e
                    khalf = _pair_halves(k128, e, lo)
                    vhalf = _pair_halves(v128, e, lo)
                    cols = _group_cols(kvh)
                    q4 = _stack(q_ref, cols, scale).astype(BF16)
                    g4 = _stack(g_ref, cols)
                    o4 = _stack(out_ref, cols)
                    dy4 = _stack(dyp_ref, cols)
                    sg = _sigmoid(g4)
                    do4 = dy4 * (g4 * sg)
                    dg4 = (dy4 * o4 * (sg * (1.0 + g4 * (1.0 - sg)))).astype(dqg_ref.dtype)
                    dod = do4 * o4
                    d_all = jnp.sum(dod, axis=1, keepdims=True)
                    d_lo = jnp.sum(jnp.where(lo_q, dod, 0.0), axis=1, keepdims=True)
                    deltas = (d_lo, d_all - d_lo)
                    do4b = do4.astype(BF16)
                    dq4 = None
                    dk_h, dv_h = [], []
                    for half in range(2):
                        s = _dot_nt(q4, khalf[half])
                        pn, psink = _softmax_sink(s, allowed, _group_sinks(sink_ref, kvh, half))
                        dp = _dot_nt(do4b, vhalf[half])
                        ds = (pn * (dp - deltas[half])).astype(BF16)
                        dq = _dot(ds, khalf[half])
                        dq4 = dq if dq4 is None else dq4 + dq
                        dk_h.append(_dot_tn(ds, q4))
                        dv_h.append(_dot_tn(pn.astype(BF16), do4b))
                        pd = psink * deltas[half]
                        for j in range(PAIRS):
                            n = kvh * GROUP + 2 * j + half
                            dsn = -jnp.sum(pd[j * WINDOW:(j + 1) * WINDOW], axis=0, keepdims=True)
                            dsink_ref[n:n + 1, :] += jnp.broadcast_to(dsn, (1, LANES))
                    dq4 = (dq4 * scale).astype(dqg_ref.dtype)
                    for j, cs in enumerate(cols):
                        dqg_ref[:, cs] = dq4[j * WINDOW:(j + 1) * WINDOW]
                        dqg_ref[:, slice(Q + cs.start, Q + cs.stop)] = dg4[j * WINDOW:(j + 1) * WINDOW]
                    acc_k = jnp.where(lo, dk_h[0], dk_h[1])
                    acc_v = jnp.where(lo, dv_h[0], dv_h[1])
                    dk_e.append(acc_k + pltpu.roll(acc_k, HEAD_DIM, 1))
                    dv_e.append(acc_v + pltpu.roll(acc_v, HEAD_DIM, 1))
                for sl, de in ((ks, dk_e), (vs, dv_e)):
                    d128 = jnp.where(lo, de[0], de[1])
                    dkv_ref[:, sl] = (carry_ref[:, sl] + d128[:WINDOW]).astype(dkv_ref.dtype)
                    carry_ref[:, sl] = d128[WINDOW:]

    q_spec, g_spec, kvc_spec, kvp_spec = _attn_specs(Q, KV, nb)
    last = lambda i: (jnp.minimum(i, nb - 1), 0)
    row = pl.BlockSpec((WINDOW, Q), last)
    return _call(
        body, name=name, grid=(nb + 1,),
        in_specs=[pl.BlockSpec(memory_space=pltpu.SMEM), q_spec, g_spec, kvc_spec, kvp_spec, row, row],
        out_specs=[pl.BlockSpec((WINDOW, 2 * Q), last),
                   pl.BlockSpec((WINDOW, 2 * KV), lambda i: (jnp.maximum(i - 1, 0), 0)),
                   pl.BlockSpec((H, LANES), lambda i: (0, 0))],
        out_shape=[jax.ShapeDtypeStruct((T, 2 * Q), BF16), jax.ShapeDtypeStruct((T, 2 * KV), BF16),
                   jax.ShapeDtypeStruct((H, LANES), F32)],
        scratch_shapes=[pltpu.VMEM((WINDOW, 2 * KV), F32)],
        semantics=("arbitrary",), args=(sinks, proj, proj, proj, proj, out, dyp), comm=comm)


LRU_CHUNK = 256


def _shift_down(x, halo8, s):
    if s == 0:
        return x
    row8 = lax.broadcasted_iota(jnp.int32, (SUBLANES, 1), 0)
    r = pltpu.roll(x, s, 0)
    top = jnp.where(row8 < s, pltpu.roll(halo8, s, 0), r[:SUBLANES])
    return jnp.concatenate([top, r[SUBLANES:]], axis=0)


def _shift_up(x, halo8, s):
    if s == 0:
        return x
    n = x.shape[0]
    row8 = lax.broadcasted_iota(jnp.int32, (SUBLANES, 1), 0)
    r = pltpu.roll(x, n - s, 0)
    bot = jnp.where(row8 >= SUBLANES - s, pltpu.roll(halo8, SUBLANES - s, 0), r[n - SUBLANES:])
    return jnp.concatenate([r[:n - SUBLANES], bot], axis=0)


def _scan_fwd(a, b, c0):
    n = a.shape[0]
    row = lax.broadcasted_iota(jnp.int32, (n, 1), 0) & (SUBLANES - 1)
    s = 1
    while s < SUBLANES:
        keep = row >= s
        ar = jnp.where(keep, pltpu.roll(a, s, 0), 1.0)
        br = jnp.where(keep, pltpu.roll(b, s, 0), 0.0)
        b = a * br + b
        a = a * ar
        s *= 2
    out, c = [], c0
    for i in range(n // SUBLANES):
        rows = slice(i * SUBLANES, (i + 1) * SUBLANES)
        h = a[rows] * c + b[rows]
        out.append(h)
        c = h[SUBLANES - 1:]
    return jnp.concatenate(out, axis=0)


def _scan_rev(al, b, c0):
    n = al.shape[0]
    row = lax.broadcasted_iota(jnp.int32, (n, 1), 0) & (SUBLANES - 1)
    s = 1
    while s < SUBLANES:
        keep = row < SUBLANES - s
        ar = jnp.where(keep, pltpu.roll(al, n - s, 0), 1.0)
        br = jnp.where(keep, pltpu.roll(b, n - s, 0), 0.0)
        b = b + al * br
        al = al * ar
        s *= 2
    out, c = [], c0
    for i in reversed(range(n // SUBLANES)):
        rows = slice(i * SUBLANES, (i + 1) * SUBLANES)
        l = b[rows] + al[rows] * c
        out.append(l)
        c = l[:1]
    return jnp.concatenate(out[::-1], axis=0)


def _log1p_pos(z):
    return jnp.where(z < 0.01, z * (1.0 - z * (0.5 - z * (1.0 / 3.0))), jnp.log(1.0 + z))


def _neg_expm1(x):
    series = -x * (1.0 + x * (0.5 + x * (1.0 / 6.0 + x * (1.0 / 24.0 + x * (1.0 / 120.0)))))
    return jnp.where(x > -0.05, series, 1.0 - jnp.exp(x))


def _softplus_neg(lam):
    return jnp.maximum(-lam, 0.0) + _log1p_pos(jnp.exp(-jnp.abs(lam)))


def _lru_gates(xb, halo, wa, wx, ba, bx, cw_ref, cb, lam):
    xs = [_shift_down(xb, halo, s) for s in range(CONV_W)]
    xc = cb + xs[3] * cw_ref[0:1, :] + xs[2] * cw_ref[1:2, :] + xs[1] * cw_ref[2:3, :] + xs[0] * cw_ref[3:4, :]
    xcb = xc.astype(BF16)
    r = _sigmoid(_dot(xcb, wa) + ba)
    ig = _sigmoid(_dot(xcb, wx) + bx)
    sp = _softplus_neg(lam)
    log_a = (-C_RG * r) * sp
    a = jnp.exp(log_a)
    mult = jnp.sqrt(_neg_expm1(2.0 * log_a))
    return xs, xc, xcb, r, ig, sp, a, mult


def _tile_rows(dtype):
    return SUBLANES * 4 // jnp.dtype(dtype).itemsize


def _last_rows(ref):
    return ref[...].astype(F32)[ref.shape[0] - SUBLANES:]


def _lru_specs(nh, nt, tc, rev):
    tix = (lambda t: nt - 1 - t) if rev else (lambda t: t)
    chunk = lambda off: pl.BlockSpec((tc, LRU_BLOCK), lambda h, t: (tix(t), h + off))
    prev8 = lambda off, rows: pl.BlockSpec((rows, LRU_BLOCK),
                                           lambda h, t: (jnp.maximum(tix(t) * (tc // rows) - 1, 0), h + off))
    wblk = pl.BlockSpec((None, LRU_BLOCK, LRU_BLOCK), lambda h, t: (h, 0, 0))
    vec = pl.BlockSpec((1, LRU_BLOCK), lambda h, t: (0, h))
    cwb = pl.BlockSpec((CONV_W, LRU_BLOCK), lambda h, t: (0, h))
    return tix, chunk, prev8, wblk, vec, cwb


def _lru_fwd(proj, wa, wx, ba, bx, cw, cb, lam, *, comm=None, name):
    T, W2 = proj.shape
    W = W2 // 2
    nh = W // LRU_BLOCK
    tc = _pick(T, LRU_CHUNK, SUBLANES)
    nt = T // tc

    def body(xb_ref, xh_ref, gt_ref, wa_ref, wx_ref, ba_ref, bx_ref, cw_ref, cb_ref, lam_ref, hs_ref, yp_ref, carry_ref):
        t = pl.program_id(1)

        @pl.when(t == 0)
        def _():
            carry_ref[...] = jnp.zeros_like(carry_ref)

        halo = jnp.where(t > 0, _last_rows(xh_ref), 0.0)
        _, xc, _, _, ig, _, a, mult = _lru_gates(xb_ref[...].astype(F32), halo, wa_ref[...], wx_ref[...], ba_ref[...], bx_ref[...],
                                               cw_ref, cb_ref[...], lam_ref[...])
        hs = _scan_fwd(a, mult * (ig * xc), carry_ref[SUBLANES - 1:SUBLANES, :])
        hs_ref[...] = hs
        carry_ref[...] = hs[tc - SUBLANES:]
        g = gt_ref[...].astype(F32)
        yp_ref[...] = (hs * (g * _sigmoid(g))).astype(BF16)

    _, chunk, prev8, wblk, vec, cwb = _lru_specs(nh, nt, tc, False)
    return _call(
        body, name=name, grid=(nh, nt),
        in_specs=[chunk(0), prev8(0, _tile_rows(proj.dtype)), chunk(nh), wblk, wblk, vec, vec, cwb, vec, vec],
        out_specs=[chunk(0), chunk(0)],
        out_shape=[jax.ShapeDtypeStruct((T, W), F32), jax.ShapeDtypeStruct((T, W), BF16)],
        scratch_shapes=[pltpu.VMEM((SUBLANES, LRU_BLOCK), F32)],
        semantics=("parallel", "arbitrary"), args=(proj, proj, proj, wa, wx, ba, bx, cw, cb, lam), comm=comm)


def _lru_bwd(proj, hs, dyp, wa, wx, ba, bx, cw, cb, lam, *, comm=None, name):
    T, W2 = proj.shape
    W = W2 // 2
    nh = W // LRU_BLOCK
    tc = _pick(T, LRU_CHUNK, SUBLANES)
    nt = T // tc

    def body(xb_ref, xh_ref, gt_ref, hs_ref, hh_ref, dyp_ref, wa_ref, wx_ref, ba_ref, bx_ref, cw_ref, cb_ref, lam_ref,
             dx_ref, dg_ref, dwa_ref, dwx_ref, dba_ref, dbx_ref, dcw_ref, dcb_ref, dlam_ref,
             ca_ref, cl_ref, cx_ref):
        t = pl.program_id(1)
        first = t == nt - 1

        @pl.when(t == 0)
        def _():
            for ref in (ca_ref, cl_ref, cx_ref, dwa_ref, dwx_ref, dba_ref, dbx_ref, dcw_ref, dcb_ref, dlam_ref):
                ref[...] = jnp.zeros_like(ref)

        xb = xb_ref[...].astype(F32)
        halo = jnp.where(first, 0.0, _last_rows(xh_ref))
        wa = wa_ref[...]
        wx = wx_ref[...]
        lam = lam_ref[...]
        xs, xc, xcb, r, ig, sp, a, mult = _lru_gates(xb, halo, wa, wx, ba_ref[...], bx_ref[...], cw_ref, cb_ref[...], lam)
        hsv = hs_ref[...]
        g = gt_ref[...].astype(F32)
        dy = dyp_ref[...].astype(F32)
        sg = _sigmoid(g)
        dg_ref[...] = (dy * hsv * (sg * (1.0 + g * (1.0 - sg)))).astype(dg_ref.dtype)
        dhs = dy * (g * sg)

        al = _shift_up(a, ca_ref[...], 1)
        lmb = _scan_rev(al, dhs, cl_ref[0:1, :])
        hprev = _shift_down(hsv, jnp.where(first, 0.0, _last_rows(hh_ref)), 1)
        da = lmb * hprev
        ixc = ig * xc
        dmult = lmb * ixc
        dlog_a = da * a - dmult * (a * a) / mult
        dr = dlog_a * (-C_RG * sp)
        dlam_ref[...] += jnp.sum(dlog_a * r, axis=0, keepdims=True) * (C_RG * _sigmoid(-lam))
        dpa = dr * (r * (1.0 - r))
        dpx = (lmb * mult * xc) * (ig * (1.0 - ig))
        dpab = dpa.astype(BF16)
        dpxb = dpx.astype(BF16)
        dwa_ref[...] += _dot_tn(xcb, dpab)
        dwx_ref[...] += _dot_tn(xcb, dpxb)
        dba_ref[...] += jnp.sum(dpa, axis=0, keepdims=True)
        dbx_ref[...] += jnp.sum(dpx, axis=0, keepdims=True)
        dxc = lmb * mult * ig + _dot_nt(dpab, wa) + _dot_nt(dpxb, wx)
        dcb_ref[...] += jnp.sum(dxc, axis=0, keepdims=True)
        for s in range(CONV_W):
            dcw_ref[CONV_W - 1 - s:CONV_W - s, :] += jnp.sum(dxc * xs[s], axis=0, keepdims=True)
        cxv = cx_ref[...]
        dxb = dxc * cw_ref[3:4, :]
        for s in range(1, CONV_W):
            dxb = dxb + _shift_up(dxc, cxv, s) * cw_ref[3 - s:4 - s, :]
        dx_ref[...] = dxb.astype(dx_ref.dtype)
        ca_ref[...] = a[:SUBLANES]
        cl_ref[...] = lmb[:SUBLANES]
        cx_ref[...] = dxc[:SUBLANES]

    tix, chunk, prev8, wblk, vec, cwb = _lru_specs(nh, nt, tc, True)
    hchunk = pl.BlockSpec((tc, LRU_BLOCK), lambda h, t: (tix(t), h))
    carry = pltpu.VMEM((SUBLANES, LRU_BLOCK), F32)
    return _call(
        body, name=name, grid=(nh, nt),
        in_specs=[chunk(0), prev8(0, _tile_rows(proj.dtype)), chunk(nh), hchunk, prev8(0, _tile_rows(hs.dtype)), hchunk,
                  wblk, wblk, vec, vec, cwb, vec, vec],
        out_specs=[hchunk, hchunk, wblk, wblk, vec, vec, cwb, vec, vec],
        out_shape=[jax.ShapeDtypeStruct((T, W), BF16), jax.ShapeDtypeStruct((T, W), BF16),
                   jax.ShapeDtypeStruct((nh, LRU_BLOCK, LRU_BLOCK), F32), jax.ShapeDtypeStruct((nh, LRU_BLOCK, LRU_BLOCK), F32),
                   jax.ShapeDtypeStruct((1, W), F32), jax.ShapeDtypeStruct((1, W), F32),
                   jax.ShapeDtypeStruct((CONV_W, W), F32), jax.ShapeDtypeStruct((1, W), F32), jax.ShapeDtypeStruct((1, W), F32)],
        scratch_shapes=[carry, carry, carry], semantics=("parallel", "arbitrary"),
        args=(proj, proj, proj, hs, hs, dyp, wa, wx, ba, bx, cw, cb, lam), comm=comm)


def _position():
    return lax.axis_index("x"), lax.axis_index("y"), lax.axis_index("c")


def _sems(n):
    return [pltpu.SemaphoreType.DMA((n, 7)), pltpu.SemaphoreType.DMA((n, 7)), pltpu.SemaphoreType.DMA((n,))]


def _gather_comm(arrs, axes):
    n = len(arrs)

    def tools(ins, outs, send_sems, recv_sems, local_sems):
        x, y, c = _position()
        me, sibling = (x, y, c), (x, y, 1 - c)
        chips = [(1 - x, y), (x, 1 - y), (1 - x, 1 - y)]

        def slot(a, pos):
            return outs[a].at[(slice(None),) * axes[a] + (pos,)]

        def copy(a, k, block, to, src=None):
            px, py, pc = block
            rows = slot(a, 4 * px + 2 * py + pc)
            return pltpu.make_async_remote_copy(
                src_ref=rows if src is None else src, dst_ref=rows,
                send_sem=send_sems.at[a, k], recv_sem=recv_sems.at[a, k],
                device_id=to, device_id_type=MESH)

        own = lambda a: pltpu.make_async_copy(ins[a], slot(a, 4 * x + 2 * y + c), local_sems.at[a])
        first = lambda a: ([copy(a, 0, me, sibling, src=ins[a])]
                           + [copy(a, 1 + j, me, (*chip, c), src=ins[a]) for j, chip in enumerate(chips)])
        passed = lambda a: [copy(a, 4 + j, (*chip, c), sibling) for j, chip in enumerate(chips)]
        return me, sibling, chips, c, copy, own, first, passed

    def start(ins, outs, *sems):
        *_, own, first, _ = tools(ins, outs, *sems)
        for a in range(n):
            own(a).start()
            for cp in first(a):
                cp.start()

    def mid(ins, outs, *sems):
        me, _, chips, c, copy, _, _, passed = tools(ins, outs, *sems)
        for a in range(n):
            fwd = passed(a)
            for j, chip in enumerate(chips):
                copy(a, 1 + j, (*chip, c), me).wait_recv()
                fwd[j].start()

    def finish(ins, outs, *sems):
        me, sibling, chips, c, copy, own, first, passed = tools(ins, outs, *sems)
        for a in range(n):
            copy(a, 0, sibling, me).wait_recv()
            for j, chip in enumerate(chips):
                copy(a, 4 + j, (*chip, 1 - c), me).wait_recv()
        for a in range(n):
            for cp in first(a) + passed(a):
                cp.wait_send()
            own(a).wait()

    shapes = [jax.ShapeDtypeStruct(a.shape[:ax] + (N_DEV,) + a.shape[ax:], a.dtype) for a, ax in zip(arrs, axes)]
    return Comm(list(arrs), shapes, _sems(n), start, mid, finish)


def _comm_call(comm, *, name):
    ci, co = len(comm.arrays), len(comm.out_shapes)

    def body(*refs):
        ins, outs, sems = refs[:ci], refs[ci:ci + co], refs[ci + co:]
        comm.start(ins, outs, *sems)
        comm.mid(ins, outs, *sems)
        comm.finish(ins, outs, *sems)

    any_spec = pl.BlockSpec(memory_space=pl.ANY)
    return pl.pallas_call(body, name=name, in_specs=[any_spec] * ci, out_specs=[any_spec] * co,
                          out_shape=comm.out_shapes, scratch_shapes=comm.sems)(*comm.arrays)


def _exchange_comm(items):
    n = len(items)

    def tools(ins, outs, send_sems, recv_sems, local_sems):
        x, y, c = _position()
        me = 4 * x + 2 * y + c

        def src(a, pos):
            if items[a][1]:
                rows = ins[a].shape[1] // N_DEV
                return ins[a].at[:, pl.ds(pl.multiple_of(pos * rows, rows), rows)]
            return ins[a].at[pos]

        copies = [pltpu.make_async_copy(src(a, me), outs[a].at[me], local_sems.at[a]) for a in range(n)]
        for k in range(1, N_DEV):
            px = x ^ ((k >> 2) & 1)
            py = y ^ ((k >> 1) & 1)
            pc = c ^ (k & 1)
            copies += [pltpu.make_async_remote_copy(
                src_ref=src(a, 4 * px + 2 * py + pc), dst_ref=outs[a].at[me],
                send_sem=send_sems.at[a, k - 1], recv_sem=recv_sems.at[a, k - 1],
                device_id=(px, py, pc), device_id_type=MESH) for a in range(n)]
        return copies

    def start(ins, outs, *sems):
        for cp in tools(ins, outs, *sems):
            cp.start()

    def mid(ins, outs, *sems):
        pass

    def finish(ins, outs, *sems):
        for cp in tools(ins, outs, *sems):
            cp.wait()

    shapes = []
    for arr, split in items:
        blk = (arr.shape[0], arr.shape[1] // N_DEV) + arr.shape[2:] if split else arr.shape[1:]
        shapes.append(jax.ShapeDtypeStruct((N_DEV,) + blk, arr.dtype))
    return Comm([arr for arr, _ in items], shapes, _sems(n), start, mid, finish)


def _adamw(parts, w, m, v, *, name):
    L, R, C = w.shape
    assert len(parts) == L
    row_bytes = 2 * (L * N_DEV * C * parts[0].dtype.itemsize + 7 * C * 4)
    tr = _pick(R, max(16, ADAMW_VMEM // row_bytes), 16)
    nr = R // tr
    c1 = 1.0 / (1.0 - ADAM_B1 ** ADAM_STEP)
    c2 = 1.0 / (1.0 - ADAM_B2 ** ADAM_STEP)

    def body(*refs):
        p_refs = refs[:L]
        w_ref, m_ref, v_ref, g_ref, d_ref, nm_ref, nv_ref = refs[L:]
        layer = pl.program_id(0)
        for idx, p_ref in enumerate(p_refs):
            @pl.when(layer == idx)
            def _():
                g = p_ref[0].astype(F32)
                for s in range(1, N_DEV):
                    g = g + p_ref[s].astype(F32)
                nm = ADAM_B1 * m_ref[...] + (1.0 - ADAM_B1) * g
                nv = ADAM_B2 * v_ref[...] + (1.0 - ADAM_B2) * (g * g)
                g_ref[...] = g
                nm_ref[...] = nm
                nv_ref[...] = nv
                d_ref[...] = -ADAM_LR * ((nm * c1) / (jnp.sqrt(nv * c2) + ADAM_EPS) + ADAM_WD * w_ref[...])

    def part_spec(idx):
        return pl.BlockSpec((N_DEV, tr, C),
                            lambda l, i: (0, jnp.where(l == idx, i, jnp.where(l < idx, 0, nr - 1)), 0))

    blk = pl.BlockSpec((None, tr, C), lambda l, i: (l, i, 0))
    return pl.pallas_call(
        body, name=name, grid=(L, nr),
        in_specs=[part_spec(idx) for idx in range(L)] + [blk, blk, blk],
        out_specs=[blk] * 4,
        out_shape=[jax.ShapeDtypeStruct((L, R, C), F32)] * 4,
        compiler_params=_cparams(("arbitrary", "arbitrary")),
    )(*parts, w, m, v)


def _pack(flat_parts, row_multiple, dtype):
    lead = flat_parts[0].shape[:-1]
    total = sum(p.shape[-1] for p in flat_parts)
    quantum = PACK_W * row_multiple
    padded = -(-total // quantum) * quantum
    parts = [p.astype(dtype) for p in flat_parts]
    if padded > total:
        parts.append(jnp.zeros(lead + (padded - total,), dtype))
    return jnp.concatenate(parts, axis=-1).reshape(lead + (padded // PACK_W, PACK_W))


def _unpack(buf, shapes):
    lead = buf.shape[:-2]
    flat = buf.reshape(lead + (-1,))
    out, off = [], 0
    for shp in shapes:
        n = math.prod(shp)
        out.append(flat[..., off:off + n].reshape(lead + tuple(shp)))
        off += n
    return out


def _to_full(seg, ax):
    shard = seg.shape[1:]
    full = shard[:ax] + (N_DEV * shard[ax],) + shard[ax + 1:]
    return jnp.moveaxis(seg, 0, ax).reshape(full)


def _to_shards(full, ax):
    shp = full.shape
    split = shp[:ax] + (N_DEV, shp[ax] // N_DEV) + shp[ax + 1:]
    return jnp.moveaxis(full.reshape(split), ax, 0).reshape(N_DEV, -1)


BIG = (("attn_w_in", 1), ("attn_w_out", 1), ("lru_w_in", 1), ("lru_w_a", 2), ("lru_w_x", 2), ("lru_w_out", 1))
SMALL = (("lru_conv_w", 2), ("lru_conv_b", 1), ("lru_b_a", 2), ("lru_b_x", 2), ("lru_lambda", 1))
REPL = ("norm_pre", "norm_post", "attn_sinks")
ORDER = ("norm_pre", "norm_post", "attn_w_in", "attn_w_out", "attn_sinks", "lru_w_in", "lru_conv_w", "lru_conv_b",
         "lru_w_a", "lru_b_a", "lru_w_x", "lru_b_x", "lru_lambda", "lru_w_out")


def kernel(x, norm_pre, norm_post, attn_w_in, attn_w_out, attn_sinks, lru_w_in, lru_conv_w, lru_conv_b, lru_w_a, lru_b_a, lru_w_x, lru_b_x, lru_lambda, lru_w_out, loss_target, m_norm_pre, m_norm_post, m_attn_w_in, m_attn_w_out, m_attn_sinks, m_lru_w_in, m_lru_conv_w, m_lru_conv_b, m_lru_w_a, m_lru_b_a, m_lru_w_x, m_lru_b_x, m_lru_lambda, m_lru_w_out, v_norm_pre, v_norm_post, v_attn_w_in, v_attn_w_out, v_attn_sinks, v_lru_w_in, v_lru_conv_w, v_lru_conv_b, v_lru_w_a, v_lru_b_a, v_lru_w_x, v_lru_b_x, v_lru_lambda, v_lru_w_out):
    W = dict(norm_pre=norm_pre, norm_post=norm_post, attn_w_in=attn_w_in, attn_w_out=attn_w_out, attn_sinks=attn_sinks,
             lru_w_in=lru_w_in, lru_conv_w=lru_conv_w, lru_conv_b=lru_conv_b, lru_w_a=lru_w_a, lru_b_a=lru_b_a,
             lru_w_x=lru_w_x, lru_b_x=lru_b_x, lru_lambda=lru_lambda, lru_w_out=lru_w_out)
    M = dict(norm_pre=m_norm_pre, norm_post=m_norm_post, attn_w_in=m_attn_w_in, attn_w_out=m_attn_w_out,
             attn_sinks=m_attn_sinks, lru_w_in=m_lru_w_in, lru_conv_w=m_lru_conv_w, lru_conv_b=m_lru_conv_b,
             lru_w_a=m_lru_w_a, lru_b_a=m_lru_b_a, lru_w_x=m_lru_w_x, lru_b_x=m_lru_b_x, lru_lambda=m_lru_lambda,
             lru_w_out=m_lru_w_out)
    V = dict(norm_pre=v_norm_pre, norm_post=v_norm_post, attn_w_in=v_attn_w_in, attn_w_out=v_attn_w_out,
             attn_sinks=v_attn_sinks, lru_w_in=v_lru_w_in, lru_conv_w=v_lru_conv_w, lru_conv_b=v_lru_conv_b,
             lru_w_a=v_lru_w_a, lru_b_a=v_lru_b_a, lru_w_x=v_lru_w_x, lru_b_x=v_lru_b_x, lru_lambda=v_lru_lambda,
             lru_w_out=v_lru_w_out)

    h0 = x[0]
    target = loss_target[0]
    T, D = h0.shape
    depth = norm_pre.shape[0]
    n_attn = attn_w_in.shape[0]
    Q = attn_w_out.shape[1] * N_DEV
    KV = Q // GROUP
    LW = lru_w_out.shape[1] * N_DEV
    nh = LW // LRU_BLOCK

    n_lru = lru_w_in.shape[0]
    big_names = [n for n, _ in BIG]
    small_names = [n for n, _ in SMALL]
    small_shapes = [W[n].shape for n in small_names]
    repl_shapes = [W[n].shape for n in REPL]

    flat = lambda a: a.reshape(-1)
    def layer_shards(layer):
        j = layer // 2
        names = ("attn_w_in", "attn_w_out") if layer % 2 == 0 else ("lru_w_in", "lru_w_a", "lru_w_x", "lru_w_out")
        return [W[n][j].astype(BF16) for n in names], [1 if n in ("lru_w_a", "lru_w_x") else 0 for n in names]

    def layer_weights(layer, gathered):
        if layer % 2 == 0:
            g_in, g_out = gathered
            w_in = jnp.moveaxis(g_in, 0, 1).reshape(D, -1)
            w_in = jnp.concatenate([w_in[:, :Q], w_in[:, Q + 2 * KV:], w_in[:, Q:Q + 2 * KV]], axis=-1)
            return dict(w_in=w_in, w_out=g_out.reshape(Q, D))
        g_in, g_wa, g_wx, g_out = gathered
        return dict(w_in=jnp.moveaxis(g_in, 0, 1).reshape(D, 2 * LW), w_a=g_wa.reshape(nh, LRU_BLOCK, LRU_BLOCK),
                    w_x=g_wx.reshape(nh, LRU_BLOCK, LRU_BLOCK), w_out=g_out.reshape(LW, D))

    arrs0, axes0 = layer_shards(0)
    first = _comm_call(_gather_comm(arrs0 + [_pack([flat(W[n]) for n in small_names], SUBLANES, F32)], axes0 + [0]),
                       name="gather_first")
    weights = {0: layer_weights(0, first[:-1])}
    full = {}
    for (n, ax), seg in zip(SMALL, _unpack(first[-1], small_shapes)):
        full[n] = _to_full(seg, ax)
    cw_f = full["lru_conv_w"]
    cb_f = full["lru_conv_b"][:, None, :]
    ba_f = full["lru_b_a"].reshape(-1, 1, LW)
    bx_f = full["lru_b_x"].reshape(-1, 1, LW)
    lam_f = full["lru_lambda"][:, None, :]

    h = h0
    saved = []
    for layer in range(depth):
        j = layer // 2
        if layer == 0:
            u = _rms_fwd(h, norm_pre[0:1], name="rms_fwd")
        wl = weights[layer]
        nxt = _gather_comm(*layer_shards(layer + 1)) if layer + 1 < depth else None
        if layer % 2 == 0:
            proj = _matmul(u, wl["w_in"], out_dtype=BF16, name="attn_in")
            (mix, ypre), got = _attn_fwd(proj, attn_sinks[j], comm=nxt, name="attn_fwd")
            y = _matmul(ypre, wl["w_out"], name="attn_out")
        else:
            proj = _matmul(u, wl["w_in"], out_dtype=BF16, name="lru_in")
            (mix, ypre), got = _lru_fwd(proj, wl["w_a"], wl["w_x"], ba_f[j], bx_f[j], cw_f[j], cb_f[j], lam_f[j],
                                        comm=nxt, name="lru_fwd")
            y = _matmul(ypre, wl["w_out"], name="lru_out")
        if nxt is not None:
            weights[layer + 1] = layer_weights(layer + 1, got)
        saved.append((h, u, proj, mix, ypre, y))
        g_next = norm_pre[layer + 1:layer + 2] if layer + 1 < depth else None
        h, u = _post_fwd(y, norm_post[layer:layer + 1], h, g_next, name="post_fwd")

    dh, loss_part = _loss_head(h, target, name="loss_head")

    g_pre = [None] * depth
    g_post = [None] * depth
    small_vec = [n for n in small_names] + ["attn_sinks"]
    grads = {n: [None] * W[n].shape[0] for n in small_vec}
    recv = {}

    def carried(keyed):
        return _exchange_comm([item for _, item in keyed]) if keyed else None

    def landed(keyed, got):
        for (key, _), r in zip(keyed, got):
            recv[key] = r

    pending = []
    for layer in reversed(range(depth)):
        j = layer // 2
        h_in, u, proj, mix, ypre, y = saved[layer]
        wl = weights[layer]
        dy, g_post[layer] = _rms_bwd(y, norm_post[layer:layer + 1], dh, None, out_dtype=BF16, name="post_bwd")
        if layer % 2 == 0:
            dyp = _matmul(dy, wl["w_out"], tb=True, out_dtype=BF16, name="attn_out_dx")
            dw_out = _matmul(ypre, dy, ta=True, out_dtype=BF16, name="attn_out_dw")
            (dqg, dkv, dsink), got = _attn_bwd(proj, mix, dyp, attn_sinks[j], comm=carried(pending), name="attn_bwd")
            landed(pending, got)
            grads["attn_sinks"][j] = dsink[:, 0]
            dproj = jnp.concatenate([dqg, dkv], axis=1)
            mine = [(("attn_w_out", j), (dw_out.reshape(N_DEV, Q // N_DEV, D), False))]
            du, got = _matmul(dproj, wl["w_in"], tb=True, comm=carried(mine), name="attn_in_dx")
            landed(mine, got)
            dw = _matmul(u, dproj, ta=True, out_dtype=BF16, name="attn_in_dw")
            dw = jnp.concatenate([dw[:, :Q], dw[:, 2 * Q:], dw[:, Q:2 * Q]], axis=1)
            pending = [(("attn_w_in", j), (jnp.moveaxis(dw.reshape(D, N_DEV, -1), 1, 0), False))]
        else:
            dyp = _matmul(dy, wl["w_out"], tb=True, out_dtype=BF16, name="lru_out_dx")
            dw_out = _matmul(ypre, dy, ta=True, out_dtype=BF16, name="lru_out_dw")
            (dxb, dgt, dwa, dwx, dba, dbx, dcw, dcb, dlam), got = _lru_bwd(
                proj, mix, dyp, wl["w_a"], wl["w_x"], ba_f[j], bx_f[j], cw_f[j], cb_f[j], lam_f[j],
                comm=carried(pending), name="lru_bwd")
            landed(pending, got)
            grads["lru_b_a"][j], grads["lru_b_x"][j] = dba.reshape(nh, LRU_BLOCK), dbx.reshape(nh, LRU_BLOCK)
            grads["lru_conv_w"][j], grads["lru_conv_b"][j], grads["lru_lambda"][j] = dcw, dcb[0], dlam[0]
            dproj = jnp.concatenate([dxb, dgt], axis=1)
            mine = [(("lru_w_out", j), (dw_out.reshape(N_DEV, LW // N_DEV, D), False))]
            du, got = _matmul(dproj, wl["w_in"], tb=True, comm=carried(mine), name="lru_in_dx")
            landed(mine, got)
            mine = [(("lru_w_a", j), (dwa, True)), (("lru_w_x", j), (dwx, True))]
            dw, got = _matmul(u, dproj, ta=True, out_dtype=BF16, by_owner=True, comm=carried(mine), name="lru_in_dw")
            landed(mine, got)
            pending = [(("lru_w_in", j), (dw, False))]
        dh, g_pre[layer] = _rms_bwd(h_in, norm_pre[layer:layer + 1], du, dh, out_dtype=F32, name="pre_bwd")

    gfull = {n: jnp.stack(g) for n, g in grads.items()}
    gfull["norm_pre"] = jnp.concatenate(g_pre, axis=0)
    gfull["norm_post"] = jnp.concatenate(g_post, axis=0)

    repl_part = [jnp.broadcast_to(gfull[n].reshape(1, -1), (N_DEV, gfull[n].size)) for n in REPL]
    loss_slot = jnp.broadcast_to(loss_part.reshape(1, 1), (N_DEV, 1))
    send_small = _pack([_to_shards(gfull[n], ax) for n, ax in SMALL] + repl_part + [loss_slot], SUBLANES, F32)
    last = pending + [(("small", 0), (send_small, False))]
    landed(last, _comm_call(carried(last), name="exchange_last"))

    zero1 = jnp.zeros((1,), F32)
    outs = {}
    for n in big_names:
        shp = W[n].shape
        as3 = lambda a: a.reshape((shp[0], -1, shp[-1]))
        parts = [recv[n, j].reshape((N_DEV, -1, shp[-1])) for j in range(shp[0])]
        res = _adamw(parts, as3(W[n]), as3(M[n]), as3(V[n]), name="adamw_" + n)
        for kind, a in zip(("grad", "delta", "new_m", "new_v"), res):
            outs[kind, n] = a.reshape(shp)
    res_small = _adamw([recv["small", 0]],
                       *[_pack([flat(S[n]) for n in small_names] + [flat(S[n]) for n in REPL] + [zero1], SUBLANES, F32)[None]
                         for S in (W, M, V)], name="adamw_small")
    for kind, rs in zip(("grad", "delta", "new_m", "new_v"), res_small):
        for n, a in zip(small_names + list(REPL) + ["loss"], _unpack(rs[0], small_shapes + repl_shapes + [(1,)])):
            outs[kind, n] = a
    loss = outs["grad", "loss"][0]
    result = [loss, dh[None]]
    for kind in ("grad", "delta", "new_m", "new_v"):
        result += [outs[kind, n] for n in ORDER]
    return tuple(result)
```

```python
import math
from typing import Callable, NamedTuple

import jax
import jax.numpy as jnp
from jax import lax
from jax.experimental import pallas as pl
from jax.experimental.pallas import tpu as pltpu

F32 = jnp.float32
BF16 = jnp.bfloat16

N_DEV = 8
HEAD_DIM = 64
GROUP = 8
WINDOW = 128
LRU_BLOCK = 256
CONV_W = 4
C_RG = 8.0
NORM_EPS = 1e-6
MASK_VALUE = -1e30

ADAM_LR = 0.001
ADAM_B1 = 0.9
ADAM_B2 = 0.999
ADAM_EPS = 1e-08
ADAM_WD = 0.01
ADAM_STEP = 10

ROW_BLOCK = 256
LANES = 128
SUBLANES = 8
PACK_W = 1024
VMEM_LIMIT = 56 * 1024 * 1024
MATMUL_VMEM = 36 * 1024 * 1024
ADAMW_VMEM = 24 * 1024 * 1024
MESH = pl.DeviceIdType.MESH


def _cparams(sem=None):
    return pltpu.CompilerParams(dimension_semantics=sem, vmem_limit_bytes=VMEM_LIMIT)


class Comm(NamedTuple):
    arrays: list
    out_shapes: list
    sems: list
    start: Callable
    mid: Callable
    finish: Callable


def _call(body, *, name, grid, in_specs, out_specs, out_shape, scratch_shapes, semantics, args, comm=None):
    if comm is None:
        res = pl.pallas_call(body, name=name, grid=grid, in_specs=in_specs, out_specs=out_specs, out_shape=out_shape,
                             scratch_shapes=scratch_shapes, compiler_params=_cparams(semantics))(*args)
        return list(res), []
    n_in, n_out, n_scr = len(in_specs), len(out_specs), len(scratch_shapes)
    ci, co = len(comm.arrays), len(comm.out_shapes)
    steps = math.prod(grid)

    def hosted(*refs):
        ins, cins = refs[:n_in], refs[n_in:n_in + ci]
        o0 = n_in + ci
        outs, couts = refs[o0:o0 + n_out], refs[o0 + n_out:o0 + n_out + co]
        s0 = o0 + n_out + co
        scr, sems = refs[s0:s0 + n_scr], refs[s0 + n_scr:]
        step = 0
        for ax, g in enumerate(grid):
            step = step * g + pl.program_id(ax)

        @pl.when(step == 0)
        def _():
            comm.start(cins, couts, *sems)

        body(*ins, *outs, *scr)

        @pl.when(step == steps // 2)
        def _():
            comm.mid(cins, couts, *sems)

        @pl.when(step == steps - 1)
        def _():
            comm.finish(cins, couts, *sems)

    any_spec = pl.BlockSpec(memory_space=pl.ANY)
    res = pl.pallas_call(
        hosted, name=name, grid=grid,
        in_specs=list(in_specs) + [any_spec] * ci, out_specs=list(out_specs) + [any_spec] * co,
        out_shape=list(out_shape) + list(comm.out_shapes), scratch_shapes=list(scratch_shapes) + list(comm.sems),
        compiler_params=_cparams(("arbitrary",) * len(grid)),
    )(*args, *comm.arrays)
    return list(res[:n_out]), list(res[n_out:])


def _pick(n, target, quantum):
    best = None
    for t in range(quantum, min(n, target) + 1, quantum):
        if n % t == 0:
            best = t
    return n if best is None else best


def _sigmoid(x):
    return 1.0 / (1.0 + jnp.exp(-x))


def _dot(a, b):
    return lax.dot_general(a, b, (((1,), (0,)), ((), ())), preferred_element_type=F32)


def _dot_nt(a, b):
    return lax.dot_general(a, b, (((1,), (1,)), ((), ())), preferred_element_type=F32)


def _dot_tn(a, b):
    return lax.dot_general(a, b, (((0,), (0,)), ((), ())), preferred_element_type=F32)


def _matmul(a, b, *, ta=False, tb=False, out_dtype=F32, by_owner=False, comm=None, name):
    if ta:
        K, M = a.shape
    else:
        M, K = a.shape
    if tb:
        N, K2 = b.shape
    else:
        K2, N = b.shape
    assert K == K2, (a.shape, b.shape, ta, tb)
    tm = _pick(M, 1024, 256)
    tn = N // N_DEV if by_owner else _pick(N, 1536, 256)
    assert tn % LANES == 0
    osz = jnp.dtype(out_dtype).itemsize

    def vmem_bytes(tk):
        acc = 0 if tk == K else tm * tn * 4
        return 2 * (tm * tk * a.dtype.itemsize + tk * tn * b.dtype.itemsize + tm * tn * osz) + acc

    tk = max([t for t in range(256, K + 1, 256) if K % t == 0 and vmem_bytes(t) <= MATMUL_VMEM] or [_pick(K, 512, 256)])
    nk = K // tk
    dot = {(False, False): _dot, (False, True): _dot_nt, (True, False): _dot_tn}[(ta, tb)]

    if nk == 1:
        def body(a_ref, b_ref, o_ref):
            o_ref[...] = dot(a_ref[...].astype(BF16), b_ref[...].astype(BF16)).astype(o_ref.dtype)
        scratch = []
    else:
        def body(a_ref, b_ref, o_ref, acc_ref):
            k = pl.program_id(2)

            @pl.when(k == 0)
            def _():
                acc_ref[...] = jnp.zeros_like(acc_ref)

            acc_ref[...] += dot(a_ref[...].astype(BF16), b_ref[...].astype(BF16))

            @pl.when(k == nk - 1)
            def _():
                o_ref[...] = acc_ref[...].astype(o_ref.dtype)
        scratch = [pltpu.VMEM((tm, tn), F32)]

    a_spec = pl.BlockSpec((tk, tm), lambda j, i, k: (k, i)) if ta else pl.BlockSpec((tm, tk), lambda j, i, k: (i, k))
    b_spec = pl.BlockSpec((tn, tk), lambda j, i, k: (j, k)) if tb else pl.BlockSpec((tk, tn), lambda j, i, k: (k, j))
    if by_owner:
        o_spec = pl.BlockSpec((None, tm, tn), lambda j, i, k: (j, i, 0))
        o_shape = jax.ShapeDtypeStruct((N_DEV, M, tn), out_dtype)
    else:
        o_spec = pl.BlockSpec((tm, tn), lambda j, i, k: (i, j))
        o_shape = jax.ShapeDtypeStruct((M, N), out_dtype)
    res, extra = _call(body, name=name, grid=(N // tn, M // tm, nk), in_specs=[a_spec, b_spec], out_specs=[o_spec],
                       out_shape=[o_shape], scratch_shapes=scratch, semantics=("parallel", "parallel", "arbitrary"),
                       args=(a, b), comm=comm)
    return res[0] if comm is None else (res[0], extra)


def _rms_fwd(h, g, *, name):
    T, D = h.shape
    tm = _pick(T, ROW_BLOCK, SUBLANES)

    def body(h_ref, g_ref, u_ref):
        x = h_ref[...]
        r = lax.rsqrt(jnp.mean(x * x, axis=-1, keepdims=True) + NORM_EPS)
        u_ref[...] = ((x * r) * g_ref[...]).astype(u_ref.dtype)

    return pl.pallas_call(
        body, name=name, grid=(T // tm,),
        in_specs=[pl.BlockSpec((tm, D), lambda i: (i, 0)), pl.BlockSpec((1, D), lambda i: (0, 0))],
        out_specs=pl.BlockSpec((tm, D), lambda i: (i, 0)),
        out_shape=jax.ShapeDtypeStruct((T, D), BF16),
        compiler_params=_cparams(("parallel",)),
    )(h, g)


def _post_fwd(y, g, h, g_next, *, name):
    T, D = y.shape
    tm = _pick(T, ROW_BLOCK, SUBLANES)

    has_next = g_next is not None

    def body(*refs):
        y_ref, g_ref, h_ref = refs[:3]
        x = y_ref[...]
        r = lax.rsqrt(jnp.mean(x * x, axis=-1, keepdims=True) + NORM_EPS)
        ho = h_ref[...] + (x * r) * g_ref[...]
        if has_next:
            gn_ref, o_ref, u_ref = refs[3:]
            rn = lax.rsqrt(jnp.mean(ho * ho, axis=-1, keepdims=True) + NORM_EPS)
            u_ref[...] = ((ho * rn) * gn_ref[...]).astype(u_ref.dtype)
        else:
            o_ref, = refs[3:]
        o_ref[...] = ho

    row = pl.BlockSpec((tm, D), lambda i: (i, 0))
    vec = pl.BlockSpec((1, D), lambda i: (0, 0))
    res = pl.pallas_call(
        body, name=name, grid=(T // tm,),
        in_specs=[row, vec, row] + ([vec] if has_next else []),
        out_specs=[row] + ([row] if has_next else []),
        out_shape=[jax.ShapeDtypeStruct((T, D), F32)] + ([jax.ShapeDtypeStruct((T, D), BF16)] if has_next else []),
        compiler_params=_cparams(("parallel",)),
    )(*([y, g, h] + ([g_next] if has_next else [])))
    return (res[0], res[1]) if has_next else (res[0], None)


def _rms_bwd(x, g, dz, res, *, out_dtype, name):
    T, D = x.shape
    tm = _pick(T, ROW_BLOCK, SUBLANES)
    has_res = res is not None

    def body(*refs):
        if has_res:
            x_ref, g_ref, dz_ref, res_ref, dx_ref, dg_ref = refs
        else:
            x_ref, g_ref, dz_ref, dx_ref, dg_ref = refs
        i = pl.program_id(0)

        @pl.when(i == 0)
        def _():
            dg_ref[...] = jnp.zeros_like(dg_ref)

        xv = x_ref[...]
        dzv = dz_ref[...]
        r = lax.rsqrt(jnp.mean(xv * xv, axis=-1, keepdims=True) + NORM_EPS)
        xhat = xv * r
        dg_ref[...] += jnp.sum(dzv * xhat, axis=0, keepdims=True)
        dxh = dzv * g_ref[...]
        dx = r * (dxh - xhat * jnp.mean(dxh * xhat, axis=-1, keepdims=True))
        if has_res:
            dx = dx + res_ref[...]
        dx_ref[...] = dx.astype(dx_ref.dtype)

    row = pl.BlockSpec((tm, D), lambda i: (i, 0))
    vec = pl.BlockSpec((1, D), lambda i: (0, 0))
    ins = [x, g, dz] + ([res] if has_res else [])
    return pl.pallas_call(
        body, name=name, grid=(T // tm,),
        in_specs=[row, vec, row] + ([row] if has_res else []),
        out_specs=[row, vec],
        out_shape=[jax.ShapeDtypeStruct((T, D), out_dtype), jax.ShapeDtypeStruct((1, D), F32)],
        compiler_params=_cparams(("arbitrary",)),
    )(*ins)


def _loss_head(h, target, *, name):
    T, D = h.shape
    tm = _pick(T, ROW_BLOCK, SUBLANES)

    def body(h_ref, t_ref, dh_ref, l_ref):
        i = pl.program_id(0)

        @pl.when(i == 0)
        def _():
            l_ref[...] = jnp.zeros_like(l_ref)

        e = h_ref[...] - t_ref[...]
        dh_ref[...] = e * (1.0 / D)
        row = jnp.sum(e * e, axis=-1, keepdims=True) * (0.5 / D)
        l_ref[...] += jnp.sum(row, axis=0, keepdims=True)

    row = pl.BlockSpec((tm, D), lambda i: (i, 0))
    return pl.pallas_call(
        body, name=name, grid=(T // tm,),
        in_specs=[row, row],
        out_specs=[row, pl.BlockSpec((1, 1), lambda i: (0, 0))],
        out_shape=[jax.ShapeDtypeStruct((T, D), F32), jax.ShapeDtypeStruct((1, 1), F32)],
        compiler_params=_cparams(("arbitrary",)),
    )(h, target)


def _attn_dims(P):
    Q = P * 4 // 9
    KV = Q // GROUP
    assert 2 * Q + 2 * KV == P and KV % LANES == 0
    return Q, KV


def _attn_specs(Q, KV, nb):
    blk = WINDOW
    q_spec = pl.BlockSpec((blk, Q), lambda i: (jnp.minimum(i, nb - 1), 0))
    g_spec = pl.BlockSpec((blk, Q), lambda i: (jnp.minimum(i, nb - 1), 1))
    kvc_spec = pl.BlockSpec((blk, 2 * KV), lambda i: (jnp.minimum(i, nb - 1), Q // KV))
    kvp_spec = pl.BlockSpec((blk, 2 * KV), lambda i: (jnp.maximum(jnp.minimum(i, nb - 1) - 1, 0), Q // KV))
    return q_spec, g_spec, kvc_spec, kvp_spec


PAIRS = GROUP // 2
STACK = PAIRS * WINDOW


def _band_mask(i):
    c = lax.broadcasted_iota(jnp.int32, (2 * WINDOW, STACK), 0)
    r = lax.broadcasted_iota(jnp.int32, (2 * WINDOW, STACK), 1) & (WINDOW - 1)
    first_key = jnp.where(i > 0, 0, WINDOW)
    return (c > r) & (c <= r + WINDOW) & (c >= first_key)


def _group_cols(kvh):
    c0 = kvh * GROUP * HEAD_DIM
    return [slice(c0 + j * LANES, c0 + (j + 1) * LANES) for j in range(PAIRS)]


def _stack(ref, cols, scale=None):
    x = jnp.concatenate([ref[:, cs] for cs in cols], axis=0).astype(F32)
    return x if scale is None else x * scale


def _group_sinks(sink_ref, kvh, half):
    return jnp.concatenate([jnp.full((1, WINDOW), sink_ref[kvh * GROUP + 2 * j + half], F32) for j in range(PAIRS)], axis=1)


def _pair_halves(x128, xt128, e):
    lo = lax.broadcasted_iota(jnp.int32, x128.shape, 1) < HEAD_DIM
    lo_t = lax.broadcasted_iota(jnp.int32, xt128.shape, 0) < HEAD_DIM
    if e == 0:
        x_lo, xt_lo = jnp.where(lo, x128, 0.0), jnp.where(lo_t, xt128, 0.0)
        x_hi, xt_hi = pltpu.roll(x_lo, HEAD_DIM, 1), pltpu.roll(xt_lo, HEAD_DIM, 0)
    else:
        x_hi, xt_hi = jnp.where(lo, 0.0, x128), jnp.where(lo_t, 0.0, xt128)
        x_lo, xt_lo = pltpu.roll(x_hi, HEAD_DIM, 1), pltpu.roll(xt_hi, HEAD_DIM, 0)
    return (x_lo.astype(BF16), x_hi.astype(BF16)), (xt_lo.astype(BF16), xt_hi.astype(BF16))


def _softmax_sink(st, allowed, sink):
    st = jnp.where(allowed, st, MASK_VALUE)
    m = jnp.maximum(jnp.max(st, axis=0, keepdims=True), sink)
    p = jnp.exp(st - m)
    es = jnp.exp(sink - m)
    inv = 1.0 / (jnp.sum(p, axis=0, keepdims=True) + es)
    return p * inv, es * inv


def _attn_fwd(proj, sinks, *, comm=None, name):
    T, P = proj.shape
    Q, KV = _attn_dims(P)
    nb = T // WINDOW
    npairs = KV // LANES
    scale = 1.0 / math.sqrt(HEAD_DIM)

    def body(sink_ref, q_ref, g_ref, kvc_ref, kvp_ref, out_ref, yp_ref):
        i = pl.program_id(0)
        allowed = _band_mask(i)
        for p in range(npairs):
            ks = slice(p * LANES, (p + 1) * LANES)
            vs = slice(KV + p * LANES, KV + (p + 1) * LANES)
            k128 = jnp.concatenate([kvp_ref[:, ks], kvc_ref[:, ks]], axis=0).astype(F32)
            v128 = jnp.concatenate([kvp_ref[:, vs], kvc_ref[:, vs]], axis=0).astype(F32)
            kt128, vt128 = k128.T, v128.T
            for e in range(2):
                kvh = 2 * p + e
                khalf, _ = _pair_halves(k128, kt128, e)
                _, vthalf = _pair_halves(v128, vt128, e)
                cols = _group_cols(kvh)
                q4 = _stack(q_ref, cols, scale).astype(BF16)
                ot = None
                for half in range(2):
                    st = _dot_nt(khalf[half], q4)
                    pn, _ = _softmax_sink(st, allowed, _group_sinks(sink_ref, kvh, half))
                    o = _dot(vthalf[half], pn.astype(BF16))
                    ot = o if ot is None else ot + o
                o4 = ot.T
                g4 = _stack(g_ref, cols)
                y4 = (o4 * (g4 * _sigmoid(g4))).astype(BF16)
                for j, cs in enumerate(cols):
                    out_ref[:, cs] = o4[j * WINDOW:(j + 1) * WINDOW]
                    yp_ref[:, cs] = y4[j * WINDOW:(j + 1) * WINDOW]

    q_spec, g_spec, kvc_spec, kvp_spec = _attn_specs(Q, KV, nb)
    row = pl.BlockSpec((WINDOW, Q), lambda i: (i, 0))
    return _call(
        body, name=name, grid=(nb,),
        in_specs=[pl.BlockSpec(memory_space=pltpu.SMEM), q_spec, g_spec, kvc_spec, kvp_spec],
        out_specs=[row, row],
        out_shape=[jax.ShapeDtypeStruct((T, Q), F32), jax.ShapeDtypeStruct((T, Q), BF16)],
        scratch_shapes=[], semantics=("parallel",), args=(sinks, proj, proj, proj, proj), comm=comm)


def _attn_bwd(proj, out, dyp, sinks, *, comm=None, name):
    T, P = proj.shape
    Q, KV = _attn_dims(P)
    nb = T // WINDOW
    npairs = KV // LANES
    H = Q // HEAD_DIM
    scale = 1.0 / math.sqrt(HEAD_DIM)

    def body(sink_ref, q_ref, g_ref, kvc_ref, kvp_ref, out_ref, dyp_ref, dqg_ref, dkv_ref, dsink_ref, carry_ref):
        i = pl.program_id(0)

        @pl.when(i == 0)
        def _():
            carry_ref[...] = jnp.zeros_like(carry_ref)
            dsink_ref[...] = jnp.zeros_like(dsink_ref)

        @pl.when(i == nb)
        def _():
            dkv_ref[...] = carry_ref[...].astype(dkv_ref.dtype)

        @pl.when(i < nb)
        def _():
            allowed = _band_mask(i)
            lo = lax.broadcasted_iota(jnp.int32, (2 * WINDOW, LANES), 1) < HEAD_DIM
            sel_lane = lax.broadcasted_iota(jnp.int32, (SUBLANES, LANES), 1) < HEAD_DIM
            sels = (jnp.where(sel_lane, 1.0, 0.0).astype(BF16), jnp.where(sel_lane, 0.0, 1.0).astype(BF16))
            for p in range(npairs):
                ks = slice(p * LANES, (p + 1) * LANES)
                vs = slice(KV + p * LANES, KV + (p + 1) * LANES)
                k128 = jnp.concatenate([kvp_ref[:, ks], kvc_ref[:, ks]], axis=0).astype(F32)
                v128 = jnp.concatenate([kvp_ref[:, vs], kvc_ref[:, vs]], axis=0).astype(F32)
                kt128, vt128 = k128.T, v128.T
                dk_e, dv_e = [], []
                for e in range(2):
                    kvh = 2 * p + e
                    khalf, kthalf = _pair_halves(k128, kt128, e)
                    vhalf, _ = _pair_halves(v128, vt128, e)
                    cols = _group_cols(kvh)
                    q4 = _stack(q_ref, cols, scale).astype(BF16)
                    g4 = _stack(g_ref, cols)
                    o4 = _stack(out_ref, cols)
                    dy4 = _stack(dyp_ref, cols)
                    sg = _sigmoid(g4)
                    do4 = dy4 * (g4 * sg)
                    dg4 = (dy4 * o4 * (sg * (1.0 + g4 * (1.0 - sg)))).astype(dqg_ref.dtype)
                    dod = do4 * o4
                    dod_hi = dod.astype(BF16)
                    dod_lo = (dod - dod_hi.astype(F32)).astype(BF16)
                    do4b = do4.astype(BF16)
                    dqt = None
                    dk_h, dv_h = [], []
                    for half in range(2):
                        delta = jnp.max(_dot_nt(sels[half], dod_hi) + _dot_nt(sels[half], dod_lo), axis=0, keepdims=True)
                        st = _dot_nt(khalf[half], q4)
                        pn, psink = _softmax_sink(st, allowed, _group_sinks(sink_ref, kvh, half))
                        dp = _dot_nt(vhalf[half], do4b)
                        ds = (pn * (dp - delta)).astype(BF16)
                        dq = _dot(kthalf[half], ds)
                        dqt = dq if dqt is None else dqt + dq
                        dk_h.append(_dot(ds, q4))
                        dv_h.append(_dot(pn.astype(BF16), do4b))
                        pd = psink * delta
                        for j in range(PAIRS):
                            n = kvh * GROUP + 2 * j + half
                            dsn = -jnp.sum(pd[:, j * WINDOW:(j + 1) * WINDOW], axis=1, keepdims=True)
                            dsink_ref[n:n + 1, :] += jnp.broadcast_to(dsn, (1, LANES))
                    dq4 = (dqt.T * scale).astype(dqg_ref.dtype)
                    for j, cs in enumerate(cols):
                        dqg_ref[:, cs] = dq4[j * WINDOW:(j + 1) * WINDOW]
                        dqg_ref[:, slice(Q + cs.start, Q + cs.stop)] = dg4[j * WINDOW:(j + 1) * WINDOW]
                    acc_k = jnp.where(lo, dk_h[0], dk_h[1])
                    acc_v = jnp.where(lo, dv_h[0], dv_h[1])
                    dk_e.append(acc_k + pltpu.roll(acc_k, HEAD_DIM, 1))
                    dv_e.append(acc_v + pltpu.roll(acc_v, HEAD_DIM, 1))
                for sl, de in ((ks, dk_e), (vs, dv_e)):
                    d128 = jnp.where(lo, de[0], de[1])
                    dkv_ref[:, sl] = (carry_ref[:, sl] + d128[:WINDOW]).astype(dkv_ref.dtype)
                    carry_ref[:, sl] = d128[WINDOW:]

    q_spec, g_spec, kvc_spec, kvp_spec = _attn_specs(Q, KV, nb)
    last = lambda i: (jnp.minimum(i, nb - 1), 0)
    row = pl.BlockSpec((WINDOW, Q), last)
    return _call(
        body, name=name, grid=(nb + 1,),
        in_specs=[pl.BlockSpec(memory_space=pltpu.SMEM), q_spec, g_spec, kvc_spec, kvp_spec, row, row],
        out_specs=[pl.BlockSpec((WINDOW, 2 * Q), last),
                   pl.BlockSpec((WINDOW, 2 * KV), lambda i: (jnp.maximum(i - 1, 0), 0)),
                   pl.BlockSpec((H, LANES), lambda i: (0, 0))],
        out_shape=[jax.ShapeDtypeStruct((T, 2 * Q), BF16), jax.ShapeDtypeStruct((T, 2 * KV), BF16),
                   jax.ShapeDtypeStruct((H, LANES), F32)],
        scratch_shapes=[pltpu.VMEM((WINDOW, 2 * KV), F32)],
        semantics=("arbitrary",), args=(sinks, proj, proj, proj, proj, out, dyp), comm=comm)


LRU_CHUNK = 256


def _shift_down(x, halo8, s):
    if s == 0:
        return x
    row8 = lax.broadcasted_iota(jnp.int32, (SUBLANES, 1), 0)
    r = pltpu.roll(x, s, 0)
    top = jnp.where(row8 < s, pltpu.roll(halo8, s, 0), r[:SUBLANES])
    return jnp.concatenate([top, r[SUBLANES:]], axis=0)


def _shift_up(x, halo8, s):
    if s == 0:
        return x
    n = x.shape[0]
    row8 = lax.broadcasted_iota(jnp.int32, (SUBLANES, 1), 0)
    r = pltpu.roll(x, n - s, 0)
    bot = jnp.where(row8 >= SUBLANES - s, pltpu.roll(halo8, SUBLANES - s, 0), r[n - SUBLANES:])
    return jnp.concatenate([r[:n - SUBLANES], bot], axis=0)


def _scan_fwd(a, b, c0):
    n = a.shape[0]
    row = lax.broadcasted_iota(jnp.int32, (n, 1), 0) & (SUBLANES - 1)
    s = 1
    while s < SUBLANES:
        keep = row >= s
        ar = jnp.where(keep, pltpu.roll(a, s, 0), 1.0)
        br = jnp.where(keep, pltpu.roll(b, s, 0), 0.0)
        b = a * br + b
        a = a * ar
        s *= 2
    out, c = [], c0
    for i in range(n // SUBLANES):
        rows = slice(i * SUBLANES, (i + 1) * SUBLANES)
        h = a[rows] * c + b[rows]
        out.append(h)
        c = h[SUBLANES - 1:]
    return jnp.concatenate(out, axis=0)


def _scan_rev(al, b, c0):
    n = al.shape[0]
    row = lax.broadcasted_iota(jnp.int32, (n, 1), 0) & (SUBLANES - 1)
    s = 1
    while s < SUBLANES:
        keep = row < SUBLANES - s
        ar = jnp.where(keep, pltpu.roll(al, n - s, 0), 1.0)
        br = jnp.where(keep, pltpu.roll(b, n - s, 0), 0.0)
        b = b + al * br
        al = al * ar
        s *= 2
    out, c = [], c0
    for i in reversed(range(n // SUBLANES)):
        rows = slice(i * SUBLANES, (i + 1) * SUBLANES)
        l = b[rows] + al[rows] * c
        out.append(l)
        c = l[:1]
    return jnp.concatenate(out[::-1], axis=0)


def _log1p_pos(z):
    return jnp.where(z < 0.01, z * (1.0 - z * (0.5 - z * (1.0 / 3.0))), jnp.log(1.0 + z))


def _neg_expm1(x):
    series = -x * (1.0 + x * (0.5 + x * (1.0 / 6.0 + x * (1.0 / 24.0 + x * (1.0 / 120.0)))))
    return jnp.where(x > -0.05, series, 1.0 - jnp.exp(x))


def _softplus_neg(lam):
    return jnp.maximum(-lam, 0.0) + _log1p_pos(jnp.exp(-jnp.abs(lam)))


def _lru_gates(xb, halo, wa, wx, ba, bx, cw_ref, cb, lam):
    xs = [_shift_down(xb, halo, s) for s in range(CONV_W)]
    xc = cb + xs[3] * cw_ref[0:1, :] + xs[2] * cw_ref[1:2, :] + xs[1] * cw_ref[2:3, :] + xs[0] * cw_ref[3:4, :]
    xcb = xc.astype(BF16)
    r = _sigmoid(_dot(xcb, wa) + ba)
    ig = _sigmoid(_dot(xcb, wx) + bx)
    sp = _softplus_neg(lam)
    log_a = (-C_RG * r) * sp
    a = jnp.exp(log_a)
    mult = jnp.sqrt(_neg_expm1(2.0 * log_a))
    return xs, xc, xcb, r, ig, sp, a, mult


def _tile_rows(dtype):
    return SUBLANES * 4 // jnp.dtype(dtype).itemsize


def _last_rows(ref):
    return ref[...].astype(F32)[ref.shape[0] - SUBLANES:]


def _lru_specs(nh, nt, tc, rev):
    tix = (lambda t: nt - 1 - t) if rev else (lambda t: t)
    chunk = lambda off: pl.BlockSpec((tc, LRU_BLOCK), lambda h, t: (tix(t), h + off))
    prev8 = lambda off, rows: pl.BlockSpec((rows, LRU_BLOCK),
                                           lambda h, t: (jnp.maximum(tix(t) * (tc // rows) - 1, 0), h + off))
    wblk = pl.BlockSpec((None, LRU_BLOCK, LRU_BLOCK), lambda h, t: (h, 0, 0))
    vec = pl.BlockSpec((1, LRU_BLOCK), lambda h, t: (0, h))
    cwb = pl.BlockSpec((CONV_W, LRU_BLOCK), lambda h, t: (0, h))
    return tix, chunk, prev8, wblk, vec, cwb


def _lru_fwd(proj, wa, wx, ba, bx, cw, cb, lam, *, comm=None, name):
    T, W2 = proj.shape
    W = W2 // 2
    nh = W // LRU_BLOCK
    tc = _pick(T, LRU_CHUNK, SUBLANES)
    nt = T // tc

    def body(xb_ref, xh_ref, gt_ref, wa_ref, wx_ref, ba_ref, bx_ref, cw_ref, cb_ref, lam_ref, hs_ref, yp_ref, carry_ref):
        t = pl.program_id(1)

        @pl.when(t == 0)
        def _():
            carry_ref[...] = jnp.zeros_like(carry_ref)

        halo = jnp.where(t > 0, _last_rows(xh_ref), 0.0)
        _, xc, _, _, ig, _, a, mult = _lru_gates(xb_ref[...].astype(F32), halo, wa_ref[...], wx_ref[...], ba_ref[...], bx_ref[...],
                                               cw_ref, cb_ref[...], lam_ref[...])
        hs = _scan_fwd(a, mult * (ig * xc), carry_ref[SUBLANES - 1:SUBLANES, :])
        hs_ref[...] = hs
        carry_ref[...] = hs[tc - SUBLANES:]
        g = gt_ref[...].astype(F32)
        yp_ref[...] = (hs * (g * _sigmoid(g))).astype(BF16)

    _, chunk, prev8, wblk, vec, cwb = _lru_specs(nh, nt, tc, False)
    return _call(
        body, name=name, grid=(nh, nt),
        in_specs=[chunk(0), prev8(0, _tile_rows(proj.dtype)), chunk(nh), wblk, wblk, vec, vec, cwb, vec, vec],
        out_specs=[chunk(0), chunk(0)],
        out_shape=[jax.ShapeDtypeStruct((T, W), F32), jax.ShapeDtypeStruct((T, W), BF16)],
        scratch_shapes=[pltpu.VMEM((SUBLANES, LRU_BLOCK), F32)],
        semantics=("parallel", "arbitrary"), args=(proj, proj, proj, wa, wx, ba, bx, cw, cb, lam), comm=comm)


def _lru_bwd(proj, hs, dyp, wa, wx, ba, bx, cw, cb, lam, *, comm=None, name):
    T, W2 = proj.shape
    W = W2 // 2
    nh = W // LRU_BLOCK
    tc = _pick(T, LRU_CHUNK, SUBLANES)
    nt = T // tc

    def body(xb_ref, xh_ref, gt_ref, hs_ref, hh_ref, dyp_ref, wa_ref, wx_ref, ba_ref, bx_ref, cw_ref, cb_ref, lam_ref,
             dx_ref, dg_ref, dwa_ref, dwx_ref, dba_ref, dbx_ref, dcw_ref, dcb_ref, dlam_ref,
             ca_ref, cl_ref, cx_ref):
        t = pl.program_id(1)
        first = t == nt - 1

        @pl.when(t == 0)
        def _():
            for ref in (ca_ref, cl_ref, cx_ref, dwa_ref, dwx_ref, dba_ref, dbx_ref, dcw_ref, dcb_ref, dlam_ref):
                ref[...] = jnp.zeros_like(ref)

        xb = xb_ref[...].astype(F32)
        halo = jnp.where(first, 0.0, _last_rows(xh_ref))
        wa = wa_ref[...]
        wx = wx_ref[...]
        lam = lam_ref[...]
        xs, xc, xcb, r, ig, sp, a, mult = _lru_gates(xb, halo, wa, wx, ba_ref[...], bx_ref[...], cw_ref, cb_ref[...], lam)
        hsv = hs_ref[...]
        g = gt_ref[...].astype(F32)
        dy = dyp_ref[...].astype(F32)
        sg = _sigmoid(g)
        dg_ref[...] = (dy * hsv * (sg * (1.0 + g * (1.0 - sg)))).astype(dg_ref.dtype)
        dhs = dy * (g * sg)

        al = _shift_up(a, ca_ref[...], 1)
        lmb = _scan_rev(al, dhs, cl_ref[0:1, :])
        hprev = _shift_down(hsv, jnp.where(first, 0.0, _last_rows(hh_ref)), 1)
        da = lmb * hprev
        ixc = ig * xc
        dmult = lmb * ixc
        dlog_a = da * a - dmult * (a * a) / mult
        dr = dlog_a * (-C_RG * sp)
        dlam_ref[...] += jnp.sum(dlog_a * r, axis=0, keepdims=True) * (C_RG * _sigmoid(-lam))
        dpa = dr * (r * (1.0 - r))
        dpx = (lmb * mult * xc) * (ig * (1.0 - ig))
        dpab = dpa.astype(BF16)
        dpxb = dpx.astype(BF16)
        dwa_ref[...] += _dot_tn(xcb, dpab)
        dwx_ref[...] += _dot_tn(xcb, dpxb)
        dba_ref[...] += jnp.sum(dpa, axis=0, keepdims=True)
        dbx_ref[...] += jnp.sum(dpx, axis=0, keepdims=True)
        dxc = lmb * mult * ig + _dot_nt(dpab, wa) + _dot_nt(dpxb, wx)
        dcb_ref[...] += jnp.sum(dxc, axis=0, keepdims=True)
        for s in range(CONV_W):
            dcw_ref[CONV_W - 1 - s:CONV_W - s, :] += jnp.sum(dxc * xs[s], axis=0, keepdims=True)
        cxv = cx_ref[...]
        dxb = dxc * cw_ref[3:4, :]
        for s in range(1, CONV_W):
            dxb = dxb + _shift_up(dxc, cxv, s) * cw_ref[3 - s:4 - s, :]
        dx_ref[...] = dxb.astype(dx_ref.dtype)
        ca_ref[...] = a[:SUBLANES]
        cl_ref[...] = lmb[:SUBLANES]
        cx_ref[...] = dxc[:SUBLANES]

    tix, chunk, prev8, wblk, vec, cwb = _lru_specs(nh, nt, tc, True)
    hchunk = pl.BlockSpec((tc, LRU_BLOCK), lambda h, t: (tix(t), h))
    carry = pltpu.VMEM((SUBLANES, LRU_BLOCK), F32)
    return _call(
        body, name=name, grid=(nh, nt),
        in_specs=[chunk(0), prev8(0, _tile_rows(proj.dtype)), chunk(nh), hchunk, prev8(0, _tile_rows(hs.dtype)), hchunk,
                  wblk, wblk, vec, vec, cwb, vec, vec],
        out_specs=[hchunk, hchunk, wblk, wblk, vec, vec, cwb, vec, vec],
        out_shape=[jax.ShapeDtypeStruct((T, W), BF16), jax.ShapeDtypeStruct((T, W), BF16),
                   jax.ShapeDtypeStruct((nh, LRU_BLOCK, LRU_BLOCK), F32), jax.ShapeDtypeStruct((nh, LRU_BLOCK, LRU_BLOCK), F32),
                   jax.ShapeDtypeStruct((1, W), F32), jax.ShapeDtypeStruct((1, W), F32),
                   jax.ShapeDtypeStruct((CONV_W, W), F32), jax.ShapeDtypeStruct((1, W), F32), jax.ShapeDtypeStruct((1, W), F32)],
        scratch_shapes=[carry, carry, carry], semantics=("parallel", "arbitrary"),
        args=(proj, proj, proj, hs, hs, dyp, wa, wx, ba, bx, cw, cb, lam), comm=comm)


def _position():
    return lax.axis_index("x"), lax.axis_index("y"), lax.axis_index("c")


def _sems(n):
    return [pltpu.SemaphoreType.DMA((n, 7)), pltpu.SemaphoreType.DMA((n, 7)), pltpu.SemaphoreType.DMA((n,))]


def _gather_comm(arrs, axes):
    n = len(arrs)

    def tools(ins, outs, send_sems, recv_sems, local_sems):
        x, y, c = _position()
        me, sibling = (x, y, c), (x, y, 1 - c)
        chips = [(1 - x, y), (x, 1 - y), (1 - x, 1 - y)]

        def slot(a, pos):
            return outs[a].at[(slice(None),) * axes[a] + (pos,)]

        def copy(a, k, block, to, src=None):
            px, py, pc = block
            rows = slot(a, 4 * px + 2 * py + pc)
            return pltpu.make_async_remote_copy(
                src_ref=rows if src is None else src, dst_ref=rows,
                send_sem=send_sems.at[a, k], recv_sem=recv_sems.at[a, k],
                device_id=to, device_id_type=MESH)

        own = lambda a: pltpu.make_async_copy(ins[a], slot(a, 4 * x + 2 * y + c), local_sems.at[a])
        first = lambda a: ([copy(a, 0, me, sibling, src=ins[a])]
                           + [copy(a, 1 + j, me, (*chip, c), src=ins[a]) for j, chip in enumerate(chips)])
        passed = lambda a: [copy(a, 4 + j, (*chip, c), sibling) for j, chip in enumerate(chips)]
        return me, sibling, chips, c, copy, own, first, passed

    def start(ins, outs, *sems):
        *_, own, first, _ = tools(ins, outs, *sems)
        for a in range(n):
            own(a).start()
            for cp in first(a):
                cp.start()

    def mid(ins, outs, *sems):
        me, _, chips, c, copy, _, _, passed = tools(ins, outs, *sems)
        for a in range(n):
            fwd = passed(a)
            for j, chip in enumerate(chips):
                copy(a, 1 + j, (*chip, c), me).wait_recv()
                fwd[j].start()

    def finish(ins, outs, *sems):
        me, sibling, chips, c, copy, own, first, passed = tools(ins, outs, *sems)
        for a in range(n):
            copy(a, 0, sibling, me).wait_recv()
            for j, chip in enumerate(chips):
                copy(a, 4 + j, (*chip, 1 - c), me).wait_recv()
        for a in range(n):
            for cp in first(a) + passed(a):
                cp.wait_send()
            own(a).wait()

    shapes = [jax.ShapeDtypeStruct(a.shape[:ax] + (N_DEV,) + a.shape[ax:], a.dtype) for a, ax in zip(arrs, axes)]
    return Comm(list(arrs), shapes, _sems(n), start, mid, finish)


def _comm_call(comm, *, name):
    ci, co = len(comm.arrays), len(comm.out_shapes)

    def body(*refs):
        ins, outs, sems = refs[:ci], refs[ci:ci + co], refs[ci + co:]
        comm.start(ins, outs, *sems)
        comm.mid(ins, outs, *sems)
        comm.finish(ins, outs, *sems)

    any_spec = pl.BlockSpec(memory_space=pl.ANY)
    return pl.pallas_call(body, name=name, in_specs=[any_spec] * ci, out_specs=[any_spec] * co,
                          out_shape=comm.out_shapes, scratch_shapes=comm.sems)(*comm.arrays)


def _exchange_comm(items):
    n = len(items)

    def tools(ins, outs, send_sems, recv_sems, local_sems):
        x, y, c = _position()
        me = 4 * x + 2 * y + c

        def src(a, pos):
            if items[a][1]:
                rows = ins[a].shape[1] // N_DEV
                return ins[a].at[:, pl.ds(pl.multiple_of(pos * rows, rows), rows)]
            return ins[a].at[pos]

        copies = [pltpu.make_async_copy(src(a, me), outs[a].at[me], local_sems.at[a]) for a in range(n)]
        for k in range(1, N_DEV):
            px = x ^ ((k >> 2) & 1)
            py = y ^ ((k >> 1) & 1)
            pc = c ^ (k & 1)
            copies += [pltpu.make_async_remote_copy(
                src_ref=src(a, 4 * px + 2 * py + pc), dst_ref=outs[a].at[me],
                send_sem=send_sems.at[a, k - 1], recv_sem=recv_sems.at[a, k - 1],
                device_id=(px, py, pc), device_id_type=MESH) for a in range(n)]
        return copies

    def start(ins, outs, *sems):
        for cp in tools(ins, outs, *sems):
            cp.start()

    def mid(ins, outs, *sems):
        pass

    def finish(ins, outs, *sems):
        for cp in tools(ins, outs, *sems):
            cp.wait()

    shapes = []
    for arr, split in items:
        blk = (arr.shape[0], arr.shape[1] // N_DEV) + arr.shape[2:] if split else arr.shape[1:]
        shapes.append(jax.ShapeDtypeStruct((N_DEV,) + blk, arr.dtype))
    return Comm([arr for arr, _ in items], shapes, _sems(n), start, mid, finish)


def _adamw(parts, w, m, v, *, name):
    L, R, C = w.shape
    assert len(parts) == L
    row_bytes = 2 * (L * N_DEV * C * parts[0].dtype.itemsize + 7 * C * 4)
    tr = _pick(R, max(16, ADAMW_VMEM // row_bytes), 16)
    nr = R // tr
    c1 = 1.0 / (1.0 - ADAM_B1 ** ADAM_STEP)
    c2 = 1.0 / (1.0 - ADAM_B2 ** ADAM_STEP)

    def body(*refs):
        p_refs = refs[:L]
        w_ref, m_ref, v_ref, g_ref, d_ref, nm_ref, nv_ref = refs[L:]
        layer = pl.program_id(0)
        for idx, p_ref in enumerate(p_refs):
            @pl.when(layer == idx)
            def _():
                g = p_ref[0].astype(F32)
                for s in range(1, N_DEV):
                    g = g + p_ref[s].astype(F32)
                nm = ADAM_B1 * m_ref[...] + (1.0 - ADAM_B1) * g
                nv = ADAM_B2 * v_ref[...] + (1.0 - ADAM_B2) * (g * g)
                g_ref[...] = g
                nm_ref[...] = nm
                nv_ref[...] = nv
                d_ref[...] = -ADAM_LR * ((nm * c1) / (jnp.sqrt(nv * c2) + ADAM_EPS) + ADAM_WD * w_ref[...])

    def part_spec(idx):
        return pl.BlockSpec((N_DEV, tr, C),
                            lambda l, i: (0, jnp.where(l == idx, i, jnp.where(l < idx, 0, nr - 1)), 0))

    blk = pl.BlockSpec((None, tr, C), lambda l, i: (l, i, 0))
    return pl.pallas_call(
        body, name=name, grid=(L, nr),
        in_specs=[part_spec(idx) for idx in range(L)] + [blk, blk, blk],
        out_specs=[blk] * 4,
        out_shape=[jax.ShapeDtypeStruct((L, R, C), F32)] * 4,
        compiler_params=_cparams(("arbitrary", "arbitrary")),
    )(*parts, w, m, v)


def _pack(flat_parts, row_multiple, dtype):
    lead = flat_parts[0].shape[:-1]
    total = sum(p.shape[-1] for p in flat_parts)
    quantum = PACK_W * row_multiple
    padded = -(-total // quantum) * quantum
    parts = [p.astype(dtype) for p in flat_parts]
    if padded > total:
        parts.append(jnp.zeros(lead + (padded - total,), dtype))
    return jnp.concatenate(parts, axis=-1).reshape(lead + (padded // PACK_W, PACK_W))


def _unpack(buf, shapes):
    lead = buf.shape[:-2]
    flat = buf.reshape(lead + (-1,))
    out, off = [], 0
    for shp in shapes:
        n = math.prod(shp)
        out.append(flat[..., off:off + n].reshape(lead + tuple(shp)))
        off += n
    return out


def _to_full(seg, ax):
    shard = seg.shape[1:]
    full = shard[:ax] + (N_DEV * shard[ax],) + shard[ax + 1:]
    return jnp.moveaxis(seg, 0, ax).reshape(full)


def _to_shards(full, ax):
    shp = full.shape
    split = shp[:ax] + (N_DEV, shp[ax] // N_DEV) + shp[ax + 1:]
    return jnp.moveaxis(full.reshape(split), ax, 0).reshape(N_DEV, -1)


BIG = (("attn_w_in", 1), ("attn_w_out", 1), ("lru_w_in", 1), ("lru_w_a", 2), ("lru_w_x", 2), ("lru_w_out", 1))
SMALL = (("lru_conv_w", 2), ("lru_conv_b", 1), ("lru_b_a", 2), ("lru_b_x", 2), ("lru_lambda", 1))
REPL = ("norm_pre", "norm_post", "attn_sinks")
ORDER = ("norm_pre", "norm_post", "attn_w_in", "attn_w_out", "attn_sinks", "lru_w_in", "lru_conv_w", "lru_conv_b",
         "lru_w_a", "lru_b_a", "lru_w_x", "lru_b_x", "lru_lambda", "lru_w_out")


def kernel(x, norm_pre, norm_post, attn_w_in, attn_w_out, attn_sinks, lru_w_in, lru_conv_w, lru_conv_b, lru_w_a, lru_b_a, lru_w_x, lru_b_x, lru_lambda, lru_w_out, loss_target, m_norm_pre, m_norm_post, m_attn_w_in, m_attn_w_out, m_attn_sinks, m_lru_w_in, m_lru_conv_w, m_lru_conv_b, m_lru_w_a, m_lru_b_a, m_lru_w_x, m_lru_b_x, m_lru_lambda, m_lru_w_out, v_norm_pre, v_norm_post, v_attn_w_in, v_attn_w_out, v_attn_sinks, v_lru_w_in, v_lru_conv_w, v_lru_conv_b, v_lru_w_a, v_lru_b_a, v_lru_w_x, v_lru_b_x, v_lru_lambda, v_lru_w_out):
    W = dict(norm_pre=norm_pre, norm_post=norm_post, attn_w_in=attn_w_in, attn_w_out=attn_w_out, attn_sinks=attn_sinks,
             lru_w_in=lru_w_in, lru_conv_w=lru_conv_w, lru_conv_b=lru_conv_b, lru_w_a=lru_w_a, lru_b_a=lru_b_a,
             lru_w_x=lru_w_x, lru_b_x=lru_b_x, lru_lambda=lru_lambda, lru_w_out=lru_w_out)
    M = dict(norm_pre=m_norm_pre, norm_post=m_norm_post, attn_w_in=m_attn_w_in, attn_w_out=m_attn_w_out,
             attn_sinks=m_attn_sinks, lru_w_in=m_lru_w_in, lru_conv_w=m_lru_conv_w, lru_conv_b=m_lru_conv_b,
             lru_w_a=m_lru_w_a, lru_b_a=m_lru_b_a, lru_w_x=m_lru_w_x, lru_b_x=m_lru_b_x, lru_lambda=m_lru_lambda,
             lru_w_out=m_lru_w_out)
    V = dict(norm_pre=v_norm_pre, norm_post=v_norm_post, attn_w_in=v_attn_w_in, attn_w_out=v_attn_w_out,
             attn_sinks=v_attn_sinks, lru_w_in=v_lru_w_in, lru_conv_w=v_lru_conv_w, lru_conv_b=v_lru_conv_b,
             lru_w_a=v_lru_w_a, lru_b_a=v_lru_b_a, lru_w_x=v_lru_w_x, lru_b_x=v_lru_b_x, lru_lambda=v_lru_lambda,
             lru_w_out=v_lru_w_out)

    h0 = x[0]
    target = loss_target[0]
    T, D = h0.shape
    depth = norm_pre.shape[0]
    n_attn = attn_w_in.shape[0]
    Q = attn_w_out.shape[1] * N_DEV
    KV = Q // GROUP
    LW = lru_w_out.shape[1] * N_DEV
    nh = LW // LRU_BLOCK

    n_lru = lru_w_in.shape[0]
    big_names = [n for n, _ in BIG]
    small_names = [n for n, _ in SMALL]
    small_shapes = [W[n].shape for n in small_names]
    repl_shapes = [W[n].shape for n in REPL]

    flat = lambda a: a.reshape(-1)
    def layer_shards(layer):
        j = layer // 2
        names = ("attn_w_in", "attn_w_out") if layer % 2 == 0 else ("lru_w_in", "lru_w_a", "lru_w_x", "lru_w_out")
        return [W[n][j].astype(BF16) for n in names], [1 if n in ("lru_w_a", "lru_w_x") else 0 for n in names]

    def layer_weights(layer, gathered):
        if layer % 2 == 0:
            g_in, g_out = gathered
            w_in = jnp.moveaxis(g_in, 0, 1).reshape(D, -1)
            w_in = jnp.concatenate([w_in[:, :Q], w_in[:, Q + 2 * KV:], w_in[:, Q:Q + 2 * KV]], axis=-1)
            return dict(w_in=w_in, w_out=None if g_out is None else g_out.reshape(Q, D))
        g_in, g_wa, g_wx, g_out = gathered
        return dict(w_in=jnp.moveaxis(g_in, 0, 1).reshape(D, 2 * LW), w_a=g_wa.reshape(nh, LRU_BLOCK, LRU_BLOCK),
                    w_x=g_wx.reshape(nh, LRU_BLOCK, LRU_BLOCK), w_out=g_out.reshape(LW, D))

    arrs0, _ = layer_shards(0)
    first = _comm_call(_gather_comm([arrs0[0], _pack([flat(W[n]) for n in small_names], SUBLANES, F32)], [0, 0]),
                       name="gather_first")
    weights = {0: layer_weights(0, (first[0], None))}
    full = {}
    for (n, ax), seg in zip(SMALL, _unpack(first[-1], small_shapes)):
        full[n] = _to_full(seg, ax)
    cw_f = full["lru_conv_w"]
    cb_f = full["lru_conv_b"][:, None, :]
    ba_f = full["lru_b_a"].reshape(-1, 1, LW)
    bx_f = full["lru_b_x"].reshape(-1, 1, LW)
    lam_f = full["lru_lambda"][:, None, :]

    h = h0
    saved = []
    for layer in range(depth):
        j = layer // 2
        if layer == 0:
            u = _rms_fwd(h, norm_pre[0:1], name="rms_fwd")
        wl = weights[layer]
        nxt = _gather_comm(*layer_shards(layer + 1)) if layer + 1 < depth else None
        if layer % 2 == 0:
            if wl["w_out"] is None:
                proj, got = _matmul(u, wl["w_in"], out_dtype=BF16, comm=_gather_comm([arrs0[1]], [0]), name="attn_in")
                wl["w_out"] = got[0].reshape(Q, D)
            else:
                proj = _matmul(u, wl["w_in"], out_dtype=BF16, name="attn_in")
            (mix, ypre), got = _attn_fwd(proj, attn_sinks[j], comm=nxt, name="attn_fwd")
            y = _matmul(ypre, wl["w_out"], name="attn_out")
        else:
            proj = _matmul(u, wl["w_in"], out_dtype=BF16, name="lru_in")
            (mix, ypre), got = _lru_fwd(proj, wl["w_a"], wl["w_x"], ba_f[j], bx_f[j], cw_f[j], cb_f[j], lam_f[j],
                                        comm=nxt, name="lru_fwd")
            y = _matmul(ypre, wl["w_out"], name="lru_out")
        if nxt is not None:
            weights[layer + 1] = layer_weights(layer + 1, got)
        saved.append((h, u, proj, mix, ypre, y))
        g_next = norm_pre[layer + 1:layer + 2] if layer + 1 < depth else None
        h, u = _post_fwd(y, norm_post[layer:layer + 1], h, g_next, name="post_fwd")

    dh, loss_part = _loss_head(h, target, name="loss_head")

    g_pre = [None] * depth
    g_post = [None] * depth
    small_vec = [n for n in small_names] + ["attn_sinks"]
    grads = {n: [None] * W[n].shape[0] for n in small_vec}
    recv = {}

    def carried(keyed):
        return _exchange_comm([item for _, item in keyed]) if keyed else None

    def landed(keyed, got):
        for (key, _), r in zip(keyed, got):
            recv[key] = r

    pending = []
    for layer in reversed(range(depth)):
        j = layer // 2
        h_in, u, proj, mix, ypre, y = saved[layer]
        wl = weights[layer]
        dy, g_post[layer] = _rms_bwd(y, norm_post[layer:layer + 1], dh, None, out_dtype=BF16, name="post_bwd")
        if layer % 2 == 0:
            dyp = _matmul(dy, wl["w_out"], tb=True, out_dtype=BF16, name="attn_out_dx")
            dw_out = _matmul(ypre, dy, ta=True, out_dtype=BF16, name="attn_out_dw")
            pending.append((("attn_w_out", j), (dw_out.reshape(N_DEV, Q // N_DEV, D), False)))
            (dqg, dkv, dsink), got = _attn_bwd(proj, mix, dyp, attn_sinks[j], comm=carried(pending), name="attn_bwd")
            landed(pending, got)
            grads["attn_sinks"][j] = dsink[:, 0]
            dproj = jnp.concatenate([dqg, dkv], axis=1)
            dw = _matmul(u, dproj, ta=True, out_dtype=BF16, name="attn_in_dw")
            dw = jnp.concatenate([dw[:, :Q], dw[:, 2 * Q:], dw[:, Q:2 * Q]], axis=1)
            pending = [(("attn_w_in", j), (jnp.moveaxis(dw.reshape(D, N_DEV, -1), 1, 0), False))]
            if layer == 0:
                du, got = _matmul(dproj, wl["w_in"], tb=True, comm=carried(pending), name="attn_in_dx")
                landed(pending, got)
                pending = []
            else:
                du = _matmul(dproj, wl["w_in"], tb=True, name="attn_in_dx")
        else:
            dyp = _matmul(dy, wl["w_out"], tb=True, out_dtype=BF16, name="lru_out_dx")
            dw_out = _matmul(ypre, dy, ta=True, out_dtype=BF16, name="lru_out_dw")
            pending.append((("lru_w_out", j), (dw_out.reshape(N_DEV, LW // N_DEV, D), False)))
            (dxb, dgt, dwa, dwx, dba, dbx, dcw, dcb, dlam), got = _lru_bwd(
                proj, mix, dyp, wl["w_a"], wl["w_x"], ba_f[j], bx_f[j], cw_f[j], cb_f[j], lam_f[j],
                comm=carried(pending), name="lru_bwd")
            landed(pending, got)
            grads["lru_b_a"][j], grads["lru_b_x"][j] = dba.reshape(nh, LRU_BLOCK), dbx.reshape(nh, LRU_BLOCK)
            grads["lru_conv_w"][j], grads["lru_conv_b"][j], grads["lru_lambda"][j] = dcw, dcb[0], dlam[0]
            dproj = jnp.concatenate([dxb, dgt], axis=1)
            du = _matmul(dproj, wl["w_in"], tb=True, name="lru_in_dx")
            mine = [(("lru_w_a", j), (dwa, True)), (("lru_w_x", j), (dwx, True))]
            dw, got = _matmul(u, dproj, ta=True, out_dtype=BF16, by_owner=True, comm=carried(mine), name="lru_in_dw")
            landed(mine, got)
            pending = [(("lru_w_in", j), (dw, False))]
        dh, g_pre[layer] = _rms_bwd(h_in, norm_pre[layer:layer + 1], du, dh, out_dtype=F32, name="pre_bwd")

    gfull = {n: jnp.stack(g) for n, g in grads.items()}
    gfull["norm_pre"] = jnp.concatenate(g_pre, axis=0)
    gfull["norm_post"] = jnp.concatenate(g_post, axis=0)

    repl_part = [jnp.broadcast_to(gfull[n].reshape(1, -1), (N_DEV, gfull[n].size)) for n in REPL]
    loss_slot = jnp.broadcast_to(loss_part.reshape(1, 1), (N_DEV, 1))
    send_small = _pack([_to_shards(gfull[n], ax) for n, ax in SMALL] + repl_part + [loss_slot], SUBLANES, F32)
    last = pending + [(("small", 0), (send_small, False))]
    landed(last, _comm_call(carried(last), name="exchange_last"))

    zero1 = jnp.zeros((1,), F32)
    outs = {}
    for n in big_names:
        shp = W[n].shape
        as3 = lambda a: a.reshape((shp[0], -1, shp[-1]))
        parts = [recv[n, j].reshape((N_DEV, -1, shp[-1])) for j in range(shp[0])]
        res = _adamw(parts, as3(W[n]), as3(M[n]), as3(V[n]), name="adamw_" + n)
        for kind, a in zip(("grad", "delta", "new_m", "new_v"), res):
            outs[kind, n] = a.reshape(shp)
    res_small = _adamw([recv["small", 0]],
                       *[_pack([flat(S[n]) for n in small_names] + [flat(S[n]) for n in REPL] + [zero1], SUBLANES, F32)[None]
                         for S in (W, M, V)], name="adamw_small")
    for kind, rs in zip(("grad", "delta", "new_m", "new_v"), res_small):
        for n, a in zip(small_names + list(REPL) + ["loss"], _unpack(rs[0], small_shapes + repl_shapes + [(1,)])):
            outs[kind, n] = a
    loss = outs["grad", "loss"][0]
    result = [loss, dh[None]]
    for kind in ("grad", "delta", "new_m", "new_v"):
        result += [outs[kind, n] for n in ORDER]
    return tuple(result)
```

```python
import math
from typing import Callable, NamedTuple

import jax
import jax.numpy as jnp
from jax import lax
from jax.experimental import pallas as pl
from jax.experimental.pallas import tpu as pltpu

F32 = jnp.float32
BF16 = jnp.bfloat16

N_DEV = 8
HEAD_DIM = 64
GROUP = 8
WINDOW = 128
LRU_BLOCK = 256
CONV_W = 4
C_RG = 8.0
NORM_EPS = 1e-6
MASK_VALUE = -1e30

ADAM_LR = 0.001
ADAM_B1 = 0.9
ADAM_B2 = 0.999
ADAM_EPS = 1e-08
ADAM_WD = 0.01
ADAM_STEP = 10

ROW_BLOCK = 256
LANES = 128
SUBLANES = 8
PACK_W = 1024
VMEM_LIMIT = 56 * 1024 * 1024
MATMUL_VMEM = 36 * 1024 * 1024
ADAMW_VMEM = 24 * 1024 * 1024
MESH = pl.DeviceIdType.MESH


def _cparams(sem=None):
    return pltpu.CompilerParams(dimension_semantics=sem, vmem_limit_bytes=VMEM_LIMIT)


class Comm(NamedTuple):
    arrays: list
    out_shapes: list
    sems: list
    start: Callable
    mid: Callable
    finish: Callable


def _call(body, *, name, grid, in_specs, out_specs, out_shape, scratch_shapes, semantics, args, comm=None):
    if comm is None:
        res = pl.pallas_call(body, name=name, grid=grid, in_specs=in_specs, out_specs=out_specs, out_shape=out_shape,
                             scratch_shapes=scratch_shapes, compiler_params=_cparams(semantics))(*args)
        return list(res), []
    n_in, n_out, n_scr = len(in_specs), len(out_specs), len(scratch_shapes)
    ci, co = len(comm.arrays), len(comm.out_shapes)
    steps = math.prod(grid)

    def hosted(*refs):
        ins, cins = refs[:n_in], refs[n_in:n_in + ci]
        o0 = n_in + ci
        outs, couts = refs[o0:o0 + n_out], refs[o0 + n_out:o0 + n_out + co]
        s0 = o0 + n_out + co
        scr, sems = refs[s0:s0 + n_scr], refs[s0 + n_scr:]
        step = 0
        for ax, g in enumerate(grid):
            step = step * g + pl.program_id(ax)

        @pl.when(step == 0)
        def _():
            comm.start(cins, couts, *sems)

        body(*ins, *outs, *scr)

        @pl.when(step == steps // 2)
        def _():
            comm.mid(cins, couts, *sems)

        @pl.when(step == steps - 1)
        def _():
            comm.finish(cins, couts, *sems)

    any_spec = pl.BlockSpec(memory_space=pl.ANY)
    res = pl.pallas_call(
        hosted, name=name, grid=grid,
        in_specs=list(in_specs) + [any_spec] * ci, out_specs=list(out_specs) + [any_spec] * co,
        out_shape=list(out_shape) + list(comm.out_shapes), scratch_shapes=list(scratch_shapes) + list(comm.sems),
        compiler_params=_cparams(("arbitrary",) * len(grid)),
    )(*args, *comm.arrays)
    return list(res[:n_out]), list(res[n_out:])


def _pick(n, target, quantum):
    best = None
    for t in range(quantum, min(n, target) + 1, quantum):
        if n % t == 0:
            best = t
    return n if best is None else best


def _sigmoid(x):
    return 1.0 / (1.0 + jnp.exp(-x))


def _dot(a, b):
    return lax.dot_general(a, b, (((1,), (0,)), ((), ())), preferred_element_type=F32)


def _dot_nt(a, b):
    return lax.dot_general(a, b, (((1,), (1,)), ((), ())), preferred_element_type=F32)


def _dot_tn(a, b):
    return lax.dot_general(a, b, (((0,), (0,)), ((), ())), preferred_element_type=F32)


def _matmul(a, b, *, ta=False, tb=False, out_dtype=F32, by_owner=False, comm=None, name):
    if ta:
        K, M = a.shape
    else:
        M, K = a.shape
    if tb:
        N, K2 = b.shape
    else:
        K2, N = b.shape
    assert K == K2, (a.shape, b.shape, ta, tb)
    tn = N // N_DEV if by_owner else _pick(N, 1536, 256)
    assert tn % LANES == 0
    osz = jnp.dtype(out_dtype).itemsize

    def vmem_bytes(tm, tk):
        acc = 0 if tk == K else tm * tn * 4
        return 2 * (tm * tk * a.dtype.itemsize + tk * tn * b.dtype.itemsize + tm * tn * osz) + acc

    def deepest(tm):
        return max([t for t in range(256, K + 1, 256) if K % t == 0 and vmem_bytes(tm, t) <= MATMUL_VMEM] or [_pick(K, 512, 256)])

    tm = _pick(M, 1024, 256)
    if deepest(tm) < K and tm % 512 == 0 and deepest(tm // 2) == K:
        tm //= 2
    tk = deepest(tm)
    nk = K // tk
    dot = {(False, False): _dot, (False, True): _dot_nt, (True, False): _dot_tn}[(ta, tb)]

    if nk == 1:
        def body(a_ref, b_ref, o_ref):
            o_ref[...] = dot(a_ref[...].astype(BF16), b_ref[...].astype(BF16)).astype(o_ref.dtype)
        scratch = []
    else:
        def body(a_ref, b_ref, o_ref, acc_ref):
            k = pl.program_id(2)

            @pl.when(k == 0)
            def _():
                acc_ref[...] = jnp.zeros_like(acc_ref)

            acc_ref[...] += dot(a_ref[...].astype(BF16), b_ref[...].astype(BF16))

            @pl.when(k == nk - 1)
            def _():
                o_ref[...] = acc_ref[...].astype(o_ref.dtype)
        scratch = [pltpu.VMEM((tm, tn), F32)]

    a_spec = pl.BlockSpec((tk, tm), lambda j, i, k: (k, i)) if ta else pl.BlockSpec((tm, tk), lambda j, i, k: (i, k))
    b_spec = pl.BlockSpec((tn, tk), lambda j, i, k: (j, k)) if tb else pl.BlockSpec((tk, tn), lambda j, i, k: (k, j))
    if by_owner:
        o_spec = pl.BlockSpec((None, tm, tn), lambda j, i, k: (j, i, 0))
        o_shape = jax.ShapeDtypeStruct((N_DEV, M, tn), out_dtype)
    else:
        o_spec = pl.BlockSpec((tm, tn), lambda j, i, k: (i, j))
        o_shape = jax.ShapeDtypeStruct((M, N), out_dtype)
    res, extra = _call(body, name=name, grid=(N // tn, M // tm, nk), in_specs=[a_spec, b_spec], out_specs=[o_spec],
                       out_shape=[o_shape], scratch_shapes=scratch, semantics=("parallel", "parallel", "arbitrary"),
                       args=(a, b), comm=comm)
    return res[0] if comm is None else (res[0], extra)


def _rms_fwd(h, g, *, comm=None, name):
    T, D = h.shape
    tm = _pick(T, ROW_BLOCK, SUBLANES)

    def body(h_ref, g_ref, u_ref):
        x = h_ref[...]
        r = lax.rsqrt(jnp.mean(x * x, axis=-1, keepdims=True) + NORM_EPS)
        u_ref[...] = ((x * r) * g_ref[...]).astype(u_ref.dtype)

    return _call(
        body, name=name, grid=(T // tm,),
        in_specs=[pl.BlockSpec((tm, D), lambda i: (i, 0)), pl.BlockSpec((1, D), lambda i: (0, 0))],
        out_specs=[pl.BlockSpec((tm, D), lambda i: (i, 0))],
        out_shape=[jax.ShapeDtypeStruct((T, D), BF16)],
        scratch_shapes=[], semantics=("parallel",), args=(h, g), comm=comm)


def _post_fwd(y, g, h, g_next, *, name):
    T, D = y.shape
    tm = _pick(T, ROW_BLOCK, SUBLANES)

    def body(y_ref, g_ref, h_ref, gn_ref, o_ref, u_ref):
        x = y_ref[...]
        r = lax.rsqrt(jnp.mean(x * x, axis=-1, keepdims=True) + NORM_EPS)
        ho = h_ref[...] + (x * r) * g_ref[...]
        o_ref[...] = ho
        rn = lax.rsqrt(jnp.mean(ho * ho, axis=-1, keepdims=True) + NORM_EPS)
        u_ref[...] = ((ho * rn) * gn_ref[...]).astype(u_ref.dtype)

    row = pl.BlockSpec((tm, D), lambda i: (i, 0))
    vec = pl.BlockSpec((1, D), lambda i: (0, 0))
    return pl.pallas_call(
        body, name=name, grid=(T // tm,),
        in_specs=[row, vec, row, vec],
        out_specs=[row, row],
        out_shape=[jax.ShapeDtypeStruct((T, D), F32), jax.ShapeDtypeStruct((T, D), BF16)],
        compiler_params=_cparams(("parallel",)),
    )(y, g, h, g_next)


def _rms_bwd(x, g, dz, res, *, out_dtype, name):
    T, D = x.shape
    tm = _pick(T, ROW_BLOCK, SUBLANES)
    has_res = res is not None

    def body(*refs):
        if has_res:
            x_ref, g_ref, dz_ref, res_ref, dx_ref, dg_ref = refs
        else:
            x_ref, g_ref, dz_ref, dx_ref, dg_ref = refs
        i = pl.program_id(0)

        @pl.when(i == 0)
        def _():
            dg_ref[...] = jnp.zeros_like(dg_ref)

        xv = x_ref[...]
        dzv = dz_ref[...]
        r = lax.rsqrt(jnp.mean(xv * xv, axis=-1, keepdims=True) + NORM_EPS)
        xhat = xv * r
        dg_ref[...] += jnp.sum(dzv * xhat, axis=0, keepdims=True)
        dxh = dzv * g_ref[...]
        dx = r * (dxh - xhat * jnp.mean(dxh * xhat, axis=-1, keepdims=True))
        if has_res:
            dx = dx + res_ref[...]
        dx_ref[...] = dx.astype(dx_ref.dtype)

    row = pl.BlockSpec((tm, D), lambda i: (i, 0))
    vec = pl.BlockSpec((1, D), lambda i: (0, 0))
    ins = [x, g, dz] + ([res] if has_res else [])
    return pl.pallas_call(
        body, name=name, grid=(T // tm,),
        in_specs=[row, vec, row] + ([row] if has_res else []),
        out_specs=[row, vec],
        out_shape=[jax.ShapeDtypeStruct((T, D), out_dtype), jax.ShapeDtypeStruct((1, D), F32)],
        compiler_params=_cparams(("arbitrary",)),
    )(*ins)


def _post_loss(y, g, h, target, *, name):
    T, D = h.shape
    tm = _pick(T, ROW_BLOCK, SUBLANES)

    def body(y_ref, g_ref, h_ref, t_ref, dh_ref, l_ref):
        i = pl.program_id(0)

        @pl.when(i == 0)
        def _():
            l_ref[...] = jnp.zeros_like(l_ref)

        x = y_ref[...]
        r = lax.rsqrt(jnp.mean(x * x, axis=-1, keepdims=True) + NORM_EPS)
        e = (h_ref[...] + (x * r) * g_ref[...]) - t_ref[...]
        dh_ref[...] = e * (1.0 / D)
        row = jnp.sum(e * e, axis=-1, keepdims=True) * (0.5 / D)
        l_ref[...] += jnp.sum(row, axis=0, keepdims=True)

    row = pl.BlockSpec((tm, D), lambda i: (i, 0))
    return pl.pallas_call(
        body, name=name, grid=(T // tm,),
        in_specs=[row, pl.BlockSpec((1, D), lambda i: (0, 0)), row, row],
        out_specs=[row, pl.BlockSpec((1, 1), lambda i: (0, 0))],
        out_shape=[jax.ShapeDtypeStruct((T, D), F32), jax.ShapeDtypeStruct((1, 1), F32)],
        compiler_params=_cparams(("arbitrary",)),
    )(y, g, h, target)


def _attn_dims(P):
    Q = P * 4 // 9
    KV = Q // GROUP
    assert 2 * Q + 2 * KV == P and KV % LANES == 0
    return Q, KV


def _attn_specs(Q, KV, nb):
    blk = WINDOW
    q_spec = pl.BlockSpec((blk, Q), lambda i: (jnp.minimum(i, nb - 1), 0))
    g_spec = pl.BlockSpec((blk, Q), lambda i: (jnp.minimum(i, nb - 1), 1))
    kvc_spec = pl.BlockSpec((blk, 2 * KV), lambda i: (jnp.minimum(i, nb - 1), Q // KV))
    kvp_spec = pl.BlockSpec((blk, 2 * KV), lambda i: (jnp.maximum(jnp.minimum(i, nb - 1) - 1, 0), Q // KV))
    return q_spec, g_spec, kvc_spec, kvp_spec


PAIRS = GROUP // 2
STACK = PAIRS * WINDOW


def _band_mask(i):
    c = lax.broadcasted_iota(jnp.int32, (2 * WINDOW, STACK), 0)
    r = lax.broadcasted_iota(jnp.int32, (2 * WINDOW, STACK), 1) & (WINDOW - 1)
    first_key = jnp.where(i > 0, 0, WINDOW)
    return (c > r) & (c <= r + WINDOW) & (c >= first_key)


def _group_cols(kvh):
    c0 = kvh * GROUP * HEAD_DIM
    return [slice(c0 + j * LANES, c0 + (j + 1) * LANES) for j in range(PAIRS)]


def _stack(ref, cols, scale=None):
    x = jnp.concatenate([ref[:, cs] for cs in cols], axis=0).astype(F32)
    return x if scale is None else x * scale


def _group_sinks(sink_ref, kvh, half):
    return jnp.concatenate([jnp.full((1, WINDOW), sink_ref[kvh * GROUP + 2 * j + half], F32) for j in range(PAIRS)], axis=1)


def _pair_halves(x128, xt128, e):
    lo = lax.broadcasted_iota(jnp.int32, x128.shape, 1) < HEAD_DIM
    lo_t = lax.broadcasted_iota(jnp.int32, xt128.shape, 0) < HEAD_DIM
    if e == 0:
        x_lo, xt_lo = jnp.where(lo, x128, 0.0), jnp.where(lo_t, xt128, 0.0)
        x_hi, xt_hi = pltpu.roll(x_lo, HEAD_DIM, 1), pltpu.roll(xt_lo, HEAD_DIM, 0)
    else:
        x_hi, xt_hi = jnp.where(lo, 0.0, x128), jnp.where(lo_t, 0.0, xt128)
        x_lo, xt_lo = pltpu.roll(x_hi, HEAD_DIM, 1), pltpu.roll(xt_hi, HEAD_DIM, 0)
    return (x_lo.astype(BF16), x_hi.astype(BF16)), (xt_lo.astype(BF16), xt_hi.astype(BF16))


def _softmax_sink(st, allowed, sink):
    st = jnp.where(allowed, st, MASK_VALUE)
    m = jnp.maximum(jnp.max(st, axis=0, keepdims=True), sink)
    p = jnp.exp(st - m)
    es = jnp.exp(sink - m)
    inv = 1.0 / (jnp.sum(p, axis=0, keepdims=True) + es)
    return p * inv, es * inv


def _attn_fwd(proj, sinks, *, comm=None, name):
    T, P = proj.shape
    Q, KV = _attn_dims(P)
    nb = T // WINDOW
    npairs = KV // LANES
    scale = 1.0 / math.sqrt(HEAD_DIM)

    def body(sink_ref, q_ref, g_ref, kvc_ref, kvp_ref, out_ref, yp_ref):
        i = pl.program_id(0)
        allowed = _band_mask(i)
        for p in range(npairs):
            ks = slice(p * LANES, (p + 1) * LANES)
            vs = slice(KV + p * LANES, KV + (p + 1) * LANES)
            k128 = jnp.concatenate([kvp_ref[:, ks], kvc_ref[:, ks]], axis=0).astype(F32)
            v128 = jnp.concatenate([kvp_ref[:, vs], kvc_ref[:, vs]], axis=0).astype(F32)
            kt128, vt128 = k128.T, v128.T
            for e in range(2):
                kvh = 2 * p + e
                khalf, _ = _pair_halves(k128, kt128, e)
                _, vthalf = _pair_halves(v128, vt128, e)
                cols = _group_cols(kvh)
                q4 = _stack(q_ref, cols, scale).astype(BF16)
                ot = None
                for half in range(2):
                    st = _dot_nt(khalf[half], q4)
                    pn, _ = _softmax_sink(st, allowed, _group_sinks(sink_ref, kvh, half))
                    o = _dot(vthalf[half], pn.astype(BF16))
                    ot = o if ot is None else ot + o
                o4 = ot.T
                g4 = _stack(g_ref, cols)
                y4 = (o4 * (g4 * _sigmoid(g4))).astype(BF16)
                for j, cs in enumerate(cols):
                    out_ref[:, cs] = o4[j * WINDOW:(j + 1) * WINDOW]
                    yp_ref[:, cs] = y4[j * WINDOW:(j + 1) * WINDOW]

    q_spec, g_spec, kvc_spec, kvp_spec = _attn_specs(Q, KV, nb)
    row = pl.BlockSpec((WINDOW, Q), lambda i: (i, 0))
    return _call(
        body, name=name, grid=(nb,),
        in_specs=[pl.BlockSpec(memory_space=pltpu.SMEM), q_spec, g_spec, kvc_spec, kvp_spec],
        out_specs=[row, row],
        out_shape=[jax.ShapeDtypeStruct((T, Q), F32), jax.ShapeDtypeStruct((T, Q), BF16)],
        scratch_shapes=[], semantics=("parallel",), args=(sinks, proj, proj, proj, proj), comm=comm)


def _attn_bwd(proj, out, dyp, sinks, *, comm=None, name):
    T, P = proj.shape
    Q, KV = _attn_dims(P)
    nb = T // WINDOW
    npairs = KV // LANES
    H = Q // HEAD_DIM
    scale = 1.0 / math.sqrt(HEAD_DIM)

    def body(sink_ref, q_ref, g_ref, kvc_ref, kvp_ref, out_ref, dyp_ref, dqg_ref, dkv_ref, dsink_ref, carry_ref):
        i = pl.program_id(0)

        @pl.when(i == 0)
        def _():
            carry_ref[...] = jnp.zeros_like(carry_ref)
            dsink_ref[...] = jnp.zeros_like(dsink_ref)

        @pl.when(i == nb)
        def _():
            dkv_ref[...] = carry_ref[...].astype(dkv_ref.dtype)

        @pl.when(i < nb)
        def _():
            allowed = _band_mask(i)
            lo = lax.broadcasted_iota(jnp.int32, (2 * WINDOW, LANES), 1) < HEAD_DIM
            sel_lane = lax.broadcasted_iota(jnp.int32, (SUBLANES, LANES), 1) < HEAD_DIM
            sels = (jnp.where(sel_lane, 1.0, 0.0).astype(BF16), jnp.where(sel_lane, 0.0, 1.0).astype(BF16))
            for p in range(npairs):
                ks = slice(p * LANES, (p + 1) * LANES)
                vs = slice(KV + p * LANES, KV + (p + 1) * LANES)
                k128 = jnp.concatenate([kvp_ref[:, ks], kvc_ref[:, ks]], axis=0).astype(F32)
                v128 = jnp.concatenate([kvp_ref[:, vs], kvc_ref[:, vs]], axis=0).astype(F32)
                kt128, vt128 = k128.T, v128.T
                dk_e, dv_e = [], []
                for e in range(2):
                    kvh = 2 * p + e
                    khalf, kthalf = _pair_halves(k128, kt128, e)
                    vhalf, _ = _pair_halves(v128, vt128, e)
                    cols = _group_cols(kvh)
                    q4 = _stack(q_ref, cols, scale).astype(BF16)
                    g4 = _stack(g_ref, cols)
                    o4 = _stack(out_ref, cols)
                    dy4 = _stack(dyp_ref, cols)
                    sg = _sigmoid(g4)
                    do4 = dy4 * (g4 * sg)
                    dg4 = (dy4 * o4 * (sg * (1.0 + g4 * (1.0 - sg)))).astype(dqg_ref.dtype)
                    dod = do4 * o4
                    dod_hi = dod.astype(BF16)
                    dod_lo = (dod - dod_hi.astype(F32)).astype(BF16)
                    do4b = do4.astype(BF16)
                    dqt = None
                    dk_h, dv_h = [], []
                    for half in range(2):
                        delta = jnp.max(_dot_nt(sels[half], dod_hi) + _dot_nt(sels[half], dod_lo), axis=0, keepdims=True)
                        st = _dot_nt(khalf[half], q4)
                        pn, psink = _softmax_sink(st, allowed, _group_sinks(sink_ref, kvh, half))
                        dp = _dot_nt(vhalf[half], do4b)
                        ds = (pn * (dp - delta)).astype(BF16)
                        dq = _dot(kthalf[half], ds)
                        dqt = dq if dqt is None else dqt + dq
                        dk_h.append(_dot(ds, q4))
                        dv_h.append(_dot(pn.astype(BF16), do4b))
                        pd = psink * delta
                        for j in range(PAIRS):
                            n = kvh * GROUP + 2 * j + half
                            dsn = -jnp.sum(pd[:, j * WINDOW:(j + 1) * WINDOW], axis=1, keepdims=True)
                            dsink_ref[n:n + 1, :] += jnp.broadcast_to(dsn, (1, LANES))
                    dq4 = (dqt.T * scale).astype(dqg_ref.dtype)
                    for j, cs in enumerate(cols):
                        dqg_ref[:, cs] = dq4[j * WINDOW:(j + 1) * WINDOW]
                        dqg_ref[:, slice(Q + cs.start, Q + cs.stop)] = dg4[j * WINDOW:(j + 1) * WINDOW]
                    acc_k = jnp.where(lo, dk_h[0], dk_h[1])
                    acc_v = jnp.where(lo, dv_h[0], dv_h[1])
                    dk_e.append(acc_k + pltpu.roll(acc_k, HEAD_DIM, 1))
                    dv_e.append(acc_v + pltpu.roll(acc_v, HEAD_DIM, 1))
                for sl, de in ((ks, dk_e), (vs, dv_e)):
                    d128 = jnp.where(lo, de[0], de[1])
                    dkv_ref[:, sl] = (carry_ref[:, sl] + d128[:WINDOW]).astype(dkv_ref.dtype)
                    carry_ref[:, sl] = d128[WINDOW:]

    q_spec, g_spec, kvc_spec, kvp_spec = _attn_specs(Q, KV, nb)
    last = lambda i: (jnp.minimum(i, nb - 1), 0)
    row = pl.BlockSpec((WINDOW, Q), last)
    return _call(
        body, name=name, grid=(nb + 1,),
        in_specs=[pl.BlockSpec(memory_space=pltpu.SMEM), q_spec, g_spec, kvc_spec, kvp_spec, row, row],
        out_specs=[pl.BlockSpec((WINDOW, 2 * Q), last),
                   pl.BlockSpec((WINDOW, 2 * KV), lambda i: (jnp.maximum(i - 1, 0), 0)),
                   pl.BlockSpec((H, LANES), lambda i: (0, 0))],
        out_shape=[jax.ShapeDtypeStruct((T, 2 * Q), BF16), jax.ShapeDtypeStruct((T, 2 * KV), BF16),
                   jax.ShapeDtypeStruct((H, LANES), F32)],
        scratch_shapes=[pltpu.VMEM((WINDOW, 2 * KV), F32)],
        semantics=("arbitrary",), args=(sinks, proj, proj, proj, proj, out, dyp), comm=comm)


LRU_CHUNK = 256


def _shift_down(x, halo8, s):
    if s == 0:
        return x
    row8 = lax.broadcasted_iota(jnp.int32, (SUBLANES, 1), 0)
    r = pltpu.roll(x, s, 0)
    top = jnp.where(row8 < s, pltpu.roll(halo8, s, 0), r[:SUBLANES])
    return jnp.concatenate([top, r[SUBLANES:]], axis=0)


def _shift_up(x, halo8, s):
    if s == 0:
        return x
    n = x.shape[0]
    row8 = lax.broadcasted_iota(jnp.int32, (SUBLANES, 1), 0)
    r = pltpu.roll(x, n - s, 0)
    bot = jnp.where(row8 >= SUBLANES - s, pltpu.roll(halo8, SUBLANES - s, 0), r[n - SUBLANES:])
    return jnp.concatenate([r[:n - SUBLANES], bot], axis=0)


def _scan_fwd(a, b, c0):
    n = a.shape[0]
    row = lax.broadcasted_iota(jnp.int32, (n, 1), 0) & (SUBLANES - 1)
    s = 1
    while s < SUBLANES:
        keep = row >= s
        ar = jnp.where(keep, pltpu.roll(a, s, 0), 1.0)
        br = jnp.where(keep, pltpu.roll(b, s, 0), 0.0)
        b = a * br + b
        a = a * ar
        s *= 2
    out, c = [], c0
    for i in range(n // SUBLANES):
        rows = slice(i * SUBLANES, (i + 1) * SUBLANES)
        h = a[rows] * c + b[rows]
        out.append(h)
        c = h[SUBLANES - 1:]
    return jnp.concatenate(out, axis=0)


def _scan_rev(al, b, c0):
    n = al.shape[0]
    row = lax.broadcasted_iota(jnp.int32, (n, 1), 0) & (SUBLANES - 1)
    s = 1
    while s < SUBLANES:
        keep = row < SUBLANES - s
        ar = jnp.where(keep, pltpu.roll(al, n - s, 0), 1.0)
        br = jnp.where(keep, pltpu.roll(b, n - s, 0), 0.0)
        b = b + al * br
        al = al * ar
        s *= 2
    out, c = [], c0
    for i in reversed(range(n // SUBLANES)):
        rows = slice(i * SUBLANES, (i + 1) * SUBLANES)
        l = b[rows] + al[rows] * c
        out.append(l)
        c = l[:1]
    return jnp.concatenate(out[::-1], axis=0)


def _log1p_pos(z):
    return jnp.where(z < 0.01, z * (1.0 - z * (0.5 - z * (1.0 / 3.0))), jnp.log(1.0 + z))


def _one_minus_sq(a, log_a):
    x = 2.0 * log_a
    series = -x * (1.0 + x * (0.5 + x * (1.0 / 6.0)))
    return jnp.where(x > -0.02, series, 1.0 - a * a)


def _softplus_neg(lam):
    return jnp.maximum(-lam, 0.0) + _log1p_pos(jnp.exp(-jnp.abs(lam)))


def _lru_gates(xb, halo, wa, wx, ba, bx, cw_ref, cb, lam, with_inverse=False):
    xs = [_shift_down(xb, halo, s) for s in range(CONV_W)]
    xc = cb + xs[3] * cw_ref[0:1, :] + xs[2] * cw_ref[1:2, :] + xs[1] * cw_ref[2:3, :] + xs[0] * cw_ref[3:4, :]
    xcb = xc.astype(BF16)
    r = _sigmoid(_dot(xcb, wa) + ba)
    ig = _sigmoid(_dot(xcb, wx) + bx)
    sp = _softplus_neg(lam)
    log_a = (-C_RG * r) * sp
    a = jnp.exp(log_a)
    z = _one_minus_sq(a, log_a)
    if not with_inverse:
        return xs, xc, xcb, r, ig, sp, a, jnp.sqrt(z), None
    rmult = lax.rsqrt(z)
    return xs, xc, xcb, r, ig, sp, a, z * rmult, rmult


def _tile_rows(dtype):
    return SUBLANES * 4 // jnp.dtype(dtype).itemsize


def _last_rows(ref):
    return ref[...].astype(F32)[ref.shape[0] - SUBLANES:]


def _lru_specs(nh, nt, tc, rev):
    tix = (lambda t: nt - 1 - t) if rev else (lambda t: t)
    chunk = lambda off: pl.BlockSpec((tc, LRU_BLOCK), lambda h, t: (tix(t), h + off))
    prev8 = lambda off, rows: pl.BlockSpec((rows, LRU_BLOCK),
                                           lambda h, t: (jnp.maximum(tix(t) * (tc // rows) - 1, 0), h + off))
    wblk = pl.BlockSpec((None, LRU_BLOCK, LRU_BLOCK), lambda h, t: (h, 0, 0))
    vec = pl.BlockSpec((1, LRU_BLOCK), lambda h, t: (0, h))
    cwb = pl.BlockSpec((CONV_W, LRU_BLOCK), lambda h, t: (0, h))
    return tix, chunk, prev8, wblk, vec, cwb


def _lru_fwd(proj, wa, wx, ba, bx, cw, cb, lam, *, comm=None, name):
    T, W2 = proj.shape
    W = W2 // 2
    nh = W // LRU_BLOCK
    tc = _pick(T, LRU_CHUNK, SUBLANES)
    nt = T // tc

    def body(xb_ref, xh_ref, gt_ref, wa_ref, wx_ref, ba_ref, bx_ref, cw_ref, cb_ref, lam_ref, hs_ref, yp_ref, carry_ref):
        t = pl.program_id(1)

        @pl.when(t == 0)
        def _():
            carry_ref[...] = jnp.zeros_like(carry_ref)

        halo = jnp.where(t > 0, _last_rows(xh_ref), 0.0)
        _, xc, _, _, ig, _, a, mult, _ = _lru_gates(xb_ref[...].astype(F32), halo, wa_ref[...], wx_ref[...], ba_ref[...],
                                                  bx_ref[...], cw_ref, cb_ref[...], lam_ref[...])
        hs = _scan_fwd(a, mult * (ig * xc), carry_ref[SUBLANES - 1:SUBLANES, :])
        hs_ref[...] = hs
        carry_ref[...] = hs[tc - SUBLANES:]
        g = gt_ref[...].astype(F32)
        yp_ref[...] = (hs * (g * _sigmoid(g))).astype(BF16)

    _, chunk, prev8, wblk, vec, cwb = _lru_specs(nh, nt, tc, False)
    return _call(
        body, name=name, grid=(nh, nt),
        in_specs=[chunk(0), prev8(0, _tile_rows(proj.dtype)), chunk(nh), wblk, wblk, vec, vec, cwb, vec, vec],
        out_specs=[chunk(0), chunk(0)],
        out_shape=[jax.ShapeDtypeStruct((T, W), F32), jax.ShapeDtypeStruct((T, W), BF16)],
        scratch_shapes=[pltpu.VMEM((SUBLANES, LRU_BLOCK), F32)],
        semantics=("parallel", "arbitrary"), args=(proj, proj, proj, wa, wx, ba, bx, cw, cb, lam), comm=comm)


def _lru_bwd(proj, hs, dyp, wa, wx, ba, bx, cw, cb, lam, *, comm=None, name):
    T, W2 = proj.shape
    W = W2 // 2
    nh = W // LRU_BLOCK
    tc = _pick(T, LRU_CHUNK, SUBLANES)
    nt = T // tc

    def body(xb_ref, xh_ref, gt_ref, hs_ref, hh_ref, dyp_ref, wa_ref, wx_ref, ba_ref, bx_ref, cw_ref, cb_ref, lam_ref,
             dx_ref, dg_ref, dwa_ref, dwx_ref, dba_ref, dbx_ref, dcw_ref, dcb_ref, dlam_ref,
             ca_ref, cl_ref, cx_ref):
        t = pl.program_id(1)
        first = t == nt - 1

        @pl.when(t == 0)
        def _():
            for ref in (ca_ref, cl_ref, cx_ref, dwa_ref, dwx_ref, dba_ref, dbx_ref, dcw_ref, dcb_ref, dlam_ref):
                ref[...] = jnp.zeros_like(ref)

        xb = xb_ref[...].astype(F32)
        halo = jnp.where(first, 0.0, _last_rows(xh_ref))
        wa = wa_ref[...]
        wx = wx_ref[...]
        lam = lam_ref[...]
        xs, xc, xcb, r, ig, sp, a, mult, rmult = _lru_gates(xb, halo, wa, wx, ba_ref[...], bx_ref[...], cw_ref, cb_ref[...], lam,
                                                            with_inverse=True)
        hsv = hs_ref[...]
        g = gt_ref[...].astype(F32)
        dy = dyp_ref[...].astype(F32)
        sg = _sigmoid(g)
        dg_ref[...] = (dy * hsv * (sg * (1.0 + g * (1.0 - sg)))).astype(dg_ref.dtype)
        dhs = dy * (g * sg)

        al = _shift_up(a, ca_ref[...], 1)
        lmb = _scan_rev(al, dhs, cl_ref[0:1, :])
        hprev = _shift_down(hsv, jnp.where(first, 0.0, _last_rows(hh_ref)), 1)
        da = lmb * hprev
        ixc = ig * xc
        dmult = lmb * ixc
        dlog_a = a * (da - dmult * a * rmult)
        dr = dlog_a * (-C_RG * sp)
        dlam_ref[...] += jnp.sum(dlog_a * r, axis=0, keepdims=True) * (C_RG * _sigmoid(-lam))
        dpa = dr * (r * (1.0 - r))
        dpx = (lmb * mult * xc) * (ig * (1.0 - ig))
        dpab = dpa.astype(BF16)
        dpxb = dpx.astype(BF16)
        dwa_ref[...] += _dot_tn(xcb, dpab)
        dwx_ref[...] += _dot_tn(xcb, dpxb)
        dba_ref[...] += jnp.sum(dpa, axis=0, keepdims=True)
        dbx_ref[...] += jnp.sum(dpx, axis=0, keepdims=True)
        dxc = lmb * mult * ig + _dot_nt(dpab, wa) + _dot_nt(dpxb, wx)
        dcb_ref[...] += jnp.sum(dxc, axis=0, keepdims=True)
        for s in range(CONV_W):
            dcw_ref[CONV_W - 1 - s:CONV_W - s, :] += jnp.sum(dxc * xs[s], axis=0, keepdims=True)
        cxv = cx_ref[...]
        dxb = dxc * cw_ref[3:4, :]
        for s in range(1, CONV_W):
            dxb = dxb + _shift_up(dxc, cxv, s) * cw_ref[3 - s:4 - s, :]
        dx_ref[...] = dxb.astype(dx_ref.dtype)
        ca_ref[...] = a[:SUBLANES]
        cl_ref[...] = lmb[:SUBLANES]
        cx_ref[...] = dxc[:SUBLANES]

    tix, chunk, prev8, wblk, vec, cwb = _lru_specs(nh, nt, tc, True)
    hchunk = pl.BlockSpec((tc, LRU_BLOCK), lambda h, t: (tix(t), h))
    carry = pltpu.VMEM((SUBLANES, LRU_BLOCK), F32)
    return _call(
        body, name=name, grid=(nh, nt),
        in_specs=[chunk(0), prev8(0, _tile_rows(proj.dtype)), chunk(nh), hchunk, prev8(0, _tile_rows(hs.dtype)), hchunk,
                  wblk, wblk, vec, vec, cwb, vec, vec],
        out_specs=[hchunk, hchunk, wblk, wblk, vec, vec, cwb, vec, vec],
        out_shape=[jax.ShapeDtypeStruct((T, W), BF16), jax.ShapeDtypeStruct((T, W), BF16),
                   jax.ShapeDtypeStruct((nh, LRU_BLOCK, LRU_BLOCK), F32), jax.ShapeDtypeStruct((nh, LRU_BLOCK, LRU_BLOCK), F32),
                   jax.ShapeDtypeStruct((1, W), F32), jax.ShapeDtypeStruct((1, W), F32),
                   jax.ShapeDtypeStruct((CONV_W, W), F32), jax.ShapeDtypeStruct((1, W), F32), jax.ShapeDtypeStruct((1, W), F32)],
        scratch_shapes=[carry, carry, carry], semantics=("parallel", "arbitrary"),
        args=(proj, proj, proj, hs, hs, dyp, wa, wx, ba, bx, cw, cb, lam), comm=comm)


def _position():
    return lax.axis_index("x"), lax.axis_index("y"), lax.axis_index("c")


def _sems(n):
    return [pltpu.SemaphoreType.DMA((n, 7)), pltpu.SemaphoreType.DMA((n, 7)), pltpu.SemaphoreType.DMA((n,))]


def _gather_comm(arrs, axes):
    n = len(arrs)

    def tools(ins, outs, send_sems, recv_sems, local_sems):
        x, y, c = _position()
        me, sibling = (x, y, c), (x, y, 1 - c)
        chips = [(1 - x, y), (x, 1 - y), (1 - x, 1 - y)]

        def slot(a, pos):
            return outs[a].at[(slice(None),) * axes[a] + (pos,)]

        def copy(a, k, block, to, src=None):
            px, py, pc = block
            rows = slot(a, 4 * px + 2 * py + pc)
            return pltpu.make_async_remote_copy(
                src_ref=rows if src is None else src, dst_ref=rows,
                send_sem=send_sems.at[a, k], recv_sem=recv_sems.at[a, k],
                device_id=to, device_id_type=MESH)

        own = lambda a: pltpu.make_async_copy(ins[a], slot(a, 4 * x + 2 * y + c), local_sems.at[a])
        first = lambda a: ([copy(a, 0, me, sibling, src=ins[a])]
                           + [copy(a, 1 + j, me, (*chip, c), src=ins[a]) for j, chip in enumerate(chips)])
        passed = lambda a: [copy(a, 4 + j, (*chip, c), sibling) for j, chip in enumerate(chips)]
        return me, sibling, chips, c, copy, own, first, passed

    def start(ins, outs, *sems):
        *_, own, first, _ = tools(ins, outs, *sems)
        for a in range(n):
            own(a).start()
            for cp in first(a):
                cp.start()

    def mid(ins, outs, *sems):
        me, _, chips, c, copy, _, _, passed = tools(ins, outs, *sems)
        for a in range(n):
            fwd = passed(a)
            for j, chip in enumerate(chips):
                copy(a, 1 + j, (*chip, c), me).wait_recv()
                fwd[j].start()

    def finish(ins, outs, *sems):
        me, sibling, chips, c, copy, own, first, passed = tools(ins, outs, *sems)
        for a in range(n):
            copy(a, 0, sibling, me).wait_recv()
            for j, chip in enumerate(chips):
                copy(a, 4 + j, (*chip, 1 - c), me).wait_recv()
        for a in range(n):
            for cp in first(a) + passed(a):
                cp.wait_send()
            own(a).wait()

    shapes = [jax.ShapeDtypeStruct(a.shape[:ax] + (N_DEV,) + a.shape[ax:], a.dtype) for a, ax in zip(arrs, axes)]
    return Comm(list(arrs), shapes, _sems(n), start, mid, finish)


def _comm_call(comm, *, name):
    ci, co = len(comm.arrays), len(comm.out_shapes)

    def body(*refs):
        ins, outs, sems = refs[:ci], refs[ci:ci + co], refs[ci + co:]
        comm.start(ins, outs, *sems)
        comm.mid(ins, outs, *sems)
        comm.finish(ins, outs, *sems)

    any_spec = pl.BlockSpec(memory_space=pl.ANY)
    return pl.pallas_call(body, name=name, in_specs=[any_spec] * ci, out_specs=[any_spec] * co,
                          out_shape=comm.out_shapes, scratch_shapes=comm.sems)(*comm.arrays)


def _exchange_comm(items):
    n = len(items)

    def tools(ins, outs, send_sems, recv_sems, local_sems):
        x, y, c = _position()
        me = 4 * x + 2 * y + c

        def src(a, pos):
            if items[a][1]:
                rows = ins[a].shape[1] // N_DEV
                return ins[a].at[:, pl.ds(pl.multiple_of(pos * rows, rows), rows)]
            return ins[a].at[pos]

        copies = [pltpu.make_async_copy(src(a, me), outs[a].at[me], local_sems.at[a]) for a in range(n)]
        for k in range(1, N_DEV):
            px = x ^ ((k >> 2) & 1)
            py = y ^ ((k >> 1) & 1)
            pc = c ^ (k & 1)
            copies += [pltpu.make_async_remote_copy(
                src_ref=src(a, 4 * px + 2 * py + pc), dst_ref=outs[a].at[me],
                send_sem=send_sems.at[a, k - 1], recv_sem=recv_sems.at[a, k - 1],
                device_id=(px, py, pc), device_id_type=MESH) for a in range(n)]
        return copies

    def start(ins, outs, *sems):
        for cp in tools(ins, outs, *sems):
            cp.start()

    def mid(ins, outs, *sems):
        pass

    def finish(ins, outs, *sems):
        for cp in tools(ins, outs, *sems):
            cp.wait()

    shapes = []
    for arr, split in items:
        blk = (arr.shape[0], arr.shape[1] // N_DEV) + arr.shape[2:] if split else arr.shape[1:]
        shapes.append(jax.ShapeDtypeStruct((N_DEV,) + blk, arr.dtype))
    return Comm([arr for arr, _ in items], shapes, _sems(n), start, mid, finish)


def _adamw(parts, w, m, v, *, name):
    L, R, C = w.shape
    assert len(parts) == L
    row_bytes = 2 * (L * N_DEV * C * parts[0].dtype.itemsize + 7 * C * 4)
    tr = _pick(R, max(16, ADAMW_VMEM // row_bytes), 16)
    nr = R // tr
    c1 = 1.0 / (1.0 - ADAM_B1 ** ADAM_STEP)
    c2 = 1.0 / (1.0 - ADAM_B2 ** ADAM_STEP)

    def body(*refs):
        p_refs = refs[:L]
        w_ref, m_ref, v_ref, g_ref, d_ref, nm_ref, nv_ref = refs[L:]
        layer = pl.program_id(0)
        for idx, p_ref in enumerate(p_refs):
            @pl.when(layer == idx)
            def _():
                g = p_ref[0].astype(F32)
                for s in range(1, N_DEV):
                    g = g + p_ref[s].astype(F32)
                nm = ADAM_B1 * m_ref[...] + (1.0 - ADAM_B1) * g
                nv = ADAM_B2 * v_ref[...] + (1.0 - ADAM_B2) * (g * g)
                g_ref[...] = g
                nm_ref[...] = nm
                nv_ref[...] = nv
                d_ref[...] = -ADAM_LR * ((nm * c1) / (jnp.sqrt(nv * c2) + ADAM_EPS) + ADAM_WD * w_ref[...])

    def part_spec(idx):
        return pl.BlockSpec((N_DEV, tr, C),
                            lambda l, i: (0, jnp.where(l == idx, i, jnp.where(l < idx, 0, nr - 1)), 0))

    blk = pl.BlockSpec((None, tr, C), lambda l, i: (l, i, 0))
    return pl.pallas_call(
        body, name=name, grid=(L, nr),
        in_specs=[part_spec(idx) for idx in range(L)] + [blk, blk, blk],
        out_specs=[blk] * 4,
        out_shape=[jax.ShapeDtypeStruct((L, R, C), F32)] * 4,
        compiler_params=_cparams(("arbitrary", "arbitrary")),
    )(*parts, w, m, v)


def _pack(flat_parts, row_multiple, dtype):
    lead = flat_parts[0].shape[:-1]
    total = sum(p.shape[-1] for p in flat_parts)
    quantum = PACK_W * row_multiple
    padded = -(-total // quantum) * quantum
    parts = [p.astype(dtype) for p in flat_parts]
    if padded > total:
        parts.append(jnp.zeros(lead + (padded - total,), dtype))
    return jnp.concatenate(parts, axis=-1).reshape(lead + (padded // PACK_W, PACK_W))


def _unpack(buf, shapes):
    lead = buf.shape[:-2]
    flat = buf.reshape(lead + (-1,))
    out, off = [], 0
    for shp in shapes:
        n = math.prod(shp)
        out.append(flat[..., off:off + n].reshape(lead + tuple(shp)))
        off += n
    return out


def _to_full(seg, ax):
    shard = seg.shape[1:]
    full = shard[:ax] + (N_DEV * shard[ax],) + shard[ax + 1:]
    return jnp.moveaxis(seg, 0, ax).reshape(full)


def _to_shards(full, ax):
    shp = full.shape
    split = shp[:ax] + (N_DEV, shp[ax] // N_DEV) + shp[ax + 1:]
    return jnp.moveaxis(full.reshape(split), ax, 0).reshape(N_DEV, -1)


BIG = (("attn_w_in", 1), ("attn_w_out", 1), ("lru_w_in", 1), ("lru_w_a", 2), ("lru_w_x", 2), ("lru_w_out", 1))
SMALL = (("lru_conv_w", 2), ("lru_conv_b", 1), ("lru_b_a", 2), ("lru_b_x", 2), ("lru_lambda", 1))
REPL = ("norm_pre", "norm_post", "attn_sinks")
ORDER = ("norm_pre", "norm_post", "attn_w_in", "attn_w_out", "attn_sinks", "lru_w_in", "lru_conv_w", "lru_conv_b",
         "lru_w_a", "lru_b_a", "lru_w_x", "lru_b_x", "lru_lambda", "lru_w_out")


def kernel(x, norm_pre, norm_post, attn_w_in, attn_w_out, attn_sinks, lru_w_in, lru_conv_w, lru_conv_b, lru_w_a, lru_b_a, lru_w_x, lru_b_x, lru_lambda, lru_w_out, loss_target, m_norm_pre, m_norm_post, m_attn_w_in, m_attn_w_out, m_attn_sinks, m_lru_w_in, m_lru_conv_w, m_lru_conv_b, m_lru_w_a, m_lru_b_a, m_lru_w_x, m_lru_b_x, m_lru_lambda, m_lru_w_out, v_norm_pre, v_norm_post, v_attn_w_in, v_attn_w_out, v_attn_sinks, v_lru_w_in, v_lru_conv_w, v_lru_conv_b, v_lru_w_a, v_lru_b_a, v_lru_w_x, v_lru_b_x, v_lru_lambda, v_lru_w_out):
    W = dict(norm_pre=norm_pre, norm_post=norm_post, attn_w_in=attn_w_in, attn_w_out=attn_w_out, attn_sinks=attn_sinks,
             lru_w_in=lru_w_in, lru_conv_w=lru_conv_w, lru_conv_b=lru_conv_b, lru_w_a=lru_w_a, lru_b_a=lru_b_a,
             lru_w_x=lru_w_x, lru_b_x=lru_b_x, lru_lambda=lru_lambda, lru_w_out=lru_w_out)
    M = dict(norm_pre=m_norm_pre, norm_post=m_norm_post, attn_w_in=m_attn_w_in, attn_w_out=m_attn_w_out,
             attn_sinks=m_attn_sinks, lru_w_in=m_lru_w_in, lru_conv_w=m_lru_conv_w, lru_conv_b=m_lru_conv_b,
             lru_w_a=m_lru_w_a, lru_b_a=m_lru_b_a, lru_w_x=m_lru_w_x, lru_b_x=m_lru_b_x, lru_lambda=m_lru_lambda,
             lru_w_out=m_lru_w_out)
    V = dict(norm_pre=v_norm_pre, norm_post=v_norm_post, attn_w_in=v_attn_w_in, attn_w_out=v_attn_w_out,
             attn_sinks=v_attn_sinks, lru_w_in=v_lru_w_in, lru_conv_w=v_lru_conv_w, lru_conv_b=v_lru_conv_b,
             lru_w_a=v_lru_w_a, lru_b_a=v_lru_b_a, lru_w_x=v_lru_w_x, lru_b_x=v_lru_b_x, lru_lambda=v_lru_lambda,
             lru_w_out=v_lru_w_out)

    h0 = x[0]
    target = loss_target[0]
    T, D = h0.shape
    depth = norm_pre.shape[0]
    n_attn = attn_w_in.shape[0]
    Q = attn_w_out.shape[1] * N_DEV
    KV = Q // GROUP
    LW = lru_w_out.shape[1] * N_DEV
    nh = LW // LRU_BLOCK

    n_lru = lru_w_in.shape[0]
    big_names = [n for n, _ in BIG]
    small_names = [n for n, _ in SMALL]
    small_shapes = [W[n].shape for n in small_names]
    repl_shapes = [W[n].shape for n in REPL]

    flat = lambda a: a.reshape(-1)
    def layer_shards(layer):
        j = layer // 2
        names = ("attn_w_in", "attn_w_out") if layer % 2 == 0 else ("lru_w_in", "lru_w_a", "lru_w_x", "lru_w_out")
        return [W[n][j].astype(BF16) for n in names], [1 if n in ("lru_w_a", "lru_w_x") else 0 for n in names]

    def layer_weights(layer, gathered):
        if layer % 2 == 0:
            g_in, g_out = gathered
            w_in = jnp.moveaxis(g_in, 0, 1).reshape(D, -1)
            w_in = jnp.concatenate([w_in[:, :Q], w_in[:, Q + 2 * KV:], w_in[:, Q:Q + 2 * KV]], axis=-1)
            return dict(w_in=w_in, w_out=None if g_out is None else g_out.reshape(Q, D))
        g_in, g_wa, g_wx, g_out = gathered
        return dict(w_in=jnp.moveaxis(g_in, 0, 1).reshape(D, 2 * LW), w_a=g_wa.reshape(nh, LRU_BLOCK, LRU_BLOCK),
                    w_x=g_wx.reshape(nh, LRU_BLOCK, LRU_BLOCK), w_out=g_out.reshape(LW, D))

    arrs0, _ = layer_shards(0)
    (u,), first = _rms_fwd(h0, norm_pre[0:1], name="rms_fwd",
                           comm=_gather_comm([arrs0[0], _pack([flat(W[n]) for n in small_names], SUBLANES, F32)], [0, 0]))
    weights = {0: layer_weights(0, (first[0], None))}
    full = {}
    for (n, ax), seg in zip(SMALL, _unpack(first[-1], small_shapes)):
        full[n] = _to_full(seg, ax)
    cw_f = full["lru_conv_w"]
    cb_f = full["lru_conv_b"][:, None, :]
    ba_f = full["lru_b_a"].reshape(-1, 1, LW)
    bx_f = full["lru_b_x"].reshape(-1, 1, LW)
    lam_f = full["lru_lambda"][:, None, :]

    h = h0
    saved = []
    for layer in range(depth):
        j = layer // 2
        wl = weights[layer]
        nxt = _gather_comm(*layer_shards(layer + 1)) if layer + 1 < depth else None
        if layer % 2 == 0:
            if wl["w_out"] is None:
                proj, got = _matmul(u, wl["w_in"], out_dtype=BF16, comm=_gather_comm([arrs0[1]], [0]), name="attn_in")
                wl["w_out"] = got[0].reshape(Q, D)
            else:
                proj = _matmul(u, wl["w_in"], out_dtype=BF16, name="attn_in")
            (mix, ypre), got = _attn_fwd(proj, attn_sinks[j], comm=nxt, name="attn_fwd")
            y = _matmul(ypre, wl["w_out"], name="attn_out")
        else:
            proj = _matmul(u, wl["w_in"], out_dtype=BF16, name="lru_in")
            (mix, ypre), got = _lru_fwd(proj, wl["w_a"], wl["w_x"], ba_f[j], bx_f[j], cw_f[j], cb_f[j], lam_f[j],
                                        comm=nxt, name="lru_fwd")
            y = _matmul(ypre, wl["w_out"], name="lru_out")
        if nxt is not None:
            weights[layer + 1] = layer_weights(layer + 1, got)
        saved.append((h, u, proj, mix, ypre, y))
        if layer + 1 < depth:
            h, u = _post_fwd(y, norm_post[layer:layer + 1], h, norm_pre[layer + 1:layer + 2], name="post_fwd")
        else:
            dh, loss_part = _post_loss(y, norm_post[layer:layer + 1], h, target, name="post_loss")

    g_pre = [None] * depth
    g_post = [None] * depth
    small_vec = [n for n in small_names] + ["attn_sinks"]
    grads = {n: [None] * W[n].shape[0] for n in small_vec}
    recv = {}

    def carried(keyed):
        return _exchange_comm([item for _, item in keyed]) if keyed else None

    def landed(keyed, got):
        for (key, _), r in zip(keyed, got):
            recv[key] = r

    pending = []
    for layer in reversed(range(depth)):
        j = layer // 2
        h_in, u, proj, mix, ypre, y = saved[layer]
        wl = weights[layer]
        dy, g_post[layer] = _rms_bwd(y, norm_post[layer:layer + 1], dh, None, out_dtype=BF16, name="post_bwd")
        if layer % 2 == 0:
            dyp = _matmul(dy, wl["w_out"], tb=True, out_dtype=BF16, name="attn_out_dx")
            dw_out = _matmul(ypre, dy, ta=True, out_dtype=BF16, name="attn_out_dw")
            pending.append((("attn_w_out", j), (dw_out.reshape(N_DEV, Q // N_DEV, D), False)))
            (dqg, dkv, dsink), got = _attn_bwd(proj, mix, dyp, attn_sinks[j], comm=carried(pending), name="attn_bwd")
            landed(pending, got)
            grads["attn_sinks"][j] = dsink[:, 0]
            dproj = jnp.concatenate([dqg, dkv], axis=1)
            dw = _matmul(u, dproj, ta=True, out_dtype=BF16, name="attn_in_dw")
            dw = jnp.concatenate([dw[:, :Q], dw[:, 2 * Q:], dw[:, Q:2 * Q]], axis=1)
            pending = [(("attn_w_in", j), (jnp.moveaxis(dw.reshape(D, N_DEV, -1), 1, 0), False))]
            if layer == 0:
                du, got = _matmul(dproj, wl["w_in"], tb=True, comm=carried(pending), name="attn_in_dx")
                landed(pending, got)
                pending = []
            else:
                du = _matmul(dproj, wl["w_in"], tb=True, name="attn_in_dx")
        else:
            dyp = _matmul(dy, wl["w_out"], tb=True, out_dtype=BF16, name="lru_out_dx")
            dw_out = _matmul(ypre, dy, ta=True, out_dtype=BF16, name="lru_out_dw")
            pending.append((("lru_w_out", j), (dw_out.reshape(N_DEV, LW // N_DEV, D), False)))
            (dxb, dgt, dwa, dwx, dba, dbx, dcw, dcb, dlam), got = _lru_bwd(
                proj, mix, dyp, wl["w_a"], wl["w_x"], ba_f[j], bx_f[j], cw_f[j], cb_f[j], lam_f[j],
                comm=carried(pending), name="lru_bwd")
            landed(pending, got)
            grads["lru_b_a"][j], grads["lru_b_x"][j] = dba.reshape(nh, LRU_BLOCK), dbx.reshape(nh, LRU_BLOCK)
            grads["lru_conv_w"][j], grads["lru_conv_b"][j], grads["lru_lambda"][j] = dcw, dcb[0], dlam[0]
            dproj = jnp.concatenate([dxb, dgt], axis=1)
            du = _matmul(dproj, wl["w_in"], tb=True, name="lru_in_dx")
            mine = [(("lru_w_a", j), (dwa, True)), (("lru_w_x", j), (dwx, True))]
            dw, got = _matmul(u, dproj, ta=True, out_dtype=BF16, by_owner=True, comm=carried(mine), name="lru_in_dw")
            landed(mine, got)
            pending = [(("lru_w_in", j), (dw, False))]
        dh, g_pre[layer] = _rms_bwd(h_in, norm_pre[layer:layer + 1], du, dh, out_dtype=F32, name="pre_bwd")

    gfull = {n: jnp.stack(g) for n, g in grads.items()}
    gfull["norm_pre"] = jnp.concatenate(g_pre, axis=0)
    gfull["norm_post"] = jnp.concatenate(g_post, axis=0)

    repl_part = [jnp.broadcast_to(gfull[n].reshape(1, -1), (N_DEV, gfull[n].size)) for n in REPL]
    loss_slot = jnp.broadcast_to(loss_part.reshape(1, 1), (N_DEV, 1))
    send_small = _pack([_to_shards(gfull[n], ax) for n, ax in SMALL] + repl_part + [loss_slot], SUBLANES, F32)
    last = pending + [(("small", 0), (send_small, False))]
    landed(last, _comm_call(carried(last), name="exchange_last"))

    zero1 = jnp.zeros((1,), F32)
    outs = {}
    for n in big_names:
        shp = W[n].shape
        as3 = lambda a: a.reshape((shp[0], -1, shp[-1]))
        parts = [recv[n, j].reshape((N_DEV, -1, shp[-1])) for j in range(shp[0])]
        res = _adamw(parts, as3(W[n]), as3(M[n]), as3(V[n]), name="adamw_" + n)
        for kind, a in zip(("grad", "delta", "new_m", "new_v"), res):
            outs[kind, n] = a.reshape(shp)
    res_small = _adamw([recv["small", 0]],
                       *[_pack([flat(S[n]) for n in small_names] + [flat(S[n]) for n in REPL] + [zero1], SUBLANES, F32)[None]
                         for S in (W, M, V)], name="adamw_small")
    for kind, rs in zip(("grad", "delta", "new_m", "new_v"), res_small):
        for n, a in zip(small_names + list(REPL) + ["loss"], _unpack(rs[0], small_shapes + repl_shapes + [(1,)])):
            outs[kind, n] = a
    loss = outs["grad", "loss"][0]
    result = [loss, dh[None]]
    for kind in ("grad", "delta", "new_m", "new_v"):
        result += [outs[kind, n] for n in ORDER]
    return tuple(result)
```

```python
import math
from typing import Callable, NamedTuple

import jax
import jax.numpy as jnp
from jax import lax
from jax.experimental import pallas as pl
from jax.experimental.pallas import tpu as pltpu

F32 = jnp.float32
BF16 = jnp.bfloat16

N_DEV = 8
HEAD_DIM = 64
GROUP = 8
WINDOW = 128
LRU_BLOCK = 256
CONV_W = 4
C_RG = 8.0
NORM_EPS = 1e-6
MASK_VALUE = -1e30

ADAM_LR = 0.001
ADAM_B1 = 0.9
ADAM_B2 = 0.999
ADAM_EPS = 1e-08
ADAM_WD = 0.01
ADAM_STEP = 10

ROW_BLOCK = 256
LANES = 128
SUBLANES = 8
PACK_W = 1024
VMEM_LIMIT = 56 * 1024 * 1024
MATMUL_VMEM = 36 * 1024 * 1024
ADAMW_VMEM = 24 * 1024 * 1024
MESH = pl.DeviceIdType.MESH


def _cparams(sem=None):
    return pltpu.CompilerParams(dimension_semantics=sem, vmem_limit_bytes=VMEM_LIMIT)


class Comm(NamedTuple):
    arrays: list
    out_shapes: list
    sems: list
    start: Callable
    mid: Callable
    finish: Callable


def _call(body, *, name, grid, in_specs, out_specs, out_shape, scratch_shapes, semantics, args, comm=None):
    if comm is None:
        res = pl.pallas_call(body, name=name, grid=grid, in_specs=in_specs, out_specs=out_specs, out_shape=out_shape,
                             scratch_shapes=scratch_shapes, compiler_params=_cparams(semantics))(*args)
        return list(res), []
    n_in, n_out, n_scr = len(in_specs), len(out_specs), len(scratch_shapes)
    ci, co = len(comm.arrays), len(comm.out_shapes)
    steps = math.prod(grid)

    def hosted(*refs):
        ins, cins = refs[:n_in], refs[n_in:n_in + ci]
        o0 = n_in + ci
        outs, couts = refs[o0:o0 + n_out], refs[o0 + n_out:o0 + n_out + co]
        s0 = o0 + n_out + co
        scr, sems = refs[s0:s0 + n_scr], refs[s0 + n_scr:]
        step = 0
        for ax, g in enumerate(grid):
            step = step * g + pl.program_id(ax)

        @pl.when(step == 0)
        def _():
            comm.start(cins, couts, *sems)

        body(*ins, *outs, *scr)

        @pl.when(step == steps // 2)
        def _():
            comm.mid(cins, couts, *sems)

        @pl.when(step == steps - 1)
        def _():
            comm.finish(cins, couts, *sems)

    any_spec = pl.BlockSpec(memory_space=pl.ANY)
    res = pl.pallas_call(
        hosted, name=name, grid=grid,
        in_specs=list(in_specs) + [any_spec] * ci, out_specs=list(out_specs) + [any_spec] * co,
        out_shape=list(out_shape) + list(comm.out_shapes), scratch_shapes=list(scratch_shapes) + list(comm.sems),
        compiler_params=_cparams(("arbitrary",) * len(grid)),
    )(*args, *comm.arrays)
    return list(res[:n_out]), list(res[n_out:])


def _pick(n, target, quantum):
    best = None
    for t in range(quantum, min(n, target) + 1, quantum):
        if n % t == 0:
            best = t
    return n if best is None else best


def _sigmoid(x):
    return 1.0 / (1.0 + jnp.exp(-x))


def _dot(a, b):
    return lax.dot_general(a, b, (((1,), (0,)), ((), ())), preferred_element_type=F32)


def _dot_nt(a, b):
    return lax.dot_general(a, b, (((1,), (1,)), ((), ())), preferred_element_type=F32)


def _dot_tn(a, b):
    return lax.dot_general(a, b, (((0,), (0,)), ((), ())), preferred_element_type=F32)


def _matmul(a, b, *, ta=False, tb=False, out_dtype=F32, by_owner=False, comm=None, name):
    if ta:
        K, M = a.shape
    else:
        M, K = a.shape
    if tb:
        N, K2 = b.shape
    else:
        K2, N = b.shape
    assert K == K2, (a.shape, b.shape, ta, tb)
    tn = N // N_DEV if by_owner else _pick(N, 1536, 256)
    assert tn % LANES == 0
    osz = jnp.dtype(out_dtype).itemsize

    def vmem_bytes(tm, tk):
        acc = 0 if tk == K else tm * tn * 4
        return 2 * (tm * tk * a.dtype.itemsize + tk * tn * b.dtype.itemsize + tm * tn * osz) + acc

    def deepest(tm):
        return max([t for t in range(256, K + 1, 256) if K % t == 0 and vmem_bytes(tm, t) <= MATMUL_VMEM] or [_pick(K, 512, 256)])

    tm = _pick(M, 1024, 256)
    if deepest(tm) < K and tm % 512 == 0 and deepest(tm // 2) == K:
        tm //= 2
    tk = deepest(tm)
    nk = K // tk
    dot = {(False, False): _dot, (False, True): _dot_nt, (True, False): _dot_tn}[(ta, tb)]

    if nk == 1:
        def body(a_ref, b_ref, o_ref):
            o_ref[...] = dot(a_ref[...].astype(BF16), b_ref[...].astype(BF16)).astype(o_ref.dtype)
        scratch = []
    else:
        def body(a_ref, b_ref, o_ref, acc_ref):
            k = pl.program_id(2)

            @pl.when(k == 0)
            def _():
                acc_ref[...] = jnp.zeros_like(acc_ref)

            acc_ref[...] += dot(a_ref[...].astype(BF16), b_ref[...].astype(BF16))

            @pl.when(k == nk - 1)
            def _():
                o_ref[...] = acc_ref[...].astype(o_ref.dtype)
        scratch = [pltpu.VMEM((tm, tn), F32)]

    a_spec = pl.BlockSpec((tk, tm), lambda j, i, k: (k, i)) if ta else pl.BlockSpec((tm, tk), lambda j, i, k: (i, k))
    b_spec = pl.BlockSpec((tn, tk), lambda j, i, k: (j, k)) if tb else pl.BlockSpec((tk, tn), lambda j, i, k: (k, j))
    if by_owner:
        o_spec = pl.BlockSpec((None, tm, tn), lambda j, i, k: (j, i, 0))
        o_shape = jax.ShapeDtypeStruct((N_DEV, M, tn), out_dtype)
    else:
        o_spec = pl.BlockSpec((tm, tn), lambda j, i, k: (i, j))
        o_shape = jax.ShapeDtypeStruct((M, N), out_dtype)
    res, extra = _call(body, name=name, grid=(N // tn, M // tm, nk), in_specs=[a_spec, b_spec], out_specs=[o_spec],
                       out_shape=[o_shape], scratch_shapes=scratch, semantics=("parallel", "parallel", "arbitrary"),
                       args=(a, b), comm=comm)
    return res[0] if comm is None else (res[0], extra)


def _rms_fwd(h, g, *, comm=None, name):
    T, D = h.shape
    tm = _pick(T, ROW_BLOCK, SUBLANES)

    def body(h_ref, g_ref, u_ref):
        x = h_ref[...]
        r = lax.rsqrt(jnp.mean(x * x, axis=-1, keepdims=True) + NORM_EPS)
        u_ref[...] = ((x * r) * g_ref[...]).astype(u_ref.dtype)

    return _call(
        body, name=name, grid=(T // tm,),
        in_specs=[pl.BlockSpec((tm, D), lambda i: (i, 0)), pl.BlockSpec((1, D), lambda i: (0, 0))],
        out_specs=[pl.BlockSpec((tm, D), lambda i: (i, 0))],
        out_shape=[jax.ShapeDtypeStruct((T, D), BF16)],
        scratch_shapes=[], semantics=("parallel",), args=(h, g), comm=comm)


def _post_fwd(y, g, h, g_next, *, name):
    T, D = y.shape
    tm = _pick(T, ROW_BLOCK, SUBLANES)

    def body(y_ref, g_ref, h_ref, gn_ref, o_ref, u_ref):
        x = y_ref[...]
        r = lax.rsqrt(jnp.mean(x * x, axis=-1, keepdims=True) + NORM_EPS)
        ho = h_ref[...] + (x * r) * g_ref[...]
        o_ref[...] = ho
        rn = lax.rsqrt(jnp.mean(ho * ho, axis=-1, keepdims=True) + NORM_EPS)
        u_ref[...] = ((ho * rn) * gn_ref[...]).astype(u_ref.dtype)

    row = pl.BlockSpec((tm, D), lambda i: (i, 0))
    vec = pl.BlockSpec((1, D), lambda i: (0, 0))
    return pl.pallas_call(
        body, name=name, grid=(T // tm,),
        in_specs=[row, vec, row, vec],
        out_specs=[row, row],
        out_shape=[jax.ShapeDtypeStruct((T, D), F32), jax.ShapeDtypeStruct((T, D), BF16)],
        compiler_params=_cparams(("parallel",)),
    )(y, g, h, g_next)


def _rms_bwd(x, g, dz, res, *, out_dtype, name):
    T, D = x.shape
    tm = _pick(T, ROW_BLOCK, SUBLANES)
    has_res = res is not None

    def body(*refs):
        if has_res:
            x_ref, g_ref, dz_ref, res_ref, dx_ref, dg_ref = refs
        else:
            x_ref, g_ref, dz_ref, dx_ref, dg_ref = refs
        i = pl.program_id(0)

        @pl.when(i == 0)
        def _():
            dg_ref[...] = jnp.zeros_like(dg_ref)

        dx, dg = _norm_dx(x_ref[...], g_ref[...], dz_ref[...])
        dg_ref[...] += dg
        if has_res:
            dx = dx + res_ref[...]
        dx_ref[...] = dx.astype(dx_ref.dtype)

    row = pl.BlockSpec((tm, D), lambda i: (i, 0))
    vec = pl.BlockSpec((1, D), lambda i: (0, 0))
    ins = [x, g, dz] + ([res] if has_res else [])
    return pl.pallas_call(
        body, name=name, grid=(T // tm,),
        in_specs=[row, vec, row] + ([row] if has_res else []),
        out_specs=[row, vec],
        out_shape=[jax.ShapeDtypeStruct((T, D), out_dtype), jax.ShapeDtypeStruct((1, D), F32)],
        compiler_params=_cparams(("arbitrary",)),
    )(*ins)


def _norm_dx(x, g, dz):
    r = lax.rsqrt(jnp.mean(x * x, axis=-1, keepdims=True) + NORM_EPS)
    xhat = x * r
    dxh = dz * g
    dx = r * (dxh - xhat * jnp.mean(dxh * xhat, axis=-1, keepdims=True))
    return dx, jnp.sum(dz * xhat, axis=0, keepdims=True)


def _pre_post_bwd(h_in, g_pre, du, dh, y_below, g_post_below, *, name):
    T, D = h_in.shape
    tm = _pick(T, ROW_BLOCK, SUBLANES)

    def body(h_ref, gp_ref, du_ref, dh_ref, y_ref, gq_ref, dhn_ref, dy_ref, dgp_ref, dgq_ref):
        i = pl.program_id(0)

        @pl.when(i == 0)
        def _():
            dgp_ref[...] = jnp.zeros_like(dgp_ref)
            dgq_ref[...] = jnp.zeros_like(dgq_ref)

        dx, dgp = _norm_dx(h_ref[...], gp_ref[...], du_ref[...])
        dhn = dh_ref[...] + dx
        dhn_ref[...] = dhn
        dgp_ref[...] += dgp
        dy, dgq = _norm_dx(y_ref[...], gq_ref[...], dhn)
        dy_ref[...] = dy.astype(dy_ref.dtype)
        dgq_ref[...] += dgq

    row = pl.BlockSpec((tm, D), lambda i: (i, 0))
    vec = pl.BlockSpec((1, D), lambda i: (0, 0))
    return pl.pallas_call(
        body, name=name, grid=(T // tm,),
        in_specs=[row, vec, row, row, row, vec],
        out_specs=[row, row, vec, vec],
        out_shape=[jax.ShapeDtypeStruct((T, D), F32), jax.ShapeDtypeStruct((T, D), BF16),
                   jax.ShapeDtypeStruct((1, D), F32), jax.ShapeDtypeStruct((1, D), F32)],
        compiler_params=_cparams(("arbitrary",)),
    )(h_in, g_pre, du, dh, y_below, g_post_below)


def _post_loss(y, g, h, target, *, name):
    T, D = h.shape
    tm = _pick(T, ROW_BLOCK, SUBLANES)

    def body(y_ref, g_ref, h_ref, t_ref, dh_ref, l_ref):
        i = pl.program_id(0)

        @pl.when(i == 0)
        def _():
            l_ref[...] = jnp.zeros_like(l_ref)

        x = y_ref[...]
        r = lax.rsqrt(jnp.mean(x * x, axis=-1, keepdims=True) + NORM_EPS)
        e = (h_ref[...] + (x * r) * g_ref[...]) - t_ref[...]
        dh_ref[...] = e * (1.0 / D)
        row = jnp.sum(e * e, axis=-1, keepdims=True) * (0.5 / D)
        l_ref[...] += jnp.sum(row, axis=0, keepdims=True)

    row = pl.BlockSpec((tm, D), lambda i: (i, 0))
    return pl.pallas_call(
        body, name=name, grid=(T // tm,),
        in_specs=[row, pl.BlockSpec((1, D), lambda i: (0, 0)), row, row],
        out_specs=[row, pl.BlockSpec((1, 1), lambda i: (0, 0))],
        out_shape=[jax.ShapeDtypeStruct((T, D), F32), jax.ShapeDtypeStruct((1, 1), F32)],
        compiler_params=_cparams(("arbitrary",)),
    )(y, g, h, target)


def _attn_dims(P):
    Q = P * 4 // 9
    KV = Q // GROUP
    assert 2 * Q + 2 * KV == P and KV % LANES == 0 and Q % (2 * KV) == 0
    return Q, KV


def _attn_specs(Q, KV, nb):
    blk = WINDOW
    row = lambda i: jnp.minimum(i, nb - 1)
    q_spec = pl.BlockSpec((blk, Q), lambda i: (row(i), 0))
    kvc_spec = pl.BlockSpec((blk, 2 * KV), lambda i: (row(i), Q // (2 * KV)))
    kvp_spec = pl.BlockSpec((blk, 2 * KV), lambda i: (jnp.maximum(row(i) - 1, 0), Q // (2 * KV)))
    g_specs = [pl.BlockSpec((blk, 2 * KV), lambda i, b=b: (row(i), Q // (2 * KV) + 1 + b)) for b in range(Q // (2 * KV))]
    return q_spec, kvc_spec, kvp_spec, g_specs


def _stack_gate(g_refs, cols):
    width = g_refs[0].shape[1]
    parts = [g_refs[cs.start // width][:, cs.start % width:cs.start % width + LANES] for cs in cols]
    return jnp.concatenate(parts, axis=0).astype(F32)


PAIRS = GROUP // 2
STACK = PAIRS * WINDOW


def _band_mask(i):
    c = lax.broadcasted_iota(jnp.int32, (2 * WINDOW, STACK), 0)
    r = lax.broadcasted_iota(jnp.int32, (2 * WINDOW, STACK), 1) & (WINDOW - 1)
    first_key = jnp.where(i > 0, 0, WINDOW)
    return (c > r) & (c <= r + WINDOW) & (c >= first_key)


def _group_cols(kvh):
    c0 = kvh * GROUP * HEAD_DIM
    return [slice(c0 + j * LANES, c0 + (j + 1) * LANES) for j in range(PAIRS)]


def _stack(ref, cols, scale=None):
    x = jnp.concatenate([ref[:, cs] for cs in cols], axis=0).astype(F32)
    return x if scale is None else x * scale


def _group_sinks(sink_ref, kvh, half):
    return jnp.concatenate([jnp.full((1, WINDOW), sink_ref[kvh * GROUP + 2 * j + half], F32) for j in range(PAIRS)], axis=1)


def _pair_halves(x128, xt128, e):
    lo = lax.broadcasted_iota(jnp.int32, x128.shape, 1) < HEAD_DIM
    lo_t = lax.broadcasted_iota(jnp.int32, xt128.shape, 0) < HEAD_DIM
    if e == 0:
        x_lo, xt_lo = jnp.where(lo, x128, 0.0), jnp.where(lo_t, xt128, 0.0)
        x_hi, xt_hi = pltpu.roll(x_lo, HEAD_DIM, 1), pltpu.roll(xt_lo, HEAD_DIM, 0)
    else:
        x_hi, xt_hi = jnp.where(lo, 0.0, x128), jnp.where(lo_t, 0.0, xt128)
        x_lo, xt_lo = pltpu.roll(x_hi, HEAD_DIM, 1), pltpu.roll(xt_hi, HEAD_DIM, 0)
    return (x_lo.astype(BF16), x_hi.astype(BF16)), (xt_lo.astype(BF16), xt_hi.astype(BF16))


def _softmax_sink(st, allowed, sink):
    st = jnp.where(allowed, st, MASK_VALUE)
    m = jnp.maximum(jnp.max(st, axis=0, keepdims=True), sink)
    p = jnp.exp(st - m)
    es = jnp.exp(sink - m)
    inv = 1.0 / (jnp.sum(p, axis=0, keepdims=True) + es)
    return p * inv, es * inv


def _attn_fwd(proj, sinks, *, comm=None, name):
    T, P = proj.shape
    Q, KV = _attn_dims(P)
    nb = T // WINDOW
    npairs = KV // LANES
    scale = 1.0 / math.sqrt(HEAD_DIM)

    ng = Q // (2 * KV)

    def body(sink_ref, q_ref, kvc_ref, kvp_ref, *rest):
        g_refs, (out_ref, yp_ref) = rest[:ng], rest[ng:]
        i = pl.program_id(0)
        allowed = _band_mask(i)
        for p in range(npairs):
            ks = slice(p * LANES, (p + 1) * LANES)
            vs = slice(KV + p * LANES, KV + (p + 1) * LANES)
            k128 = jnp.concatenate([kvp_ref[:, ks], kvc_ref[:, ks]], axis=0).astype(F32)
            v128 = jnp.concatenate([kvp_ref[:, vs], kvc_ref[:, vs]], axis=0).astype(F32)
            kt128, vt128 = k128.T, v128.T
            for e in range(2):
                kvh = 2 * p + e
                khalf, _ = _pair_halves(k128, kt128, e)
                _, vthalf = _pair_halves(v128, vt128, e)
                cols = _group_cols(kvh)
                q4 = _stack(q_ref, cols, scale).astype(BF16)
                ot = None
                for half in range(2):
                    st = _dot_nt(khalf[half], q4)
                    pn, _ = _softmax_sink(st, allowed, _group_sinks(sink_ref, kvh, half))
                    o = _dot(vthalf[half], pn.astype(BF16))
                    ot = o if ot is None else ot + o
                o4 = ot.T
                g4 = _stack_gate(g_refs, cols)
                y4 = (o4 * (g4 * _sigmoid(g4))).astype(BF16)
                for j, cs in enumerate(cols):
                    out_ref[:, cs] = o4[j * WINDOW:(j + 1) * WINDOW]
                    yp_ref[:, cs] = y4[j * WINDOW:(j + 1) * WINDOW]

    q_spec, kvc_spec, kvp_spec, g_specs = _attn_specs(Q, KV, nb)
    row = pl.BlockSpec((WINDOW, Q), lambda i: (i, 0))
    return _call(
        body, name=name, grid=(nb,),
        in_specs=[pl.BlockSpec(memory_space=pltpu.SMEM), q_spec, kvc_spec, kvp_spec] + g_specs,
        out_specs=[row, row],
        out_shape=[jax.ShapeDtypeStruct((T, Q), F32), jax.ShapeDtypeStruct((T, Q), BF16)],
        scratch_shapes=[], semantics=("parallel",), args=(sinks, proj, proj, proj) + (proj,) * ng, comm=comm)


def _attn_bwd(proj, out, dyp, sinks, *, comm=None, name):
    T, P = proj.shape
    Q, KV = _attn_dims(P)
    nb = T // WINDOW
    npairs = KV // LANES
    H = Q // HEAD_DIM
    ng = Q // (2 * KV)
    scale = 1.0 / math.sqrt(HEAD_DIM)

    def body(sink_ref, q_ref, kvc_ref, kvp_ref, *rest):
        g_refs, (out_ref, dyp_ref, dqg_ref, dkv_ref, dsink_ref, carry_ref) = rest[:ng], rest[ng:]
        i = pl.program_id(0)

        @pl.when(i == 0)
        def _():
            carry_ref[...] = jnp.zeros_like(carry_ref)
            dsink_ref[...] = jnp.zeros_like(dsink_ref)

        @pl.when(i == nb)
        def _():
            dkv_ref[...] = carry_ref[...].astype(dkv_ref.dtype)

        @pl.when(i < nb)
        def _():
            allowed = _band_mask(i)
            lo = lax.broadcasted_iota(jnp.int32, (2 * WINDOW, LANES), 1) < HEAD_DIM
            sel_lane = lax.broadcasted_iota(jnp.int32, (SUBLANES, LANES), 1) < HEAD_DIM
            sels = (jnp.where(sel_lane, 1.0, 0.0).astype(BF16), jnp.where(sel_lane, 0.0, 1.0).astype(BF16))
            for p in range(npairs):
                ks = slice(p * LANES, (p + 1) * LANES)
                vs = slice(KV + p * LANES, KV + (p + 1) * LANES)
                k128 = jnp.concatenate([kvp_ref[:, ks], kvc_ref[:, ks]], axis=0).astype(F32)
                v128 = jnp.concatenate([kvp_ref[:, vs], kvc_ref[:, vs]], axis=0).astype(F32)
                kt128, vt128 = k128.T, v128.T
                dk_e, dv_e = [], []
                for e in range(2):
                    kvh = 2 * p + e
                    khalf, kthalf = _pair_halves(k128, kt128, e)
                    vhalf, _ = _pair_halves(v128, vt128, e)
                    cols = _group_cols(kvh)
                    q4 = _stack(q_ref, cols, scale).astype(BF16)
                    g4 = _stack_gate(g_refs, cols)
                    o4 = _stack(out_ref, cols)
                    dy4 = _stack(dyp_ref, cols)
                    sg = _sigmoid(g4)
                    do4 = dy4 * (g4 * sg)
                    dg4 = (dy4 * o4 * (sg * (1.0 + g4 * (1.0 - sg)))).astype(dqg_ref.dtype)
                    dod = do4 * o4
                    dod_hi = dod.astype(BF16)
                    dod_lo = (dod - dod_hi.astype(F32)).astype(BF16)
                    do4b = do4.astype(BF16)
                    dqt = None
                    dk_h, dv_h = [], []
                    for half in range(2):
                        delta = jnp.max(_dot_nt(sels[half], dod_hi) + _dot_nt(sels[half], dod_lo), axis=0, keepdims=True)
                        st = _dot_nt(khalf[half], q4)
                        pn, psink = _softmax_sink(st, allowed, _group_sinks(sink_ref, kvh, half))
                        dp = _dot_nt(vhalf[half], do4b)
                        ds = (pn * (dp - delta)).astype(BF16)
                        dq = _dot(kthalf[half], ds)
                        dqt = dq if dqt is None else dqt + dq
                        dk_h.append(_dot(ds, q4))
                        dv_h.append(_dot(pn.astype(BF16), do4b))
                        pd = psink * delta
                        for j in range(PAIRS):
                            n = kvh * GROUP + 2 * j + half
                            dsn = -jnp.sum(pd[:, j * WINDOW:(j + 1) * WINDOW], axis=1, keepdims=True)
                            dsink_ref[n:n + 1, :] += jnp.broadcast_to(dsn, (1, LANES))
                    dq4 = (dqt.T * scale).astype(dqg_ref.dtype)
                    for j, cs in enumerate(cols):
                        dqg_ref[:, cs] = dq4[j * WINDOW:(j + 1) * WINDOW]
                        dqg_ref[:, slice(Q + 2 * KV + cs.start, Q + 2 * KV + cs.stop)] = dg4[j * WINDOW:(j + 1) * WINDOW]
                    acc_k = jnp.where(lo, dk_h[0], dk_h[1])
                    acc_v = jnp.where(lo, dv_h[0], dv_h[1])
                    dk_e.append(acc_k + pltpu.roll(acc_k, HEAD_DIM, 1))
                    dv_e.append(acc_v + pltpu.roll(acc_v, HEAD_DIM, 1))
                for sl, de in ((ks, dk_e), (vs, dv_e)):
                    d128 = jnp.where(lo, de[0], de[1])
                    dkv_ref[:, sl] = (carry_ref[:, sl] + d128[:WINDOW]).astype(dkv_ref.dtype)
                    carry_ref[:, sl] = d128[WINDOW:]

    q_spec, kvc_spec, kvp_spec, g_specs = _attn_specs(Q, KV, nb)
    last = lambda i: (jnp.minimum(i, nb - 1), 0)
    row = pl.BlockSpec((WINDOW, Q), last)
    return _call(
        body, name=name, grid=(nb + 1,),
        in_specs=[pl.BlockSpec(memory_space=pltpu.SMEM), q_spec, kvc_spec, kvp_spec] + g_specs + [row, row],
        out_specs=[pl.BlockSpec((WINDOW, P), last),
                   pl.BlockSpec((WINDOW, 2 * KV), lambda i: (jnp.maximum(i - 1, 0), 0)),
                   pl.BlockSpec((H, LANES), lambda i: (0, 0))],
        out_shape=[jax.ShapeDtypeStruct((T, P), BF16), jax.ShapeDtypeStruct((T, 2 * KV), BF16),
                   jax.ShapeDtypeStruct((H, LANES), F32)],
        scratch_shapes=[pltpu.VMEM((WINDOW, 2 * KV), F32)],
        semantics=("arbitrary",), args=(sinks, proj, proj, proj) + (proj,) * ng + (out, dyp), comm=comm)


def _fill_columns(full, part, col_block, *, name):
    T, w = part.shape
    tm = _pick(T, 1024, SUBLANES * 2)

    def body(full_ref, part_ref, o_ref):
        del full_ref
        o_ref[...] = part_ref[...]

    return pl.pallas_call(
        body, name=name, grid=(T // tm,),
        in_specs=[pl.BlockSpec(memory_space=pl.ANY), pl.BlockSpec((tm, w), lambda i: (i, 0))],
        out_specs=pl.BlockSpec((tm, w), lambda i: (i, col_block)),
        out_shape=jax.ShapeDtypeStruct(full.shape, full.dtype),
        input_output_aliases={0: 0},
        compiler_params=_cparams(("parallel",)),
    )(full, part)


LRU_CHUNK = 256


def _shift_down(x, halo8, s):
    if s == 0:
        return x
    row8 = lax.broadcasted_iota(jnp.int32, (SUBLANES, 1), 0)
    r = pltpu.roll(x, s, 0)
    top = jnp.where(row8 < s, pltpu.roll(halo8, s, 0), r[:SUBLANES])
    return jnp.concatenate([top, r[SUBLANES:]], axis=0)


def _shift_up(x, halo8, s):
    if s == 0:
        return x
    n = x.shape[0]
    row8 = lax.broadcasted_iota(jnp.int32, (SUBLANES, 1), 0)
    r = pltpu.roll(x, n - s, 0)
    bot = jnp.where(row8 >= SUBLANES - s, pltpu.roll(halo8, SUBLANES - s, 0), r[n - SUBLANES:])
    return jnp.concatenate([r[:n - SUBLANES], bot], axis=0)


def _scan_fwd(a, b, c0):
    n = a.shape[0]
    row = lax.broadcasted_iota(jnp.int32, (n, 1), 0) & (SUBLANES - 1)
    s = 1
    while s < SUBLANES:
        keep = row >= s
        ar = jnp.where(keep, pltpu.roll(a, s, 0), 1.0)
        br = jnp.where(keep, pltpu.roll(b, s, 0), 0.0)
        b = a * br + b
        a = a * ar
        s *= 2
    out, c = [], c0
    for i in range(n // SUBLANES):
        rows = slice(i * SUBLANES, (i + 1) * SUBLANES)
        h = a[rows] * c + b[rows]
        out.append(h)
        c = h[SUBLANES - 1:]
    return jnp.concatenate(out, axis=0)


def _scan_rev(al, b, c0):
    n = al.shape[0]
    row = lax.broadcasted_iota(jnp.int32, (n, 1), 0) & (SUBLANES - 1)
    s = 1
    while s < SUBLANES:
        keep = row < SUBLANES - s
        ar = jnp.where(keep, pltpu.roll(al, n - s, 0), 1.0)
        br = jnp.where(keep, pltpu.roll(b, n - s, 0), 0.0)
        b = b + al * br
        al = al * ar
        s *= 2
    out, c = [], c0
    for i in reversed(range(n // SUBLANES)):
        rows = slice(i * SUBLANES, (i + 1) * SUBLANES)
        l = b[rows] + al[rows] * c
        out.append(l)
        c = l[:1]
    return jnp.concatenate(out[::-1], axis=0)


def _log1p_pos(z):
    return jnp.where(z < 0.01, z * (1.0 - z * (0.5 - z * (1.0 / 3.0))), jnp.log(1.0 + z))


def _one_minus_sq(a, log_a):
    x = 2.0 * log_a
    series = -x * (1.0 + x * (0.5 + x * (1.0 / 6.0)))
    return jnp.where(x > -0.02, series, 1.0 - a * a)


def _softplus_neg(lam):
    return jnp.maximum(-lam, 0.0) + _log1p_pos(jnp.exp(-jnp.abs(lam)))


def _lru_gates(xb, halo, wa, wx, ba, bx, cw_ref, cb, lam, with_inverse=False):
    xs = [_shift_down(xb, halo, s) for s in range(CONV_W)]
    xc = cb + xs[3] * cw_ref[0:1, :] + xs[2] * cw_ref[1:2, :] + xs[1] * cw_ref[2:3, :] + xs[0] * cw_ref[3:4, :]
    xcb = xc.astype(BF16)
    r = _sigmoid(_dot(xcb, wa) + ba)
    ig = _sigmoid(_dot(xcb, wx) + bx)
    sp = _softplus_neg(lam)
    log_a = (-C_RG * r) * sp
    a = jnp.exp(log_a)
    z = _one_minus_sq(a, log_a)
    if not with_inverse:
        return xs, xc, xcb, r, ig, sp, a, jnp.sqrt(z), None
    rmult = lax.rsqrt(z)
    return xs, xc, xcb, r, ig, sp, a, z * rmult, rmult


def _tile_rows(dtype):
    return SUBLANES * 4 // jnp.dtype(dtype).itemsize


def _last_rows(ref):
    return ref[...].astype(F32)[ref.shape[0] - SUBLANES:]


def _lru_specs(nh, nt, tc, rev):
    tix = (lambda t: nt - 1 - t) if rev else (lambda t: t)
    chunk = lambda off: pl.BlockSpec((tc, LRU_BLOCK), lambda h, t: (tix(t), h + off))
    prev8 = lambda off, rows: pl.BlockSpec((rows, LRU_BLOCK),
                                           lambda h, t: (jnp.maximum(tix(t) * (tc // rows) - 1, 0), h + off))
    wblk = pl.BlockSpec((None, LRU_BLOCK, LRU_BLOCK), lambda h, t: (h, 0, 0))
    vec = pl.BlockSpec((1, LRU_BLOCK), lambda h, t: (0, h))
    cwb = pl.BlockSpec((CONV_W, LRU_BLOCK), lambda h, t: (0, h))
    return tix, chunk, prev8, wblk, vec, cwb


def _lru_fwd(proj, wa, wx, ba, bx, cw, cb, lam, *, comm=None, name):
    T, W2 = proj.shape
    W = W2 // 2
    nh = W // LRU_BLOCK
    tc = _pick(T, LRU_CHUNK, SUBLANES)
    nt = T // tc

    def body(xb_ref, xh_ref, gt_ref, wa_ref, wx_ref, ba_ref, bx_ref, cw_ref, cb_ref, lam_ref, hs_ref, yp_ref, carry_ref):
        t = pl.program_id(1)

        @pl.when(t == 0)
        def _():
            carry_ref[...] = jnp.zeros_like(carry_ref)

        halo = jnp.where(t > 0, _last_rows(xh_ref), 0.0)
        _, xc, _, _, ig, _, a, mult, _ = _lru_gates(xb_ref[...].astype(F32), halo, wa_ref[...], wx_ref[...], ba_ref[...],
                                                  bx_ref[...], cw_ref, cb_ref[...], lam_ref[...])
        hs = _scan_fwd(a, mult * (ig * xc), carry_ref[SUBLANES - 1:SUBLANES, :])
        hs_ref[...] = hs
        carry_ref[...] = hs[tc - SUBLANES:]
        g = gt_ref[...].astype(F32)
        yp_ref[...] = (hs * (g * _sigmoid(g))).astype(BF16)

    _, chunk, prev8, wblk, vec, cwb = _lru_specs(nh, nt, tc, False)
    return _call(
        body, name=name, grid=(nh, nt),
        in_specs=[chunk(0), prev8(0, _tile_rows(proj.dtype)), chunk(nh), wblk, wblk, vec, vec, cwb, vec, vec],
        out_specs=[chunk(0), chunk(0)],
        out_shape=[jax.ShapeDtypeStruct((T, W), F32), jax.ShapeDtypeStruct((T, W), BF16)],
        scratch_shapes=[pltpu.VMEM((SUBLANES, LRU_BLOCK), F32)],
        semantics=("parallel", "arbitrary"), args=(proj, proj, proj, wa, wx, ba, bx, cw, cb, lam), comm=comm)


def _lru_bwd(proj, hs, dyp, wa, wx, ba, bx, cw, cb, lam, *, comm=None, name):
    T, W2 = proj.shape
    W = W2 // 2
    nh = W // LRU_BLOCK
    tc = _pick(T, LRU_CHUNK, SUBLANES)
    nt = T // tc

    def body(xb_ref, xh_ref, gt_ref, hs_ref, hh_ref, dyp_ref, wa_ref, wx_ref, ba_ref, bx_ref, cw_ref, cb_ref, lam_ref,
             dx_ref, dg_ref, dwa_ref, dwx_ref, dba_ref, dbx_ref, dcw_ref, dcb_ref, dlam_ref,
             ca_ref, cl_ref, cx_ref):
        t = pl.program_id(1)
        first = t == nt - 1

        @pl.when(t == 0)
        def _():
            for ref in (ca_ref, cl_ref, cx_ref, dwa_ref, dwx_ref, dba_ref, dbx_ref, dcw_ref, dcb_ref, dlam_ref):
                ref[...] = jnp.zeros_like(ref)

        xb = xb_ref[...].astype(F32)
        halo = jnp.where(first, 0.0, _last_rows(xh_ref))
        wa = wa_ref[...]
        wx = wx_ref[...]
        lam = lam_ref[...]
        xs, xc, xcb, r, ig, sp, a, mult, rmult = _lru_gates(xb, halo, wa, wx, ba_ref[...], bx_ref[...], cw_ref, cb_ref[...], lam,
                                                            with_inverse=True)
        hsv = hs_ref[...]
        g = gt_ref[...].astype(F32)
        dy = dyp_ref[...].astype(F32)
        sg = _sigmoid(g)
        dg_ref[...] = (dy * hsv * (sg * (1.0 + g * (1.0 - sg)))).astype(dg_ref.dtype)
        dhs = dy * (g * sg)

        al = _shift_up(a, ca_ref[...], 1)
        lmb = _scan_rev(al, dhs, cl_ref[0:1, :])
        hprev = _shift_down(hsv, jnp.where(first, 0.0, _last_rows(hh_ref)), 1)
        da = lmb * hprev
        ixc = ig * xc
        dmult = lmb * ixc
        dlog_a = a * (da - dmult * a * rmult)
        dr = dlog_a * (-C_RG * sp)
        dlam_ref[...] += jnp.sum(dlog_a * r, axis=0, keepdims=True) * (C_RG * _sigmoid(-lam))
        dpa = dr * (r * (1.0 - r))
        dpx = (lmb * mult * xc) * (ig * (1.0 - ig))
        dpab = dpa.astype(BF16)
        dpxb = dpx.astype(BF16)
        dwa_ref[...] += _dot_tn(xcb, dpab)
        dwx_ref[...] += _dot_tn(xcb, dpxb)
        dba_ref[...] += jnp.sum(dpa, axis=0, keepdims=True)
        dbx_ref[...] += jnp.sum(dpx, axis=0, keepdims=True)
        dxc = lmb * mult * ig + _dot_nt(dpab, wa) + _dot_nt(dpxb, wx)
        dcb_ref[...] += jnp.sum(dxc, axis=0, keepdims=True)
        for s in range(CONV_W):
            dcw_ref[CONV_W - 1 - s:CONV_W - s, :] += jnp.sum(dxc * xs[s], axis=0, keepdims=True)
        cxv = cx_ref[...]
        dxb = dxc * cw_ref[3:4, :]
        for s in range(1, CONV_W):
            dxb = dxb + _shift_up(dxc, cxv, s) * cw_ref[3 - s:4 - s, :]
        dx_ref[...] = dxb.astype(dx_ref.dtype)
        ca_ref[...] = a[:SUBLANES]
        cl_ref[...] = lmb[:SUBLANES]
        cx_ref[...] = dxc[:SUBLANES]

    tix, chunk, prev8, wblk, vec, cwb = _lru_specs(nh, nt, tc, True)
    hchunk = pl.BlockSpec((tc, LRU_BLOCK), lambda h, t: (tix(t), h))
    carry = pltpu.VMEM((SUBLANES, LRU_BLOCK), F32)
    return _call(
        body, name=name, grid=(nh, nt),
        in_specs=[chunk(0), prev8(0, _tile_rows(proj.dtype)), chunk(nh), hchunk, prev8(0, _tile_rows(hs.dtype)), hchunk,
                  wblk, wblk, vec, vec, cwb, vec, vec],
        out_specs=[hchunk, hchunk, wblk, wblk, vec, vec, cwb, vec, vec],
        out_shape=[jax.ShapeDtypeStruct((T, W), BF16), jax.ShapeDtypeStruct((T, W), BF16),
                   jax.ShapeDtypeStruct((nh, LRU_BLOCK, LRU_BLOCK), F32), jax.ShapeDtypeStruct((nh, LRU_BLOCK, LRU_BLOCK), F32),
                   jax.ShapeDtypeStruct((1, W), F32), jax.ShapeDtypeStruct((1, W), F32),
                   jax.ShapeDtypeStruct((CONV_W, W), F32), jax.ShapeDtypeStruct((1, W), F32), jax.ShapeDtypeStruct((1, W), F32)],
        scratch_shapes=[carry, carry, carry], semantics=("parallel", "arbitrary"),
        args=(proj, proj, proj, hs, hs, dyp, wa, wx, ba, bx, cw, cb, lam), comm=comm)


def _position():
    return lax.axis_index("x"), lax.axis_index("y"), lax.axis_index("c")


def _sems(n):
    return [pltpu.SemaphoreType.DMA((n, 7)), pltpu.SemaphoreType.DMA((n, 7)), pltpu.SemaphoreType.DMA((n,))]


def _gather_comm(arrs, axes):
    n = len(arrs)

    def tools(ins, outs, send_sems, recv_sems, local_sems):
        x, y, c = _position()
        me, sibling = (x, y, c), (x, y, 1 - c)
        chips = [(1 - x, y), (x, 1 - y), (1 - x, 1 - y)]

        def slot(a, pos):
            return outs[a].at[(slice(None),) * axes[a] + (pos,)]

        def copy(a, k, block, to, src=None):
            px, py, pc = block
            rows = slot(a, 4 * px + 2 * py + pc)
            return pltpu.make_async_remote_copy(
                src_ref=rows if src is None else src, dst_ref=rows,
                send_sem=send_sems.at[a, k], recv_sem=recv_sems.at[a, k],
                device_id=to, device_id_type=MESH)

        own = lambda a: pltpu.make_async_copy(ins[a], slot(a, 4 * x + 2 * y + c), local_sems.at[a])
        first = lambda a: ([copy(a, 0, me, sibling, src=ins[a])]
                           + [copy(a, 1 + j, me, (*chip, c), src=ins[a]) for j, chip in enumerate(chips)])
        passed = lambda a: [copy(a, 4 + j, (*chip, c), sibling) for j, chip in enumerate(chips)]
        return me, sibling, chips, c, copy, own, first, passed

    def start(ins, outs, *sems):
        *_, own, first, _ = tools(ins, outs, *sems)
        for a in range(n):
            own(a).start()
            for cp in first(a):
                cp.start()

    def mid(ins, outs, *sems):
        me, _, chips, c, copy, _, _, passed = tools(ins, outs, *sems)
        for a in range(n):
            fwd = passed(a)
            for j, chip in enumerate(chips):
                copy(a, 1 + j, (*chip, c), me).wait_recv()
                fwd[j].start()

    def finish(ins, outs, *sems):
        me, sibling, chips, c, copy, own, first, passed = tools(ins, outs, *sems)
        for a in range(n):
            copy(a, 0, sibling, me).wait_recv()
            for j, chip in enumerate(chips):
                copy(a, 4 + j, (*chip, 1 - c), me).wait_recv()
        for a in range(n):
            for cp in first(a) + passed(a):
                cp.wait_send()
            own(a).wait()

    shapes = [jax.ShapeDtypeStruct(a.shape[:ax] + (N_DEV,) + a.shape[ax:], a.dtype) for a, ax in zip(arrs, axes)]
    return Comm(list(arrs), shapes, _sems(n), start, mid, finish)


def _comm_call(comm, *, name):
    ci, co = len(comm.arrays), len(comm.out_shapes)

    def body(*refs):
        ins, outs, sems = refs[:ci], refs[ci:ci + co], refs[ci + co:]
        comm.start(ins, outs, *sems)
        comm.mid(ins, outs, *sems)
        comm.finish(ins, outs, *sems)

    any_spec = pl.BlockSpec(memory_space=pl.ANY)
    return pl.pallas_call(body, name=name, in_specs=[any_spec] * ci, out_specs=[any_spec] * co,
                          out_shape=comm.out_shapes, scratch_shapes=comm.sems)(*comm.arrays)


def _exchange_comm(items):
    n = len(items)

    def tools(ins, outs, send_sems, recv_sems, local_sems):
        x, y, c = _position()
        me = 4 * x + 2 * y + c

        def src(a, pos):
            if items[a][1]:
                rows = ins[a].shape[1] // N_DEV
                return ins[a].at[:, pl.ds(pl.multiple_of(pos * rows, rows), rows)]
            return ins[a].at[pos]

        copies = [pltpu.make_async_copy(src(a, me), outs[a].at[me], local_sems.at[a]) for a in range(n)]
        for k in range(1, N_DEV):
            px = x ^ ((k >> 2) & 1)
            py = y ^ ((k >> 1) & 1)
            pc = c ^ (k & 1)
            copies += [pltpu.make_async_remote_copy(
                src_ref=src(a, 4 * px + 2 * py + pc), dst_ref=outs[a].at[me],
                send_sem=send_sems.at[a, k - 1], recv_sem=recv_sems.at[a, k - 1],
                device_id=(px, py, pc), device_id_type=MESH) for a in range(n)]
        return copies

    def start(ins, outs, *sems):
        for cp in tools(ins, outs, *sems):
            cp.start()

    def mid(ins, outs, *sems):
        pass

    def finish(ins, outs, *sems):
        for cp in tools(ins, outs, *sems):
            cp.wait()

    shapes = []
    for arr, split in items:
        blk = (arr.shape[0], arr.shape[1] // N_DEV) + arr.shape[2:] if split else arr.shape[1:]
        shapes.append(jax.ShapeDtypeStruct((N_DEV,) + blk, arr.dtype))
    return Comm([arr for arr, _ in items], shapes, _sems(n), start, mid, finish)


def _adamw(parts, w, m, v, *, name):
    L, R, C = w.shape
    assert len(parts) == L
    row_bytes = 2 * (L * N_DEV * C * parts[0].dtype.itemsize + 7 * C * 4)
    tr = _pick(R, max(16, ADAMW_VMEM // row_bytes), 16)
    nr = R // tr
    c1 = 1.0 / (1.0 - ADAM_B1 ** ADAM_STEP)
    c2 = 1.0 / (1.0 - ADAM_B2 ** ADAM_STEP)

    def body(*refs):
        p_refs = refs[:L]
        w_ref, m_ref, v_ref, g_ref, d_ref, nm_ref, nv_ref = refs[L:]
        layer = pl.program_id(0)
        for idx, p_ref in enumerate(p_refs):
            @pl.when(layer == idx)
            def _():
                g = p_ref[0].astype(F32)
                for s in range(1, N_DEV):
                    g = g + p_ref[s].astype(F32)
                nm = ADAM_B1 * m_ref[...] + (1.0 - ADAM_B1) * g
                nv = ADAM_B2 * v_ref[...] + (1.0 - ADAM_B2) * (g * g)
                g_ref[...] = g
                nm_ref[...] = nm
                nv_ref[...] = nv
                d_ref[...] = -ADAM_LR * ((nm * c1) / (jnp.sqrt(nv * c2) + ADAM_EPS) + ADAM_WD * w_ref[...])

    def part_spec(idx):
        return pl.BlockSpec((N_DEV, tr, C),
                            lambda l, i: (0, jnp.where(l == idx, i, jnp.where(l < idx, 0, nr - 1)), 0))

    blk = pl.BlockSpec((None, tr, C), lambda l, i: (l, i, 0))
    return pl.pallas_call(
        body, name=name, grid=(L, nr),
        in_specs=[part_spec(idx) for idx in range(L)] + [blk, blk, blk],
        out_specs=[blk] * 4,
        out_shape=[jax.ShapeDtypeStruct((L, R, C), F32)] * 4,
        compiler_params=_cparams(("arbitrary", "arbitrary")),
    )(*parts, w, m, v)


def _pack(flat_parts, row_multiple, dtype):
    lead = flat_parts[0].shape[:-1]
    total = sum(p.shape[-1] for p in flat_parts)
    quantum = PACK_W * row_multiple
    padded = -(-total // quantum) * quantum
    parts = [p.astype(dtype) for p in flat_parts]
    if padded > total:
        parts.append(jnp.zeros(lead + (padded - total,), dtype))
    return jnp.concatenate(parts, axis=-1).reshape(lead + (padded // PACK_W, PACK_W))


def _unpack(buf, shapes):
    lead = buf.shape[:-2]
    flat = buf.reshape(lead + (-1,))
    out, off = [], 0
    for shp in shapes:
        n = math.prod(shp)
        out.append(flat[..., off:off + n].reshape(lead + tuple(shp)))
        off += n
    return out


def _to_full(seg, ax):
    shard = seg.shape[1:]
    full = shard[:ax] + (N_DEV * shard[ax],) + shard[ax + 1:]
    return jnp.moveaxis(seg, 0, ax).reshape(full)


def _to_shards(full, ax):
    shp = full.shape
    split = shp[:ax] + (N_DEV, shp[ax] // N_DEV) + shp[ax + 1:]
    return jnp.moveaxis(full.reshape(split), ax, 0).reshape(N_DEV, -1)


BIG = (("attn_w_in", 1), ("attn_w_out", 1), ("lru_w_in", 1), ("lru_w_a", 2), ("lru_w_x", 2), ("lru_w_out", 1))
SMALL = (("lru_conv_w", 2), ("lru_conv_b", 1), ("lru_b_a", 2), ("lru_b_x", 2), ("lru_lambda", 1))
REPL = ("norm_pre", "norm_post", "attn_sinks")
ORDER = ("norm_pre", "norm_post", "attn_w_in", "attn_w_out", "attn_sinks", "lru_w_in", "lru_conv_w", "lru_conv_b",
         "lru_w_a", "lru_b_a", "lru_w_x", "lru_b_x", "lru_lambda", "lru_w_out")


def kernel(x, norm_pre, norm_post, attn_w_in, attn_w_out, attn_sinks, lru_w_in, lru_conv_w, lru_conv_b, lru_w_a, lru_b_a, lru_w_x, lru_b_x, lru_lambda, lru_w_out, loss_target, m_norm_pre, m_norm_post, m_attn_w_in, m_attn_w_out, m_attn_sinks, m_lru_w_in, m_lru_conv_w, m_lru_conv_b, m_lru_w_a, m_lru_b_a, m_lru_w_x, m_lru_b_x, m_lru_lambda, m_lru_w_out, v_norm_pre, v_norm_post, v_attn_w_in, v_attn_w_out, v_attn_sinks, v_lru_w_in, v_lru_conv_w, v_lru_conv_b, v_lru_w_a, v_lru_b_a, v_lru_w_x, v_lru_b_x, v_lru_lambda, v_lru_w_out):
    W = dict(norm_pre=norm_pre, norm_post=norm_post, attn_w_in=attn_w_in, attn_w_out=attn_w_out, attn_sinks=attn_sinks,
             lru_w_in=lru_w_in, lru_conv_w=lru_conv_w, lru_conv_b=lru_conv_b, lru_w_a=lru_w_a, lru_b_a=lru_b_a,
             lru_w_x=lru_w_x, lru_b_x=lru_b_x, lru_lambda=lru_lambda, lru_w_out=lru_w_out)
    M = dict(norm_pre=m_norm_pre, norm_post=m_norm_post, attn_w_in=m_attn_w_in, attn_w_out=m_attn_w_out,
             attn_sinks=m_attn_sinks, lru_w_in=m_lru_w_in, lru_conv_w=m_lru_conv_w, lru_conv_b=m_lru_conv_b,
             lru_w_a=m_lru_w_a, lru_b_a=m_lru_b_a, lru_w_x=m_lru_w_x, lru_b_x=m_lru_b_x, lru_lambda=m_lru_lambda,
             lru_w_out=m_lru_w_out)
    V = dict(norm_pre=v_norm_pre, norm_post=v_norm_post, attn_w_in=v_attn_w_in, attn_w_out=v_attn_w_out,
             attn_sinks=v_attn_sinks, lru_w_in=v_lru_w_in, lru_conv_w=v_lru_conv_w, lru_conv_b=v_lru_conv_b,
             lru_w_a=v_lru_w_a, lru_b_a=v_lru_b_a, lru_w_x=v_lru_w_x, lru_b_x=v_lru_b_x, lru_lambda=v_lru_lambda,
             lru_w_out=v_lru_w_out)

    h0 = x[0]
    target = loss_target[0]
    T, D = h0.shape
    depth = norm_pre.shape[0]
    n_attn = attn_w_in.shape[0]
    Q = attn_w_out.shape[1] * N_DEV
    KV = Q // GROUP
    LW = lru_w_out.shape[1] * N_DEV
    nh = LW // LRU_BLOCK

    n_lru = lru_w_in.shape[0]
    big_names = [n for n, _ in BIG]
    small_names = [n for n, _ in SMALL]
    small_shapes = [W[n].shape for n in small_names]
    repl_shapes = [W[n].shape for n in REPL]

    flat = lambda a: a.reshape(-1)
    def layer_shards(layer):
        j = layer // 2
        names = ("attn_w_in", "attn_w_out") if layer % 2 == 0 else ("lru_w_in", "lru_w_a", "lru_w_x", "lru_w_out")
        return [W[n][j].astype(BF16) for n in names], [1 if n in ("lru_w_a", "lru_w_x") else 0 for n in names]

    def layer_weights(layer, gathered):
        if layer % 2 == 0:
            g_in, g_out = gathered
            w_in = jnp.moveaxis(g_in, 0, 1).reshape(D, -1)
            return dict(w_in=w_in, w_out=None if g_out is None else g_out.reshape(Q, D))
        g_in, g_wa, g_wx, g_out = gathered
        return dict(w_in=jnp.moveaxis(g_in, 0, 1).reshape(D, 2 * LW), w_a=g_wa.reshape(nh, LRU_BLOCK, LRU_BLOCK),
                    w_x=g_wx.reshape(nh, LRU_BLOCK, LRU_BLOCK), w_out=g_out.reshape(LW, D))

    arrs0, _ = layer_shards(0)
    (u,), first = _rms_fwd(h0, norm_pre[0:1], name="rms_fwd",
                           comm=_gather_comm([arrs0[0], _pack([flat(W[n]) for n in small_names], SUBLANES, F32)], [0, 0]))
    weights = {0: layer_weights(0, (first[0], None))}
    full = {}
    for (n, ax), seg in zip(SMALL, _unpack(first[-1], small_shapes)):
        full[n] = _to_full(seg, ax)
    cw_f = full["lru_conv_w"]
    cb_f = full["lru_conv_b"][:, None, :]
    ba_f = full["lru_b_a"].reshape(-1, 1, LW)
    bx_f = full["lru_b_x"].reshape(-1, 1, LW)
    lam_f = full["lru_lambda"][:, None, :]

    h = h0
    saved = []
    for layer in range(depth):
        j = layer // 2
        wl = weights[layer]
        nxt = _gather_comm(*layer_shards(layer + 1)) if layer + 1 < depth else None
        if layer % 2 == 0:
            if wl["w_out"] is None:
                proj, got = _matmul(u, wl["w_in"], out_dtype=BF16, comm=_gather_comm([arrs0[1]], [0]), name="attn_in")
                wl["w_out"] = got[0].reshape(Q, D)
            else:
                proj = _matmul(u, wl["w_in"], out_dtype=BF16, name="attn_in")
            (mix, ypre), got = _attn_fwd(proj, attn_sinks[j], comm=nxt, name="attn_fwd")
            y = _matmul(ypre, wl["w_out"], name="attn_out")
        else:
            proj = _matmul(u, wl["w_in"], out_dtype=BF16, name="lru_in")
            (mix, ypre), got = _lru_fwd(proj, wl["w_a"], wl["w_x"], ba_f[j], bx_f[j], cw_f[j], cb_f[j], lam_f[j],
                                        comm=nxt, name="lru_fwd")
            y = _matmul(ypre, wl["w_out"], name="lru_out")
        if nxt is not None:
            weights[layer + 1] = layer_weights(layer + 1, got)
        saved.append((h, u, proj, mix, ypre, y))
        if layer + 1 < depth:
            h, u = _post_fwd(y, norm_post[layer:layer + 1], h, norm_pre[layer + 1:layer + 2], name="post_fwd")
        else:
            dh, loss_part = _post_loss(y, norm_post[layer:layer + 1], h, target, name="post_loss")

    g_pre = [None] * depth
    g_post = [None] * depth
    small_vec = [n for n in small_names] + ["attn_sinks"]
    grads = {n: [None] * W[n].shape[0] for n in small_vec}
    recv = {}

    def carried(keyed):
        return _exchange_comm([item for _, item in keyed]) if keyed else None

    def landed(keyed, got):
        for (key, _), r in zip(keyed, got):
            recv[key] = r

    pending = []
    for layer in reversed(range(depth)):
        j = layer // 2
        h_in, u, proj, mix, ypre, y = saved[layer]
        wl = weights[layer]
        if layer == depth - 1:
            dy, g_post[layer] = _rms_bwd(y, norm_post[layer:layer + 1], dh, None, out_dtype=BF16, name="post_bwd")
        if layer % 2 == 0:
            dyp = _matmul(dy, wl["w_out"], tb=True, out_dtype=BF16, name="attn_out_dx")
            dw_out = _matmul(ypre, dy, ta=True, out_dtype=BF16, name="attn_out_dw")
            pending.append((("attn_w_out", j), (dw_out.reshape(N_DEV, Q // N_DEV, D), False)))
            (dqg, dkv, dsink), got = _attn_bwd(proj, mix, dyp, attn_sinks[j], comm=carried(pending), name="attn_bwd")
            landed(pending, got)
            grads["attn_sinks"][j] = dsink[:, 0]
            dproj = _fill_columns(dqg, dkv, Q // (2 * KV), name="attn_dproj")
            dw = _matmul(u, dproj, ta=True, out_dtype=BF16, name="attn_in_dw")
            pending = [(("attn_w_in", j), (jnp.moveaxis(dw.reshape(D, N_DEV, -1), 1, 0), False))]
            if layer == 0:
                du, got = _matmul(dproj, wl["w_in"], tb=True, comm=carried(pending), name="attn_in_dx")
                landed(pending, got)
                pending = []
            else:
                du = _matmul(dproj, wl["w_in"], tb=True, name="attn_in_dx")
        else:
            dyp = _matmul(dy, wl["w_out"], tb=True, out_dtype=BF16, name="lru_out_dx")
            dw_out = _matmul(ypre, dy, ta=True, out_dtype=BF16, name="lru_out_dw")
            pending.append((("lru_w_out", j), (dw_out.reshape(N_DEV, LW // N_DEV, D), False)))
            (dxb, dgt, dwa, dwx, dba, dbx, dcw, dcb, dlam), got = _lru_bwd(
                proj, mix, dyp, wl["w_a"], wl["w_x"], ba_f[j], bx_f[j], cw_f[j], cb_f[j], lam_f[j],
                comm=carried(pending), name="lru_bwd")
            landed(pending, got)
            grads["lru_b_a"][j], grads["lru_b_x"][j] = dba.reshape(nh, LRU_BLOCK), dbx.reshape(nh, LRU_BLOCK)
            grads["lru_conv_w"][j], grads["lru_conv_b"][j], grads["lru_lambda"][j] = dcw, dcb[0], dlam[0]
            dproj = jnp.concatenate([dxb, dgt], axis=1)
            du = _matmul(dproj, wl["w_in"], tb=True, name="lru_in_dx")
            mine = [(("lru_w_a", j), (dwa, True)), (("lru_w_x", j), (dwx, True))]
            dw, got = _matmul(u, dproj, ta=True, out_dtype=BF16, by_owner=True, comm=carried(mine), name="lru_in_dw")
            landed(mine, got)
            pending = [(("lru_w_in", j), (dw, False))]
        if layer > 0:
            dh, dy, g_pre[layer], g_post[layer - 1] = _pre_post_bwd(
                h_in, norm_pre[layer:layer + 1], du, dh, saved[layer - 1][5], norm_post[layer - 1:layer], name="pre_post_bwd")
        else:
            dh, g_pre[layer] = _rms_bwd(h_in, norm_pre[layer:layer + 1], du, dh, out_dtype=F32, name="pre_bwd")

    gfull = {n: jnp.stack(g) for n, g in grads.items()}
    gfull["norm_pre"] = jnp.concatenate(g_pre, axis=0)
    gfull["norm_post"] = jnp.concatenate(g_post, axis=0)

    repl_part = [jnp.broadcast_to(gfull[n].reshape(1, -1), (N_DEV, gfull[n].size)) for n in REPL]
    loss_slot = jnp.broadcast_to(loss_part.reshape(1, 1), (N_DEV, 1))
    send_small = _pack([_to_shards(gfull[n], ax) for n, ax in SMALL] + repl_part + [loss_slot], SUBLANES, F32)
    last = pending + [(("small", 0), (send_small, False))]
    landed(last, _comm_call(carried(last), name="exchange_last"))

    zero1 = jnp.zeros((1,), F32)
    outs = {}
    for n in big_names:
        shp = W[n].shape
        as3 = lambda a: a.reshape((shp[0], -1, shp[-1]))
        parts = [recv[n, j].reshape((N_DEV, -1, shp[-1])) for j in range(shp[0])]
        res = _adamw(parts, as3(W[n]), as3(M[n]), as3(V[n]), name="adamw_" + n)
        for kind, a in zip(("grad", "delta", "new_m", "new_v"), res):
            outs[kind, n] = a.reshape(shp)
    res_small = _adamw([recv["small", 0]],
                       *[_pack([flat(S[n]) for n in small_names] + [flat(S[n]) for n in REPL] + [zero1], SUBLANES, F32)[None]
                         for S in (W, M, V)], name="adamw_small")
    for kind, rs in zip(("grad", "delta", "new_m", "new_v"), res_small):
        for n, a in zip(small_names + list(REPL) + ["loss"], _unpack(rs[0], small_shapes + repl_shapes + [(1,)])):
            outs[kind, n] = a
    loss = outs["grad", "loss"][0]
    result = [loss, dh[None]]
    for kind in ("grad", "delta", "new_m", "new_v"):
        result += [outs[kind, n] for n in ORDER]
    return tuple(result)
```

```python
import math
from typing import Callable, NamedTuple

import jax
import jax.numpy as jnp
from jax import lax
from jax.experimental import pallas as pl
from jax.experimental.pallas import tpu as pltpu

F32 = jnp.float32
BF16 = jnp.bfloat16

N_DEV = 8
HEAD_DIM = 64
GROUP = 8
WINDOW = 128
LRU_BLOCK = 256
CONV_W = 4
C_RG = 8.0
NORM_EPS = 1e-6
MASK_VALUE = -1e30

ADAM_LR = 0.001
ADAM_B1 = 0.9
ADAM_B2 = 0.999
ADAM_EPS = 1e-08
ADAM_WD = 0.01
ADAM_STEP = 10

ROW_BLOCK = 256
LANES = 128
SUBLANES = 8
PACK_W = 1024
VMEM_LIMIT = 56 * 1024 * 1024
MATMUL_VMEM = 36 * 1024 * 1024
ADAMW_VMEM = 24 * 1024 * 1024
MESH = pl.DeviceIdType.MESH


def _cparams(sem=None):
    return pltpu.CompilerParams(dimension_semantics=sem, vmem_limit_bytes=VMEM_LIMIT)


class Comm(NamedTuple):
    arrays: list
    out_shapes: list
    sems: list
    start: Callable
    mid: Callable
    finish: Callable


def _call(body, *, name, grid, in_specs, out_specs, out_shape, scratch_shapes, semantics, args, comm=None):
    if comm is None:
        res = pl.pallas_call(body, name=name, grid=grid, in_specs=in_specs, out_specs=out_specs, out_shape=out_shape,
                             scratch_shapes=scratch_shapes, compiler_params=_cparams(semantics))(*args)
        return list(res), []
    n_in, n_out, n_scr = len(in_specs), len(out_specs), len(scratch_shapes)
    ci, co = len(comm.arrays), len(comm.out_shapes)
    steps = math.prod(grid)

    def hosted(*refs):
        ins, cins = refs[:n_in], refs[n_in:n_in + ci]
        o0 = n_in + ci
        outs, couts = refs[o0:o0 + n_out], refs[o0 + n_out:o0 + n_out + co]
        s0 = o0 + n_out + co
        scr, sems = refs[s0:s0 + n_scr], refs[s0 + n_scr:]
        step = 0
        for ax, g in enumerate(grid):
            step = step * g + pl.program_id(ax)

        @pl.when(step == 0)
        def _():
            comm.start(cins, couts, *sems)

        body(*ins, *outs, *scr)

        @pl.when(step == steps // 2)
        def _():
            comm.mid(cins, couts, *sems)

        @pl.when(step == steps - 1)
        def _():
            comm.finish(cins, couts, *sems)

    any_spec = pl.BlockSpec(memory_space=pl.ANY)
    res = pl.pallas_call(
        hosted, name=name, grid=grid,
        in_specs=list(in_specs) + [any_spec] * ci, out_specs=list(out_specs) + [any_spec] * co,
        out_shape=list(out_shape) + list(comm.out_shapes), scratch_shapes=list(scratch_shapes) + list(comm.sems),
        compiler_params=_cparams(("arbitrary",) * len(grid)),
    )(*args, *comm.arrays)
    return list(res[:n_out]), list(res[n_out:])


def _pick(n, target, quantum):
    best = None
    for t in range(quantum, min(n, target) + 1, quantum):
        if n % t == 0:
            best = t
    return n if best is None else best


def _sigmoid(x):
    return 1.0 / (1.0 + jnp.exp(-x))


def _dot(a, b):
    return lax.dot_general(a, b, (((1,), (0,)), ((), ())), preferred_element_type=F32)


def _dot_nt(a, b):
    return lax.dot_general(a, b, (((1,), (1,)), ((), ())), preferred_element_type=F32)


def _dot_tn(a, b):
    return lax.dot_general(a, b, (((0,), (0,)), ((), ())), preferred_element_type=F32)


def _matmul(a, b, *, ta=False, tb=False, out_dtype=F32, by_owner=False, comm=None, name):
    if ta:
        K, M = a.shape
    else:
        M, K = a.shape
    if tb:
        N, K2 = b.shape
    else:
        K2, N = b.shape
    assert K == K2, (a.shape, b.shape, ta, tb)
    tn = N // N_DEV if by_owner else _pick(N, 1536, 256)
    assert tn % LANES == 0
    osz = jnp.dtype(out_dtype).itemsize

    def vmem_bytes(tm, tk):
        acc = 0 if tk == K else tm * tn * 4
        return 2 * (tm * tk * a.dtype.itemsize + tk * tn * b.dtype.itemsize + tm * tn * osz) + acc

    def deepest(tm):
        return max([t for t in range(256, K + 1, 256) if K % t == 0 and vmem_bytes(tm, t) <= MATMUL_VMEM] or [_pick(K, 512, 256)])

    tm = _pick(M, 1024, 256)
    if deepest(tm) < K and tm % 512 == 0 and deepest(tm // 2) == K:
        tm //= 2
    tk = deepest(tm)
    nk = K // tk
    dot = {(False, False): _dot, (False, True): _dot_nt, (True, False): _dot_tn}[(ta, tb)]

    if nk == 1:
        def body(a_ref, b_ref, o_ref):
            o_ref[...] = dot(a_ref[...].astype(BF16), b_ref[...].astype(BF16)).astype(o_ref.dtype)
        scratch = []
    else:
        def body(a_ref, b_ref, o_ref, acc_ref):
            k = pl.program_id(2)

            @pl.when(k == 0)
            def _():
                acc_ref[...] = jnp.zeros_like(acc_ref)

            acc_ref[...] += dot(a_ref[...].astype(BF16), b_ref[...].astype(BF16))

            @pl.when(k == nk - 1)
            def _():
                o_ref[...] = acc_ref[...].astype(o_ref.dtype)
        scratch = [pltpu.VMEM((tm, tn), F32)]

    a_spec = pl.BlockSpec((tk, tm), lambda j, i, k: (k, i)) if ta else pl.BlockSpec((tm, tk), lambda j, i, k: (i, k))
    b_spec = pl.BlockSpec((tn, tk), lambda j, i, k: (j, k)) if tb else pl.BlockSpec((tk, tn), lambda j, i, k: (k, j))
    if by_owner:
        o_spec = pl.BlockSpec((None, tm, tn), lambda j, i, k: (j, i, 0))
        o_shape = jax.ShapeDtypeStruct((N_DEV, M, tn), out_dtype)
    else:
        o_spec = pl.BlockSpec((tm, tn), lambda j, i, k: (i, j))
        o_shape = jax.ShapeDtypeStruct((M, N), out_dtype)
    res, extra = _call(body, name=name, grid=(N // tn, M // tm, nk), in_specs=[a_spec, b_spec], out_specs=[o_spec],
                       out_shape=[o_shape], scratch_shapes=scratch, semantics=("parallel", "parallel", "arbitrary"),
                       args=(a, b), comm=comm)
    return res[0] if comm is None else (res[0], extra)


def _rms_fwd(h, g, *, comm=None, name):
    T, D = h.shape
    tm = _pick(T, ROW_BLOCK, SUBLANES)

    def body(h_ref, g_ref, u_ref):
        x = h_ref[...]
        r = lax.rsqrt(jnp.mean(x * x, axis=-1, keepdims=True) + NORM_EPS)
        u_ref[...] = ((x * r) * g_ref[...]).astype(u_ref.dtype)

    return _call(
        body, name=name, grid=(T // tm,),
        in_specs=[pl.BlockSpec((tm, D), lambda i: (i, 0)), pl.BlockSpec((1, D), lambda i: (0, 0))],
        out_specs=[pl.BlockSpec((tm, D), lambda i: (i, 0))],
        out_shape=[jax.ShapeDtypeStruct((T, D), BF16)],
        scratch_shapes=[], semantics=("parallel",), args=(h, g), comm=comm)


def _post_fwd(y, g, h, g_next, *, name):
    T, D = y.shape
    tm = _pick(T, ROW_BLOCK, SUBLANES)

    def body(y_ref, g_ref, h_ref, gn_ref, o_ref, u_ref):
        x = y_ref[...]
        r = lax.rsqrt(jnp.mean(x * x, axis=-1, keepdims=True) + NORM_EPS)
        ho = h_ref[...] + (x * r) * g_ref[...]
        o_ref[...] = ho
        rn = lax.rsqrt(jnp.mean(ho * ho, axis=-1, keepdims=True) + NORM_EPS)
        u_ref[...] = ((ho * rn) * gn_ref[...]).astype(u_ref.dtype)

    row = pl.BlockSpec((tm, D), lambda i: (i, 0))
    vec = pl.BlockSpec((1, D), lambda i: (0, 0))
    return pl.pallas_call(
        body, name=name, grid=(T // tm,),
        in_specs=[row, vec, row, vec],
        out_specs=[row, row],
        out_shape=[jax.ShapeDtypeStruct((T, D), F32), jax.ShapeDtypeStruct((T, D), BF16)],
        compiler_params=_cparams(("parallel",)),
    )(y, g, h, g_next)


def _rms_bwd(x, g, dz, res, *, out_dtype, name):
    T, D = x.shape
    tm = _pick(T, ROW_BLOCK, SUBLANES)
    has_res = res is not None

    def body(*refs):
        if has_res:
            x_ref, g_ref, dz_ref, res_ref, dx_ref, dg_ref = refs
        else:
            x_ref, g_ref, dz_ref, dx_ref, dg_ref = refs
        i = pl.program_id(0)

        @pl.when(i == 0)
        def _():
            dg_ref[...] = jnp.zeros_like(dg_ref)

        dx, dg = _norm_dx(x_ref[...], g_ref[...], dz_ref[...])
        dg_ref[...] += dg
        if has_res:
            dx = dx + res_ref[...]
        dx_ref[...] = dx.astype(dx_ref.dtype)

    row = pl.BlockSpec((tm, D), lambda i: (i, 0))
    vec = pl.BlockSpec((1, D), lambda i: (0, 0))
    ins = [x, g, dz] + ([res] if has_res else [])
    return pl.pallas_call(
        body, name=name, grid=(T // tm,),
        in_specs=[row, vec, row] + ([row] if has_res else []),
        out_specs=[row, vec],
        out_shape=[jax.ShapeDtypeStruct((T, D), out_dtype), jax.ShapeDtypeStruct((1, D), F32)],
        compiler_params=_cparams(("arbitrary",)),
    )(*ins)


def _norm_dx(x, g, dz):
    r = lax.rsqrt(jnp.mean(x * x, axis=-1, keepdims=True) + NORM_EPS)
    xhat = x * r
    dxh = dz * g
    dx = r * (dxh - xhat * jnp.mean(dxh * xhat, axis=-1, keepdims=True))
    return dx, jnp.sum(dz * xhat, axis=0, keepdims=True)


def _pre_post_bwd(h_in, g_pre, du, dh, y_below, g_post_below, *, name):
    T, D = h_in.shape
    tm = _pick(T, ROW_BLOCK, SUBLANES)

    def body(h_ref, gp_ref, du_ref, dh_ref, y_ref, gq_ref, dhn_ref, dy_ref, dgp_ref, dgq_ref):
        i = pl.program_id(0)

        @pl.when(i == 0)
        def _():
            dgp_ref[...] = jnp.zeros_like(dgp_ref)
            dgq_ref[...] = jnp.zeros_like(dgq_ref)

        dx, dgp = _norm_dx(h_ref[...], gp_ref[...], du_ref[...])
        dhn = dh_ref[...] + dx
        dhn_ref[...] = dhn
        dgp_ref[...] += dgp
        dy, dgq = _norm_dx(y_ref[...], gq_ref[...], dhn)
        dy_ref[...] = dy.astype(dy_ref.dtype)
        dgq_ref[...] += dgq

    row = pl.BlockSpec((tm, D), lambda i: (i, 0))
    vec = pl.BlockSpec((1, D), lambda i: (0, 0))
    return pl.pallas_call(
        body, name=name, grid=(T // tm,),
        in_specs=[row, vec, row, row, row, vec],
        out_specs=[row, row, vec, vec],
        out_shape=[jax.ShapeDtypeStruct((T, D), F32), jax.ShapeDtypeStruct((T, D), BF16),
                   jax.ShapeDtypeStruct((1, D), F32), jax.ShapeDtypeStruct((1, D), F32)],
        compiler_params=_cparams(("arbitrary",)),
    )(h_in, g_pre, du, dh, y_below, g_post_below)


def _post_loss(y, g, h, target, *, name):
    T, D = h.shape
    tm = _pick(T, ROW_BLOCK, SUBLANES)

    def body(y_ref, g_ref, h_ref, t_ref, dh_ref, l_ref):
        i = pl.program_id(0)

        @pl.when(i == 0)
        def _():
            l_ref[...] = jnp.zeros_like(l_ref)

        x = y_ref[...]
        r = lax.rsqrt(jnp.mean(x * x, axis=-1, keepdims=True) + NORM_EPS)
        e = (h_ref[...] + (x * r) * g_ref[...]) - t_ref[...]
        dh_ref[...] = e * (1.0 / D)
        row = jnp.sum(e * e, axis=-1, keepdims=True) * (0.5 / D)
        l_ref[...] += jnp.sum(row, axis=0, keepdims=True)

    row = pl.BlockSpec((tm, D), lambda i: (i, 0))
    return pl.pallas_call(
        body, name=name, grid=(T // tm,),
        in_specs=[row, pl.BlockSpec((1, D), lambda i: (0, 0)), row, row],
        out_specs=[row, pl.BlockSpec((1, 1), lambda i: (0, 0))],
        out_shape=[jax.ShapeDtypeStruct((T, D), F32), jax.ShapeDtypeStruct((1, 1), F32)],
        compiler_params=_cparams(("arbitrary",)),
    )(y, g, h, target)


def _attn_dims(P):
    Q = P * 4 // 9
    KV = Q // GROUP
    assert 2 * Q + 2 * KV == P and KV % LANES == 0 and Q % (2 * KV) == 0
    return Q, KV


def _attn_specs(Q, KV, nb):
    blk = WINDOW
    row = lambda i: jnp.minimum(i, nb - 1)
    q_spec = pl.BlockSpec((blk, Q), lambda i: (row(i), 0))
    kvc_spec = pl.BlockSpec((blk, 2 * KV), lambda i: (row(i), Q // (2 * KV)))
    kvp_spec = pl.BlockSpec((blk, 2 * KV), lambda i: (jnp.maximum(row(i) - 1, 0), Q // (2 * KV)))
    g_specs = [pl.BlockSpec((blk, 2 * KV), lambda i, b=b: (row(i), Q // (2 * KV) + 1 + b)) for b in range(Q // (2 * KV))]
    return q_spec, kvc_spec, kvp_spec, g_specs


def _stack_gate(g_refs, cols):
    width = g_refs[0].shape[1]
    parts = [g_refs[cs.start // width][:, cs.start % width:cs.start % width + LANES] for cs in cols]
    return jnp.concatenate(parts, axis=0).astype(F32)


PAIRS = GROUP // 2
STACK = PAIRS * WINDOW


def _band_mask(i):
    c = lax.broadcasted_iota(jnp.int32, (2 * WINDOW, STACK), 0)
    r = lax.broadcasted_iota(jnp.int32, (2 * WINDOW, STACK), 1) & (WINDOW - 1)
    first_key = jnp.where(i > 0, 0, WINDOW)
    return (c > r) & (c <= r + WINDOW) & (c >= first_key)


def _group_cols(kvh):
    c0 = kvh * GROUP * HEAD_DIM
    return [slice(c0 + j * LANES, c0 + (j + 1) * LANES) for j in range(PAIRS)]


def _stack(ref, cols, scale=None):
    x = jnp.concatenate([ref[:, cs] for cs in cols], axis=0).astype(F32)
    return x if scale is None else x * scale


def _group_sinks(sink_ref, kvh, half):
    return jnp.concatenate([jnp.full((1, WINDOW), sink_ref[kvh * GROUP + 2 * j + half], F32) for j in range(PAIRS)], axis=1)


def _pair_halves(x128, xt128, e):
    lo = lax.broadcasted_iota(jnp.int32, x128.shape, 1) < HEAD_DIM
    lo_t = lax.broadcasted_iota(jnp.int32, xt128.shape, 0) < HEAD_DIM
    if e == 0:
        x_lo, xt_lo = jnp.where(lo, x128, 0.0), jnp.where(lo_t, xt128, 0.0)
        x_hi, xt_hi = pltpu.roll(x_lo, HEAD_DIM, 1), pltpu.roll(xt_lo, HEAD_DIM, 0)
    else:
        x_hi, xt_hi = jnp.where(lo, 0.0, x128), jnp.where(lo_t, 0.0, xt128)
        x_lo, xt_lo = pltpu.roll(x_hi, HEAD_DIM, 1), pltpu.roll(xt_hi, HEAD_DIM, 0)
    return (x_lo.astype(BF16), x_hi.astype(BF16)), (xt_lo.astype(BF16), xt_hi.astype(BF16))


def _softmax_sink(st, allowed, sink):
    st = jnp.where(allowed, st, MASK_VALUE)
    m = jnp.maximum(jnp.max(st, axis=0, keepdims=True), sink)
    p = jnp.exp(st - m)
    es = jnp.exp(sink - m)
    inv = 1.0 / (jnp.sum(p, axis=0, keepdims=True) + es)
    return p * inv, es * inv


def _attn_fwd(proj, sinks, *, comm=None, name):
    T, P = proj.shape
    Q, KV = _attn_dims(P)
    nb = T // WINDOW
    npairs = KV // LANES
    scale = 1.0 / math.sqrt(HEAD_DIM)

    ng = Q // (2 * KV)

    def body(sink_ref, q_ref, kvc_ref, kvp_ref, *rest):
        g_refs, (out_ref, yp_ref) = rest[:ng], rest[ng:]
        i = pl.program_id(0)
        allowed = _band_mask(i)
        for p in range(npairs):
            ks = slice(p * LANES, (p + 1) * LANES)
            vs = slice(KV + p * LANES, KV + (p + 1) * LANES)
            k128 = jnp.concatenate([kvp_ref[:, ks], kvc_ref[:, ks]], axis=0).astype(F32)
            v128 = jnp.concatenate([kvp_ref[:, vs], kvc_ref[:, vs]], axis=0).astype(F32)
            kt128, vt128 = k128.T, v128.T
            for e in range(2):
                kvh = 2 * p + e
                khalf, _ = _pair_halves(k128, kt128, e)
                _, vthalf = _pair_halves(v128, vt128, e)
                cols = _group_cols(kvh)
                q4 = _stack(q_ref, cols, scale).astype(BF16)
                ot = None
                for half in range(2):
                    st = _dot_nt(khalf[half], q4)
                    pn, _ = _softmax_sink(st, allowed, _group_sinks(sink_ref, kvh, half))
                    o = _dot(vthalf[half], pn.astype(BF16))
                    ot = o if ot is None else ot + o
                o4 = ot.T
                g4 = _stack_gate(g_refs, cols)
                y4 = (o4 * (g4 * _sigmoid(g4))).astype(BF16)
                for j, cs in enumerate(cols):
                    out_ref[:, cs] = o4[j * WINDOW:(j + 1) * WINDOW]
                    yp_ref[:, cs] = y4[j * WINDOW:(j + 1) * WINDOW]

    q_spec, kvc_spec, kvp_spec, g_specs = _attn_specs(Q, KV, nb)
    row = pl.BlockSpec((WINDOW, Q), lambda i: (i, 0))
    return _call(
        body, name=name, grid=(nb,),
        in_specs=[pl.BlockSpec(memory_space=pltpu.SMEM), q_spec, kvc_spec, kvp_spec] + g_specs,
        out_specs=[row, row],
        out_shape=[jax.ShapeDtypeStruct((T, Q), F32), jax.ShapeDtypeStruct((T, Q), BF16)],
        scratch_shapes=[], semantics=("parallel",), args=(sinks, proj, proj, proj) + (proj,) * ng, comm=comm)


def _attn_bwd(proj, out, dyp, sinks, *, comm=None, name):
    T, P = proj.shape
    Q, KV = _attn_dims(P)
    nb = T // WINDOW
    npairs = KV // LANES
    H = Q // HEAD_DIM
    ng = Q // (2 * KV)
    scale = 1.0 / math.sqrt(HEAD_DIM)

    def body(sink_ref, q_ref, kvc_ref, kvp_ref, *rest):
        g_refs, (out_ref, dyp_ref, dqg_ref, dkv_ref, dsink_ref, carry_ref) = rest[:ng], rest[ng:]
        i = pl.program_id(0)

        @pl.when(i == 0)
        def _():
            carry_ref[...] = jnp.zeros_like(carry_ref)
            dsink_ref[...] = jnp.zeros_like(dsink_ref)

        @pl.when(i == nb)
        def _():
            dkv_ref[...] = carry_ref[...].astype(dkv_ref.dtype)

        @pl.when(i < nb)
        def _():
            allowed = _band_mask(i)
            lo = lax.broadcasted_iota(jnp.int32, (2 * WINDOW, LANES), 1) < HEAD_DIM
            sel_lane = lax.broadcasted_iota(jnp.int32, (SUBLANES, LANES), 1) < HEAD_DIM
            sels = (jnp.where(sel_lane, 1.0, 0.0).astype(BF16), jnp.where(sel_lane, 0.0, 1.0).astype(BF16))
            for p in range(npairs):
                ks = slice(p * LANES, (p + 1) * LANES)
                vs = slice(KV + p * LANES, KV + (p + 1) * LANES)
                k128 = jnp.concatenate([kvp_ref[:, ks], kvc_ref[:, ks]], axis=0).astype(F32)
                v128 = jnp.concatenate([kvp_ref[:, vs], kvc_ref[:, vs]], axis=0).astype(F32)
                kt128, vt128 = k128.T, v128.T
                dk_e, dv_e = [], []
                for e in range(2):
                    kvh = 2 * p + e
                    khalf, kthalf = _pair_halves(k128, kt128, e)
                    vhalf, _ = _pair_halves(v128, vt128, e)
                    cols = _group_cols(kvh)
                    q4 = _stack(q_ref, cols, scale).astype(BF16)
                    g4 = _stack_gate(g_refs, cols)
                    o4 = _stack(out_ref, cols)
                    dy4 = _stack(dyp_ref, cols)
                    sg = _sigmoid(g4)
                    do4 = dy4 * (g4 * sg)
                    dg4 = (dy4 * o4 * (sg * (1.0 + g4 * (1.0 - sg)))).astype(dqg_ref.dtype)
                    dod = do4 * o4
                    dod_hi = dod.astype(BF16)
                    dod_lo = (dod - dod_hi.astype(F32)).astype(BF16)
                    do4b = do4.astype(BF16)
                    dqt = None
                    dk_h, dv_h = [], []
                    for half in range(2):
                        delta = jnp.max(_dot_nt(sels[half], dod_hi) + _dot_nt(sels[half], dod_lo), axis=0, keepdims=True)
                        st = _dot_nt(khalf[half], q4)
                        pn, psink = _softmax_sink(st, allowed, _group_sinks(sink_ref, kvh, half))
                        dp = _dot_nt(vhalf[half], do4b)
                        ds = (pn * (dp - delta)).astype(BF16)
                        dq = _dot(kthalf[half], ds)
                        dqt = dq if dqt is None else dqt + dq
                        dk_h.append(_dot(ds, q4))
                        dv_h.append(_dot(pn.astype(BF16), do4b))
                        pd = psink * delta
                        for j in range(PAIRS):
                            n = kvh * GROUP + 2 * j + half
                            dsn = -jnp.sum(pd[:, j * WINDOW:(j + 1) * WINDOW], axis=1, keepdims=True)
                            dsink_ref[n:n + 1, :] += jnp.broadcast_to(dsn, (1, LANES))
                    dq4 = (dqt.T * scale).astype(dqg_ref.dtype)
                    for j, cs in enumerate(cols):
                        dqg_ref[:, cs] = dq4[j * WINDOW:(j + 1) * WINDOW]
                        dqg_ref[:, slice(Q + 2 * KV + cs.start, Q + 2 * KV + cs.stop)] = dg4[j * WINDOW:(j + 1) * WINDOW]
                    acc_k = jnp.where(lo, dk_h[0], dk_h[1])
                    acc_v = jnp.where(lo, dv_h[0], dv_h[1])
                    dk_e.append(acc_k + pltpu.roll(acc_k, HEAD_DIM, 1))
                    dv_e.append(acc_v + pltpu.roll(acc_v, HEAD_DIM, 1))
                for sl, de in ((ks, dk_e), (vs, dv_e)):
                    d128 = jnp.where(lo, de[0], de[1])
                    dkv_ref[:, sl] = (carry_ref[:, sl] + d128[:WINDOW]).astype(dkv_ref.dtype)
                    carry_ref[:, sl] = d128[WINDOW:]

    q_spec, kvc_spec, kvp_spec, g_specs = _attn_specs(Q, KV, nb)
    last = lambda i: (jnp.minimum(i, nb - 1), 0)
    row = pl.BlockSpec((WINDOW, Q), last)
    return _call(
        body, name=name, grid=(nb + 1,),
        in_specs=[pl.BlockSpec(memory_space=pltpu.SMEM), q_spec, kvc_spec, kvp_spec] + g_specs + [row, row],
        out_specs=[pl.BlockSpec((WINDOW, P), last),
                   pl.BlockSpec((WINDOW, 2 * KV), lambda i: (jnp.maximum(i - 1, 0), 0)),
                   pl.BlockSpec((H, LANES), lambda i: (0, 0))],
        out_shape=[jax.ShapeDtypeStruct((T, P), BF16), jax.ShapeDtypeStruct((T, 2 * KV), BF16),
                   jax.ShapeDtypeStruct((H, LANES), F32)],
        scratch_shapes=[pltpu.VMEM((WINDOW, 2 * KV), F32)],
        semantics=("arbitrary",), args=(sinks, proj, proj, proj) + (proj,) * ng + (out, dyp), comm=comm)


def _fill_columns(full, part, col_block, *, name):
    T, w = part.shape
    tm = _pick(T, 1024, SUBLANES * 2)

    def body(full_ref, part_ref, o_ref):
        del full_ref
        o_ref[...] = part_ref[...]

    return pl.pallas_call(
        body, name=name, grid=(T // tm,),
        in_specs=[pl.BlockSpec(memory_space=pl.ANY), pl.BlockSpec((tm, w), lambda i: (i, 0))],
        out_specs=pl.BlockSpec((tm, w), lambda i: (i, col_block)),
        out_shape=jax.ShapeDtypeStruct(full.shape, full.dtype),
        input_output_aliases={0: 0},
        compiler_params=_cparams(("parallel",)),
    )(full, part)


LRU_CHUNK = 1024


def _shift_down(x, halo8, s):
    if s == 0:
        return x
    row8 = lax.broadcasted_iota(jnp.int32, (SUBLANES, 1), 0)
    r = pltpu.roll(x, s, 0)
    top = jnp.where(row8 < s, pltpu.roll(halo8, s, 0), r[:SUBLANES])
    return jnp.concatenate([top, r[SUBLANES:]], axis=0)


def _shift_up(x, halo8, s):
    if s == 0:
        return x
    n = x.shape[0]
    row8 = lax.broadcasted_iota(jnp.int32, (SUBLANES, 1), 0)
    r = pltpu.roll(x, n - s, 0)
    bot = jnp.where(row8 >= SUBLANES - s, pltpu.roll(halo8, SUBLANES - s, 0), r[n - SUBLANES:])
    return jnp.concatenate([r[:n - SUBLANES], bot], axis=0)


def _scan_fwd(a, b, c0):
    n = a.shape[0]
    row = lax.broadcasted_iota(jnp.int32, (n, 1), 0) & (SUBLANES - 1)
    s = 1
    while s < SUBLANES:
        keep = row >= s
        ar = jnp.where(keep, pltpu.roll(a, s, 0), 1.0)
        br = jnp.where(keep, pltpu.roll(b, s, 0), 0.0)
        b = a * br + b
        a = a * ar
        s *= 2
    out, c = [], c0
    for i in range(n // SUBLANES):
        rows = slice(i * SUBLANES, (i + 1) * SUBLANES)
        h = a[rows] * c + b[rows]
        out.append(h)
        c = h[SUBLANES - 1:]
    return jnp.concatenate(out, axis=0)


def _scan_rev(al, b, c0):
    n = al.shape[0]
    row = lax.broadcasted_iota(jnp.int32, (n, 1), 0) & (SUBLANES - 1)
    s = 1
    while s < SUBLANES:
        keep = row < SUBLANES - s
        ar = jnp.where(keep, pltpu.roll(al, n - s, 0), 1.0)
        br = jnp.where(keep, pltpu.roll(b, n - s, 0), 0.0)
        b = b + al * br
        al = al * ar
        s *= 2
    out, c = [], c0
    for i in reversed(range(n // SUBLANES)):
        rows = slice(i * SUBLANES, (i + 1) * SUBLANES)
        l = b[rows] + al[rows] * c
        out.append(l)
        c = l[:1]
    return jnp.concatenate(out[::-1], axis=0)


def _log1p_pos(z):
    return jnp.where(z < 0.01, z * (1.0 - z * (0.5 - z * (1.0 / 3.0))), jnp.log(1.0 + z))


def _one_minus_sq(a, log_a):
    x = 2.0 * log_a
    series = -x * (1.0 + x * (0.5 + x * (1.0 / 6.0)))
    return jnp.where(x > -0.02, series, 1.0 - a * a)


def _softplus_neg(lam):
    return jnp.maximum(-lam, 0.0) + _log1p_pos(jnp.exp(-jnp.abs(lam)))


def _lru_gates(xb, halo, wa, wx, ba, bx, cw_ref, cb, lam, with_inverse=False):
    xs = [_shift_down(xb, halo, s) for s in range(CONV_W)]
    xc = cb + xs[3] * cw_ref[0:1, :] + xs[2] * cw_ref[1:2, :] + xs[1] * cw_ref[2:3, :] + xs[0] * cw_ref[3:4, :]
    xcb = xc.astype(BF16)
    r = _sigmoid(_dot(xcb, wa) + ba)
    ig = _sigmoid(_dot(xcb, wx) + bx)
    sp = _softplus_neg(lam)
    log_a = (-C_RG * r) * sp
    a = jnp.exp(log_a)
    z = _one_minus_sq(a, log_a)
    if not with_inverse:
        return xs, xc, xcb, r, ig, sp, a, jnp.sqrt(z), None
    rmult = lax.rsqrt(z)
    return xs, xc, xcb, r, ig, sp, a, z * rmult, rmult


def _tile_rows(dtype):
    return SUBLANES * 4 // jnp.dtype(dtype).itemsize


def _last_rows(ref):
    return ref[...].astype(F32)[ref.shape[0] - SUBLANES:]


def _lru_specs(nh, nt, tc, rev):
    tix = (lambda t: nt - 1 - t) if rev else (lambda t: t)
    chunk = lambda off: pl.BlockSpec((tc, LRU_BLOCK), lambda h, t: (tix(t), h + off))
    prev8 = lambda off, rows: pl.BlockSpec((rows, LRU_BLOCK),
                                           lambda h, t: (jnp.maximum(tix(t) * (tc // rows) - 1, 0), h + off))
    wblk = pl.BlockSpec((None, LRU_BLOCK, LRU_BLOCK), lambda h, t: (h, 0, 0))
    vec = pl.BlockSpec((1, LRU_BLOCK), lambda h, t: (0, h))
    cwb = pl.BlockSpec((CONV_W, LRU_BLOCK), lambda h, t: (0, h))
    return tix, chunk, prev8, wblk, vec, cwb


def _lru_fwd(proj, wa, wx, ba, bx, cw, cb, lam, *, comm=None, name):
    T, W2 = proj.shape
    W = W2 // 2
    nh = W // LRU_BLOCK
    tc = _pick(T, LRU_CHUNK, SUBLANES)
    nt = T // tc

    def body(xb_ref, xh_ref, gt_ref, wa_ref, wx_ref, ba_ref, bx_ref, cw_ref, cb_ref, lam_ref, hs_ref, yp_ref, carry_ref):
        t = pl.program_id(1)

        @pl.when(t == 0)
        def _():
            carry_ref[...] = jnp.zeros_like(carry_ref)

        halo = jnp.where(t > 0, _last_rows(xh_ref), 0.0)
        _, xc, _, _, ig, _, a, mult, _ = _lru_gates(xb_ref[...].astype(F32), halo, wa_ref[...], wx_ref[...], ba_ref[...],
                                                  bx_ref[...], cw_ref, cb_ref[...], lam_ref[...])
        hs = _scan_fwd(a, mult * (ig * xc), carry_ref[SUBLANES - 1:SUBLANES, :])
        hs_ref[...] = hs
        carry_ref[...] = hs[tc - SUBLANES:]
        g = gt_ref[...].astype(F32)
        yp_ref[...] = (hs * (g * _sigmoid(g))).astype(BF16)

    _, chunk, prev8, wblk, vec, cwb = _lru_specs(nh, nt, tc, False)
    return _call(
        body, name=name, grid=(nh, nt),
        in_specs=[chunk(0), prev8(0, _tile_rows(proj.dtype)), chunk(nh), wblk, wblk, vec, vec, cwb, vec, vec],
        out_specs=[chunk(0), chunk(0)],
        out_shape=[jax.ShapeDtypeStruct((T, W), F32), jax.ShapeDtypeStruct((T, W), BF16)],
        scratch_shapes=[pltpu.VMEM((SUBLANES, LRU_BLOCK), F32)],
        semantics=("parallel", "arbitrary"), args=(proj, proj, proj, wa, wx, ba, bx, cw, cb, lam), comm=comm)


def _lru_bwd(proj, hs, dyp, wa, wx, ba, bx, cw, cb, lam, *, comm=None, name):
    T, W2 = proj.shape
    W = W2 // 2
    nh = W // LRU_BLOCK
    tc = _pick(T, LRU_CHUNK, SUBLANES)
    nt = T // tc

    def body(xb_ref, xh_ref, gt_ref, hs_ref, hh_ref, dyp_ref, wa_ref, wx_ref, ba_ref, bx_ref, cw_ref, cb_ref, lam_ref,
             dx_ref, dg_ref, dwa_ref, dwx_ref, dba_ref, dbx_ref, dcw_ref, dcb_ref, dlam_ref,
             ca_ref, cl_ref, cx_ref):
        t = pl.program_id(1)
        first = t == nt - 1

        @pl.when(t == 0)
        def _():
            for ref in (ca_ref, cl_ref, cx_ref, dwa_ref, dwx_ref, dba_ref, dbx_ref, dcw_ref, dcb_ref, dlam_ref):
                ref[...] = jnp.zeros_like(ref)

        xb = xb_ref[...].astype(F32)
        halo = jnp.where(first, 0.0, _last_rows(xh_ref))
        wa = wa_ref[...]
        wx = wx_ref[...]
        lam = lam_ref[...]
        xs, xc, xcb, r, ig, sp, a, mult, rmult = _lru_gates(xb, halo, wa, wx, ba_ref[...], bx_ref[...], cw_ref, cb_ref[...], lam,
                                                            with_inverse=True)
        hsv = hs_ref[...]
        g = gt_ref[...].astype(F32)
        dy = dyp_ref[...].astype(F32)
        sg = _sigmoid(g)
        dg_ref[...] = (dy * hsv * (sg * (1.0 + g * (1.0 - sg)))).astype(dg_ref.dtype)
        dhs = dy * (g * sg)

        al = _shift_up(a, ca_ref[...], 1)
        lmb = _scan_rev(al, dhs, cl_ref[0:1, :])
        hprev = _shift_down(hsv, jnp.where(first, 0.0, _last_rows(hh_ref)), 1)
        da = lmb * hprev
        ixc = ig * xc
        dmult = lmb * ixc
        dlog_a = a * (da - dmult * a * rmult)
        dr = dlog_a * (-C_RG * sp)
        dlam_ref[...] += jnp.sum(dlog_a * r, axis=0, keepdims=True) * (C_RG * _sigmoid(-lam))
        dpa = dr * (r * (1.0 - r))
        dpx = (lmb * mult * xc) * (ig * (1.0 - ig))
        dpab = dpa.astype(BF16)
        dpxb = dpx.astype(BF16)
        dwa_ref[...] += _dot_tn(xcb, dpab)
        dwx_ref[...] += _dot_tn(xcb, dpxb)
        dba_ref[...] += jnp.sum(dpa, axis=0, keepdims=True)
        dbx_ref[...] += jnp.sum(dpx, axis=0, keepdims=True)
        dxc = lmb * mult * ig + _dot_nt(dpab, wa) + _dot_nt(dpxb, wx)
        dcb_ref[...] += jnp.sum(dxc, axis=0, keepdims=True)
        for s in range(CONV_W):
            dcw_ref[CONV_W - 1 - s:CONV_W - s, :] += jnp.sum(dxc * xs[s], axis=0, keepdims=True)
        cxv = cx_ref[...]
        dxb = dxc * cw_ref[3:4, :]
        for s in range(1, CONV_W):
            dxb = dxb + _shift_up(dxc, cxv, s) * cw_ref[3 - s:4 - s, :]
        dx_ref[...] = dxb.astype(dx_ref.dtype)
        ca_ref[...] = a[:SUBLANES]
        cl_ref[...] = lmb[:SUBLANES]
        cx_ref[...] = dxc[:SUBLANES]

    tix, chunk, prev8, wblk, vec, cwb = _lru_specs(nh, nt, tc, True)
    hchunk = pl.BlockSpec((tc, LRU_BLOCK), lambda h, t: (tix(t), h))
    carry = pltpu.VMEM((SUBLANES, LRU_BLOCK), F32)
    return _call(
        body, name=name, grid=(nh, nt),
        in_specs=[chunk(0), prev8(0, _tile_rows(proj.dtype)), chunk(nh), hchunk, prev8(0, _tile_rows(hs.dtype)), hchunk,
                  wblk, wblk, vec, vec, cwb, vec, vec],
        out_specs=[hchunk, hchunk, wblk, wblk, vec, vec, cwb, vec, vec],
        out_shape=[jax.ShapeDtypeStruct((T, W2), BF16), jax.ShapeDtypeStruct((T, W), BF16),
                   jax.ShapeDtypeStruct((nh, LRU_BLOCK, LRU_BLOCK), F32), jax.ShapeDtypeStruct((nh, LRU_BLOCK, LRU_BLOCK), F32),
                   jax.ShapeDtypeStruct((1, W), F32), jax.ShapeDtypeStruct((1, W), F32),
                   jax.ShapeDtypeStruct((CONV_W, W), F32), jax.ShapeDtypeStruct((1, W), F32), jax.ShapeDtypeStruct((1, W), F32)],
        scratch_shapes=[carry, carry, carry], semantics=("parallel", "arbitrary"),
        args=(proj, proj, proj, hs, hs, dyp, wa, wx, ba, bx, cw, cb, lam), comm=comm)


def _position():
    return lax.axis_index("x"), lax.axis_index("y"), lax.axis_index("c")


def _sems(n):
    return [pltpu.SemaphoreType.DMA((n, 7)), pltpu.SemaphoreType.DMA((n, 7)), pltpu.SemaphoreType.DMA((n,))]


BY_COLUMNS = -1


def _gather_comm(arrs, axes):
    n = len(arrs)

    def tools(ins, outs, send_sems, recv_sems, local_sems):
        x, y, c = _position()
        me, sibling = (x, y, c), (x, y, 1 - c)
        chips = [(1 - x, y), (x, 1 - y), (1 - x, 1 - y)]

        def slot(a, pos):
            if axes[a] == BY_COLUMNS:
                w = ins[a].shape[1]
                return outs[a].at[:, pl.ds(pl.multiple_of(pos * w, LANES), w)]
            return outs[a].at[(slice(None),) * axes[a] + (pos,)]

        def copy(a, k, block, to, src=None):
            px, py, pc = block
            rows = slot(a, 4 * px + 2 * py + pc)
            return pltpu.make_async_remote_copy(
                src_ref=rows if src is None else src, dst_ref=rows,
                send_sem=send_sems.at[a, k], recv_sem=recv_sems.at[a, k],
                device_id=to, device_id_type=MESH)

        own = lambda a: pltpu.make_async_copy(ins[a], slot(a, 4 * x + 2 * y + c), local_sems.at[a])
        first = lambda a: ([copy(a, 0, me, sibling, src=ins[a])]
                           + [copy(a, 1 + j, me, (*chip, c), src=ins[a]) for j, chip in enumerate(chips)])
        passed = lambda a: [copy(a, 4 + j, (*chip, c), sibling) for j, chip in enumerate(chips)]
        return me, sibling, chips, c, copy, own, first, passed

    def start(ins, outs, *sems):
        *_, own, first, _ = tools(ins, outs, *sems)
        for a in range(n):
            own(a).start()
            for cp in first(a):
                cp.start()

    def mid(ins, outs, *sems):
        me, _, chips, c, copy, _, _, passed = tools(ins, outs, *sems)
        for a in range(n):
            fwd = passed(a)
            for j, chip in enumerate(chips):
                copy(a, 1 + j, (*chip, c), me).wait_recv()
                fwd[j].start()

    def finish(ins, outs, *sems):
        me, sibling, chips, c, copy, own, first, passed = tools(ins, outs, *sems)
        for a in range(n):
            copy(a, 0, sibling, me).wait_recv()
            for j, chip in enumerate(chips):
                copy(a, 4 + j, (*chip, 1 - c), me).wait_recv()
        for a in range(n):
            for cp in first(a) + passed(a):
                cp.wait_send()
            own(a).wait()

    shapes = [jax.ShapeDtypeStruct((a.shape[0], N_DEV * a.shape[1]) if ax == BY_COLUMNS else
                                   a.shape[:ax] + (N_DEV,) + a.shape[ax:], a.dtype) for a, ax in zip(arrs, axes)]
    return Comm(list(arrs), shapes, _sems(n), start, mid, finish)


def _comm_call(comm, *, name):
    ci, co = len(comm.arrays), len(comm.out_shapes)

    def body(*refs):
        ins, outs, sems = refs[:ci], refs[ci:ci + co], refs[ci + co:]
        comm.start(ins, outs, *sems)
        comm.mid(ins, outs, *sems)
        comm.finish(ins, outs, *sems)

    any_spec = pl.BlockSpec(memory_space=pl.ANY)
    return pl.pallas_call(body, name=name, in_specs=[any_spec] * ci, out_specs=[any_spec] * co,
                          out_shape=comm.out_shapes, scratch_shapes=comm.sems)(*comm.arrays)


def _exchange_comm(items):
    n = len(items)

    def tools(ins, outs, send_sems, recv_sems, local_sems):
        x, y, c = _position()
        me = 4 * x + 2 * y + c

        def src(a, pos):
            if items[a][1]:
                rows = ins[a].shape[1] // N_DEV
                return ins[a].at[:, pl.ds(pl.multiple_of(pos * rows, rows), rows)]
            return ins[a].at[pos]

        copies = [pltpu.make_async_copy(src(a, me), outs[a].at[me], local_sems.at[a]) for a in range(n)]
        for k in range(1, N_DEV):
            px = x ^ ((k >> 2) & 1)
            py = y ^ ((k >> 1) & 1)
            pc = c ^ (k & 1)
            copies += [pltpu.make_async_remote_copy(
                src_ref=src(a, 4 * px + 2 * py + pc), dst_ref=outs[a].at[me],
                send_sem=send_sems.at[a, k - 1], recv_sem=recv_sems.at[a, k - 1],
                device_id=(px, py, pc), device_id_type=MESH) for a in range(n)]
        return copies

    def start(ins, outs, *sems):
        for cp in tools(ins, outs, *sems):
            cp.start()

    def mid(ins, outs, *sems):
        pass

    def finish(ins, outs, *sems):
        for cp in tools(ins, outs, *sems):
            cp.wait()

    shapes = []
    for arr, split in items:
        blk = (arr.shape[0], arr.shape[1] // N_DEV) + arr.shape[2:] if split else arr.shape[1:]
        shapes.append(jax.ShapeDtypeStruct((N_DEV,) + blk, arr.dtype))
    return Comm([arr for arr, _ in items], shapes, _sems(n), start, mid, finish)


def _adamw(parts, w, m, v, *, name):
    L, R, C = w.shape
    assert len(parts) == L
    row_bytes = 2 * (L * N_DEV * C * parts[0].dtype.itemsize + 7 * C * 4)
    tr = _pick(R, max(16, ADAMW_VMEM // row_bytes), 16)
    nr = R // tr
    c1 = 1.0 / (1.0 - ADAM_B1 ** ADAM_STEP)
    c2 = 1.0 / (1.0 - ADAM_B2 ** ADAM_STEP)

    def body(*refs):
        p_refs = refs[:L]
        w_ref, m_ref, v_ref, g_ref, d_ref, nm_ref, nv_ref = refs[L:]
        layer = pl.program_id(0)
        for idx, p_ref in enumerate(p_refs):
            @pl.when(layer == idx)
            def _():
                g = p_ref[0].astype(F32)
                for s in range(1, N_DEV):
                    g = g + p_ref[s].astype(F32)
                nm = ADAM_B1 * m_ref[...] + (1.0 - ADAM_B1) * g
                nv = ADAM_B2 * v_ref[...] + (1.0 - ADAM_B2) * (g * g)
                g_ref[...] = g
                nm_ref[...] = nm
                nv_ref[...] = nv
                d_ref[...] = -ADAM_LR * ((nm * c1) / (jnp.sqrt(nv * c2) + ADAM_EPS) + ADAM_WD * w_ref[...])

    def part_spec(idx):
        return pl.BlockSpec((N_DEV, tr, C),
                            lambda l, i: (0, jnp.where(l == idx, i, jnp.where(l < idx, 0, nr - 1)), 0))

    blk = pl.BlockSpec((None, tr, C), lambda l, i: (l, i, 0))
    return pl.pallas_call(
        body, name=name, grid=(L, nr),
        in_specs=[part_spec(idx) for idx in range(L)] + [blk, blk, blk],
        out_specs=[blk] * 4,
        out_shape=[jax.ShapeDtypeStruct((L, R, C), F32)] * 4,
        compiler_params=_cparams(("arbitrary", "arbitrary")),
    )(*parts, w, m, v)


def _pack(flat_parts, row_multiple, dtype):
    lead = flat_parts[0].shape[:-1]
    total = sum(p.shape[-1] for p in flat_parts)
    quantum = PACK_W * row_multiple
    padded = -(-total // quantum) * quantum
    parts = [p.astype(dtype) for p in flat_parts]
    if padded > total:
        parts.append(jnp.zeros(lead + (padded - total,), dtype))
    return jnp.concatenate(parts, axis=-1).reshape(lead + (padded // PACK_W, PACK_W))


def _unpack(buf, shapes):
    lead = buf.shape[:-2]
    flat = buf.reshape(lead + (-1,))
    out, off = [], 0
    for shp in shapes:
        n = math.prod(shp)
        out.append(flat[..., off:off + n].reshape(lead + tuple(shp)))
        off += n
    return out


def _to_full(seg, ax):
    shard = seg.shape[1:]
    full = shard[:ax] + (N_DEV * shard[ax],) + shard[ax + 1:]
    return jnp.moveaxis(seg, 0, ax).reshape(full)


def _to_shards(full, ax):
    shp = full.shape
    split = shp[:ax] + (N_DEV, shp[ax] // N_DEV) + shp[ax + 1:]
    return jnp.moveaxis(full.reshape(split), ax, 0).reshape(N_DEV, -1)


BIG = (("attn_w_in", 1), ("attn_w_out", 1), ("lru_w_in", 1), ("lru_w_a", 2), ("lru_w_x", 2), ("lru_w_out", 1))
SMALL = (("lru_conv_w", 2), ("lru_conv_b", 1), ("lru_b_a", 2), ("lru_b_x", 2), ("lru_lambda", 1))
REPL = ("norm_pre", "norm_post", "attn_sinks")
ORDER = ("norm_pre", "norm_post", "attn_w_in", "attn_w_out", "attn_sinks", "lru_w_in", "lru_conv_w", "lru_conv_b",
         "lru_w_a", "lru_b_a", "lru_w_x", "lru_b_x", "lru_lambda", "lru_w_out")


def kernel(x, norm_pre, norm_post, attn_w_in, attn_w_out, attn_sinks, lru_w_in, lru_conv_w, lru_conv_b, lru_w_a, lru_b_a, lru_w_x, lru_b_x, lru_lambda, lru_w_out, loss_target, m_norm_pre, m_norm_post, m_attn_w_in, m_attn_w_out, m_attn_sinks, m_lru_w_in, m_lru_conv_w, m_lru_conv_b, m_lru_w_a, m_lru_b_a, m_lru_w_x, m_lru_b_x, m_lru_lambda, m_lru_w_out, v_norm_pre, v_norm_post, v_attn_w_in, v_attn_w_out, v_attn_sinks, v_lru_w_in, v_lru_conv_w, v_lru_conv_b, v_lru_w_a, v_lru_b_a, v_lru_w_x, v_lru_b_x, v_lru_lambda, v_lru_w_out):
    W = dict(norm_pre=norm_pre, norm_post=norm_post, attn_w_in=attn_w_in, attn_w_out=attn_w_out, attn_sinks=attn_sinks,
             lru_w_in=lru_w_in, lru_conv_w=lru_conv_w, lru_conv_b=lru_conv_b, lru_w_a=lru_w_a, lru_b_a=lru_b_a,
             lru_w_x=lru_w_x, lru_b_x=lru_b_x, lru_lambda=lru_lambda, lru_w_out=lru_w_out)
    M = dict(norm_pre=m_norm_pre, norm_post=m_norm_post, attn_w_in=m_attn_w_in, attn_w_out=m_attn_w_out,
             attn_sinks=m_attn_sinks, lru_w_in=m_lru_w_in, lru_conv_w=m_lru_conv_w, lru_conv_b=m_lru_conv_b,
             lru_w_a=m_lru_w_a, lru_b_a=m_lru_b_a, lru_w_x=m_lru_w_x, lru_b_x=m_lru_b_x, lru_lambda=m_lru_lambda,
             lru_w_out=m_lru_w_out)
    V = dict(norm_pre=v_norm_pre, norm_post=v_norm_post, attn_w_in=v_attn_w_in, attn_w_out=v_attn_w_out,
             attn_sinks=v_attn_sinks, lru_w_in=v_lru_w_in, lru_conv_w=v_lru_conv_w, lru_conv_b=v_lru_conv_b,
             lru_w_a=v_lru_w_a, lru_b_a=v_lru_b_a, lru_w_x=v_lru_w_x, lru_b_x=v_lru_b_x, lru_lambda=v_lru_lambda,
             lru_w_out=v_lru_w_out)

    h0 = x[0]
    target = loss_target[0]
    T, D = h0.shape
    depth = norm_pre.shape[0]
    n_attn = attn_w_in.shape[0]
    Q = attn_w_out.shape[1] * N_DEV
    KV = Q // GROUP
    LW = lru_w_out.shape[1] * N_DEV
    nh = LW // LRU_BLOCK

    n_lru = lru_w_in.shape[0]
    big_names = [n for n, _ in BIG]
    small_names = [n for n, _ in SMALL]
    small_shapes = [W[n].shape for n in small_names]
    repl_shapes = [W[n].shape for n in REPL]

    flat = lambda a: a.reshape(-1)
    def layer_shards(layer):
        j = layer // 2
        names = ("attn_w_in", "attn_w_out") if layer % 2 == 0 else ("lru_w_in", "lru_w_a", "lru_w_x", "lru_w_out")
        axes = dict(lru_w_a=1, lru_w_x=1, lru_w_in=BY_COLUMNS)
        return [W[n][j].astype(BF16) for n in names], [axes.get(n, 0) for n in names]

    def layer_weights(layer, gathered):
        if layer % 2 == 0:
            g_in, g_out = gathered
            w_in = jnp.moveaxis(g_in, 0, 1).reshape(D, -1)
            return dict(w_in=w_in, w_out=None if g_out is None else g_out.reshape(Q, D))
        g_in, g_wa, g_wx, g_out = gathered
        return dict(w_in=g_in, w_a=g_wa.reshape(nh, LRU_BLOCK, LRU_BLOCK),
                    w_x=g_wx.reshape(nh, LRU_BLOCK, LRU_BLOCK), w_out=g_out.reshape(LW, D))

    arrs0, _ = layer_shards(0)
    (u,), first = _rms_fwd(h0, norm_pre[0:1], name="rms_fwd",
                           comm=_gather_comm([arrs0[0], _pack([flat(W[n]) for n in small_names], SUBLANES, F32)], [0, 0]))
    weights = {0: layer_weights(0, (first[0], None))}
    full = {}
    for (n, ax), seg in zip(SMALL, _unpack(first[-1], small_shapes)):
        full[n] = _to_full(seg, ax)
    cw_f = full["lru_conv_w"]
    cb_f = full["lru_conv_b"][:, None, :]
    ba_f = full["lru_b_a"].reshape(-1, 1, LW)
    bx_f = full["lru_b_x"].reshape(-1, 1, LW)
    lam_f = full["lru_lambda"][:, None, :]

    h = h0
    saved = []
    for layer in range(depth):
        j = layer // 2
        wl = weights[layer]
        nxt = _gather_comm(*layer_shards(layer + 1)) if layer + 1 < depth else None
        if layer % 2 == 0:
            if wl["w_out"] is None:
                proj, got = _matmul(u, wl["w_in"], out_dtype=BF16, comm=_gather_comm([arrs0[1]], [0]), name="attn_in")
                wl["w_out"] = got[0].reshape(Q, D)
            else:
                proj = _matmul(u, wl["w_in"], out_dtype=BF16, name="attn_in")
            (mix, ypre), got = _attn_fwd(proj, attn_sinks[j], comm=nxt, name="attn_fwd")
            y = _matmul(ypre, wl["w_out"], name="attn_out")
        else:
            proj = _matmul(u, wl["w_in"], out_dtype=BF16, name="lru_in")
            (mix, ypre), got = _lru_fwd(proj, wl["w_a"], wl["w_x"], ba_f[j], bx_f[j], cw_f[j], cb_f[j], lam_f[j],
                                        comm=nxt, name="lru_fwd")
            y = _matmul(ypre, wl["w_out"], name="lru_out")
        if nxt is not None:
            weights[layer + 1] = layer_weights(layer + 1, got)
        saved.append((h, u, proj, mix, ypre, y))
        if layer + 1 < depth:
            h, u = _post_fwd(y, norm_post[layer:layer + 1], h, norm_pre[layer + 1:layer + 2], name="post_fwd")
        else:
            dh, loss_part = _post_loss(y, norm_post[layer:layer + 1], h, target, name="post_loss")

    g_pre = [None] * depth
    g_post = [None] * depth
    small_vec = [n for n in small_names] + ["attn_sinks"]
    grads = {n: [None] * W[n].shape[0] for n in small_vec}
    recv = {}

    def carried(keyed):
        return _exchange_comm([item for _, item in keyed]) if keyed else None

    def landed(keyed, got):
        for (key, _), r in zip(keyed, got):
            recv[key] = r

    pending = []
    for layer in reversed(range(depth)):
        j = layer // 2
        h_in, u, proj, mix, ypre, y = saved[layer]
        wl = weights[layer]
        if layer == depth - 1:
            dy, g_post[layer] = _rms_bwd(y, norm_post[layer:layer + 1], dh, None, out_dtype=BF16, name="post_bwd")
        if layer % 2 == 0:
            dyp = _matmul(dy, wl["w_out"], tb=True, out_dtype=BF16, name="attn_out_dx")
            dw_out = _matmul(ypre, dy, ta=True, out_dtype=BF16, name="attn_out_dw")
            pending.append((("attn_w_out", j), (dw_out.reshape(N_DEV, Q // N_DEV, D), False)))
            (dqg, dkv, dsink), got = _attn_bwd(proj, mix, dyp, attn_sinks[j], comm=carried(pending), name="attn_bwd")
            landed(pending, got)
            grads["attn_sinks"][j] = dsink[:, 0]
            dproj = _fill_columns(dqg, dkv, Q // (2 * KV), name="attn_dproj")
            dw = _matmul(u, dproj, ta=True, out_dtype=BF16, name="attn_in_dw")
            pending = [(("attn_w_in", j), (jnp.moveaxis(dw.reshape(D, N_DEV, -1), 1, 0), False))]
            if layer == 0:
                du, got = _matmul(dproj, wl["w_in"], tb=True, comm=carried(pending), name="attn_in_dx")
                landed(pending, got)
                pending = []
            else:
                du = _matmul(dproj, wl["w_in"], tb=True, name="attn_in_dx")
        else:
            dyp = _matmul(dy, wl["w_out"], tb=True, out_dtype=BF16, name="lru_out_dx")
            dw_out = _matmul(ypre, dy, ta=True, out_dtype=BF16, name="lru_out_dw")
            pending.append((("lru_w_out", j), (dw_out.reshape(N_DEV, LW // N_DEV, D), False)))
            (dxb, dgt, dwa, dwx, dba, dbx, dcw, dcb, dlam), got = _lru_bwd(
                proj, mix, dyp, wl["w_a"], wl["w_x"], ba_f[j], bx_f[j], cw_f[j], cb_f[j], lam_f[j],
                comm=carried(pending), name="lru_bwd")
            landed(pending, got)
            grads["lru_b_a"][j], grads["lru_b_x"][j] = dba.reshape(nh, LRU_BLOCK), dbx.reshape(nh, LRU_BLOCK)
            grads["lru_conv_w"][j], grads["lru_conv_b"][j], grads["lru_lambda"][j] = dcw, dcb[0], dlam[0]
            dproj = _fill_columns(dxb, dgt, 1, name="lru_dproj")
            du = _matmul(dproj, wl["w_in"], tb=True, name="lru_in_dx")
            mine = [(("lru_w_a", j), (dwa, True)), (("lru_w_x", j), (dwx, True))]
            dw, got = _matmul(u, dproj, ta=True, out_dtype=BF16, by_owner=True, comm=carried(mine), name="lru_in_dw")
            landed(mine, got)
            pending = [(("lru_w_in", j), (dw, False))]
        if layer > 0:
            dh, dy, g_pre[layer], g_post[layer - 1] = _pre_post_bwd(
                h_in, norm_pre[layer:layer + 1], du, dh, saved[layer - 1][5], norm_post[layer - 1:layer], name="pre_post_bwd")
        else:
            dh, g_pre[layer] = _rms_bwd(h_in, norm_pre[layer:layer + 1], du, dh, out_dtype=F32, name="pre_bwd")

    gfull = {n: jnp.stack(g) for n, g in grads.items()}
    gfull["norm_pre"] = jnp.concatenate(g_pre, axis=0)
    gfull["norm_post"] = jnp.concatenate(g_post, axis=0)

    repl_part = [jnp.broadcast_to(gfull[n].reshape(1, -1), (N_DEV, gfull[n].size)) for n in REPL]
    loss_slot = jnp.broadcast_to(loss_part.reshape(1, 1), (N_DEV, 1))
    send_small = _pack([_to_shards(gfull[n], ax) for n, ax in SMALL] + repl_part + [loss_slot], SUBLANES, F32)
    last = pending + [(("small", 0), (send_small, False))]
    landed(last, _comm_call(carried(last), name="exchange_last"))

    zero1 = jnp.zeros((1,), F32)
    outs = {}
    for n in big_names:
        shp = W[n].shape
        as3 = lambda a: a.reshape((shp[0], -1, shp[-1]))
        parts = [recv[n, j].reshape((N_DEV, -1, shp[-1])) for j in range(shp[0])]
        res = _adamw(parts, as3(W[n]), as3(M[n]), as3(V[n]), name="adamw_" + n)
        for kind, a in zip(("grad", "delta", "new_m", "new_v"), res):
            outs[kind, n] = a.reshape(shp)
    res_small = _adamw([recv["small", 0]],
                       *[_pack([flat(S[n]) for n in small_names] + [flat(S[n]) for n in REPL] + [zero1], SUBLANES, F32)[None]
                         for S in (W, M, V)], name="adamw_small")
    for kind, rs in zip(("grad", "delta", "new_m", "new_v"), res_small):
        for n, a in zip(small_names + list(REPL) + ["loss"], _unpack(rs[0], small_shapes + repl_shapes + [(1,)])):
            outs[kind, n] = a
    loss = outs["grad", "loss"][0]
    result = [loss, dh[None]]
    for kind in ("grad", "delta", "new_m", "new_v"):
        result += [outs[kind, n] for n in ORDER]
    return tuple(result)
```

```python
import math
from typing import Callable, NamedTuple

import jax
import jax.numpy as jnp
from jax import lax
from jax.experimental import pallas as pl
from jax.experimental.pallas import tpu as pltpu

F32 = jnp.float32
BF16 = jnp.bfloat16

N_DEV = 8
HEAD_DIM = 64
GROUP = 8
WINDOW = 128
LRU_BLOCK = 256
CONV_W = 4
C_RG = 8.0
NORM_EPS = 1e-6
MASK_VALUE = -1e30

ADAM_LR = 0.001
ADAM_B1 = 0.9
ADAM_B2 = 0.999
ADAM_EPS = 1e-08
ADAM_WD = 0.01
ADAM_STEP = 10

ROW_BLOCK = 256
LANES = 128
SUBLANES = 8
PACK_W = 1024
VMEM_LIMIT = 56 * 1024 * 1024
MATMUL_VMEM = 36 * 1024 * 1024
ADAMW_VMEM = 24 * 1024 * 1024
MESH = pl.DeviceIdType.MESH


def _cparams(sem=None):
    return pltpu.CompilerParams(dimension_semantics=sem, vmem_limit_bytes=VMEM_LIMIT)


class Comm(NamedTuple):
    arrays: list
    out_shapes: list
    sems: list
    start: Callable
    mid: Callable
    finish: Callable


def _call(body, *, name, grid, in_specs, out_specs, out_shape, scratch_shapes, semantics, args, comm=None):
    if comm is None:
        res = pl.pallas_call(body, name=name, grid=grid, in_specs=in_specs, out_specs=out_specs, out_shape=out_shape,
                             scratch_shapes=scratch_shapes, compiler_params=_cparams(semantics))(*args)
        return list(res), []
    n_in, n_out, n_scr = len(in_specs), len(out_specs), len(scratch_shapes)
    ci, co = len(comm.arrays), len(comm.out_shapes)
    steps = math.prod(grid)

    def hosted(*refs):
        ins, cins = refs[:n_in], refs[n_in:n_in + ci]
        o0 = n_in + ci
        outs, couts = refs[o0:o0 + n_out], refs[o0 + n_out:o0 + n_out + co]
        s0 = o0 + n_out + co
        scr, sems = refs[s0:s0 + n_scr], refs[s0 + n_scr:]
        step = 0
        for ax, g in enumerate(grid):
            step = step * g + pl.program_id(ax)

        @pl.when(step == 0)
        def _():
            comm.start(cins, couts, *sems)

        body(*ins, *outs, *scr)

        @pl.when(step == steps // 2)
        def _():
            comm.mid(cins, couts, *sems)

        @pl.when(step == steps - 1)
        def _():
            comm.finish(cins, couts, *sems)

    any_spec = pl.BlockSpec(memory_space=pl.ANY)
    res = pl.pallas_call(
        hosted, name=name, grid=grid,
        in_specs=list(in_specs) + [any_spec] * ci, out_specs=list(out_specs) + [any_spec] * co,
        out_shape=list(out_shape) + list(comm.out_shapes), scratch_shapes=list(scratch_shapes) + list(comm.sems),
        compiler_params=_cparams(("arbitrary",) * len(grid)),
    )(*args, *comm.arrays)
    return list(res[:n_out]), list(res[n_out:])


def _pick(n, target, quantum):
    best = None
    for t in range(quantum, min(n, target) + 1, quantum):
        if n % t == 0:
            best = t
    return n if best is None else best


def _sigmoid(x):
    return 1.0 / (1.0 + jnp.exp(-x))


def _dot(a, b):
    return lax.dot_general(a, b, (((1,), (0,)), ((), ())), preferred_element_type=F32)


def _dot_nt(a, b):
    return lax.dot_general(a, b, (((1,), (1,)), ((), ())), preferred_element_type=F32)


def _dot_tn(a, b):
    return lax.dot_general(a, b, (((0,), (0,)), ((), ())), preferred_element_type=F32)


def _matmul(a, b, *, ta=False, tb=False, out_dtype=F32, by_owner=False, comm=None, name):
    if ta:
        K, M = a.shape
    else:
        M, K = a.shape
    if tb:
        N, K2 = b.shape
    else:
        K2, N = b.shape
    assert K == K2, (a.shape, b.shape, ta, tb)
    tn = N // N_DEV if by_owner else _pick(N, 1536, 256)
    assert tn % LANES == 0
    osz = jnp.dtype(out_dtype).itemsize

    def vmem_bytes(tm, tk):
        acc = 0 if tk == K else tm * tn * 4
        return 2 * (tm * tk * a.dtype.itemsize + tk * tn * b.dtype.itemsize + tm * tn * osz) + acc

    def deepest(tm):
        return max([t for t in range(256, K + 1, 256) if K % t == 0 and vmem_bytes(tm, t) <= MATMUL_VMEM] or [_pick(K, 512, 256)])

    tm = _pick(M, 1024, 256)
    if deepest(tm) < K and tm % 512 == 0 and deepest(tm // 2) == K:
        tm //= 2
    tk = deepest(tm)
    nk = K // tk
    dot = {(False, False): _dot, (False, True): _dot_nt, (True, False): _dot_tn}[(ta, tb)]

    if nk == 1:
        def body(a_ref, b_ref, o_ref):
            o_ref[...] = dot(a_ref[...].astype(BF16), b_ref[...].astype(BF16)).astype(o_ref.dtype)
        scratch = []
    else:
        def body(a_ref, b_ref, o_ref, acc_ref):
            k = pl.program_id(2)

            @pl.when(k == 0)
            def _():
                acc_ref[...] = jnp.zeros_like(acc_ref)

            acc_ref[...] += dot(a_ref[...].astype(BF16), b_ref[...].astype(BF16))

            @pl.when(k == nk - 1)
            def _():
                o_ref[...] = acc_ref[...].astype(o_ref.dtype)
        scratch = [pltpu.VMEM((tm, tn), F32)]

    a_spec = pl.BlockSpec((tk, tm), lambda j, i, k: (k, i)) if ta else pl.BlockSpec((tm, tk), lambda j, i, k: (i, k))
    b_spec = pl.BlockSpec((tn, tk), lambda j, i, k: (j, k)) if tb else pl.BlockSpec((tk, tn), lambda j, i, k: (k, j))
    if by_owner:
        o_spec = pl.BlockSpec((None, tm, tn), lambda j, i, k: (j, i, 0))
        o_shape = jax.ShapeDtypeStruct((N_DEV, M, tn), out_dtype)
    else:
        o_spec = pl.BlockSpec((tm, tn), lambda j, i, k: (i, j))
        o_shape = jax.ShapeDtypeStruct((M, N), out_dtype)
    res, extra = _call(body, name=name, grid=(N // tn, M // tm, nk), in_specs=[a_spec, b_spec], out_specs=[o_spec],
                       out_shape=[o_shape], scratch_shapes=scratch, semantics=("parallel", "parallel", "arbitrary"),
                       args=(a, b), comm=comm)
    return res[0] if comm is None else (res[0], extra)


def _rms_fwd(h, g, *, comm=None, name):
    T, D = h.shape
    tm = _pick(T, ROW_BLOCK, SUBLANES)

    def body(h_ref, g_ref, u_ref):
        x = h_ref[...]
        r = lax.rsqrt(jnp.mean(x * x, axis=-1, keepdims=True) + NORM_EPS)
        u_ref[...] = ((x * r) * g_ref[...]).astype(u_ref.dtype)

    return _call(
        body, name=name, grid=(T // tm,),
        in_specs=[pl.BlockSpec((tm, D), lambda i: (i, 0)), pl.BlockSpec((1, D), lambda i: (0, 0))],
        out_specs=[pl.BlockSpec((tm, D), lambda i: (i, 0))],
        out_shape=[jax.ShapeDtypeStruct((T, D), BF16)],
        scratch_shapes=[], semantics=("parallel",), args=(h, g), comm=comm)


def _post_fwd(y, g, h, g_next, *, name):
    T, D = y.shape
    tm = _pick(T, ROW_BLOCK, SUBLANES)

    def body(y_ref, g_ref, h_ref, gn_ref, o_ref, u_ref):
        x = y_ref[...]
        r = lax.rsqrt(jnp.mean(x * x, axis=-1, keepdims=True) + NORM_EPS)
        ho = h_ref[...] + (x * r) * g_ref[...]
        o_ref[...] = ho
        rn = lax.rsqrt(jnp.mean(ho * ho, axis=-1, keepdims=True) + NORM_EPS)
        u_ref[...] = ((ho * rn) * gn_ref[...]).astype(u_ref.dtype)

    row = pl.BlockSpec((tm, D), lambda i: (i, 0))
    vec = pl.BlockSpec((1, D), lambda i: (0, 0))
    return pl.pallas_call(
        body, name=name, grid=(T // tm,),
        in_specs=[row, vec, row, vec],
        out_specs=[row, row],
        out_shape=[jax.ShapeDtypeStruct((T, D), F32), jax.ShapeDtypeStruct((T, D), BF16)],
        compiler_params=_cparams(("parallel",)),
    )(y, g, h, g_next)


def _rms_bwd(x, g, dz, res, *, out_dtype, name):
    T, D = x.shape
    tm = _pick(T, ROW_BLOCK, SUBLANES)
    has_res = res is not None

    def body(*refs):
        if has_res:
            x_ref, g_ref, dz_ref, res_ref, dx_ref, dg_ref = refs
        else:
            x_ref, g_ref, dz_ref, dx_ref, dg_ref = refs
        i = pl.program_id(0)

        @pl.when(i == 0)
        def _():
            dg_ref[...] = jnp.zeros_like(dg_ref)

        dx, dg = _norm_dx(x_ref[...], g_ref[...], dz_ref[...])
        dg_ref[...] += dg
        if has_res:
            dx = dx + res_ref[...]
        dx_ref[...] = dx.astype(dx_ref.dtype)

    row = pl.BlockSpec((tm, D), lambda i: (i, 0))
    vec = pl.BlockSpec((1, D), lambda i: (0, 0))
    ins = [x, g, dz] + ([res] if has_res else [])
    return pl.pallas_call(
        body, name=name, grid=(T // tm,),
        in_specs=[row, vec, row] + ([row] if has_res else []),
        out_specs=[row, vec],
        out_shape=[jax.ShapeDtypeStruct((T, D), out_dtype), jax.ShapeDtypeStruct((1, D), F32)],
        compiler_params=_cparams(("arbitrary",)),
    )(*ins)


def _norm_dx(x, g, dz):
    r = lax.rsqrt(jnp.mean(x * x, axis=-1, keepdims=True) + NORM_EPS)
    xhat = x * r
    dxh = dz * g
    dx = r * (dxh - xhat * jnp.mean(dxh * xhat, axis=-1, keepdims=True))
    return dx, jnp.sum(dz * xhat, axis=0, keepdims=True)


def _pre_post_bwd(h_in, g_pre, du, dh, y_below, g_post_below, *, name):
    T, D = h_in.shape
    tm = _pick(T, ROW_BLOCK, SUBLANES)

    def body(h_ref, gp_ref, du_ref, dh_ref, y_ref, gq_ref, dhn_ref, dy_ref, dgp_ref, dgq_ref):
        i = pl.program_id(0)

        @pl.when(i == 0)
        def _():
            dgp_ref[...] = jnp.zeros_like(dgp_ref)
            dgq_ref[...] = jnp.zeros_like(dgq_ref)

        dx, dgp = _norm_dx(h_ref[...], gp_ref[...], du_ref[...])
        dhn = dh_ref[...] + dx
        dhn_ref[...] = dhn
        dgp_ref[...] += dgp
        dy, dgq = _norm_dx(y_ref[...], gq_ref[...], dhn)
        dy_ref[...] = dy.astype(dy_ref.dtype)
        dgq_ref[...] += dgq

    row = pl.BlockSpec((tm, D), lambda i: (i, 0))
    vec = pl.BlockSpec((1, D), lambda i: (0, 0))
    return pl.pallas_call(
        body, name=name, grid=(T // tm,),
        in_specs=[row, vec, row, row, row, vec],
        out_specs=[row, row, vec, vec],
        out_shape=[jax.ShapeDtypeStruct((T, D), F32), jax.ShapeDtypeStruct((T, D), BF16),
                   jax.ShapeDtypeStruct((1, D), F32), jax.ShapeDtypeStruct((1, D), F32)],
        compiler_params=_cparams(("arbitrary",)),
    )(h_in, g_pre, du, dh, y_below, g_post_below)


def _post_loss(y, g, h, target, *, name):
    T, D = h.shape
    tm = _pick(T, ROW_BLOCK, SUBLANES)

    def body(y_ref, g_ref, h_ref, t_ref, dh_ref, l_ref):
        i = pl.program_id(0)

        @pl.when(i == 0)
        def _():
            l_ref[...] = jnp.zeros_like(l_ref)

        x = y_ref[...]
        r = lax.rsqrt(jnp.mean(x * x, axis=-1, keepdims=True) + NORM_EPS)
        e = (h_ref[...] + (x * r) * g_ref[...]) - t_ref[...]
        dh_ref[...] = e * (1.0 / D)
        row = jnp.sum(e * e, axis=-1, keepdims=True) * (0.5 / D)
        l_ref[...] += jnp.sum(row, axis=0, keepdims=True)

    row = pl.BlockSpec((tm, D), lambda i: (i, 0))
    return pl.pallas_call(
        body, name=name, grid=(T // tm,),
        in_specs=[row, pl.BlockSpec((1, D), lambda i: (0, 0)), row, row],
        out_specs=[row, pl.BlockSpec((1, 1), lambda i: (0, 0))],
        out_shape=[jax.ShapeDtypeStruct((T, D), F32), jax.ShapeDtypeStruct((1, 1), F32)],
        compiler_params=_cparams(("arbitrary",)),
    )(y, g, h, target)


def _attn_dims(P):
    Q = P * 4 // 9
    KV = Q // GROUP
    assert 2 * Q + 2 * KV == P and KV % LANES == 0 and Q % (2 * KV) == 0
    return Q, KV


def _attn_specs(Q, KV, nb):
    blk = WINDOW
    row = lambda i: jnp.minimum(i, nb - 1)
    q_spec = pl.BlockSpec((blk, Q), lambda i: (row(i), 0))
    kvc_spec = pl.BlockSpec((blk, 2 * KV), lambda i: (row(i), Q // (2 * KV)))
    kvp_spec = pl.BlockSpec((blk, 2 * KV), lambda i: (jnp.maximum(row(i) - 1, 0), Q // (2 * KV)))
    g_specs = [pl.BlockSpec((blk, 2 * KV), lambda i, b=b: (row(i), Q // (2 * KV) + 1 + b)) for b in range(Q // (2 * KV))]
    return q_spec, kvc_spec, kvp_spec, g_specs


def _stack_gate(g_refs, cols):
    width = g_refs[0].shape[1]
    parts = [g_refs[cs.start // width][:, cs.start % width:cs.start % width + LANES] for cs in cols]
    return jnp.concatenate(parts, axis=0).astype(F32)


PAIRS = GROUP // 2
STACK = PAIRS * WINDOW


def _band_mask(i):
    c = lax.broadcasted_iota(jnp.int32, (2 * WINDOW, STACK), 0)
    r = lax.broadcasted_iota(jnp.int32, (2 * WINDOW, STACK), 1) & (WINDOW - 1)
    first_key = jnp.where(i > 0, 0, WINDOW)
    return (c > r) & (c <= r + WINDOW) & (c >= first_key)


def _group_cols(kvh):
    c0 = kvh * GROUP * HEAD_DIM
    return [slice(c0 + j * LANES, c0 + (j + 1) * LANES) for j in range(PAIRS)]


def _stack(ref, cols, scale=None):
    x = jnp.concatenate([ref[:, cs] for cs in cols], axis=0).astype(F32)
    return x if scale is None else x * scale


def _group_sinks(sink_ref, kvh, half):
    return jnp.concatenate([jnp.full((1, WINDOW), sink_ref[kvh * GROUP + 2 * j + half], F32) for j in range(PAIRS)], axis=1)


def _pair_halves(x128, xt128, e):
    lo = lax.broadcasted_iota(jnp.int32, x128.shape, 1) < HEAD_DIM
    lo_t = lax.broadcasted_iota(jnp.int32, xt128.shape, 0) < HEAD_DIM
    if e == 0:
        x_lo, xt_lo = jnp.where(lo, x128, 0.0), jnp.where(lo_t, xt128, 0.0)
        x_hi, xt_hi = pltpu.roll(x_lo, HEAD_DIM, 1), pltpu.roll(xt_lo, HEAD_DIM, 0)
    else:
        x_hi, xt_hi = jnp.where(lo, 0.0, x128), jnp.where(lo_t, 0.0, xt128)
        x_lo, xt_lo = pltpu.roll(x_hi, HEAD_DIM, 1), pltpu.roll(xt_hi, HEAD_DIM, 0)
    return (x_lo.astype(BF16), x_hi.astype(BF16)), (xt_lo.astype(BF16), xt_hi.astype(BF16))


def _softmax_sink(st, allowed, sink):
    st = jnp.where(allowed, st, MASK_VALUE)
    m = jnp.maximum(jnp.max(st, axis=0, keepdims=True), sink)
    p = jnp.exp(st - m)
    es = jnp.exp(sink - m)
    inv = 1.0 / (jnp.sum(p, axis=0, keepdims=True) + es)
    return p * inv, es * inv


def _attn_fwd(proj, sinks, *, comm=None, name):
    T, P = proj.shape
    Q, KV = _attn_dims(P)
    nb = T // WINDOW
    npairs = KV // LANES
    scale = 1.0 / math.sqrt(HEAD_DIM)

    ng = Q // (2 * KV)

    def body(sink_ref, q_ref, kvc_ref, kvp_ref, *rest):
        g_refs, (out_ref, yp_ref) = rest[:ng], rest[ng:]
        i = pl.program_id(0)
        allowed = _band_mask(i)
        for p in range(npairs):
            ks = slice(p * LANES, (p + 1) * LANES)
            vs = slice(KV + p * LANES, KV + (p + 1) * LANES)
            k128 = jnp.concatenate([kvp_ref[:, ks], kvc_ref[:, ks]], axis=0).astype(F32)
            v128 = jnp.concatenate([kvp_ref[:, vs], kvc_ref[:, vs]], axis=0).astype(F32)
            kt128, vt128 = k128.T, v128.T
            for e in range(2):
                kvh = 2 * p + e
                khalf, _ = _pair_halves(k128, kt128, e)
                _, vthalf = _pair_halves(v128, vt128, e)
                cols = _group_cols(kvh)
                q4 = _stack(q_ref, cols, scale).astype(BF16)
                ot = None
                for half in range(2):
                    st = _dot_nt(khalf[half], q4)
                    pn, _ = _softmax_sink(st, allowed, _group_sinks(sink_ref, kvh, half))
                    o = _dot(vthalf[half], pn.astype(BF16))
                    ot = o if ot is None else ot + o
                o4 = ot.T
                g4 = _stack_gate(g_refs, cols)
                y4 = (o4 * (g4 * _sigmoid(g4))).astype(BF16)
                for j, cs in enumerate(cols):
                    out_ref[:, cs] = o4[j * WINDOW:(j + 1) * WINDOW]
                    yp_ref[:, cs] = y4[j * WINDOW:(j + 1) * WINDOW]

    q_spec, kvc_spec, kvp_spec, g_specs = _attn_specs(Q, KV, nb)
    row = pl.BlockSpec((WINDOW, Q), lambda i: (i, 0))
    return _call(
        body, name=name, grid=(nb,),
        in_specs=[pl.BlockSpec(memory_space=pltpu.SMEM), q_spec, kvc_spec, kvp_spec] + g_specs,
        out_specs=[row, row],
        out_shape=[jax.ShapeDtypeStruct((T, Q), F32), jax.ShapeDtypeStruct((T, Q), BF16)],
        scratch_shapes=[], semantics=("parallel",), args=(sinks, proj, proj, proj) + (proj,) * ng, comm=comm)


def _attn_bwd(proj, out, dyp, sinks, *, comm=None, name):
    T, P = proj.shape
    Q, KV = _attn_dims(P)
    nb = T // WINDOW
    npairs = KV // LANES
    H = Q // HEAD_DIM
    ng = Q // (2 * KV)
    scale = 1.0 / math.sqrt(HEAD_DIM)

    def body(sink_ref, q_ref, kvc_ref, kvp_ref, *rest):
        g_refs, (out_ref, dyp_ref, dqg_ref, dkv_ref, dsink_ref, carry_ref) = rest[:ng], rest[ng:]
        i = pl.program_id(0)

        @pl.when(i == 0)
        def _():
            carry_ref[...] = jnp.zeros_like(carry_ref)
            dsink_ref[...] = jnp.zeros_like(dsink_ref)

        @pl.when(i == nb)
        def _():
            dkv_ref[...] = carry_ref[...].astype(dkv_ref.dtype)

        @pl.when(i < nb)
        def _():
            allowed = _band_mask(i)
            lo = lax.broadcasted_iota(jnp.int32, (2 * WINDOW, LANES), 1) < HEAD_DIM
            sel_lane = lax.broadcasted_iota(jnp.int32, (SUBLANES, LANES), 1) < HEAD_DIM
            sels = (jnp.where(sel_lane, 1.0, 0.0).astype(BF16), jnp.where(sel_lane, 0.0, 1.0).astype(BF16))
            for p in range(npairs):
                ks = slice(p * LANES, (p + 1) * LANES)
                vs = slice(KV + p * LANES, KV + (p + 1) * LANES)
                k128 = jnp.concatenate([kvp_ref[:, ks], kvc_ref[:, ks]], axis=0).astype(F32)
                v128 = jnp.concatenate([kvp_ref[:, vs], kvc_ref[:, vs]], axis=0).astype(F32)
                kt128, vt128 = k128.T, v128.T
                dk_e, dv_e = [], []
                for e in range(2):
                    kvh = 2 * p + e
                    khalf, kthalf = _pair_halves(k128, kt128, e)
                    vhalf, _ = _pair_halves(v128, vt128, e)
                    cols = _group_cols(kvh)
                    q4 = _stack(q_ref, cols, scale).astype(BF16)
                    g4 = _stack_gate(g_refs, cols)
                    o4 = _stack(out_ref, cols)
                    dy4 = _stack(dyp_ref, cols)
                    sg = _sigmoid(g4)
                    do4 = dy4 * (g4 * sg)
                    dg4 = (dy4 * o4 * (sg * (1.0 + g4 * (1.0 - sg)))).astype(dqg_ref.dtype)
                    dod = do4 * o4
                    dod_hi = dod.astype(BF16)
                    dod_lo = (dod - dod_hi.astype(F32)).astype(BF16)
                    do4b = do4.astype(BF16)
                    dqt = None
                    dk_h, dv_h = [], []
                    for half in range(2):
                        delta = jnp.max(_dot_nt(sels[half], dod_hi) + _dot_nt(sels[half], dod_lo), axis=0, keepdims=True)
                        st = _dot_nt(khalf[half], q4)
                        pn, psink = _softmax_sink(st, allowed, _group_sinks(sink_ref, kvh, half))
                        dp = _dot_nt(vhalf[half], do4b)
                        ds = (pn * (dp - delta)).astype(BF16)
                        dq = _dot(kthalf[half], ds)
                        dqt = dq if dqt is None else dqt + dq
                        dk_h.append(_dot(ds, q4))
                        dv_h.append(_dot(pn.astype(BF16), do4b))
                        pd = psink * delta
                        for j in range(PAIRS):
                            n = kvh * GROUP + 2 * j + half
                            dsn = -jnp.sum(pd[:, j * WINDOW:(j + 1) * WINDOW], axis=1, keepdims=True)
                            dsink_ref[n:n + 1, :] += jnp.broadcast_to(dsn, (1, LANES))
                    dq4 = (dqt.T * scale).astype(dqg_ref.dtype)
                    for j, cs in enumerate(cols):
                        dqg_ref[:, cs] = dq4[j * WINDOW:(j + 1) * WINDOW]
                        dqg_ref[:, slice(Q + 2 * KV + cs.start, Q + 2 * KV + cs.stop)] = dg4[j * WINDOW:(j + 1) * WINDOW]
                    acc_k = jnp.where(lo, dk_h[0], dk_h[1])
                    acc_v = jnp.where(lo, dv_h[0], dv_h[1])
                    dk_e.append(acc_k + pltpu.roll(acc_k, HEAD_DIM, 1))
                    dv_e.append(acc_v + pltpu.roll(acc_v, HEAD_DIM, 1))
                for sl, de in ((ks, dk_e), (vs, dv_e)):
                    d128 = jnp.where(lo, de[0], de[1])
                    dkv_ref[:, sl] = (carry_ref[:, sl] + d128[:WINDOW]).astype(dkv_ref.dtype)
                    carry_ref[:, sl] = d128[WINDOW:]

    q_spec, kvc_spec, kvp_spec, g_specs = _attn_specs(Q, KV, nb)
    last = lambda i: (jnp.minimum(i, nb - 1), 0)
    row = pl.BlockSpec((WINDOW, Q), last)
    return _call(
        body, name=name, grid=(nb + 1,),
        in_specs=[pl.BlockSpec(memory_space=pltpu.SMEM), q_spec, kvc_spec, kvp_spec] + g_specs + [row, row],
        out_specs=[pl.BlockSpec((WINDOW, P), last),
                   pl.BlockSpec((WINDOW, 2 * KV), lambda i: (jnp.maximum(i - 1, 0), 0)),
                   pl.BlockSpec((H, LANES), lambda i: (0, 0))],
        out_shape=[jax.ShapeDtypeStruct((T, P), BF16), jax.ShapeDtypeStruct((T, 2 * KV), BF16),
                   jax.ShapeDtypeStruct((H, LANES), F32)],
        scratch_shapes=[pltpu.VMEM((WINDOW, 2 * KV), F32)],
        semantics=("arbitrary",), args=(sinks, proj, proj, proj) + (proj,) * ng + (out, dyp), comm=comm)


def _fill_columns(full, part, col_block, *, name):
    T, w = part.shape
    tm = _pick(T, 1024, SUBLANES * 2)

    def body(full_ref, part_ref, o_ref):
        del full_ref
        o_ref[...] = part_ref[...]

    return pl.pallas_call(
        body, name=name, grid=(T // tm,),
        in_specs=[pl.BlockSpec(memory_space=pl.ANY), pl.BlockSpec((tm, w), lambda i: (i, 0))],
        out_specs=pl.BlockSpec((tm, w), lambda i: (i, col_block)),
        out_shape=jax.ShapeDtypeStruct(full.shape, full.dtype),
        input_output_aliases={0: 0},
        compiler_params=_cparams(("parallel",)),
    )(full, part)


LRU_CHUNK = 2048


def _shift_down(x, halo8, s):
    if s == 0:
        return x
    row8 = lax.broadcasted_iota(jnp.int32, (SUBLANES, 1), 0)
    r = pltpu.roll(x, s, 0)
    top = jnp.where(row8 < s, pltpu.roll(halo8, s, 0), r[:SUBLANES])
    return jnp.concatenate([top, r[SUBLANES:]], axis=0)


def _shift_up(x, halo8, s):
    if s == 0:
        return x
    n = x.shape[0]
    row8 = lax.broadcasted_iota(jnp.int32, (SUBLANES, 1), 0)
    r = pltpu.roll(x, n - s, 0)
    bot = jnp.where(row8 >= SUBLANES - s, pltpu.roll(halo8, SUBLANES - s, 0), r[n - SUBLANES:])
    return jnp.concatenate([r[:n - SUBLANES], bot], axis=0)


def _scan_fwd(a, b, c0):
    n = a.shape[0]
    row = lax.broadcasted_iota(jnp.int32, (n, 1), 0) & (SUBLANES - 1)
    s = 1
    while s < SUBLANES:
        keep = row >= s
        ar = jnp.where(keep, pltpu.roll(a, s, 0), 1.0)
        br = jnp.where(keep, pltpu.roll(b, s, 0), 0.0)
        b = a * br + b
        a = a * ar
        s *= 2
    out, c = [], c0
    for i in range(n // SUBLANES):
        rows = slice(i * SUBLANES, (i + 1) * SUBLANES)
        h = a[rows] * c + b[rows]
        out.append(h)
        c = h[SUBLANES - 1:]
    return jnp.concatenate(out, axis=0)


def _scan_rev(al, b, c0):
    n = al.shape[0]
    row = lax.broadcasted_iota(jnp.int32, (n, 1), 0) & (SUBLANES - 1)
    s = 1
    while s < SUBLANES:
        keep = row < SUBLANES - s
        ar = jnp.where(keep, pltpu.roll(al, n - s, 0), 1.0)
        br = jnp.where(keep, pltpu.roll(b, n - s, 0), 0.0)
        b = b + al * br
        al = al * ar
        s *= 2
    out, c = [], c0
    for i in reversed(range(n // SUBLANES)):
        rows = slice(i * SUBLANES, (i + 1) * SUBLANES)
        l = b[rows] + al[rows] * c
        out.append(l)
        c = l[:1]
    return jnp.concatenate(out[::-1], axis=0)


def _log1p_pos(z):
    return jnp.where(z < 0.01, z * (1.0 - z * (0.5 - z * (1.0 / 3.0))), jnp.log(1.0 + z))


def _one_minus_sq(a, log_a):
    x = 2.0 * log_a
    series = -x * (1.0 + x * (0.5 + x * (1.0 / 6.0)))
    return jnp.where(x > -0.02, series, 1.0 - a * a)


def _softplus_neg(lam):
    return jnp.maximum(-lam, 0.0) + _log1p_pos(jnp.exp(-jnp.abs(lam)))


def _lru_gates(xb, halo, wa, wx, ba, bx, cw_ref, cb, lam, with_inverse=False):
    xs = [_shift_down(xb, halo, s) for s in range(CONV_W)]
    xc = cb + xs[3] * cw_ref[0:1, :] + xs[2] * cw_ref[1:2, :] + xs[1] * cw_ref[2:3, :] + xs[0] * cw_ref[3:4, :]
    xcb = xc.astype(BF16)
    r = _sigmoid(_dot(xcb, wa) + ba)
    ig = _sigmoid(_dot(xcb, wx) + bx)
    sp = _softplus_neg(lam)
    log_a = (-C_RG * r) * sp
    a = jnp.exp(log_a)
    z = _one_minus_sq(a, log_a)
    if not with_inverse:
        return xs, xc, xcb, r, ig, sp, a, jnp.sqrt(z), None
    rmult = lax.rsqrt(z)
    return xs, xc, xcb, r, ig, sp, a, z * rmult, rmult


def _tile_rows(dtype):
    return SUBLANES * 4 // jnp.dtype(dtype).itemsize


def _last_rows(ref):
    return ref[...].astype(F32)[ref.shape[0] - SUBLANES:]


def _lru_specs(nh, nt, tc, rev):
    tix = (lambda t: nt - 1 - t) if rev else (lambda t: t)
    chunk = lambda off: pl.BlockSpec((tc, LRU_BLOCK), lambda h, t: (tix(t), h + off))
    prev8 = lambda off, rows: pl.BlockSpec((rows, LRU_BLOCK),
                                           lambda h, t: (jnp.maximum(tix(t) * (tc // rows) - 1, 0), h + off))
    wblk = pl.BlockSpec((None, LRU_BLOCK, LRU_BLOCK), lambda h, t: (h, 0, 0))
    vec = pl.BlockSpec((1, LRU_BLOCK), lambda h, t: (0, h))
    cwb = pl.BlockSpec((CONV_W, LRU_BLOCK), lambda h, t: (0, h))
    return tix, chunk, prev8, wblk, vec, cwb


def _lru_fwd(proj, wa, wx, ba, bx, cw, cb, lam, *, comm=None, name):
    T, W2 = proj.shape
    W = W2 // 2
    nh = W // LRU_BLOCK
    tc = _pick(T, LRU_CHUNK, SUBLANES)
    nt = T // tc

    def body(xb_ref, xh_ref, gt_ref, wa_ref, wx_ref, ba_ref, bx_ref, cw_ref, cb_ref, lam_ref, hs_ref, yp_ref, carry_ref):
        t = pl.program_id(1)

        @pl.when(t == 0)
        def _():
            carry_ref[...] = jnp.zeros_like(carry_ref)

        halo = jnp.where(t > 0, _last_rows(xh_ref), 0.0)
        _, xc, _, _, ig, _, a, mult, _ = _lru_gates(xb_ref[...].astype(F32), halo, wa_ref[...], wx_ref[...], ba_ref[...],
                                                  bx_ref[...], cw_ref, cb_ref[...], lam_ref[...])
        hs = _scan_fwd(a, mult * (ig * xc), carry_ref[SUBLANES - 1:SUBLANES, :])
        hs_ref[...] = hs
        carry_ref[...] = hs[tc - SUBLANES:]
        g = gt_ref[...].astype(F32)
        yp_ref[...] = (hs * (g * _sigmoid(g))).astype(BF16)

    _, chunk, prev8, wblk, vec, cwb = _lru_specs(nh, nt, tc, False)
    return _call(
        body, name=name, grid=(nh, nt),
        in_specs=[chunk(0), prev8(0, _tile_rows(proj.dtype)), chunk(nh), wblk, wblk, vec, vec, cwb, vec, vec],
        out_specs=[chunk(0), chunk(0)],
        out_shape=[jax.ShapeDtypeStruct((T, W), F32), jax.ShapeDtypeStruct((T, W), BF16)],
        scratch_shapes=[pltpu.VMEM((SUBLANES, LRU_BLOCK), F32)],
        semantics=("parallel", "arbitrary"), args=(proj, proj, proj, wa, wx, ba, bx, cw, cb, lam), comm=comm)


def _lru_bwd(proj, hs, dyp, wa, wx, ba, bx, cw, cb, lam, *, comm=None, name):
    T, W2 = proj.shape
    W = W2 // 2
    nh = W // LRU_BLOCK
    tc = _pick(T, LRU_CHUNK, SUBLANES)
    nt = T // tc

    def body(xb_ref, xh_ref, gt_ref, hs_ref, hh_ref, dyp_ref, wa_ref, wx_ref, ba_ref, bx_ref, cw_ref, cb_ref, lam_ref,
             dx_ref, dg_ref, dwa_ref, dwx_ref, dba_ref, dbx_ref, dcw_ref, dcb_ref, dlam_ref,
             ca_ref, cl_ref, cx_ref):
        t = pl.program_id(1)
        first = t == nt - 1

        @pl.when(t == 0)
        def _():
            for ref in (ca_ref, cl_ref, cx_ref, dwa_ref, dwx_ref, dba_ref, dbx_ref, dcw_ref, dcb_ref, dlam_ref):
                ref[...] = jnp.zeros_like(ref)

        xb = xb_ref[...].astype(F32)
        halo = jnp.where(first, 0.0, _last_rows(xh_ref))
        wa = wa_ref[...]
        wx = wx_ref[...]
        lam = lam_ref[...]
        xs, xc, xcb, r, ig, sp, a, mult, rmult = _lru_gates(xb, halo, wa, wx, ba_ref[...], bx_ref[...], cw_ref, cb_ref[...], lam,
                                                            with_inverse=True)
        hsv = hs_ref[...]
        g = gt_ref[...].astype(F32)
        dy = dyp_ref[...].astype(F32)
        sg = _sigmoid(g)
        dg_ref[...] = (dy * hsv * (sg * (1.0 + g * (1.0 - sg)))).astype(dg_ref.dtype)
        dhs = dy * (g * sg)

        al = _shift_up(a, ca_ref[...], 1)
        lmb = _scan_rev(al, dhs, cl_ref[0:1, :])
        hprev = _shift_down(hsv, jnp.where(first, 0.0, _last_rows(hh_ref)), 1)
        da = lmb * hprev
        ixc = ig * xc
        dmult = lmb * ixc
        dlog_a = a * (da - dmult * a * rmult)
        dr = dlog_a * (-C_RG * sp)
        dlam_ref[...] += jnp.sum(dlog_a * r, axis=0, keepdims=True) * (C_RG * _sigmoid(-lam))
        dpa = dr * (r * (1.0 - r))
        dpx = (lmb * mult * xc) * (ig * (1.0 - ig))
        dpab = dpa.astype(BF16)
        dpxb = dpx.astype(BF16)
        dwa_ref[...] += _dot_tn(xcb, dpab)
        dwx_ref[...] += _dot_tn(xcb, dpxb)
        dba_ref[...] += jnp.sum(dpa, axis=0, keepdims=True)
        dbx_ref[...] += jnp.sum(dpx, axis=0, keepdims=True)
        dxc = lmb * mult * ig + _dot_nt(dpab, wa) + _dot_nt(dpxb, wx)
        dcb_ref[...] += jnp.sum(dxc, axis=0, keepdims=True)
        for s in range(CONV_W):
            dcw_ref[CONV_W - 1 - s:CONV_W - s, :] += jnp.sum(dxc * xs[s], axis=0, keepdims=True)
        cxv = cx_ref[...]
        dxb = dxc * cw_ref[3:4, :]
        for s in range(1, CONV_W):
            dxb = dxb + _shift_up(dxc, cxv, s) * cw_ref[3 - s:4 - s, :]
        dx_ref[...] = dxb.astype(dx_ref.dtype)
        ca_ref[...] = a[:SUBLANES]
        cl_ref[...] = lmb[:SUBLANES]
        cx_ref[...] = dxc[:SUBLANES]

    tix, chunk, prev8, wblk, vec, cwb = _lru_specs(nh, nt, tc, True)
    hchunk = pl.BlockSpec((tc, LRU_BLOCK), lambda h, t: (tix(t), h))
    carry = pltpu.VMEM((SUBLANES, LRU_BLOCK), F32)
    return _call(
        body, name=name, grid=(nh, nt),
        in_specs=[chunk(0), prev8(0, _tile_rows(proj.dtype)), chunk(nh), hchunk, prev8(0, _tile_rows(hs.dtype)), hchunk,
                  wblk, wblk, vec, vec, cwb, vec, vec],
        out_specs=[hchunk, hchunk, wblk, wblk, vec, vec, cwb, vec, vec],
        out_shape=[jax.ShapeDtypeStruct((T, W2), BF16), jax.ShapeDtypeStruct((T, W), BF16),
                   jax.ShapeDtypeStruct((nh, LRU_BLOCK, LRU_BLOCK), F32), jax.ShapeDtypeStruct((nh, LRU_BLOCK, LRU_BLOCK), F32),
                   jax.ShapeDtypeStruct((1, W), F32), jax.ShapeDtypeStruct((1, W), F32),
                   jax.ShapeDtypeStruct((CONV_W, W), F32), jax.ShapeDtypeStruct((1, W), F32), jax.ShapeDtypeStruct((1, W), F32)],
        scratch_shapes=[carry, carry, carry], semantics=("parallel", "arbitrary"),
        args=(proj, proj, proj, hs, hs, dyp, wa, wx, ba, bx, cw, cb, lam), comm=comm)


def _position():
    return lax.axis_index("x"), lax.axis_index("y"), lax.axis_index("c")


def _sems(n):
    return [pltpu.SemaphoreType.DMA((n, 7)), pltpu.SemaphoreType.DMA((n, 7)), pltpu.SemaphoreType.DMA((n,))]


BY_COLUMNS = -1


def _gather_comm(arrs, axes):
    n = len(arrs)

    def tools(ins, outs, send_sems, recv_sems, local_sems):
        x, y, c = _position()
        me, sibling = (x, y, c), (x, y, 1 - c)
        chips = [(1 - x, y), (x, 1 - y), (1 - x, 1 - y)]

        def slot(a, pos):
            if axes[a] == BY_COLUMNS:
                w = ins[a].shape[1]
                return outs[a].at[:, pl.ds(pl.multiple_of(pos * w, LANES), w)]
            return outs[a].at[(slice(None),) * axes[a] + (pos,)]

        def copy(a, k, block, to, src=None):
            px, py, pc = block
            rows = slot(a, 4 * px + 2 * py + pc)
            return pltpu.make_async_remote_copy(
                src_ref=rows if src is None else src, dst_ref=rows,
                send_sem=send_sems.at[a, k], recv_sem=recv_sems.at[a, k],
                device_id=to, device_id_type=MESH)

        own = lambda a: pltpu.make_async_copy(ins[a], slot(a, 4 * x + 2 * y + c), local_sems.at[a])
        first = lambda a: ([copy(a, 0, me, sibling, src=ins[a])]
                           + [copy(a, 1 + j, me, (*chip, c), src=ins[a]) for j, chip in enumerate(chips)])
        passed = lambda a: [copy(a, 4 + j, (*chip, c), sibling) for j, chip in enumerate(chips)]
        return me, sibling, chips, c, copy, own, first, passed

    def start(ins, outs, *sems):
        *_, own, first, _ = tools(ins, outs, *sems)
        for a in range(n):
            own(a).start()
            for cp in first(a):
                cp.start()

    def mid(ins, outs, *sems):
        me, _, chips, c, copy, _, _, passed = tools(ins, outs, *sems)
        for a in range(n):
            fwd = passed(a)
            for j, chip in enumerate(chips):
                copy(a, 1 + j, (*chip, c), me).wait_recv()
                fwd[j].start()

    def finish(ins, outs, *sems):
        me, sibling, chips, c, copy, own, first, passed = tools(ins, outs, *sems)
        for a in range(n):
            copy(a, 0, sibling, me).wait_recv()
            for j, chip in enumerate(chips):
                copy(a, 4 + j, (*chip, 1 - c), me).wait_recv()
        for a in range(n):
            for cp in first(a) + passed(a):
                cp.wait_send()
            own(a).wait()

    shapes = [jax.ShapeDtypeStruct((a.shape[0], N_DEV * a.shape[1]) if ax == BY_COLUMNS else
                                   a.shape[:ax] + (N_DEV,) + a.shape[ax:], a.dtype) for a, ax in zip(arrs, axes)]
    return Comm(list(arrs), shapes, _sems(n), start, mid, finish)


def _comm_call(comm, *, name):
    ci, co = len(comm.arrays), len(comm.out_shapes)

    def body(*refs):
        ins, outs, sems = refs[:ci], refs[ci:ci + co], refs[ci + co:]
        comm.start(ins, outs, *sems)
        comm.mid(ins, outs, *sems)
        comm.finish(ins, outs, *sems)

    any_spec = pl.BlockSpec(memory_space=pl.ANY)
    return pl.pallas_call(body, name=name, in_specs=[any_spec] * ci, out_specs=[any_spec] * co,
                          out_shape=comm.out_shapes, scratch_shapes=comm.sems)(*comm.arrays)


def _exchange_comm(items):
    n = len(items)

    def tools(ins, outs, send_sems, recv_sems, local_sems):
        x, y, c = _position()
        me = 4 * x + 2 * y + c

        def src(a, pos):
            if items[a][1]:
                rows = ins[a].shape[1] // N_DEV
                return ins[a].at[:, pl.ds(pl.multiple_of(pos * rows, rows), rows)]
            return ins[a].at[pos]

        copies = [pltpu.make_async_copy(src(a, me), outs[a].at[me], local_sems.at[a]) for a in range(n)]
        for k in range(1, N_DEV):
            px = x ^ ((k >> 2) & 1)
            py = y ^ ((k >> 1) & 1)
            pc = c ^ (k & 1)
            copies += [pltpu.make_async_remote_copy(
                src_ref=src(a, 4 * px + 2 * py + pc), dst_ref=outs[a].at[me],
                send_sem=send_sems.at[a, k - 1], recv_sem=recv_sems.at[a, k - 1],
                device_id=(px, py, pc), device_id_type=MESH) for a in range(n)]
        return copies

    def start(ins, outs, *sems):
        for cp in tools(ins, outs, *sems):
            cp.start()

    def mid(ins, outs, *sems):
        pass

    def finish(ins, outs, *sems):
        for cp in tools(ins, outs, *sems):
            cp.wait()

    shapes = []
    for arr, split in items:
        blk = (arr.shape[0], arr.shape[1] // N_DEV) + arr.shape[2:] if split else arr.shape[1:]
        shapes.append(jax.ShapeDtypeStruct((N_DEV,) + blk, arr.dtype))
    return Comm([arr for arr, _ in items], shapes, _sems(n), start, mid, finish)


def _adamw(parts, w, m, v, *, name):
    L, R, C = w.shape
    assert len(parts) == L
    row_bytes = 2 * (L * N_DEV * C * parts[0].dtype.itemsize + 7 * C * 4)
    tr = _pick(R, max(16, ADAMW_VMEM // row_bytes), 16)
    nr = R // tr
    c1 = 1.0 / (1.0 - ADAM_B1 ** ADAM_STEP)
    c2 = 1.0 / (1.0 - ADAM_B2 ** ADAM_STEP)

    def body(*refs):
        p_refs = refs[:L]
        w_ref, m_ref, v_ref, g_ref, d_ref, nm_ref, nv_ref = refs[L:]
        layer = pl.program_id(0)
        for idx, p_ref in enumerate(p_refs):
            @pl.when(layer == idx)
            def _():
                g = p_ref[0].astype(F32)
                for s in range(1, N_DEV):
                    g = g + p_ref[s].astype(F32)
                nm = ADAM_B1 * m_ref[...] + (1.0 - ADAM_B1) * g
                nv = ADAM_B2 * v_ref[...] + (1.0 - ADAM_B2) * (g * g)
                g_ref[...] = g
                nm_ref[...] = nm
                nv_ref[...] = nv
                d_ref[...] = -ADAM_LR * ((nm * c1) / (jnp.sqrt(nv * c2) + ADAM_EPS) + ADAM_WD * w_ref[...])

    def part_spec(idx):
        return pl.BlockSpec((N_DEV, tr, C),
                            lambda l, i: (0, jnp.where(l == idx, i, jnp.where(l < idx, 0, nr - 1)), 0))

    blk = pl.BlockSpec((None, tr, C), lambda l, i: (l, i, 0))
    return pl.pallas_call(
        body, name=name, grid=(L, nr),
        in_specs=[part_spec(idx) for idx in range(L)] + [blk, blk, blk],
        out_specs=[blk] * 4,
        out_shape=[jax.ShapeDtypeStruct((L, R, C), F32)] * 4,
        compiler_params=_cparams(("arbitrary", "arbitrary")),
    )(*parts, w, m, v)


def _pack(flat_parts, row_multiple, dtype):
    lead = flat_parts[0].shape[:-1]
    total = sum(p.shape[-1] for p in flat_parts)
    quantum = PACK_W * row_multiple
    padded = -(-total // quantum) * quantum
    parts = [p.astype(dtype) for p in flat_parts]
    if padded > total:
        parts.append(jnp.zeros(lead + (padded - total,), dtype))
    return jnp.concatenate(parts, axis=-1).reshape(lead + (padded // PACK_W, PACK_W))


def _unpack(buf, shapes):
    lead = buf.shape[:-2]
    flat = buf.reshape(lead + (-1,))
    out, off = [], 0
    for shp in shapes:
        n = math.prod(shp)
        out.append(flat[..., off:off + n].reshape(lead + tuple(shp)))
        off += n
    return out


def _to_full(seg, ax):
    shard = seg.shape[1:]
    full = shard[:ax] + (N_DEV * shard[ax],) + shard[ax + 1:]
    return jnp.moveaxis(seg, 0, ax).reshape(full)


def _to_shards(full, ax):
    shp = full.shape
    split = shp[:ax] + (N_DEV, shp[ax] // N_DEV) + shp[ax + 1:]
    return jnp.moveaxis(full.reshape(split), ax, 0).reshape(N_DEV, -1)


BIG = (("attn_w_in", 1), ("attn_w_out", 1), ("lru_w_in", 1), ("lru_w_a", 2), ("lru_w_x", 2), ("lru_w_out", 1))
SMALL = (("lru_conv_w", 2), ("lru_conv_b", 1), ("lru_b_a", 2), ("lru_b_x", 2), ("lru_lambda", 1))
REPL = ("norm_pre", "norm_post", "attn_sinks")
ORDER = ("norm_pre", "norm_post", "attn_w_in", "attn_w_out", "attn_sinks", "lru_w_in", "lru_conv_w", "lru_conv_b",
         "lru_w_a", "lru_b_a", "lru_w_x", "lru_b_x", "lru_lambda", "lru_w_out")


def kernel(x, norm_pre, norm_post, attn_w_in, attn_w_out, attn_sinks, lru_w_in, lru_conv_w, lru_conv_b, lru_w_a, lru_b_a, lru_w_x, lru_b_x, lru_lambda, lru_w_out, loss_target, m_norm_pre, m_norm_post, m_attn_w_in, m_attn_w_out, m_attn_sinks, m_lru_w_in, m_lru_conv_w, m_lru_conv_b, m_lru_w_a, m_lru_b_a, m_lru_w_x, m_lru_b_x, m_lru_lambda, m_lru_w_out, v_norm_pre, v_norm_post, v_attn_w_in, v_attn_w_out, v_attn_sinks, v_lru_w_in, v_lru_conv_w, v_lru_conv_b, v_lru_w_a, v_lru_b_a, v_lru_w_x, v_lru_b_x, v_lru_lambda, v_lru_w_out):
    W = dict(norm_pre=norm_pre, norm_post=norm_post, attn_w_in=attn_w_in, attn_w_out=attn_w_out, attn_sinks=attn_sinks,
             lru_w_in=lru_w_in, lru_conv_w=lru_conv_w, lru_conv_b=lru_conv_b, lru_w_a=lru_w_a, lru_b_a=lru_b_a,
             lru_w_x=lru_w_x, lru_b_x=lru_b_x, lru_lambda=lru_lambda, lru_w_out=lru_w_out)
    M = dict(norm_pre=m_norm_pre, norm_post=m_norm_post, attn_w_in=m_attn_w_in, attn_w_out=m_attn_w_out,
             attn_sinks=m_attn_sinks, lru_w_in=m_lru_w_in, lru_conv_w=m_lru_conv_w, lru_conv_b=m_lru_conv_b,
             lru_w_a=m_lru_w_a, lru_b_a=m_lru_b_a, lru_w_x=m_lru_w_x, lru_b_x=m_lru_b_x, lru_lambda=m_lru_lambda,
             lru_w_out=m_lru_w_out)
    V = dict(norm_pre=v_norm_pre, norm_post=v_norm_post, attn_w_in=v_attn_w_in, attn_w_out=v_attn_w_out,
             attn_sinks=v_attn_sinks, lru_w_in=v_lru_w_in, lru_conv_w=v_lru_conv_w, lru_conv_b=v_lru_conv_b,
             lru_w_a=v_lru_w_a, lru_b_a=v_lru_b_a, lru_w_x=v_lru_w_x, lru_b_x=v_lru_b_x, lru_lambda=v_lru_lambda,
             lru_w_out=v_lru_w_out)

    h0 = x[0]
    target = loss_target[0]
    T, D = h0.shape
    depth = norm_pre.shape[0]
    n_attn = attn_w_in.shape[0]
    Q = attn_w_out.shape[1] * N_DEV
    KV = Q // GROUP
    LW = lru_w_out.shape[1] * N_DEV
    nh = LW // LRU_BLOCK

    n_lru = lru_w_in.shape[0]
    big_names = [n for n, _ in BIG]
    small_names = [n for n, _ in SMALL]
    small_shapes = [W[n].shape for n in small_names]
    repl_shapes = [W[n].shape for n in REPL]

    flat = lambda a: a.reshape(-1)
    def layer_shards(layer):
        j = layer // 2
        names = ("attn_w_in", "attn_w_out") if layer % 2 == 0 else ("lru_w_in", "lru_w_a", "lru_w_x", "lru_w_out")
        axes = dict(lru_w_a=1, lru_w_x=1, lru_w_in=BY_COLUMNS)
        return [W[n][j].astype(BF16) for n in names], [axes.get(n, 0) for n in names]

    def layer_weights(layer, gathered):
        if layer % 2 == 0:
            g_in, g_out = gathered
            w_in = jnp.moveaxis(g_in, 0, 1).reshape(D, -1)
            return dict(w_in=w_in, w_out=None if g_out is None else g_out.reshape(Q, D))
        g_in, g_wa, g_wx, g_out = gathered
        return dict(w_in=g_in, w_a=g_wa.reshape(nh, LRU_BLOCK, LRU_BLOCK),
                    w_x=g_wx.reshape(nh, LRU_BLOCK, LRU_BLOCK), w_out=g_out.reshape(LW, D))

    arrs0, _ = layer_shards(0)
    (u,), first = _rms_fwd(h0, norm_pre[0:1], name="rms_fwd",
                           comm=_gather_comm([arrs0[0], _pack([flat(W[n]) for n in small_names], SUBLANES, F32)], [0, 0]))
    weights = {0: layer_weights(0, (first[0], None))}
    full = {}
    for (n, ax), seg in zip(SMALL, _unpack(first[-1], small_shapes)):
        full[n] = _to_full(seg, ax)
    cw_f = full["lru_conv_w"]
    cb_f = full["lru_conv_b"][:, None, :]
    ba_f = full["lru_b_a"].reshape(-1, 1, LW)
    bx_f = full["lru_b_x"].reshape(-1, 1, LW)
    lam_f = full["lru_lambda"][:, None, :]

    h = h0
    saved = []
    for layer in range(depth):
        j = layer // 2
        wl = weights[layer]
        nxt = _gather_comm(*layer_shards(layer + 1)) if layer + 1 < depth else None
        if layer % 2 == 0:
            if wl["w_out"] is None:
                proj, got = _matmul(u, wl["w_in"], out_dtype=BF16, comm=_gather_comm([arrs0[1]], [0]), name="attn_in")
                wl["w_out"] = got[0].reshape(Q, D)
            else:
                proj = _matmul(u, wl["w_in"], out_dtype=BF16, name="attn_in")
            (mix, ypre), got = _attn_fwd(proj, attn_sinks[j], comm=nxt, name="attn_fwd")
            y = _matmul(ypre, wl["w_out"], name="attn_out")
        else:
            proj = _matmul(u, wl["w_in"], out_dtype=BF16, name="lru_in")
            (mix, ypre), got = _lru_fwd(proj, wl["w_a"], wl["w_x"], ba_f[j], bx_f[j], cw_f[j], cb_f[j], lam_f[j],
                                        comm=nxt, name="lru_fwd")
            y = _matmul(ypre, wl["w_out"], name="lru_out")
        if nxt is not None:
            weights[layer + 1] = layer_weights(layer + 1, got)
        saved.append((h, u, proj, mix, ypre, y))
        if layer + 1 < depth:
            h, u = _post_fwd(y, norm_post[layer:layer + 1], h, norm_pre[layer + 1:layer + 2], name="post_fwd")
        else:
            dh, loss_part = _post_loss(y, norm_post[layer:layer + 1], h, target, name="post_loss")

    g_pre = [None] * depth
    g_post = [None] * depth
    small_vec = [n for n in small_names] + ["attn_sinks"]
    grads = {n: [None] * W[n].shape[0] for n in small_vec}
    recv = {}

    def carried(keyed):
        return _exchange_comm([item for _, item in keyed]) if keyed else None

    def landed(keyed, got):
        for (key, _), r in zip(keyed, got):
            recv[key] = r

    pending = []
    for layer in reversed(range(depth)):
        j = layer // 2
        h_in, u, proj, mix, ypre, y = saved[layer]
        wl = weights[layer]
        if layer == depth - 1:
            dy, g_post[layer] = _rms_bwd(y, norm_post[layer:layer + 1], dh, None, out_dtype=BF16, name="post_bwd")
        if layer % 2 == 0:
            dyp = _matmul(dy, wl["w_out"], tb=True, out_dtype=BF16, name="attn_out_dx")
            dw_out = _matmul(ypre, dy, ta=True, out_dtype=BF16, name="attn_out_dw")
            pending.append((("attn_w_out", j), (dw_out.reshape(N_DEV, Q // N_DEV, D), False)))
            (dqg, dkv, dsink), got = _attn_bwd(proj, mix, dyp, attn_sinks[j], comm=carried(pending), name="attn_bwd")
            landed(pending, got)
            grads["attn_sinks"][j] = dsink[:, 0]
            dproj = _fill_columns(dqg, dkv, Q // (2 * KV), name="attn_dproj")
            dw = _matmul(u, dproj, ta=True, out_dtype=BF16, name="attn_in_dw")
            pending = [(("attn_w_in", j), (jnp.moveaxis(dw.reshape(D, N_DEV, -1), 1, 0), False))]
            if layer == 0:
                du, got = _matmul(dproj, wl["w_in"], tb=True, comm=carried(pending), name="attn_in_dx")
                landed(pending, got)
                pending = []
            else:
                du = _matmul(dproj, wl["w_in"], tb=True, name="attn_in_dx")
        else:
            dyp = _matmul(dy, wl["w_out"], tb=True, out_dtype=BF16, name="lru_out_dx")
            dw_out = _matmul(ypre, dy, ta=True, out_dtype=BF16, name="lru_out_dw")
            pending.append((("lru_w_out", j), (dw_out.reshape(N_DEV, LW // N_DEV, D), False)))
            (dxb, dgt, dwa, dwx, dba, dbx, dcw, dcb, dlam), got = _lru_bwd(
                proj, mix, dyp, wl["w_a"], wl["w_x"], ba_f[j], bx_f[j], cw_f[j], cb_f[j], lam_f[j],
                comm=carried(pending), name="lru_bwd")
            landed(pending, got)
            grads["lru_b_a"][j], grads["lru_b_x"][j] = dba.reshape(nh, LRU_BLOCK), dbx.reshape(nh, LRU_BLOCK)
            grads["lru_conv_w"][j], grads["lru_conv_b"][j], grads["lru_lambda"][j] = dcw, dcb[0], dlam[0]
            dproj = _fill_columns(dxb, dgt, 1, name="lru_dproj")
            du = _matmul(dproj, wl["w_in"], tb=True, name="lru_in_dx")
            mine = [(("lru_w_a", j), (dwa, True)), (("lru_w_x", j), (dwx, True))]
            dw, got = _matmul(u, dproj, ta=True, out_dtype=BF16, by_owner=True, comm=carried(mine), name="lru_in_dw")
            landed(mine, got)
            pending = [(("lru_w_in", j), (dw, False))]
        if layer > 0:
            dh, dy, g_pre[layer], g_post[layer - 1] = _pre_post_bwd(
                h_in, norm_pre[layer:layer + 1], du, dh, saved[layer - 1][5], norm_post[layer - 1:layer], name="pre_post_bwd")
        else:
            dh, g_pre[layer] = _rms_bwd(h_in, norm_pre[layer:layer + 1], du, dh, out_dtype=F32, name="pre_bwd")

    gfull = {n: jnp.stack(g) for n, g in grads.items()}
    gfull["norm_pre"] = jnp.concatenate(g_pre, axis=0)
    gfull["norm_post"] = jnp.concatenate(g_post, axis=0)

    repl_part = [jnp.broadcast_to(gfull[n].reshape(1, -1), (N_DEV, gfull[n].size)) for n in REPL]
    loss_slot = jnp.broadcast_to(loss_part.reshape(1, 1), (N_DEV, 1))
    send_small = _pack([_to_shards(gfull[n], ax) for n, ax in SMALL] + repl_part + [loss_slot], SUBLANES, F32)
    last = pending + [(("small", 0), (send_small, False))]
    landed(last, _comm_call(carried(last), name="exchange_last"))

    zero1 = jnp.zeros((1,), F32)
    outs = {}
    for n in big_names:
        shp = W[n].shape
        as3 = lambda a: a.reshape((shp[0], -1, shp[-1]))
        parts = [recv[n, j].reshape((N_DEV, -1, shp[-1])) for j in range(shp[0])]
        res = _adamw(parts, as3(W[n]), as3(M[n]), as3(V[n]), name="adamw_" + n)
        for kind, a in zip(("grad", "delta", "new_m", "new_v"), res):
            outs[kind, n] = a.reshape(shp)
    res_small = _adamw([recv["small", 0]],
                       *[_pack([flat(S[n]) for n in small_names] + [flat(S[n]) for n in REPL] + [zero1], SUBLANES, F32)[None]
                         for S in (W, M, V)], name="adamw_small")
    for kind, rs in zip(("grad", "delta", "new_m", "new_v"), res_small):
        for n, a in zip(small_names + list(REPL) + ["loss"], _unpack(rs[0], small_shapes + repl_shapes + [(1,)])):
            outs[kind, n] = a
    loss = outs["grad", "loss"][0]
    result = [loss, dh[None]]
    for kind in ("grad", "delta", "new_m", "new_v"):
        result += [outs[kind, n] for n in ORDER]
    return tuple(result)
```

```python
import math
from typing import Callable, NamedTuple

import jax
import jax.numpy as jnp
from jax import lax
from jax.experimental import pallas as pl
from jax.experimental.pallas import tpu as pltpu

F32 = jnp.float32
BF16 = jnp.bfloat16

N_DEV = 8
HEAD_DIM = 64
GROUP = 8
WINDOW = 128
LRU_BLOCK = 256
CONV_W = 4
C_RG = 8.0
NORM_EPS = 1e-6
MASK_VALUE = -1e30

ADAM_LR = 0.001
ADAM_B1 = 0.9
ADAM_B2 = 0.999
ADAM_EPS = 1e-08
ADAM_WD = 0.01
ADAM_STEP = 10

ROW_BLOCK = 256
LANES = 128
SUBLANES = 8
PACK_W = 1024
VMEM_LIMIT = 56 * 1024 * 1024
MATMUL_VMEM = 36 * 1024 * 1024
MXU_DIM = 256
MATMUL_TM = 1024
MATMUL_TN = 1536
ADAMW_VMEM = 24 * 1024 * 1024
MESH = pl.DeviceIdType.MESH


def _cparams(sem=None):
    return pltpu.CompilerParams(dimension_semantics=sem, vmem_limit_bytes=VMEM_LIMIT)


class Comm(NamedTuple):
    arrays: list
    out_shapes: list
    sems: list
    start: Callable
    mid: Callable
    finish: Callable


def _call(body, *, name, grid, in_specs, out_specs, out_shape, scratch_shapes, semantics, args, comm=None):
    if comm is None:
        res = pl.pallas_call(body, name=name, grid=grid, in_specs=in_specs, out_specs=out_specs, out_shape=out_shape,
                             scratch_shapes=scratch_shapes, compiler_params=_cparams(semantics))(*args)
        return list(res), []
    n_in, n_out, n_scr = len(in_specs), len(out_specs), len(scratch_shapes)
    ci, co = len(comm.arrays), len(comm.out_shapes)
    steps = math.prod(grid)

    def hosted(*refs):
        ins, cins = refs[:n_in], refs[n_in:n_in + ci]
        o0 = n_in + ci
        outs, couts = refs[o0:o0 + n_out], refs[o0 + n_out:o0 + n_out + co]
        s0 = o0 + n_out + co
        scr, sems = refs[s0:s0 + n_scr], refs[s0 + n_scr:]
        step = 0
        for ax, g in enumerate(grid):
            step = step * g + pl.program_id(ax)

        @pl.when(step == 0)
        def _():
            comm.start(cins, couts, *sems)

        body(*ins, *outs, *scr)

        @pl.when(step == (steps * 3) // 4)
        def _():
            comm.mid(cins, couts, *sems)

        @pl.when(step == steps - 1)
        def _():
            comm.finish(cins, couts, *sems)

    any_spec = pl.BlockSpec(memory_space=pl.ANY)
    res = pl.pallas_call(
        hosted, name=name, grid=grid,
        in_specs=list(in_specs) + [any_spec] * ci, out_specs=list(out_specs) + [any_spec] * co,
        out_shape=list(out_shape) + list(comm.out_shapes), scratch_shapes=list(scratch_shapes) + list(comm.sems),
        compiler_params=_cparams(("arbitrary",) * len(grid)),
    )(*args, *comm.arrays)
    return list(res[:n_out]), list(res[n_out:])


def _pick(n, target, quantum):
    best = None
    for t in range(quantum, min(n, target) + 1, quantum):
        if n % t == 0:
            best = t
    return n if best is None else best


def _sigmoid(x):
    return 1.0 / (1.0 + jnp.exp(-x))


def _dot(a, b):
    return lax.dot_general(a, b, (((1,), (0,)), ((), ())), preferred_element_type=F32)


def _dot_nt(a, b):
    return lax.dot_general(a, b, (((1,), (1,)), ((), ())), preferred_element_type=F32)


def _dot_tn(a, b):
    return lax.dot_general(a, b, (((0,), (0,)), ((), ())), preferred_element_type=F32)


def _matmul(a, b, *, ta=False, tb=False, out_dtype=F32, by_owner=False, comm=None, name):
    if ta:
        K, M = a.shape
    else:
        M, K = a.shape
    if tb:
        N, K2 = b.shape
    else:
        K2, N = b.shape
    assert K == K2, (a.shape, b.shape, ta, tb)
    tn = N // N_DEV if by_owner else _pick(N, MATMUL_TN, MXU_DIM)
    assert tn % LANES == 0
    osz = jnp.dtype(out_dtype).itemsize

    def vmem_bytes(tm, tk):
        acc = 0 if tk == K else tm * tn * 4
        return 2 * (tm * tk * a.dtype.itemsize + tk * tn * b.dtype.itemsize + tm * tn * osz) + acc

    def deepest(tm):
        fits = [t for t in range(MXU_DIM, K + 1, MXU_DIM) if K % t == 0 and vmem_bytes(tm, t) <= MATMUL_VMEM]
        return max(fits or [_pick(K, 2 * MXU_DIM, MXU_DIM)])

    tm = _pick(M, MATMUL_TM, MXU_DIM)
    if deepest(tm) < K and tm % (2 * MXU_DIM) == 0 and deepest(tm // 2) == K:
        tm //= 2
    tk = deepest(tm)
    nk = K // tk
    dot = {(False, False): _dot, (False, True): _dot_nt, (True, False): _dot_tn}[(ta, tb)]

    if nk == 1:
        def body(a_ref, b_ref, o_ref):
            o_ref[...] = dot(a_ref[...].astype(BF16), b_ref[...].astype(BF16)).astype(o_ref.dtype)
        scratch = []
    else:
        def body(a_ref, b_ref, o_ref, acc_ref):
            k = pl.program_id(2)

            @pl.when(k == 0)
            def _():
                acc_ref[...] = jnp.zeros_like(acc_ref)

            acc_ref[...] += dot(a_ref[...].astype(BF16), b_ref[...].astype(BF16))

            @pl.when(k == nk - 1)
            def _():
                o_ref[...] = acc_ref[...].astype(o_ref.dtype)
        scratch = [pltpu.VMEM((tm, tn), F32)]

    a_spec = pl.BlockSpec((tk, tm), lambda j, i, k: (k, i)) if ta else pl.BlockSpec((tm, tk), lambda j, i, k: (i, k))
    b_spec = pl.BlockSpec((tn, tk), lambda j, i, k: (j, k)) if tb else pl.BlockSpec((tk, tn), lambda j, i, k: (k, j))
    if by_owner:
        o_spec = pl.BlockSpec((None, tm, tn), lambda j, i, k: (j, i, 0))
        o_shape = jax.ShapeDtypeStruct((N_DEV, M, tn), out_dtype)
    else:
        o_spec = pl.BlockSpec((tm, tn), lambda j, i, k: (i, j))
        o_shape = jax.ShapeDtypeStruct((M, N), out_dtype)
    res, extra = _call(body, name=name, grid=(N // tn, M // tm, nk), in_specs=[a_spec, b_spec], out_specs=[o_spec],
                       out_shape=[o_shape], scratch_shapes=scratch, semantics=("parallel", "parallel", "arbitrary"),
                       args=(a, b), comm=comm)
    return res[0] if comm is None else (res[0], extra)


def _rms_fwd(h, g, *, comm=None, name):
    T, D = h.shape
    tm = _pick(T, ROW_BLOCK, SUBLANES)

    def body(h_ref, g_ref, u_ref):
        x = h_ref[...]
        r = lax.rsqrt(jnp.mean(x * x, axis=-1, keepdims=True) + NORM_EPS)
        u_ref[...] = ((x * r) * g_ref[...]).astype(u_ref.dtype)

    return _call(
        body, name=name, grid=(T // tm,),
        in_specs=[pl.BlockSpec((tm, D), lambda i: (i, 0)), pl.BlockSpec((1, D), lambda i: (0, 0))],
        out_specs=[pl.BlockSpec((tm, D), lambda i: (i, 0))],
        out_shape=[jax.ShapeDtypeStruct((T, D), BF16)],
        scratch_shapes=[], semantics=("parallel",), args=(h, g), comm=comm)


def _post_fwd(y, g, h, g_next, *, name):
    T, D = y.shape
    tm = _pick(T, ROW_BLOCK, SUBLANES)

    def body(y_ref, g_ref, h_ref, gn_ref, o_ref, u_ref):
        x = y_ref[...]
        r = lax.rsqrt(jnp.mean(x * x, axis=-1, keepdims=True) + NORM_EPS)
        ho = h_ref[...] + (x * r) * g_ref[...]
        o_ref[...] = ho
        rn = lax.rsqrt(jnp.mean(ho * ho, axis=-1, keepdims=True) + NORM_EPS)
        u_ref[...] = ((ho * rn) * gn_ref[...]).astype(u_ref.dtype)

    row = pl.BlockSpec((tm, D), lambda i: (i, 0))
    vec = pl.BlockSpec((1, D), lambda i: (0, 0))
    return pl.pallas_call(
        body, name=name, grid=(T // tm,),
        in_specs=[row, vec, row, vec],
        out_specs=[row, row],
        out_shape=[jax.ShapeDtypeStruct((T, D), F32), jax.ShapeDtypeStruct((T, D), BF16)],
        compiler_params=_cparams(("parallel",)),
    )(y, g, h, g_next)


def _rms_bwd(x, g, dz, res, *, out_dtype, name):
    T, D = x.shape
    tm = _pick(T, ROW_BLOCK, SUBLANES)
    has_res = res is not None

    def body(*refs):
        if has_res:
            x_ref, g_ref, dz_ref, res_ref, dx_ref, dg_ref = refs
        else:
            x_ref, g_ref, dz_ref, dx_ref, dg_ref = refs
        i = pl.program_id(0)

        @pl.when(i == 0)
        def _():
            dg_ref[...] = jnp.zeros_like(dg_ref)

        dx, dg = _norm_dx(x_ref[...], g_ref[...], dz_ref[...])
        dg_ref[...] += dg
        if has_res:
            dx = dx + res_ref[...]
        dx_ref[...] = dx.astype(dx_ref.dtype)

    row = pl.BlockSpec((tm, D), lambda i: (i, 0))
    vec = pl.BlockSpec((1, D), lambda i: (0, 0))
    ins = [x, g, dz] + ([res] if has_res else [])
    return pl.pallas_call(
        body, name=name, grid=(T // tm,),
        in_specs=[row, vec, row] + ([row] if has_res else []),
        out_specs=[row, vec],
        out_shape=[jax.ShapeDtypeStruct((T, D), out_dtype), jax.ShapeDtypeStruct((1, D), F32)],
        compiler_params=_cparams(("arbitrary",)),
    )(*ins)


def _norm_dx(x, g, dz):
    r = lax.rsqrt(jnp.mean(x * x, axis=-1, keepdims=True) + NORM_EPS)
    xhat = x * r
    dxh = dz * g
    dx = r * (dxh - xhat * jnp.mean(dxh * xhat, axis=-1, keepdims=True))
    return dx, jnp.sum(dz * xhat, axis=0, keepdims=True)


def _pre_post_bwd(h_in, g_pre, du, dh, y_below, g_post_below, *, name):
    T, D = h_in.shape
    tm = _pick(T, ROW_BLOCK, SUBLANES)

    def body(h_ref, gp_ref, du_ref, dh_ref, y_ref, gq_ref, dhn_ref, dy_ref, dgp_ref, dgq_ref):
        i = pl.program_id(0)

        @pl.when(i == 0)
        def _():
            dgp_ref[...] = jnp.zeros_like(dgp_ref)
            dgq_ref[...] = jnp.zeros_like(dgq_ref)

        dx, dgp = _norm_dx(h_ref[...], gp_ref[...], du_ref[...])
        dhn = dh_ref[...] + dx
        dhn_ref[...] = dhn
        dgp_ref[...] += dgp
        dy, dgq = _norm_dx(y_ref[...], gq_ref[...], dhn)
        dy_ref[...] = dy.astype(dy_ref.dtype)
        dgq_ref[...] += dgq

    row = pl.BlockSpec((tm, D), lambda i: (i, 0))
    vec = pl.BlockSpec((1, D), lambda i: (0, 0))
    return pl.pallas_call(
        body, name=name, grid=(T // tm,),
        in_specs=[row, vec, row, row, row, vec],
        out_specs=[row, row, vec, vec],
        out_shape=[jax.ShapeDtypeStruct((T, D), F32), jax.ShapeDtypeStruct((T, D), BF16),
                   jax.ShapeDtypeStruct((1, D), F32), jax.ShapeDtypeStruct((1, D), F32)],
        compiler_params=_cparams(("arbitrary",)),
    )(h_in, g_pre, du, dh, y_below, g_post_below)


def _post_loss(y, g, h, target, *, name):
    T, D = h.shape
    tm = _pick(T, ROW_BLOCK, SUBLANES)

    def body(y_ref, g_ref, h_ref, t_ref, dh_ref, l_ref):
        i = pl.program_id(0)

        @pl.when(i == 0)
        def _():
            l_ref[...] = jnp.zeros_like(l_ref)

        x = y_ref[...]
        r = lax.rsqrt(jnp.mean(x * x, axis=-1, keepdims=True) + NORM_EPS)
        e = (h_ref[...] + (x * r) * g_ref[...]) - t_ref[...]
        dh_ref[...] = e * (1.0 / D)
        row = jnp.sum(e * e, axis=-1, keepdims=True) * (0.5 / D)
        l_ref[...] += jnp.sum(row, axis=0, keepdims=True)

    row = pl.BlockSpec((tm, D), lambda i: (i, 0))
    return pl.pallas_call(
        body, name=name, grid=(T // tm,),
        in_specs=[row, pl.BlockSpec((1, D), lambda i: (0, 0)), row, row],
        out_specs=[row, pl.BlockSpec((1, 1), lambda i: (0, 0))],
        out_shape=[jax.ShapeDtypeStruct((T, D), F32), jax.ShapeDtypeStruct((1, 1), F32)],
        compiler_params=_cparams(("arbitrary",)),
    )(y, g, h, target)


def _attn_dims(P):
    Q = P * 4 // 9
    KV = Q // GROUP
    assert 2 * Q + 2 * KV == P and KV % LANES == 0 and Q % (2 * KV) == 0
    return Q, KV


def _attn_specs(Q, KV, nb):
    blk = WINDOW
    row = lambda i: jnp.minimum(i, nb - 1)
    q_spec = pl.BlockSpec((blk, Q), lambda i: (row(i), 0))
    kvc_spec = pl.BlockSpec((blk, 2 * KV), lambda i: (row(i), Q // (2 * KV)))
    kvp_spec = pl.BlockSpec((blk, 2 * KV), lambda i: (jnp.maximum(row(i) - 1, 0), Q // (2 * KV)))
    g_specs = [pl.BlockSpec((blk, 2 * KV), lambda i, b=b: (row(i), Q // (2 * KV) + 1 + b)) for b in range(Q // (2 * KV))]
    return q_spec, kvc_spec, kvp_spec, g_specs


def _stack_gate(g_refs, cols):
    width = g_refs[0].shape[1]
    parts = [g_refs[cs.start // width][:, cs.start % width:cs.start % width + LANES] for cs in cols]
    return jnp.concatenate(parts, axis=0).astype(F32)


PAIRS = GROUP // 2
STACK = PAIRS * WINDOW


def _band_mask(i):
    c = lax.broadcasted_iota(jnp.int32, (2 * WINDOW, STACK), 0)
    r = lax.broadcasted_iota(jnp.int32, (2 * WINDOW, STACK), 1) & (WINDOW - 1)
    first_key = jnp.where(i > 0, 0, WINDOW)
    return (c > r) & (c <= r + WINDOW) & (c >= first_key)


def _group_cols(kvh):
    c0 = kvh * GROUP * HEAD_DIM
    return [slice(c0 + j * LANES, c0 + (j + 1) * LANES) for j in range(PAIRS)]


def _stack(ref, cols, scale=None):
    x = jnp.concatenate([ref[:, cs] for cs in cols], axis=0).astype(F32)
    return x if scale is None else x * scale


def _group_sinks(sink_ref, kvh, half):
    return jnp.concatenate([jnp.full((1, WINDOW), sink_ref[kvh * GROUP + 2 * j + half], F32) for j in range(PAIRS)], axis=1)


def _pair_halves(x128, xt128, e):
    lo = lax.broadcasted_iota(jnp.int32, x128.shape, 1) < HEAD_DIM
    lo_t = lax.broadcasted_iota(jnp.int32, xt128.shape, 0) < HEAD_DIM
    if e == 0:
        x_lo, xt_lo = jnp.where(lo, x128, 0.0), jnp.where(lo_t, xt128, 0.0)
        x_hi, xt_hi = pltpu.roll(x_lo, HEAD_DIM, 1), pltpu.roll(xt_lo, HEAD_DIM, 0)
    else:
        x_hi, xt_hi = jnp.where(lo, 0.0, x128), jnp.where(lo_t, 0.0, xt128)
        x_lo, xt_lo = pltpu.roll(x_hi, HEAD_DIM, 1), pltpu.roll(xt_hi, HEAD_DIM, 0)
    return (x_lo.astype(BF16), x_hi.astype(BF16)), (xt_lo.astype(BF16), xt_hi.astype(BF16))


def _softmax_sink(st, allowed, sink):
    st = jnp.where(allowed, st, MASK_VALUE)
    m = jnp.maximum(jnp.max(st, axis=0, keepdims=True), sink)
    p = jnp.exp(st - m)
    es = jnp.exp(sink - m)
    inv = 1.0 / (jnp.sum(p, axis=0, keepdims=True) + es)
    return p * inv, es * inv


def _attn_fwd(proj, sinks, *, comm=None, name):
    T, P = proj.shape
    Q, KV = _attn_dims(P)
    nb = T // WINDOW
    npairs = KV // LANES
    scale = 1.0 / math.sqrt(HEAD_DIM)

    ng = Q // (2 * KV)

    def body(sink_ref, q_ref, kvc_ref, kvp_ref, *rest):
        g_refs, (out_ref, yp_ref) = rest[:ng], rest[ng:]
        i = pl.program_id(0)
        allowed = _band_mask(i)
        for p in range(npairs):
            ks = slice(p * LANES, (p + 1) * LANES)
            vs = slice(KV + p * LANES, KV + (p + 1) * LANES)
            k128 = jnp.concatenate([kvp_ref[:, ks], kvc_ref[:, ks]], axis=0).astype(F32)
            v128 = jnp.concatenate([kvp_ref[:, vs], kvc_ref[:, vs]], axis=0).astype(F32)
            kt128, vt128 = k128.T, v128.T
            for e in range(2):
                kvh = 2 * p + e
                khalf, _ = _pair_halves(k128, kt128, e)
                _, vthalf = _pair_halves(v128, vt128, e)
                cols = _group_cols(kvh)
                q4 = _stack(q_ref, cols, scale).astype(BF16)
                ot = None
                for half in range(2):
                    st = _dot_nt(khalf[half], q4)
                    pn, _ = _softmax_sink(st, allowed, _group_sinks(sink_ref, kvh, half))
                    o = _dot(vthalf[half], pn.astype(BF16))
                    ot = o if ot is None else ot + o
                o4 = ot.T
                g4 = _stack_gate(g_refs, cols)
                y4 = (o4 * (g4 * _sigmoid(g4))).astype(BF16)
                for j, cs in enumerate(cols):
                    out_ref[:, cs] = o4[j * WINDOW:(j + 1) * WINDOW]
                    yp_ref[:, cs] = y4[j * WINDOW:(j + 1) * WINDOW]

    q_spec, kvc_spec, kvp_spec, g_specs = _attn_specs(Q, KV, nb)
    row = pl.BlockSpec((WINDOW, Q), lambda i: (i, 0))
    return _call(
        body, name=name, grid=(nb,),
        in_specs=[pl.BlockSpec(memory_space=pltpu.SMEM), q_spec, kvc_spec, kvp_spec] + g_specs,
        out_specs=[row, row],
        out_shape=[jax.ShapeDtypeStruct((T, Q), F32), jax.ShapeDtypeStruct((T, Q), BF16)],
        scratch_shapes=[], semantics=("parallel",), args=(sinks, proj, proj, proj) + (proj,) * ng, comm=comm)


def _attn_bwd(proj, out, dyp, sinks, *, comm=None, name):
    T, P = proj.shape
    Q, KV = _attn_dims(P)
    nb = T // WINDOW
    npairs = KV // LANES
    H = Q // HEAD_DIM
    ng = Q // (2 * KV)
    scale = 1.0 / math.sqrt(HEAD_DIM)

    def body(sink_ref, q_ref, kvc_ref, kvp_ref, *rest):
        g_refs, (out_ref, dyp_ref, dqg_ref, dkv_ref, dsink_ref, carry_ref) = rest[:ng], rest[ng:]
        i = pl.program_id(0)

        @pl.when(i == 0)
        def _():
            carry_ref[...] = jnp.zeros_like(carry_ref)
            dsink_ref[...] = jnp.zeros_like(dsink_ref)

        @pl.when(i == nb)
        def _():
            dkv_ref[...] = carry_ref[...].astype(dkv_ref.dtype)

        @pl.when(i < nb)
        def _():
            allowed = _band_mask(i)
            lo = lax.broadcasted_iota(jnp.int32, (2 * WINDOW, LANES), 1) < HEAD_DIM
            sel_lane = lax.broadcasted_iota(jnp.int32, (SUBLANES, LANES), 1) < HEAD_DIM
            sels = (jnp.where(sel_lane, 1.0, 0.0).astype(BF16), jnp.where(sel_lane, 0.0, 1.0).astype(BF16))
            for p in range(npairs):
                ks = slice(p * LANES, (p + 1) * LANES)
                vs = slice(KV + p * LANES, KV + (p + 1) * LANES)
                k128 = jnp.concatenate([kvp_ref[:, ks], kvc_ref[:, ks]], axis=0).astype(F32)
                v128 = jnp.concatenate([kvp_ref[:, vs], kvc_ref[:, vs]], axis=0).astype(F32)
                kt128, vt128 = k128.T, v128.T
                dk_e, dv_e = [], []
                for e in range(2):
                    kvh = 2 * p + e
                    khalf, kthalf = _pair_halves(k128, kt128, e)
                    vhalf, _ = _pair_halves(v128, vt128, e)
                    cols = _group_cols(kvh)
                    q4 = _stack(q_ref, cols, scale).astype(BF16)
                    g4 = _stack_gate(g_refs, cols)
                    o4 = _stack(out_ref, cols)
                    dy4 = _stack(dyp_ref, cols)
                    sg = _sigmoid(g4)
                    do4 = dy4 * (g4 * sg)
                    dg4 = (dy4 * o4 * (sg * (1.0 + g4 * (1.0 - sg)))).astype(dqg_ref.dtype)
                    dod = do4 * o4
                    dod_hi = dod.astype(BF16)
                    dod_lo = (dod - dod_hi.astype(F32)).astype(BF16)
                    do4b = do4.astype(BF16)
                    dqt = None
                    dk_h, dv_h = [], []
                    for half in range(2):
                        delta = jnp.max(_dot_nt(sels[half], dod_hi) + _dot_nt(sels[half], dod_lo), axis=0, keepdims=True)
                        st = _dot_nt(khalf[half], q4)
                        pn, psink = _softmax_sink(st, allowed, _group_sinks(sink_ref, kvh, half))
                        dp = _dot_nt(vhalf[half], do4b)
                        ds = (pn * (dp - delta)).astype(BF16)
                        dq = _dot(kthalf[half], ds)
                        dqt = dq if dqt is None else dqt + dq
                        dk_h.append(_dot(ds, q4))
                        dv_h.append(_dot(pn.astype(BF16), do4b))
                        pd = psink * delta
                        for j in range(PAIRS):
                            n = kvh * GROUP + 2 * j + half
                            dsn = -jnp.sum(pd[:, j * WINDOW:(j + 1) * WINDOW], axis=1, keepdims=True)
                            dsink_ref[n:n + 1, :] += jnp.broadcast_to(dsn, (1, LANES))
                    dq4 = (dqt.T * scale).astype(dqg_ref.dtype)
                    for j, cs in enumerate(cols):
                        dqg_ref[:, cs] = dq4[j * WINDOW:(j + 1) * WINDOW]
                        dqg_ref[:, slice(Q + 2 * KV + cs.start, Q + 2 * KV + cs.stop)] = dg4[j * WINDOW:(j + 1) * WINDOW]
                    acc_k = jnp.where(lo, dk_h[0], dk_h[1])
                    acc_v = jnp.where(lo, dv_h[0], dv_h[1])
                    dk_e.append(acc_k + pltpu.roll(acc_k, HEAD_DIM, 1))
                    dv_e.append(acc_v + pltpu.roll(acc_v, HEAD_DIM, 1))
                for sl, de in ((ks, dk_e), (vs, dv_e)):
                    d128 = jnp.where(lo, de[0], de[1])
                    dkv_ref[:, sl] = (carry_ref[:, sl] + d128[:WINDOW]).astype(dkv_ref.dtype)
                    carry_ref[:, sl] = d128[WINDOW:]

    q_spec, kvc_spec, kvp_spec, g_specs = _attn_specs(Q, KV, nb)
    last = lambda i: (jnp.minimum(i, nb - 1), 0)
    row = pl.BlockSpec((WINDOW, Q), last)
    return _call(
        body, name=name, grid=(nb + 1,),
        in_specs=[pl.BlockSpec(memory_space=pltpu.SMEM), q_spec, kvc_spec, kvp_spec] + g_specs + [row, row],
        out_specs=[pl.BlockSpec((WINDOW, P), last),
                   pl.BlockSpec((WINDOW, 2 * KV), lambda i: (jnp.maximum(i - 1, 0), 0)),
                   pl.BlockSpec((H, LANES), lambda i: (0, 0))],
        out_shape=[jax.ShapeDtypeStruct((T, P), BF16), jax.ShapeDtypeStruct((T, 2 * KV), BF16),
                   jax.ShapeDtypeStruct((H, LANES), F32)],
        scratch_shapes=[pltpu.VMEM((WINDOW, 2 * KV), F32)],
        semantics=("arbitrary",), args=(sinks, proj, proj, proj) + (proj,) * ng + (out, dyp), comm=comm)


def _fill_columns(full, part, col_block, *, name):
    T, w = part.shape
    tm = _pick(T, MATMUL_TM, SUBLANES * 2)

    def body(full_ref, part_ref, o_ref):
        del full_ref
        o_ref[...] = part_ref[...]

    return pl.pallas_call(
        body, name=name, grid=(T // tm,),
        in_specs=[pl.BlockSpec(memory_space=pl.ANY), pl.BlockSpec((tm, w), lambda i: (i, 0))],
        out_specs=pl.BlockSpec((tm, w), lambda i: (i, col_block)),
        out_shape=jax.ShapeDtypeStruct(full.shape, full.dtype),
        input_output_aliases={0: 0},
        compiler_params=_cparams(("parallel",)),
    )(full, part)


LRU_CHUNK = 2048


def _shift_down(x, halo8, s):
    if s == 0:
        return x
    row8 = lax.broadcasted_iota(jnp.int32, (SUBLANES, 1), 0)
    r = pltpu.roll(x, s, 0)
    top = jnp.where(row8 < s, pltpu.roll(halo8, s, 0), r[:SUBLANES])
    return jnp.concatenate([top, r[SUBLANES:]], axis=0)


def _shift_up(x, halo8, s):
    if s == 0:
        return x
    n = x.shape[0]
    row8 = lax.broadcasted_iota(jnp.int32, (SUBLANES, 1), 0)
    r = pltpu.roll(x, n - s, 0)
    bot = jnp.where(row8 >= SUBLANES - s, pltpu.roll(halo8, SUBLANES - s, 0), r[n - SUBLANES:])
    return jnp.concatenate([r[:n - SUBLANES], bot], axis=0)


def _scan_fwd(a, b, c0):
    n = a.shape[0]
    row = lax.broadcasted_iota(jnp.int32, (n, 1), 0) & (SUBLANES - 1)
    s = 1
    while s < SUBLANES:
        keep = row >= s
        ar = jnp.where(keep, pltpu.roll(a, s, 0), 1.0)
        br = jnp.where(keep, pltpu.roll(b, s, 0), 0.0)
        b = a * br + b
        a = a * ar
        s *= 2
    out, c = [], c0
    for i in range(n // SUBLANES):
        rows = slice(i * SUBLANES, (i + 1) * SUBLANES)
        h = a[rows] * c + b[rows]
        out.append(h)
        c = h[SUBLANES - 1:]
    return jnp.concatenate(out, axis=0)


def _scan_rev(al, b, c0):
    n = al.shape[0]
    row = lax.broadcasted_iota(jnp.int32, (n, 1), 0) & (SUBLANES - 1)
    s = 1
    while s < SUBLANES:
        keep = row < SUBLANES - s
        ar = jnp.where(keep, pltpu.roll(al, n - s, 0), 1.0)
        br = jnp.where(keep, pltpu.roll(b, n - s, 0), 0.0)
        b = b + al * br
        al = al * ar
        s *= 2
    out, c = [], c0
    for i in reversed(range(n // SUBLANES)):
        rows = slice(i * SUBLANES, (i + 1) * SUBLANES)
        l = b[rows] + al[rows] * c
        out.append(l)
        c = l[:1]
    return jnp.concatenate(out[::-1], axis=0)


def _log1p_pos(z):
    return jnp.where(z < 0.01, z * (1.0 - z * (0.5 - z * (1.0 / 3.0))), jnp.log(1.0 + z))


def _one_minus_sq(a, log_a):
    x = 2.0 * log_a
    series = -x * (1.0 + x * (0.5 + x * (1.0 / 6.0)))
    return jnp.where(x > -0.02, series, 1.0 - a * a)


def _softplus_neg(lam):
    return jnp.maximum(-lam, 0.0) + _log1p_pos(jnp.exp(-jnp.abs(lam)))


def _lru_gates(xb, halo, wa, wx, ba, bx, cw_ref, cb, lam, with_inverse=False):
    xs = [_shift_down(xb, halo, s) for s in range(CONV_W)]
    xc = cb + xs[3] * cw_ref[0:1, :] + xs[2] * cw_ref[1:2, :] + xs[1] * cw_ref[2:3, :] + xs[0] * cw_ref[3:4, :]
    xcb = xc.astype(BF16)
    r = _sigmoid(_dot(xcb, wa) + ba)
    ig = _sigmoid(_dot(xcb, wx) + bx)
    sp = _softplus_neg(lam)
    log_a = (-C_RG * r) * sp
    a = jnp.exp(log_a)
    z = _one_minus_sq(a, log_a)
    if not with_inverse:
        return xs, xc, xcb, r, ig, sp, a, jnp.sqrt(z), None
    rmult = lax.rsqrt(z)
    return xs, xc, xcb, r, ig, sp, a, z * rmult, rmult


def _tile_rows(dtype):
    return SUBLANES * 4 // jnp.dtype(dtype).itemsize


def _last_rows(ref):
    return ref[...].astype(F32)[ref.shape[0] - SUBLANES:]


def _lru_specs(nh, nt, tc, rev):
    tix = (lambda t: nt - 1 - t) if rev else (lambda t: t)
    chunk = lambda off: pl.BlockSpec((tc, LRU_BLOCK), lambda h, t: (tix(t), h + off))
    prev8 = lambda off, rows: pl.BlockSpec((rows, LRU_BLOCK),
                                           lambda h, t: (jnp.maximum(tix(t) * (tc // rows) - 1, 0), h + off))
    wblk = pl.BlockSpec((None, LRU_BLOCK, LRU_BLOCK), lambda h, t: (h, 0, 0))
    vec = pl.BlockSpec((1, LRU_BLOCK), lambda h, t: (0, h))
    cwb = pl.BlockSpec((CONV_W, LRU_BLOCK), lambda h, t: (0, h))
    return tix, chunk, prev8, wblk, vec, cwb


def _lru_fwd(proj, wa, wx, ba, bx, cw, cb, lam, *, comm=None, name):
    T, W2 = proj.shape
    W = W2 // 2
    nh = W // LRU_BLOCK
    tc = _pick(T, LRU_CHUNK, SUBLANES)
    nt = T // tc

    def body(xb_ref, xh_ref, gt_ref, wa_ref, wx_ref, ba_ref, bx_ref, cw_ref, cb_ref, lam_ref, hs_ref, yp_ref, carry_ref):
        t = pl.program_id(1)

        @pl.when(t == 0)
        def _():
            carry_ref[...] = jnp.zeros_like(carry_ref)

        halo = jnp.where(t > 0, _last_rows(xh_ref), 0.0)
        _, xc, _, _, ig, _, a, mult, _ = _lru_gates(xb_ref[...].astype(F32), halo, wa_ref[...], wx_ref[...], ba_ref[...],
                                                  bx_ref[...], cw_ref, cb_ref[...], lam_ref[...])
        hs = _scan_fwd(a, mult * (ig * xc), carry_ref[SUBLANES - 1:SUBLANES, :])
        hs_ref[...] = hs
        carry_ref[...] = hs[tc - SUBLANES:]
        g = gt_ref[...].astype(F32)
        yp_ref[...] = (hs * (g * _sigmoid(g))).astype(BF16)

    _, chunk, prev8, wblk, vec, cwb = _lru_specs(nh, nt, tc, False)
    return _call(
        body, name=name, grid=(nh, nt),
        in_specs=[chunk(0), prev8(0, _tile_rows(proj.dtype)), chunk(nh), wblk, wblk, vec, vec, cwb, vec, vec],
        out_specs=[chunk(0), chunk(0)],
        out_shape=[jax.ShapeDtypeStruct((T, W), F32), jax.ShapeDtypeStruct((T, W), BF16)],
        scratch_shapes=[pltpu.VMEM((SUBLANES, LRU_BLOCK), F32)],
        semantics=("parallel", "arbitrary"), args=(proj, proj, proj, wa, wx, ba, bx, cw, cb, lam), comm=comm)


def _lru_bwd(proj, hs, dyp, wa, wx, ba, bx, cw, cb, lam, *, comm=None, name):
    T, W2 = proj.shape
    W = W2 // 2
    nh = W // LRU_BLOCK
    tc = _pick(T, LRU_CHUNK, SUBLANES)
    nt = T // tc

    def body(xb_ref, xh_ref, gt_ref, hs_ref, hh_ref, dyp_ref, wa_ref, wx_ref, ba_ref, bx_ref, cw_ref, cb_ref, lam_ref,
             dx_ref, dg_ref, dwa_ref, dwx_ref, dba_ref, dbx_ref, dcw_ref, dcb_ref, dlam_ref,
             ca_ref, cl_ref, cx_ref):
        t = pl.program_id(1)
        first = t == nt - 1

        @pl.when(t == 0)
        def _():
            for ref in (ca_ref, cl_ref, cx_ref, dwa_ref, dwx_ref, dba_ref, dbx_ref, dcw_ref, dcb_ref, dlam_ref):
                ref[...] = jnp.zeros_like(ref)

        xb = xb_ref[...].astype(F32)
        halo = jnp.where(first, 0.0, _last_rows(xh_ref))
        wa = wa_ref[...]
        wx = wx_ref[...]
        lam = lam_ref[...]
        xs, xc, xcb, r, ig, sp, a, mult, rmult = _lru_gates(xb, halo, wa, wx, ba_ref[...], bx_ref[...], cw_ref, cb_ref[...], lam,
                                                            with_inverse=True)
        hsv = hs_ref[...]
        g = gt_ref[...].astype(F32)
        dy = dyp_ref[...].astype(F32)
        sg = _sigmoid(g)
        dg_ref[...] = (dy * hsv * (sg * (1.0 + g * (1.0 - sg)))).astype(dg_ref.dtype)
        dhs = dy * (g * sg)

        al = _shift_up(a, ca_ref[...], 1)
        lmb = _scan_rev(al, dhs, cl_ref[0:1, :])
        hprev = _shift_down(hsv, jnp.where(first, 0.0, _last_rows(hh_ref)), 1)
        da = lmb * hprev
        ixc = ig * xc
        dmult = lmb * ixc
        dlog_a = a * (da - dmult * a * rmult)
        dr = dlog_a * (-C_RG * sp)
        dlam_ref[...] += jnp.sum(dlog_a * r, axis=0, keepdims=True) * (C_RG * _sigmoid(-lam))
        dpa = dr * (r * (1.0 - r))
        dpx = (lmb * mult * xc) * (ig * (1.0 - ig))
        dpab = dpa.astype(BF16)
        dpxb = dpx.astype(BF16)
        dwa_ref[...] += _dot_tn(xcb, dpab)
        dwx_ref[...] += _dot_tn(xcb, dpxb)
        dba_ref[...] += jnp.sum(dpa, axis=0, keepdims=True)
        dbx_ref[...] += jnp.sum(dpx, axis=0, keepdims=True)
        dxc = lmb * mult * ig + _dot_nt(dpab, wa) + _dot_nt(dpxb, wx)
        dcb_ref[...] += jnp.sum(dxc, axis=0, keepdims=True)
        for s in range(CONV_W):
            dcw_ref[CONV_W - 1 - s:CONV_W - s, :] += jnp.sum(dxc * xs[s], axis=0, keepdims=True)
        cxv = cx_ref[...]
        dxb = dxc * cw_ref[3:4, :]
        for s in range(1, CONV_W):
            dxb = dxb + _shift_up(dxc, cxv, s) * cw_ref[3 - s:4 - s, :]
        dx_ref[...] = dxb.astype(dx_ref.dtype)
        ca_ref[...] = a[:SUBLANES]
        cl_ref[...] = lmb[:SUBLANES]
        cx_ref[...] = dxc[:SUBLANES]

    tix, chunk, prev8, wblk, vec, cwb = _lru_specs(nh, nt, tc, True)
    hchunk = pl.BlockSpec((tc, LRU_BLOCK), lambda h, t: (tix(t), h))
    carry = pltpu.VMEM((SUBLANES, LRU_BLOCK), F32)
    return _call(
        body, name=name, grid=(nh, nt),
        in_specs=[chunk(0), prev8(0, _tile_rows(proj.dtype)), chunk(nh), hchunk, prev8(0, _tile_rows(hs.dtype)), hchunk,
                  wblk, wblk, vec, vec, cwb, vec, vec],
        out_specs=[hchunk, hchunk, wblk, wblk, vec, vec, cwb, vec, vec],
        out_shape=[jax.ShapeDtypeStruct((T, W2), BF16), jax.ShapeDtypeStruct((T, W), BF16),
                   jax.ShapeDtypeStruct((nh, LRU_BLOCK, LRU_BLOCK), F32), jax.ShapeDtypeStruct((nh, LRU_BLOCK, LRU_BLOCK), F32),
                   jax.ShapeDtypeStruct((1, W), F32), jax.ShapeDtypeStruct((1, W), F32),
                   jax.ShapeDtypeStruct((CONV_W, W), F32), jax.ShapeDtypeStruct((1, W), F32), jax.ShapeDtypeStruct((1, W), F32)],
        scratch_shapes=[carry, carry, carry], semantics=("parallel", "arbitrary"),
        args=(proj, proj, proj, hs, hs, dyp, wa, wx, ba, bx, cw, cb, lam), comm=comm)


def _position():
    return lax.axis_index("x"), lax.axis_index("y"), lax.axis_index("c")


def _sems(n):
    return [pltpu.SemaphoreType.DMA((n, 7)), pltpu.SemaphoreType.DMA((n, 7)), pltpu.SemaphoreType.DMA((n,))]


BY_COLUMNS = -1


def _gather_comm(arrs, axes):
    n = len(arrs)

    def tools(ins, outs, send_sems, recv_sems, local_sems):
        x, y, c = _position()
        me, sibling = (x, y, c), (x, y, 1 - c)
        chips = [(1 - x, y), (x, 1 - y), (1 - x, 1 - y)]

        def slot(a, pos):
            if axes[a] == BY_COLUMNS:
                w = ins[a].shape[1]
                return outs[a].at[:, pl.ds(pl.multiple_of(pos * w, LANES), w)]
            return outs[a].at[(slice(None),) * axes[a] + (pos,)]

        def copy(a, k, block, to, src=None):
            px, py, pc = block
            rows = slot(a, 4 * px + 2 * py + pc)
            return pltpu.make_async_remote_copy(
                src_ref=rows if src is None else src, dst_ref=rows,
                send_sem=send_sems.at[a, k], recv_sem=recv_sems.at[a, k],
                device_id=to, device_id_type=MESH)

        own = lambda a: pltpu.make_async_copy(ins[a], slot(a, 4 * x + 2 * y + c), local_sems.at[a])
        first = lambda a: ([copy(a, 0, me, sibling, src=ins[a])]
                           + [copy(a, 1 + j, me, (*chip, c), src=ins[a]) for j, chip in enumerate(chips)])
        passed = lambda a: [copy(a, 4 + j, (*chip, c), sibling) for j, chip in enumerate(chips)]
        return me, sibling, chips, c, copy, own, first, passed

    def start(ins, outs, *sems):
        *_, own, first, _ = tools(ins, outs, *sems)
        for a in range(n):
            own(a).start()
            for cp in first(a):
                cp.start()

    def mid(ins, outs, *sems):
        me, _, chips, c, copy, _, _, passed = tools(ins, outs, *sems)
        for a in range(n):
            fwd = passed(a)
            for j, chip in enumerate(chips):
                copy(a, 1 + j, (*chip, c), me).wait_recv()
                fwd[j].start()

    def finish(ins, outs, *sems):
        me, sibling, chips, c, copy, own, first, passed = tools(ins, outs, *sems)
        for a in range(n):
            copy(a, 0, sibling, me).wait_recv()
            for j, chip in enumerate(chips):
                copy(a, 4 + j, (*chip, 1 - c), me).wait_recv()
        for a in range(n):
            for cp in first(a) + passed(a):
                cp.wait_send()
            own(a).wait()

    shapes = [jax.ShapeDtypeStruct((a.shape[0], N_DEV * a.shape[1]) if ax == BY_COLUMNS else
                                   a.shape[:ax] + (N_DEV,) + a.shape[ax:], a.dtype) for a, ax in zip(arrs, axes)]
    return Comm(list(arrs), shapes, _sems(n), start, mid, finish)


def _comm_call(comm, *, name):
    ci, co = len(comm.arrays), len(comm.out_shapes)

    def body(*refs):
        ins, outs, sems = refs[:ci], refs[ci:ci + co], refs[ci + co:]
        comm.start(ins, outs, *sems)
        comm.mid(ins, outs, *sems)
        comm.finish(ins, outs, *sems)

    any_spec = pl.BlockSpec(memory_space=pl.ANY)
    return pl.pallas_call(body, name=name, in_specs=[any_spec] * ci, out_specs=[any_spec] * co,
                          out_shape=comm.out_shapes, scratch_shapes=comm.sems)(*comm.arrays)


def _exchange_comm(items):
    n = len(items)

    def tools(ins, outs, send_sems, recv_sems, local_sems):
        x, y, c = _position()
        me = 4 * x + 2 * y + c

        def src(a, pos):
            if items[a][1]:
                rows = ins[a].shape[1] // N_DEV
                return ins[a].at[:, pl.ds(pl.multiple_of(pos * rows, rows), rows)]
            return ins[a].at[pos]

        copies = [pltpu.make_async_copy(src(a, me), outs[a].at[me], local_sems.at[a]) for a in range(n)]
        for k in range(1, N_DEV):
            px = x ^ ((k >> 2) & 1)
            py = y ^ ((k >> 1) & 1)
            pc = c ^ (k & 1)
            copies += [pltpu.make_async_remote_copy(
                src_ref=src(a, 4 * px + 2 * py + pc), dst_ref=outs[a].at[me],
                send_sem=send_sems.at[a, k - 1], recv_sem=recv_sems.at[a, k - 1],
                device_id=(px, py, pc), device_id_type=MESH) for a in range(n)]
        return copies

    def start(ins, outs, *sems):
        for cp in tools(ins, outs, *sems):
            cp.start()

    def mid(ins, outs, *sems):
        pass

    def finish(ins, outs, *sems):
        for cp in tools(ins, outs, *sems):
            cp.wait()

    shapes = []
    for arr, split in items:
        blk = (arr.shape[0], arr.shape[1] // N_DEV) + arr.shape[2:] if split else arr.shape[1:]
        shapes.append(jax.ShapeDtypeStruct((N_DEV,) + blk, arr.dtype))
    return Comm([arr for arr, _ in items], shapes, _sems(n), start, mid, finish)


def _adamw(parts, w, m, v, *, name):
    L, R, C = w.shape
    assert len(parts) == L
    row_bytes = 2 * (L * N_DEV * C * parts[0].dtype.itemsize + 7 * C * 4)
    tr = _pick(R, max(16, ADAMW_VMEM // row_bytes), 16)
    nr = R // tr
    c1 = 1.0 / (1.0 - ADAM_B1 ** ADAM_STEP)
    c2 = 1.0 / (1.0 - ADAM_B2 ** ADAM_STEP)

    def body(*refs):
        p_refs = refs[:L]
        w_ref, m_ref, v_ref, g_ref, d_ref, nm_ref, nv_ref = refs[L:]
        layer = pl.program_id(0)
        for idx, p_ref in enumerate(p_refs):
            @pl.when(layer == idx)
            def _():
                g = p_ref[0].astype(F32)
                for s in range(1, N_DEV):
                    g = g + p_ref[s].astype(F32)
                nm = ADAM_B1 * m_ref[...] + (1.0 - ADAM_B1) * g
                nv = ADAM_B2 * v_ref[...] + (1.0 - ADAM_B2) * (g * g)
                g_ref[...] = g
                nm_ref[...] = nm
                nv_ref[...] = nv
                d_ref[...] = -ADAM_LR * ((nm * c1) / (jnp.sqrt(nv * c2) + ADAM_EPS) + ADAM_WD * w_ref[...])

    def part_spec(idx):
        return pl.BlockSpec((N_DEV, tr, C),
                            lambda l, i: (0, jnp.where(l == idx, i, jnp.where(l < idx, 0, nr - 1)), 0))

    blk = pl.BlockSpec((None, tr, C), lambda l, i: (l, i, 0))
    return pl.pallas_call(
        body, name=name, grid=(L, nr),
        in_specs=[part_spec(idx) for idx in range(L)] + [blk, blk, blk],
        out_specs=[blk] * 4,
        out_shape=[jax.ShapeDtypeStruct((L, R, C), F32)] * 4,
        compiler_params=_cparams(("arbitrary", "arbitrary")),
    )(*parts, w, m, v)


def _pack(flat_parts, row_multiple, dtype):
    lead = flat_parts[0].shape[:-1]
    total = sum(p.shape[-1] for p in flat_parts)
    quantum = PACK_W * row_multiple
    padded = -(-total // quantum) * quantum
    parts = [p.astype(dtype) for p in flat_parts]
    if padded > total:
        parts.append(jnp.zeros(lead + (padded - total,), dtype))
    return jnp.concatenate(parts, axis=-1).reshape(lead + (padded // PACK_W, PACK_W))


def _unpack(buf, shapes):
    lead = buf.shape[:-2]
    flat = buf.reshape(lead + (-1,))
    out, off = [], 0
    for shp in shapes:
        n = math.prod(shp)
        out.append(flat[..., off:off + n].reshape(lead + tuple(shp)))
        off += n
    return out


def _to_full(seg, ax):
    shard = seg.shape[1:]
    full = shard[:ax] + (N_DEV * shard[ax],) + shard[ax + 1:]
    return jnp.moveaxis(seg, 0, ax).reshape(full)


def _to_shards(full, ax):
    shp = full.shape
    split = shp[:ax] + (N_DEV, shp[ax] // N_DEV) + shp[ax + 1:]
    return jnp.moveaxis(full.reshape(split), ax, 0).reshape(N_DEV, -1)


BIG = (("attn_w_in", 1), ("attn_w_out", 1), ("lru_w_in", 1), ("lru_w_a", 2), ("lru_w_x", 2), ("lru_w_out", 1))
SMALL = (("lru_conv_w", 2), ("lru_conv_b", 1), ("lru_b_a", 2), ("lru_b_x", 2), ("lru_lambda", 1))
REPL = ("norm_pre", "norm_post", "attn_sinks")
ORDER = ("norm_pre", "norm_post", "attn_w_in", "attn_w_out", "attn_sinks", "lru_w_in", "lru_conv_w", "lru_conv_b",
         "lru_w_a", "lru_b_a", "lru_w_x", "lru_b_x", "lru_lambda", "lru_w_out")


def kernel(x, norm_pre, norm_post, attn_w_in, attn_w_out, attn_sinks, lru_w_in, lru_conv_w, lru_conv_b, lru_w_a, lru_b_a, lru_w_x, lru_b_x, lru_lambda, lru_w_out, loss_target, m_norm_pre, m_norm_post, m_attn_w_in, m_attn_w_out, m_attn_sinks, m_lru_w_in, m_lru_conv_w, m_lru_conv_b, m_lru_w_a, m_lru_b_a, m_lru_w_x, m_lru_b_x, m_lru_lambda, m_lru_w_out, v_norm_pre, v_norm_post, v_attn_w_in, v_attn_w_out, v_attn_sinks, v_lru_w_in, v_lru_conv_w, v_lru_conv_b, v_lru_w_a, v_lru_b_a, v_lru_w_x, v_lru_b_x, v_lru_lambda, v_lru_w_out):
    W = dict(norm_pre=norm_pre, norm_post=norm_post, attn_w_in=attn_w_in, attn_w_out=attn_w_out, attn_sinks=attn_sinks,
             lru_w_in=lru_w_in, lru_conv_w=lru_conv_w, lru_conv_b=lru_conv_b, lru_w_a=lru_w_a, lru_b_a=lru_b_a,
             lru_w_x=lru_w_x, lru_b_x=lru_b_x, lru_lambda=lru_lambda, lru_w_out=lru_w_out)
    M = dict(norm_pre=m_norm_pre, norm_post=m_norm_post, attn_w_in=m_attn_w_in, attn_w_out=m_attn_w_out,
             attn_sinks=m_attn_sinks, lru_w_in=m_lru_w_in, lru_conv_w=m_lru_conv_w, lru_conv_b=m_lru_conv_b,
             lru_w_a=m_lru_w_a, lru_b_a=m_lru_b_a, lru_w_x=m_lru_w_x, lru_b_x=m_lru_b_x, lru_lambda=m_lru_lambda,
             lru_w_out=m_lru_w_out)
    V = dict(norm_pre=v_norm_pre, norm_post=v_norm_post, attn_w_in=v_attn_w_in, attn_w_out=v_attn_w_out,
             attn_sinks=v_attn_sinks, lru_w_in=v_lru_w_in, lru_conv_w=v_lru_conv_w, lru_conv_b=v_lru_conv_b,
             lru_w_a=v_lru_w_a, lru_b_a=v_lru_b_a, lru_w_x=v_lru_w_x, lru_b_x=v_lru_b_x, lru_lambda=v_lru_lambda,
             lru_w_out=v_lru_w_out)

    h0 = x[0]
    target = loss_target[0]
    T, D = h0.shape
    depth = norm_pre.shape[0]
    n_attn = attn_w_in.shape[0]
    Q = attn_w_out.shape[1] * N_DEV
    KV = Q // GROUP
    LW = lru_w_out.shape[1] * N_DEV
    nh = LW // LRU_BLOCK

    n_lru = lru_w_in.shape[0]
    big_names = [n for n, _ in BIG]
    small_names = [n for n, _ in SMALL]
    small_shapes = [W[n].shape for n in small_names]
    repl_shapes = [W[n].shape for n in REPL]

    flat = lambda a: a.reshape(-1)
    def layer_shards(layer):
        j = layer // 2
        names = ("attn_w_in", "attn_w_out") if layer % 2 == 0 else ("lru_w_in", "lru_w_a", "lru_w_x", "lru_w_out")
        axes = dict(lru_w_a=1, lru_w_x=1, lru_w_in=BY_COLUMNS)
        return [W[n][j].astype(BF16) for n in names], [axes.get(n, 0) for n in names]

    def layer_weights(layer, gathered):
        if layer % 2 == 0:
            g_in, g_out = gathered
            w_in = jnp.moveaxis(g_in, 0, 1).reshape(D, -1)
            return dict(w_in=w_in, w_out=None if g_out is None else g_out.reshape(Q, D))
        g_in, g_wa, g_wx, g_out = gathered
        return dict(w_in=g_in, w_a=g_wa.reshape(nh, LRU_BLOCK, LRU_BLOCK),
                    w_x=g_wx.reshape(nh, LRU_BLOCK, LRU_BLOCK), w_out=g_out.reshape(LW, D))

    arrs0, _ = layer_shards(0)
    (u,), first = _rms_fwd(h0, norm_pre[0:1], name="rms_fwd",
                           comm=_gather_comm([arrs0[0], _pack([flat(W[n]) for n in small_names], SUBLANES, F32)], [0, 0]))
    weights = {0: layer_weights(0, (first[0], None))}
    full = {}
    for (n, ax), seg in zip(SMALL, _unpack(first[-1], small_shapes)):
        full[n] = _to_full(seg, ax)
    cw_f = full["lru_conv_w"]
    cb_f = full["lru_conv_b"][:, None, :]
    ba_f = full["lru_b_a"].reshape(-1, 1, LW)
    bx_f = full["lru_b_x"].reshape(-1, 1, LW)
    lam_f = full["lru_lambda"][:, None, :]

    h = h0
    saved = []
    for layer in range(depth):
        j = layer // 2
        wl = weights[layer]
        nxt = _gather_comm(*layer_shards(layer + 1)) if layer + 1 < depth else None
        if layer % 2 == 0:
            if wl["w_out"] is None:
                proj, got = _matmul(u, wl["w_in"], out_dtype=BF16, comm=_gather_comm([arrs0[1]], [0]), name="attn_in")
                wl["w_out"] = got[0].reshape(Q, D)
            else:
                proj = _matmul(u, wl["w_in"], out_dtype=BF16, name="attn_in")
            (mix, ypre), got = _attn_fwd(proj, attn_sinks[j], comm=nxt, name="attn_fwd")
            y = _matmul(ypre, wl["w_out"], name="attn_out")
        else:
            proj = _matmul(u, wl["w_in"], out_dtype=BF16, name="lru_in")
            (mix, ypre), got = _lru_fwd(proj, wl["w_a"], wl["w_x"], ba_f[j], bx_f[j], cw_f[j], cb_f[j], lam_f[j],
                                        comm=nxt, name="lru_fwd")
            y = _matmul(ypre, wl["w_out"], name="lru_out")
        if nxt is not None:
            weights[layer + 1] = layer_weights(layer + 1, got)
        saved.append((h, u, proj, mix, ypre, y))
        if layer + 1 < depth:
            h, u = _post_fwd(y, norm_post[layer:layer + 1], h, norm_pre[layer + 1:layer + 2], name="post_fwd")
        else:
            dh, loss_part = _post_loss(y, norm_post[layer:layer + 1], h, target, name="post_loss")

    g_pre = [None] * depth
    g_post = [None] * depth
    small_vec = [n for n in small_names] + ["attn_sinks"]
    grads = {n: [None] * W[n].shape[0] for n in small_vec}
    recv = {}

    def carried(keyed):
        return _exchange_comm([item for _, item in keyed]) if keyed else None

    def landed(keyed, got):
        for (key, _), r in zip(keyed, got):
            recv[key] = r

    pending = []
    for layer in reversed(range(depth)):
        j = layer // 2
        h_in, u, proj, mix, ypre, y = saved[layer]
        wl = weights[layer]
        if layer == depth - 1:
            dy, g_post[layer] = _rms_bwd(y, norm_post[layer:layer + 1], dh, None, out_dtype=BF16, name="post_bwd")
        if layer % 2 == 0:
            dyp = _matmul(dy, wl["w_out"], tb=True, out_dtype=BF16, name="attn_out_dx")
            dw_out = _matmul(ypre, dy, ta=True, out_dtype=BF16, name="attn_out_dw")
            pending.append((("attn_w_out", j), (dw_out.reshape(N_DEV, Q // N_DEV, D), False)))
            (dqg, dkv, dsink), got = _attn_bwd(proj, mix, dyp, attn_sinks[j], comm=carried(pending), name="attn_bwd")
            landed(pending, got)
            grads["attn_sinks"][j] = dsink[:, 0]
            dproj = _fill_columns(dqg, dkv, Q // (2 * KV), name="attn_dproj")
            dw = _matmul(u, dproj, ta=True, out_dtype=BF16, name="attn_in_dw")
            pending = [(("attn_w_in", j), (jnp.moveaxis(dw.reshape(D, N_DEV, -1), 1, 0), False))]
            if layer == 0:
                du, got = _matmul(dproj, wl["w_in"], tb=True, comm=carried(pending), name="attn_in_dx")
                landed(pending, got)
                pending = []
            else:
                du = _matmul(dproj, wl["w_in"], tb=True, name="attn_in_dx")
        else:
            dyp = _matmul(dy, wl["w_out"], tb=True, out_dtype=BF16, name="lru_out_dx")
            dw_out = _matmul(ypre, dy, ta=True, out_dtype=BF16, name="lru_out_dw")
            pending.append((("lru_w_out", j), (dw_out.reshape(N_DEV, LW // N_DEV, D), False)))
            (dxb, dgt, dwa, dwx, dba, dbx, dcw, dcb, dlam), got = _lru_bwd(
                proj, mix, dyp, wl["w_a"], wl["w_x"], ba_f[j], bx_f[j], cw_f[j], cb_f[j], lam_f[j],
                comm=carried(pending), name="lru_bwd")
            landed(pending, got)
            grads["lru_b_a"][j], grads["lru_b_x"][j] = dba.reshape(nh, LRU_BLOCK), dbx.reshape(nh, LRU_BLOCK)
            grads["lru_conv_w"][j], grads["lru_conv_b"][j], grads["lru_lambda"][j] = dcw, dcb[0], dlam[0]
            dproj = _fill_columns(dxb, dgt, 1, name="lru_dproj")
            du = _matmul(dproj, wl["w_in"], tb=True, name="lru_in_dx")
            mine = [(("lru_w_a", j), (dwa, True)), (("lru_w_x", j), (dwx, True))]
            dw, got = _matmul(u, dproj, ta=True, out_dtype=BF16, by_owner=True, comm=carried(mine), name="lru_in_dw")
            landed(mine, got)
            pending = [(("lru_w_in", j), (dw, False))]
        if layer > 0:
            dh, dy, g_pre[layer], g_post[layer - 1] = _pre_post_bwd(
                h_in, norm_pre[layer:layer + 1], du, dh, saved[layer - 1][5], norm_post[layer - 1:layer], name="pre_post_bwd")
        else:
            dh, g_pre[layer] = _rms_bwd(h_in, norm_pre[layer:layer + 1], du, dh, out_dtype=F32, name="pre_bwd")

    gfull = {n: jnp.stack(g) for n, g in grads.items()}
    gfull["norm_pre"] = jnp.concatenate(g_pre, axis=0)
    gfull["norm_post"] = jnp.concatenate(g_post, axis=0)

    repl_part = [jnp.broadcast_to(gfull[n].reshape(1, -1), (N_DEV, gfull[n].size)) for n in REPL]
    loss_slot = jnp.broadcast_to(loss_part.reshape(1, 1), (N_DEV, 1))
    send_small = _pack([_to_shards(gfull[n], ax) for n, ax in SMALL] + repl_part + [loss_slot], SUBLANES, F32)
    last = pending + [(("small", 0), (send_small, False))]
    landed(last, _comm_call(carried(last), name="exchange_last"))

    zero1 = jnp.zeros((1,), F32)
    outs = {}
    for n in big_names:
        shp = W[n].shape
        as3 = lambda a: a.reshape((shp[0], -1, shp[-1]))
        parts = [recv[n, j].reshape((N_DEV, -1, shp[-1])) for j in range(shp[0])]
        res = _adamw(parts, as3(W[n]), as3(M[n]), as3(V[n]), name="adamw_" + n)
        for kind, a in zip(("grad", "delta", "new_m", "new_v"), res):
            outs[kind, n] = a.reshape(shp)
    res_small = _adamw([recv["small", 0]],
                       *[_pack([flat(S[n]) for n in small_names] + [flat(S[n]) for n in REPL] + [zero1], SUBLANES, F32)[None]
                         for S in (W, M, V)], name="adamw_small")
    for kind, rs in zip(("grad", "delta", "new_m", "new_v"), res_small):
        for n, a in zip(small_names + list(REPL) + ["loss"], _unpack(rs[0], small_shapes + repl_shapes + [(1,)])):
            outs[kind, n] = a
    loss = outs["grad", "loss"][0]
    result = [loss, dh[None]]
    for kind in ("grad", "delta", "new_m", "new_v"):
        result += [outs[kind, n] for n in ORDER]
    return tuple(result)
```

```python
import math
from typing import Callable, NamedTuple

import jax
import jax.numpy as jnp
from jax import lax
from jax.experimental import pallas as pl
from jax.experimental.pallas import tpu as pltpu

F32 = jnp.float32
BF16 = jnp.bfloat16

N_DEV = 8
HEAD_DIM = 64
GROUP = 8
WINDOW = 128
LRU_BLOCK = 256
CONV_W = 4
C_RG = 8.0
NORM_EPS = 1e-6
MASK_VALUE = -1e30

ADAM_LR = 0.001
ADAM_B1 = 0.9
ADAM_B2 = 0.999
ADAM_EPS = 1e-08
ADAM_WD = 0.01
ADAM_STEP = 10

ROW_BLOCK = 256
LANES = 128
SUBLANES = 8
PACK_W = 1024
VMEM_LIMIT = 56 * 1024 * 1024
MATMUL_VMEM = 36 * 1024 * 1024
MXU_DIM = 256
MATMUL_TM = 1024
MATMUL_TN = 1536
MATMUL_TN_WHOLE = 2048
ADAMW_VMEM = 24 * 1024 * 1024
MESH = pl.DeviceIdType.MESH


def _cparams(sem=None):
    return pltpu.CompilerParams(dimension_semantics=sem, vmem_limit_bytes=VMEM_LIMIT)


class Comm(NamedTuple):
    arrays: list
    out_shapes: list
    sems: list
    start: Callable
    mid: Callable
    finish: Callable


def _call(body, *, name, grid, in_specs, out_specs, out_shape, scratch_shapes, semantics, args, comm=None):
    if comm is None:
        res = pl.pallas_call(body, name=name, grid=grid, in_specs=in_specs, out_specs=out_specs, out_shape=out_shape,
                             scratch_shapes=scratch_shapes, compiler_params=_cparams(semantics))(*args)
        return list(res), []
    n_in, n_out, n_scr = len(in_specs), len(out_specs), len(scratch_shapes)
    ci, co = len(comm.arrays), len(comm.out_shapes)
    steps = math.prod(grid)

    def hosted(*refs):
        ins, cins = refs[:n_in], refs[n_in:n_in + ci]
        o0 = n_in + ci
        outs, couts = refs[o0:o0 + n_out], refs[o0 + n_out:o0 + n_out + co]
        s0 = o0 + n_out + co
        scr, sems = refs[s0:s0 + n_scr], refs[s0 + n_scr:]
        step = 0
        for ax, g in enumerate(grid):
            step = step * g + pl.program_id(ax)

        @pl.when(step == 0)
        def _():
            comm.start(cins, couts, *sems)

        body(*ins, *outs, *scr)

        @pl.when(step == (steps * 3) // 4)
        def _():
            comm.mid(cins, couts, *sems)

        @pl.when(step == steps - 1)
        def _():
            comm.finish(cins, couts, *sems)

    any_spec = pl.BlockSpec(memory_space=pl.ANY)
    res = pl.pallas_call(
        hosted, name=name, grid=grid,
        in_specs=list(in_specs) + [any_spec] * ci, out_specs=list(out_specs) + [any_spec] * co,
        out_shape=list(out_shape) + list(comm.out_shapes), scratch_shapes=list(scratch_shapes) + list(comm.sems),
        compiler_params=_cparams(("arbitrary",) * len(grid)),
    )(*args, *comm.arrays)
    return list(res[:n_out]), list(res[n_out:])


def _pick(n, target, quantum):
    best = None
    for t in range(quantum, min(n, target) + 1, quantum):
        if n % t == 0:
            best = t
    return n if best is None else best


def _sigmoid(x):
    return 1.0 / (1.0 + jnp.exp(-x))


def _dot(a, b):
    return lax.dot_general(a, b, (((1,), (0,)), ((), ())), preferred_element_type=F32)


def _dot_nt(a, b):
    return lax.dot_general(a, b, (((1,), (1,)), ((), ())), preferred_element_type=F32)


def _dot_tn(a, b):
    return lax.dot_general(a, b, (((0,), (0,)), ((), ())), preferred_element_type=F32)


def _matmul(a, b, *, ta=False, tb=False, out_dtype=F32, by_owner=False, comm=None, name):
    if ta:
        K, M = a.shape
    else:
        M, K = a.shape
    if tb:
        N, K2 = b.shape
    else:
        K2, N = b.shape
    assert K == K2, (a.shape, b.shape, ta, tb)
    osz = jnp.dtype(out_dtype).itemsize

    def plan(tn):
        def vmem_bytes(tm, tk):
            acc = 0 if tk == K else tm * tn * 4
            return 2 * (tm * tk * a.dtype.itemsize + tk * tn * b.dtype.itemsize + tm * tn * osz) + acc

        def deepest(tm):
            fits = [t for t in range(MXU_DIM, K + 1, MXU_DIM) if K % t == 0 and vmem_bytes(tm, t) <= MATMUL_VMEM]
            return max(fits or [_pick(K, 2 * MXU_DIM, MXU_DIM)])

        tm = _pick(M, MATMUL_TM, MXU_DIM)
        if deepest(tm) < K and tm % (2 * MXU_DIM) == 0 and deepest(tm // 2) == K:
            tm //= 2
        return tm, deepest(tm)

    tn = N // N_DEV if by_owner else _pick(N, MATMUL_TN, MXU_DIM)
    tm, tk = plan(tn)
    if not by_owner and not ta and N <= MATMUL_TN_WHOLE and plan(N)[1] == K:
        tn = N
        tm, tk = plan(tn)
    assert tn % LANES == 0
    nk = K // tk
    dot = {(False, False): _dot, (False, True): _dot_nt, (True, False): _dot_tn}[(ta, tb)]

    if nk == 1:
        def body(a_ref, b_ref, o_ref):
            o_ref[...] = dot(a_ref[...].astype(BF16), b_ref[...].astype(BF16)).astype(o_ref.dtype)
        scratch = []
    else:
        def body(a_ref, b_ref, o_ref, acc_ref):
            k = pl.program_id(2)

            @pl.when(k == 0)
            def _():
                acc_ref[...] = jnp.zeros_like(acc_ref)

            acc_ref[...] += dot(a_ref[...].astype(BF16), b_ref[...].astype(BF16))

            @pl.when(k == nk - 1)
            def _():
                o_ref[...] = acc_ref[...].astype(o_ref.dtype)
        scratch = [pltpu.VMEM((tm, tn), F32)]

    a_spec = pl.BlockSpec((tk, tm), lambda j, i, k: (k, i)) if ta else pl.BlockSpec((tm, tk), lambda j, i, k: (i, k))
    b_spec = pl.BlockSpec((tn, tk), lambda j, i, k: (j, k)) if tb else pl.BlockSpec((tk, tn), lambda j, i, k: (k, j))
    if by_owner:
        o_spec = pl.BlockSpec((None, tm, tn), lambda j, i, k: (j, i, 0))
        o_shape = jax.ShapeDtypeStruct((N_DEV, M, tn), out_dtype)
    else:
        o_spec = pl.BlockSpec((tm, tn), lambda j, i, k: (i, j))
        o_shape = jax.ShapeDtypeStruct((M, N), out_dtype)
    res, extra = _call(body, name=name, grid=(N // tn, M // tm, nk), in_specs=[a_spec, b_spec], out_specs=[o_spec],
                       out_shape=[o_shape], scratch_shapes=scratch, semantics=("parallel", "parallel", "arbitrary"),
                       args=(a, b), comm=comm)
    return res[0] if comm is None else (res[0], extra)


def _rms_fwd(h, g, *, comm=None, name):
    T, D = h.shape
    tm = _pick(T, ROW_BLOCK, SUBLANES)

    def body(h_ref, g_ref, u_ref):
        x = h_ref[...]
        r = lax.rsqrt(jnp.mean(x * x, axis=-1, keepdims=True) + NORM_EPS)
        u_ref[...] = ((x * r) * g_ref[...]).astype(u_ref.dtype)

    return _call(
        body, name=name, grid=(T // tm,),
        in_specs=[pl.BlockSpec((tm, D), lambda i: (i, 0)), pl.BlockSpec((1, D), lambda i: (0, 0))],
        out_specs=[pl.BlockSpec((tm, D), lambda i: (i, 0))],
        out_shape=[jax.ShapeDtypeStruct((T, D), BF16)],
        scratch_shapes=[], semantics=("parallel",), args=(h, g), comm=comm)


def _post_fwd(y, g, h, g_next, *, name):
    T, D = y.shape
    tm = _pick(T, ROW_BLOCK, SUBLANES)

    def body(y_ref, g_ref, h_ref, gn_ref, o_ref, u_ref):
        x = y_ref[...]
        r = lax.rsqrt(jnp.mean(x * x, axis=-1, keepdims=True) + NORM_EPS)
        ho = h_ref[...] + (x * r) * g_ref[...]
        o_ref[...] = ho
        rn = lax.rsqrt(jnp.mean(ho * ho, axis=-1, keepdims=True) + NORM_EPS)
        u_ref[...] = ((ho * rn) * gn_ref[...]).astype(u_ref.dtype)

    row = pl.BlockSpec((tm, D), lambda i: (i, 0))
    vec = pl.BlockSpec((1, D), lambda i: (0, 0))
    return pl.pallas_call(
        body, name=name, grid=(T // tm,),
        in_specs=[row, vec, row, vec],
        out_specs=[row, row],
        out_shape=[jax.ShapeDtypeStruct((T, D), F32), jax.ShapeDtypeStruct((T, D), BF16)],
        compiler_params=_cparams(("parallel",)),
    )(y, g, h, g_next)


def _rms_bwd(x, g, dz, res, *, out_dtype, name):
    T, D = x.shape
    tm = _pick(T, ROW_BLOCK, SUBLANES)
    has_res = res is not None

    def body(*refs):
        if has_res:
            x_ref, g_ref, dz_ref, res_ref, dx_ref, dg_ref = refs
        else:
            x_ref, g_ref, dz_ref, dx_ref, dg_ref = refs
        i = pl.program_id(0)

        @pl.when(i == 0)
        def _():
            dg_ref[...] = jnp.zeros_like(dg_ref)

        dx, dg = _norm_dx(x_ref[...], g_ref[...], dz_ref[...])
        dg_ref[...] += dg
        if has_res:
            dx = dx + res_ref[...]
        dx_ref[...] = dx.astype(dx_ref.dtype)

    row = pl.BlockSpec((tm, D), lambda i: (i, 0))
    vec = pl.BlockSpec((1, D), lambda i: (0, 0))
    ins = [x, g, dz] + ([res] if has_res else [])
    return pl.pallas_call(
        body, name=name, grid=(T // tm,),
        in_specs=[row, vec, row] + ([row] if has_res else []),
        out_specs=[row, vec],
        out_shape=[jax.ShapeDtypeStruct((T, D), out_dtype), jax.ShapeDtypeStruct((1, D), F32)],
        compiler_params=_cparams(("arbitrary",)),
    )(*ins)


def _norm_dx(x, g, dz):
    r = lax.rsqrt(jnp.mean(x * x, axis=-1, keepdims=True) + NORM_EPS)
    xhat = x * r
    dxh = dz * g
    dx = r * (dxh - xhat * jnp.mean(dxh * xhat, axis=-1, keepdims=True))
    return dx, jnp.sum(dz * xhat, axis=0, keepdims=True)


def _pre_post_bwd(h_in, g_pre, du, dh, y_below, g_post_below, *, name):
    T, D = h_in.shape
    tm = _pick(T, ROW_BLOCK, SUBLANES)

    def body(h_ref, gp_ref, du_ref, dh_ref, y_ref, gq_ref, dhn_ref, dy_ref, dgp_ref, dgq_ref):
        i = pl.program_id(0)

        @pl.when(i == 0)
        def _():
            dgp_ref[...] = jnp.zeros_like(dgp_ref)
            dgq_ref[...] = jnp.zeros_like(dgq_ref)

        dx, dgp = _norm_dx(h_ref[...], gp_ref[...], du_ref[...])
        dhn = dh_ref[...] + dx
        dhn_ref[...] = dhn
        dgp_ref[...] += dgp
        dy, dgq = _norm_dx(y_ref[...], gq_ref[...], dhn)
        dy_ref[...] = dy.astype(dy_ref.dtype)
        dgq_ref[...] += dgq

    row = pl.BlockSpec((tm, D), lambda i: (i, 0))
    vec = pl.BlockSpec((1, D), lambda i: (0, 0))
    return pl.pallas_call(
        body, name=name, grid=(T // tm,),
        in_specs=[row, vec, row, row, row, vec],
        out_specs=[row, row, vec, vec],
        out_shape=[jax.ShapeDtypeStruct((T, D), F32), jax.ShapeDtypeStruct((T, D), BF16),
                   jax.ShapeDtypeStruct((1, D), F32), jax.ShapeDtypeStruct((1, D), F32)],
        compiler_params=_cparams(("arbitrary",)),
    )(h_in, g_pre, du, dh, y_below, g_post_below)


def _post_loss(y, g, h, target, *, name):
    T, D = h.shape
    tm = _pick(T, ROW_BLOCK, SUBLANES)

    def body(y_ref, g_ref, h_ref, t_ref, dh_ref, l_ref):
        i = pl.program_id(0)

        @pl.when(i == 0)
        def _():
            l_ref[...] = jnp.zeros_like(l_ref)

        x = y_ref[...]
        r = lax.rsqrt(jnp.mean(x * x, axis=-1, keepdims=True) + NORM_EPS)
        e = (h_ref[...] + (x * r) * g_ref[...]) - t_ref[...]
        dh_ref[...] = e * (1.0 / D)
        row = jnp.sum(e * e, axis=-1, keepdims=True) * (0.5 / D)
        l_ref[...] += jnp.sum(row, axis=0, keepdims=True)

    row = pl.BlockSpec((tm, D), lambda i: (i, 0))
    return pl.pallas_call(
        body, name=name, grid=(T // tm,),
        in_specs=[row, pl.BlockSpec((1, D), lambda i: (0, 0)), row, row],
        out_specs=[row, pl.BlockSpec((1, 1), lambda i: (0, 0))],
        out_shape=[jax.ShapeDtypeStruct((T, D), F32), jax.ShapeDtypeStruct((1, 1), F32)],
        compiler_params=_cparams(("arbitrary",)),
    )(y, g, h, target)


def _attn_dims(P):
    Q = P * 4 // 9
    KV = Q // GROUP
    assert 2 * Q + 2 * KV == P and KV % LANES == 0 and Q % (2 * KV) == 0
    return Q, KV


def _attn_specs(Q, KV, nb):
    blk = WINDOW
    row = lambda i: jnp.minimum(i, nb - 1)
    q_spec = pl.BlockSpec((blk, Q), lambda i: (row(i), 0))
    kvc_spec = pl.BlockSpec((blk, 2 * KV), lambda i: (row(i), Q // (2 * KV)))
    kvp_spec = pl.BlockSpec((blk, 2 * KV), lambda i: (jnp.maximum(row(i) - 1, 0), Q // (2 * KV)))
    g_specs = [pl.BlockSpec((blk, 2 * KV), lambda i, b=b: (row(i), Q // (2 * KV) + 1 + b)) for b in range(Q // (2 * KV))]
    return q_spec, kvc_spec, kvp_spec, g_specs


def _stack_gate(g_refs, cols):
    width = g_refs[0].shape[1]
    parts = [g_refs[cs.start // width][:, cs.start % width:cs.start % width + LANES] for cs in cols]
    return jnp.concatenate(parts, axis=0).astype(F32)


PAIRS = GROUP // 2
STACK = PAIRS * WINDOW


def _band_bias():
    c = lax.broadcasted_iota(jnp.int32, (2, 2 * WINDOW, STACK), 1)
    r = lax.broadcasted_iota(jnp.int32, (2, 2 * WINDOW, STACK), 2) & (WINDOW - 1)
    first = lax.broadcasted_iota(jnp.int32, (2, 2 * WINDOW, STACK), 0) == 0
    allowed = (c > r) & (c <= r + WINDOW) & ((c >= WINDOW) | ~first)
    return jnp.where(allowed, 0.0, MASK_VALUE).astype(F32)


BIAS_SPEC = pl.BlockSpec((None, 2 * WINDOW, STACK), lambda i: (jnp.minimum(i, 1), 0, 0))


def _group_cols(kvh):
    c0 = kvh * GROUP * HEAD_DIM
    return [slice(c0 + j * LANES, c0 + (j + 1) * LANES) for j in range(PAIRS)]


def _stack(ref, cols, scale=None):
    x = jnp.concatenate([ref[:, cs] for cs in cols], axis=0).astype(F32)
    return x if scale is None else x * scale


def _group_sinks(sink_ref, kvh, half):
    return jnp.concatenate([jnp.full((1, WINDOW), sink_ref[kvh * GROUP + 2 * j + half], F32) for j in range(PAIRS)], axis=1)


def _pair_halves(x128, xt128, e):
    lo = lax.broadcasted_iota(jnp.int32, x128.shape, 1) < HEAD_DIM
    lo_t = lax.broadcasted_iota(jnp.int32, xt128.shape, 0) < HEAD_DIM
    if e == 0:
        x_lo, xt_lo = jnp.where(lo, x128, 0.0), jnp.where(lo_t, xt128, 0.0)
        x_hi, xt_hi = pltpu.roll(x_lo, HEAD_DIM, 1), pltpu.roll(xt_lo, HEAD_DIM, 0)
    else:
        x_hi, xt_hi = jnp.where(lo, 0.0, x128), jnp.where(lo_t, 0.0, xt128)
        x_lo, xt_lo = pltpu.roll(x_hi, HEAD_DIM, 1), pltpu.roll(xt_hi, HEAD_DIM, 0)
    return (x_lo.astype(BF16), x_hi.astype(BF16)), (xt_lo.astype(BF16), xt_hi.astype(BF16))


def _softmax_sink(st, bias, sink):
    st = st + bias
    m = jnp.maximum(jnp.max(st, axis=0, keepdims=True), sink)
    p = jnp.exp(st - m)
    es = jnp.exp(sink - m)
    inv = 1.0 / (jnp.sum(p, axis=0, keepdims=True) + es)
    return p * inv, es * inv


def _attn_fwd(proj, sinks, *, comm=None, name):
    T, P = proj.shape
    Q, KV = _attn_dims(P)
    nb = T // WINDOW
    npairs = KV // LANES
    scale = 1.0 / math.sqrt(HEAD_DIM)

    ng = Q // (2 * KV)

    def body(sink_ref, bias_ref, q_ref, kvc_ref, kvp_ref, *rest):
        g_refs, (out_ref, yp_ref) = rest[:ng], rest[ng:]
        allowed = bias_ref[...]
        for p in range(npairs):
            ks = slice(p * LANES, (p + 1) * LANES)
            vs = slice(KV + p * LANES, KV + (p + 1) * LANES)
            k128 = jnp.concatenate([kvp_ref[:, ks], kvc_ref[:, ks]], axis=0).astype(F32)
            v128 = jnp.concatenate([kvp_ref[:, vs], kvc_ref[:, vs]], axis=0).astype(F32)
            kt128, vt128 = k128.T, v128.T
            for e in range(2):
                kvh = 2 * p + e
                khalf, _ = _pair_halves(k128, kt128, e)
                _, vthalf = _pair_halves(v128, vt128, e)
                cols = _group_cols(kvh)
                q4 = _stack(q_ref, cols, scale).astype(BF16)
                ot = None
                for half in range(2):
                    st = _dot_nt(khalf[half], q4)
                    pn, _ = _softmax_sink(st, allowed, _group_sinks(sink_ref, kvh, half))
                    o = _dot(vthalf[half], pn.astype(BF16))
                    ot = o if ot is None else ot + o
                o4 = ot.T
                g4 = _stack_gate(g_refs, cols)
                y4 = (o4 * (g4 * _sigmoid(g4))).astype(BF16)
                for j, cs in enumerate(cols):
                    out_ref[:, cs] = o4[j * WINDOW:(j + 1) * WINDOW]
                    yp_ref[:, cs] = y4[j * WINDOW:(j + 1) * WINDOW]

    q_spec, kvc_spec, kvp_spec, g_specs = _attn_specs(Q, KV, nb)
    row = pl.BlockSpec((WINDOW, Q), lambda i: (i, 0))
    return _call(
        body, name=name, grid=(nb,),
        in_specs=[pl.BlockSpec(memory_space=pltpu.SMEM), BIAS_SPEC, q_spec, kvc_spec, kvp_spec] + g_specs,
        out_specs=[row, row],
        out_shape=[jax.ShapeDtypeStruct((T, Q), F32), jax.ShapeDtypeStruct((T, Q), BF16)],
        scratch_shapes=[], semantics=("parallel",), args=(sinks, _band_bias(), proj, proj, proj) + (proj,) * ng, comm=comm)


def _attn_bwd(proj, out, dyp, sinks, *, comm=None, name):
    T, P = proj.shape
    Q, KV = _attn_dims(P)
    nb = T // WINDOW
    npairs = KV // LANES
    H = Q // HEAD_DIM
    ng = Q // (2 * KV)
    scale = 1.0 / math.sqrt(HEAD_DIM)

    def body(sink_ref, bias_ref, q_ref, kvc_ref, kvp_ref, *rest):
        g_refs, (out_ref, dyp_ref, dqg_ref, dkv_ref, dsink_ref, carry_ref) = rest[:ng], rest[ng:]
        i = pl.program_id(0)

        @pl.when(i == 0)
        def _():
            carry_ref[...] = jnp.zeros_like(carry_ref)
            dsink_ref[...] = jnp.zeros_like(dsink_ref)

        @pl.when(i == nb)
        def _():
            dkv_ref[...] = carry_ref[...].astype(dkv_ref.dtype)

        @pl.when(i < nb)
        def _():
            allowed = bias_ref[...]
            lo = lax.broadcasted_iota(jnp.int32, (2 * WINDOW, LANES), 1) < HEAD_DIM
            sel_lane = lax.broadcasted_iota(jnp.int32, (SUBLANES, LANES), 1) < HEAD_DIM
            sels = (jnp.where(sel_lane, 1.0, 0.0).astype(BF16), jnp.where(sel_lane, 0.0, 1.0).astype(BF16))
            for p in range(npairs):
                ks = slice(p * LANES, (p + 1) * LANES)
                vs = slice(KV + p * LANES, KV + (p + 1) * LANES)
                k128 = jnp.concatenate([kvp_ref[:, ks], kvc_ref[:, ks]], axis=0).astype(F32)
                v128 = jnp.concatenate([kvp_ref[:, vs], kvc_ref[:, vs]], axis=0).astype(F32)
                kt128, vt128 = k128.T, v128.T
                dk_e, dv_e = [], []
                for e in range(2):
                    kvh = 2 * p + e
                    khalf, kthalf = _pair_halves(k128, kt128, e)
                    vhalf, _ = _pair_halves(v128, vt128, e)
                    cols = _group_cols(kvh)
                    q4 = _stack(q_ref, cols, scale).astype(BF16)
                    g4 = _stack_gate(g_refs, cols)
                    o4 = _stack(out_ref, cols)
                    dy4 = _stack(dyp_ref, cols)
                    sg = _sigmoid(g4)
                    do4 = dy4 * (g4 * sg)
                    dg4 = (dy4 * o4 * (sg * (1.0 + g4 * (1.0 - sg)))).astype(dqg_ref.dtype)
                    dod = do4 * o4
                    dod_hi = dod.astype(BF16)
                    dod_lo = (dod - dod_hi.astype(F32)).astype(BF16)
                    do4b = do4.astype(BF16)
                    dqt = None
                    dk_h, dv_h = [], []
                    for half in range(2):
                        delta = jnp.max(_dot_nt(sels[half], dod_hi) + _dot_nt(sels[half], dod_lo), axis=0, keepdims=True)
                        st = _dot_nt(khalf[half], q4)
                        pn, psink = _softmax_sink(st, allowed, _group_sinks(sink_ref, kvh, half))
                        dp = _dot_nt(vhalf[half], do4b)
                        ds = (pn * (dp - delta)).astype(BF16)
                        dq = _dot(kthalf[half], ds)
                        dqt = dq if dqt is None else dqt + dq
                        dk_h.append(_dot(ds, q4))
                        dv_h.append(_dot(pn.astype(BF16), do4b))
                        pd = psink * delta
                        for j in range(PAIRS):
                            n = kvh * GROUP + 2 * j + half
                            dsn = -jnp.sum(pd[:, j * WINDOW:(j + 1) * WINDOW], axis=1, keepdims=True)
                            dsink_ref[n:n + 1, :] += jnp.broadcast_to(dsn, (1, LANES))
                    dq4 = (dqt.T * scale).astype(dqg_ref.dtype)
                    for j, cs in enumerate(cols):
                        dqg_ref[:, cs] = dq4[j * WINDOW:(j + 1) * WINDOW]
                        dqg_ref[:, slice(Q + 2 * KV + cs.start, Q + 2 * KV + cs.stop)] = dg4[j * WINDOW:(j + 1) * WINDOW]
                    acc_k = jnp.where(lo, dk_h[0], dk_h[1])
                    acc_v = jnp.where(lo, dv_h[0], dv_h[1])
                    dk_e.append(acc_k + pltpu.roll(acc_k, HEAD_DIM, 1))
                    dv_e.append(acc_v + pltpu.roll(acc_v, HEAD_DIM, 1))
                for sl, de in ((ks, dk_e), (vs, dv_e)):
                    d128 = jnp.where(lo, de[0], de[1])
                    dkv_ref[:, sl] = (carry_ref[:, sl] + d128[:WINDOW]).astype(dkv_ref.dtype)
                    carry_ref[:, sl] = d128[WINDOW:]

    q_spec, kvc_spec, kvp_spec, g_specs = _attn_specs(Q, KV, nb)
    last = lambda i: (jnp.minimum(i, nb - 1), 0)
    row = pl.BlockSpec((WINDOW, Q), last)
    return _call(
        body, name=name, grid=(nb + 1,),
        in_specs=[pl.BlockSpec(memory_space=pltpu.SMEM), BIAS_SPEC, q_spec, kvc_spec, kvp_spec] + g_specs + [row, row],
        out_specs=[pl.BlockSpec((WINDOW, P), last),
                   pl.BlockSpec((WINDOW, 2 * KV), lambda i: (jnp.maximum(i - 1, 0), 0)),
                   pl.BlockSpec((H, LANES), lambda i: (0, 0))],
        out_shape=[jax.ShapeDtypeStruct((T, P), BF16), jax.ShapeDtypeStruct((T, 2 * KV), BF16),
                   jax.ShapeDtypeStruct((H, LANES), F32)],
        scratch_shapes=[pltpu.VMEM((WINDOW, 2 * KV), F32)],
        semantics=("arbitrary",), args=(sinks, _band_bias(), proj, proj, proj) + (proj,) * ng + (out, dyp), comm=comm)


def _fill_columns(full, part, col_block, *, name):
    T, w = part.shape
    tm = _pick(T, MATMUL_TM, SUBLANES * 2)

    def body(full_ref, part_ref, o_ref):
        del full_ref
        o_ref[...] = part_ref[...]

    return pl.pallas_call(
        body, name=name, grid=(T // tm,),
        in_specs=[pl.BlockSpec(memory_space=pl.ANY), pl.BlockSpec((tm, w), lambda i: (i, 0))],
        out_specs=pl.BlockSpec((tm, w), lambda i: (i, col_block)),
        out_shape=jax.ShapeDtypeStruct(full.shape, full.dtype),
        input_output_aliases={0: 0},
        compiler_params=_cparams(("parallel",)),
    )(full, part)


LRU_CHUNK = 2048


def _shift_down(x, halo8, s):
    if s == 0:
        return x
    row8 = lax.broadcasted_iota(jnp.int32, (SUBLANES, 1), 0)
    r = pltpu.roll(x, s, 0)
    top = jnp.where(row8 < s, pltpu.roll(halo8, s, 0), r[:SUBLANES])
    return jnp.concatenate([top, r[SUBLANES:]], axis=0)


def _shift_up(x, halo8, s):
    if s == 0:
        return x
    n = x.shape[0]
    row8 = lax.broadcasted_iota(jnp.int32, (SUBLANES, 1), 0)
    r = pltpu.roll(x, n - s, 0)
    bot = jnp.where(row8 >= SUBLANES - s, pltpu.roll(halo8, SUBLANES - s, 0), r[n - SUBLANES:])
    return jnp.concatenate([r[:n - SUBLANES], bot], axis=0)


def _scan_fwd(a, b, c0):
    n = a.shape[0]
    row = lax.broadcasted_iota(jnp.int32, (n, 1), 0) & (SUBLANES - 1)
    s = 1
    while s < SUBLANES:
        keep = row >= s
        ar = jnp.where(keep, pltpu.roll(a, s, 0), 1.0)
        br = jnp.where(keep, pltpu.roll(b, s, 0), 0.0)
        b = a * br + b
        a = a * ar
        s *= 2
    out, c = [], c0
    for i in range(n // SUBLANES):
        rows = slice(i * SUBLANES, (i + 1) * SUBLANES)
        h = a[rows] * c + b[rows]
        out.append(h)
        c = h[SUBLANES - 1:]
    return jnp.concatenate(out, axis=0)


def _scan_rev(al, b, c0):
    n = al.shape[0]
    row = lax.broadcasted_iota(jnp.int32, (n, 1), 0) & (SUBLANES - 1)
    s = 1
    while s < SUBLANES:
        keep = row < SUBLANES - s
        ar = jnp.where(keep, pltpu.roll(al, n - s, 0), 1.0)
        br = jnp.where(keep, pltpu.roll(b, n - s, 0), 0.0)
        b = b + al * br
        al = al * ar
        s *= 2
    out, c = [], c0
    for i in reversed(range(n // SUBLANES)):
        rows = slice(i * SUBLANES, (i + 1) * SUBLANES)
        l = b[rows] + al[rows] * c
        out.append(l)
        c = l[:1]
    return jnp.concatenate(out[::-1], axis=0)


def _log1p_pos(z):
    return jnp.where(z < 0.01, z * (1.0 - z * (0.5 - z * (1.0 / 3.0))), jnp.log(1.0 + z))


def _one_minus_sq(a, log_a):
    x = 2.0 * log_a
    series = -x * (1.0 + x * (0.5 + x * (1.0 / 6.0)))
    return jnp.where(x > -0.02, series, 1.0 - a * a)


def _softplus_neg(lam):
    return jnp.maximum(-lam, 0.0) + _log1p_pos(jnp.exp(-jnp.abs(lam)))


def _lru_gates(xb, halo, wa, wx, ba, bx, cw_ref, cb, lam, with_inverse=False):
    xs = [_shift_down(xb, halo, s) for s in range(CONV_W)]
    xc = cb + xs[3] * cw_ref[0:1, :] + xs[2] * cw_ref[1:2, :] + xs[1] * cw_ref[2:3, :] + xs[0] * cw_ref[3:4, :]
    xcb = xc.astype(BF16)
    r = _sigmoid(_dot(xcb, wa) + ba)
    ig = _sigmoid(_dot(xcb, wx) + bx)
    sp = _softplus_neg(lam)
    log_a = (-C_RG * r) * sp
    a = jnp.exp(log_a)
    z = _one_minus_sq(a, log_a)
    if not with_inverse:
        return xs, xc, xcb, r, ig, sp, a, jnp.sqrt(z), None
    rmult = lax.rsqrt(z)
    return xs, xc, xcb, r, ig, sp, a, z * rmult, rmult


def _tile_rows(dtype):
    return SUBLANES * 4 // jnp.dtype(dtype).itemsize


def _last_rows(ref):
    return ref[...].astype(F32)[ref.shape[0] - SUBLANES:]


def _lru_specs(nh, nt, tc, rev):
    tix = (lambda t: nt - 1 - t) if rev else (lambda t: t)
    chunk = lambda off: pl.BlockSpec((tc, LRU_BLOCK), lambda h, t: (tix(t), h + off))
    prev8 = lambda off, rows: pl.BlockSpec((rows, LRU_BLOCK),
                                           lambda h, t: (jnp.maximum(tix(t) * (tc // rows) - 1, 0), h + off))
    wblk = pl.BlockSpec((None, LRU_BLOCK, LRU_BLOCK), lambda h, t: (h, 0, 0))
    vec = pl.BlockSpec((1, LRU_BLOCK), lambda h, t: (0, h))
    cwb = pl.BlockSpec((CONV_W, LRU_BLOCK), lambda h, t: (0, h))
    return tix, chunk, prev8, wblk, vec, cwb


def _lru_fwd(proj, wa, wx, ba, bx, cw, cb, lam, *, comm=None, name):
    T, W2 = proj.shape
    W = W2 // 2
    nh = W // LRU_BLOCK
    tc = _pick(T, LRU_CHUNK, SUBLANES)
    nt = T // tc

    def body(xb_ref, xh_ref, gt_ref, wa_ref, wx_ref, ba_ref, bx_ref, cw_ref, cb_ref, lam_ref, hs_ref, yp_ref, carry_ref):
        t = pl.program_id(1)

        @pl.when(t == 0)
        def _():
            carry_ref[...] = jnp.zeros_like(carry_ref)

        halo = jnp.where(t > 0, _last_rows(xh_ref), 0.0)
        _, xc, _, _, ig, _, a, mult, _ = _lru_gates(xb_ref[...].astype(F32), halo, wa_ref[...], wx_ref[...], ba_ref[...],
                                                  bx_ref[...], cw_ref, cb_ref[...], lam_ref[...])
        hs = _scan_fwd(a, mult * (ig * xc), carry_ref[SUBLANES - 1:SUBLANES, :])
        hs_ref[...] = hs
        carry_ref[...] = hs[tc - SUBLANES:]
        g = gt_ref[...].astype(F32)
        yp_ref[...] = (hs * (g * _sigmoid(g))).astype(BF16)

    _, chunk, prev8, wblk, vec, cwb = _lru_specs(nh, nt, tc, False)
    return _call(
        body, name=name, grid=(nh, nt),
        in_specs=[chunk(0), prev8(0, _tile_rows(proj.dtype)), chunk(nh), wblk, wblk, vec, vec, cwb, vec, vec],
        out_specs=[chunk(0), chunk(0)],
        out_shape=[jax.ShapeDtypeStruct((T, W), F32), jax.ShapeDtypeStruct((T, W), BF16)],
        scratch_shapes=[pltpu.VMEM((SUBLANES, LRU_BLOCK), F32)],
        semantics=("parallel", "arbitrary"), args=(proj, proj, proj, wa, wx, ba, bx, cw, cb, lam), comm=comm)


def _lru_bwd(proj, hs, dyp, wa, wx, ba, bx, cw, cb, lam, *, comm=None, name):
    T, W2 = proj.shape
    W = W2 // 2
    nh = W // LRU_BLOCK
    tc = _pick(T, LRU_CHUNK, SUBLANES)
    nt = T // tc

    def body(xb_ref, xh_ref, gt_ref, hs_ref, hh_ref, dyp_ref, wa_ref, wx_ref, ba_ref, bx_ref, cw_ref, cb_ref, lam_ref,
             dx_ref, dg_ref, dwa_ref, dwx_ref, dba_ref, dbx_ref, dcw_ref, dcb_ref, dlam_ref,
             ca_ref, cl_ref, cx_ref):
        t = pl.program_id(1)
        first = t == nt - 1

        @pl.when(t == 0)
        def _():
            for ref in (ca_ref, cl_ref, cx_ref, dwa_ref, dwx_ref, dba_ref, dbx_ref, dcw_ref, dcb_ref, dlam_ref):
                ref[...] = jnp.zeros_like(ref)

        xb = xb_ref[...].astype(F32)
        halo = jnp.where(first, 0.0, _last_rows(xh_ref))
        wa = wa_ref[...]
        wx = wx_ref[...]
        lam = lam_ref[...]
        xs, xc, xcb, r, ig, sp, a, mult, rmult = _lru_gates(xb, halo, wa, wx, ba_ref[...], bx_ref[...], cw_ref, cb_ref[...], lam,
                                                            with_inverse=True)
        hsv = hs_ref[...]
        g = gt_ref[...].astype(F32)
        dy = dyp_ref[...].astype(F32)
        sg = _sigmoid(g)
        dg_ref[...] = (dy * hsv * (sg * (1.0 + g * (1.0 - sg)))).astype(dg_ref.dtype)
        dhs = dy * (g * sg)

        al = _shift_up(a, ca_ref[...], 1)
        lmb = _scan_rev(al, dhs, cl_ref[0:1, :])
        hprev = _shift_down(hsv, jnp.where(first, 0.0, _last_rows(hh_ref)), 1)
        da = lmb * hprev
        ixc = ig * xc
        dmult = lmb * ixc
        dlog_a = a * (da - dmult * a * rmult)
        dr = dlog_a * (-C_RG * sp)
        dlam_ref[...] += jnp.sum(dlog_a * r, axis=0, keepdims=True) * (C_RG * _sigmoid(-lam))
        dpa = dr * (r * (1.0 - r))
        dpx = (lmb * mult * xc) * (ig * (1.0 - ig))
        dpab = dpa.astype(BF16)
        dpxb = dpx.astype(BF16)
        dwa_ref[...] += _dot_tn(xcb, dpab)
        dwx_ref[...] += _dot_tn(xcb, dpxb)
        dba_ref[...] += jnp.sum(dpa, axis=0, keepdims=True)
        dbx_ref[...] += jnp.sum(dpx, axis=0, keepdims=True)
        dxc = lmb * mult * ig + _dot_nt(dpab, wa) + _dot_nt(dpxb, wx)
        dcb_ref[...] += jnp.sum(dxc, axis=0, keepdims=True)
        for s in range(CONV_W):
            dcw_ref[CONV_W - 1 - s:CONV_W - s, :] += jnp.sum(dxc * xs[s], axis=0, keepdims=True)
        cxv = cx_ref[...]
        dxb = dxc * cw_ref[3:4, :]
        for s in range(1, CONV_W):
            dxb = dxb + _shift_up(dxc, cxv, s) * cw_ref[3 - s:4 - s, :]
        dx_ref[...] = dxb.astype(dx_ref.dtype)
        ca_ref[...] = a[:SUBLANES]
        cl_ref[...] = lmb[:SUBLANES]
        cx_ref[...] = dxc[:SUBLANES]

    tix, chunk, prev8, wblk, vec, cwb = _lru_specs(nh, nt, tc, True)
    hchunk = pl.BlockSpec((tc, LRU_BLOCK), lambda h, t: (tix(t), h))
    carry = pltpu.VMEM((SUBLANES, LRU_BLOCK), F32)
    return _call(
        body, name=name, grid=(nh, nt),
        in_specs=[chunk(0), prev8(0, _tile_rows(proj.dtype)), chunk(nh), hchunk, prev8(0, _tile_rows(hs.dtype)), hchunk,
                  wblk, wblk, vec, vec, cwb, vec, vec],
        out_specs=[hchunk, hchunk, wblk, wblk, vec, vec, cwb, vec, vec],
        out_shape=[jax.ShapeDtypeStruct((T, W2), BF16), jax.ShapeDtypeStruct((T, W), BF16),
                   jax.ShapeDtypeStruct((nh, LRU_BLOCK, LRU_BLOCK), F32), jax.ShapeDtypeStruct((nh, LRU_BLOCK, LRU_BLOCK), F32),
                   jax.ShapeDtypeStruct((1, W), F32), jax.ShapeDtypeStruct((1, W), F32),
                   jax.ShapeDtypeStruct((CONV_W, W), F32), jax.ShapeDtypeStruct((1, W), F32), jax.ShapeDtypeStruct((1, W), F32)],
        scratch_shapes=[carry, carry, carry], semantics=("parallel", "arbitrary"),
        args=(proj, proj, proj, hs, hs, dyp, wa, wx, ba, bx, cw, cb, lam), comm=comm)


def _position():
    return lax.axis_index("x"), lax.axis_index("y"), lax.axis_index("c")


def _sems(n):
    return [pltpu.SemaphoreType.DMA((n, 7)), pltpu.SemaphoreType.DMA((n, 7)), pltpu.SemaphoreType.DMA((n,))]


BY_COLUMNS = -1


def _gather_comm(arrs, axes):
    n = len(arrs)

    def tools(ins, outs, send_sems, recv_sems, local_sems):
        x, y, c = _position()
        me, sibling = (x, y, c), (x, y, 1 - c)
        chips = [(1 - x, y), (x, 1 - y), (1 - x, 1 - y)]

        def slot(a, pos):
            if axes[a] == BY_COLUMNS:
                w = ins[a].shape[1]
                return outs[a].at[:, pl.ds(pl.multiple_of(pos * w, LANES), w)]
            return outs[a].at[(slice(None),) * axes[a] + (pos,)]

        def copy(a, k, block, to, src=None):
            px, py, pc = block
            rows = slot(a, 4 * px + 2 * py + pc)
            return pltpu.make_async_remote_copy(
                src_ref=rows if src is None else src, dst_ref=rows,
                send_sem=send_sems.at[a, k], recv_sem=recv_sems.at[a, k],
                device_id=to, device_id_type=MESH)

        own = lambda a: pltpu.make_async_copy(ins[a], slot(a, 4 * x + 2 * y + c), local_sems.at[a])
        first = lambda a: ([copy(a, 0, me, sibling, src=ins[a])]
                           + [copy(a, 1 + j, me, (*chip, c), src=ins[a]) for j, chip in enumerate(chips)])
        passed = lambda a: [copy(a, 4 + j, (*chip, c), sibling) for j, chip in enumerate(chips)]
        return me, sibling, chips, c, copy, own, first, passed

    def start(ins, outs, *sems):
        *_, own, first, _ = tools(ins, outs, *sems)
        for a in range(n):
            own(a).start()
            for cp in first(a):
                cp.start()

    def mid(ins, outs, *sems):
        me, _, chips, c, copy, _, _, passed = tools(ins, outs, *sems)
        for a in range(n):
            fwd = passed(a)
            for j, chip in enumerate(chips):
                copy(a, 1 + j, (*chip, c), me).wait_recv()
                fwd[j].start()

    def finish(ins, outs, *sems):
        me, sibling, chips, c, copy, own, first, passed = tools(ins, outs, *sems)
        for a in range(n):
            copy(a, 0, sibling, me).wait_recv()
            for j, chip in enumerate(chips):
                copy(a, 4 + j, (*chip, 1 - c), me).wait_recv()
        for a in range(n):
            for cp in first(a) + passed(a):
                cp.wait_send()
            own(a).wait()

    shapes = [jax.ShapeDtypeStruct((a.shape[0], N_DEV * a.shape[1]) if ax == BY_COLUMNS else
                                   a.shape[:ax] + (N_DEV,) + a.shape[ax:], a.dtype) for a, ax in zip(arrs, axes)]
    return Comm(list(arrs), shapes, _sems(n), start, mid, finish)


def _comm_call(comm, *, name):
    ci, co = len(comm.arrays), len(comm.out_shapes)

    def body(*refs):
        ins, outs, sems = refs[:ci], refs[ci:ci + co], refs[ci + co:]
        comm.start(ins, outs, *sems)
        comm.mid(ins, outs, *sems)
        comm.finish(ins, outs, *sems)

    any_spec = pl.BlockSpec(memory_space=pl.ANY)
    return pl.pallas_call(body, name=name, in_specs=[any_spec] * ci, out_specs=[any_spec] * co,
                          out_shape=comm.out_shapes, scratch_shapes=comm.sems)(*comm.arrays)


def _exchange_comm(items):
    n = len(items)

    def tools(ins, outs, send_sems, recv_sems, local_sems):
        x, y, c = _position()
        me = 4 * x + 2 * y + c

        def src(a, pos):
            if items[a][1]:
                rows = ins[a].shape[1] // N_DEV
                return ins[a].at[:, pl.ds(pl.multiple_of(pos * rows, rows), rows)]
            return ins[a].at[pos]

        copies = [pltpu.make_async_copy(src(a, me), outs[a].at[me], local_sems.at[a]) for a in range(n)]
        for k in range(1, N_DEV):
            px = x ^ ((k >> 2) & 1)
            py = y ^ ((k >> 1) & 1)
            pc = c ^ (k & 1)
            copies += [pltpu.make_async_remote_copy(
                src_ref=src(a, 4 * px + 2 * py + pc), dst_ref=outs[a].at[me],
                send_sem=send_sems.at[a, k - 1], recv_sem=recv_sems.at[a, k - 1],
                device_id=(px, py, pc), device_id_type=MESH) for a in range(n)]
        return copies

    def start(ins, outs, *sems):
        for cp in tools(ins, outs, *sems):
            cp.start()

    def mid(ins, outs, *sems):
        pass

    def finish(ins, outs, *sems):
        for cp in tools(ins, outs, *sems):
            cp.wait()

    shapes = []
    for arr, split in items:
        blk = (arr.shape[0], arr.shape[1] // N_DEV) + arr.shape[2:] if split else arr.shape[1:]
        shapes.append(jax.ShapeDtypeStruct((N_DEV,) + blk, arr.dtype))
    return Comm([arr for arr, _ in items], shapes, _sems(n), start, mid, finish)


def _adamw(parts, w, m, v, *, name):
    L, R, C = w.shape
    assert len(parts) == L
    row_bytes = 2 * (L * N_DEV * C * parts[0].dtype.itemsize + 7 * C * 4)
    tr = _pick(R, max(16, ADAMW_VMEM // row_bytes), 16)
    nr = R // tr
    c1 = 1.0 / (1.0 - ADAM_B1 ** ADAM_STEP)
    c2 = 1.0 / (1.0 - ADAM_B2 ** ADAM_STEP)

    def body(*refs):
        p_refs = refs[:L]
        w_ref, m_ref, v_ref, g_ref, d_ref, nm_ref, nv_ref = refs[L:]
        layer = pl.program_id(0)
        for idx, p_ref in enumerate(p_refs):
            @pl.when(layer == idx)
            def _():
                g = p_ref[0].astype(F32)
                for s in range(1, N_DEV):
                    g = g + p_ref[s].astype(F32)
                nm = ADAM_B1 * m_ref[...] + (1.0 - ADAM_B1) * g
                nv = ADAM_B2 * v_ref[...] + (1.0 - ADAM_B2) * (g * g)
                g_ref[...] = g
                nm_ref[...] = nm
                nv_ref[...] = nv
                d_ref[...] = -ADAM_LR * ((nm * c1) / (jnp.sqrt(nv * c2) + ADAM_EPS) + ADAM_WD * w_ref[...])

    def part_spec(idx):
        return pl.BlockSpec((N_DEV, tr, C),
                            lambda l, i: (0, jnp.where(l == idx, i, jnp.where(l < idx, 0, nr - 1)), 0))

    blk = pl.BlockSpec((None, tr, C), lambda l, i: (l, i, 0))
    return pl.pallas_call(
        body, name=name, grid=(L, nr),
        in_specs=[part_spec(idx) for idx in range(L)] + [blk, blk, blk],
        out_specs=[blk] * 4,
        out_shape=[jax.ShapeDtypeStruct((L, R, C), F32)] * 4,
        compiler_params=_cparams(("arbitrary", "arbitrary")),
    )(*parts, w, m, v)


def _pack(flat_parts, row_multiple, dtype):
    lead = flat_parts[0].shape[:-1]
    total = sum(p.shape[-1] for p in flat_parts)
    quantum = PACK_W * row_multiple
    padded = -(-total // quantum) * quantum
    parts = [p.astype(dtype) for p in flat_parts]
    if padded > total:
        parts.append(jnp.zeros(lead + (padded - total,), dtype))
    return jnp.concatenate(parts, axis=-1).reshape(lead + (padded // PACK_W, PACK_W))


def _unpack(buf, shapes):
    lead = buf.shape[:-2]
    flat = buf.reshape(lead + (-1,))
    out, off = [], 0
    for shp in shapes:
        n = math.prod(shp)
        out.append(flat[..., off:off + n].reshape(lead + tuple(shp)))
        off += n
    return out


def _to_full(seg, ax):
    shard = seg.shape[1:]
    full = shard[:ax] + (N_DEV * shard[ax],) + shard[ax + 1:]
    return jnp.moveaxis(seg, 0, ax).reshape(full)


def _to_shards(full, ax):
    shp = full.shape
    split = shp[:ax] + (N_DEV, shp[ax] // N_DEV) + shp[ax + 1:]
    return jnp.moveaxis(full.reshape(split), ax, 0).reshape(N_DEV, -1)


BIG = (("attn_w_in", 1), ("attn_w_out", 1), ("lru_w_in", 1), ("lru_w_a", 2), ("lru_w_x", 2), ("lru_w_out", 1))
SMALL = (("lru_conv_w", 2), ("lru_conv_b", 1), ("lru_b_a", 2), ("lru_b_x", 2), ("lru_lambda", 1))
REPL = ("norm_pre", "norm_post", "attn_sinks")
ORDER = ("norm_pre", "norm_post", "attn_w_in", "attn_w_out", "attn_sinks", "lru_w_in", "lru_conv_w", "lru_conv_b",
         "lru_w_a", "lru_b_a", "lru_w_x", "lru_b_x", "lru_lambda", "lru_w_out")


def kernel(x, norm_pre, norm_post, attn_w_in, attn_w_out, attn_sinks, lru_w_in, lru_conv_w, lru_conv_b, lru_w_a, lru_b_a, lru_w_x, lru_b_x, lru_lambda, lru_w_out, loss_target, m_norm_pre, m_norm_post, m_attn_w_in, m_attn_w_out, m_attn_sinks, m_lru_w_in, m_lru_conv_w, m_lru_conv_b, m_lru_w_a, m_lru_b_a, m_lru_w_x, m_lru_b_x, m_lru_lambda, m_lru_w_out, v_norm_pre, v_norm_post, v_attn_w_in, v_attn_w_out, v_attn_sinks, v_lru_w_in, v_lru_conv_w, v_lru_conv_b, v_lru_w_a, v_lru_b_a, v_lru_w_x, v_lru_b_x, v_lru_lambda, v_lru_w_out):
    W = dict(norm_pre=norm_pre, norm_post=norm_post, attn_w_in=attn_w_in, attn_w_out=attn_w_out, attn_sinks=attn_sinks,
             lru_w_in=lru_w_in, lru_conv_w=lru_conv_w, lru_conv_b=lru_conv_b, lru_w_a=lru_w_a, lru_b_a=lru_b_a,
             lru_w_x=lru_w_x, lru_b_x=lru_b_x, lru_lambda=lru_lambda, lru_w_out=lru_w_out)
    M = dict(norm_pre=m_norm_pre, norm_post=m_norm_post, attn_w_in=m_attn_w_in, attn_w_out=m_attn_w_out,
             attn_sinks=m_attn_sinks, lru_w_in=m_lru_w_in, lru_conv_w=m_lru_conv_w, lru_conv_b=m_lru_conv_b,
             lru_w_a=m_lru_w_a, lru_b_a=m_lru_b_a, lru_w_x=m_lru_w_x, lru_b_x=m_lru_b_x, lru_lambda=m_lru_lambda,
             lru_w_out=m_lru_w_out)
    V = dict(norm_pre=v_norm_pre, norm_post=v_norm_post, attn_w_in=v_attn_w_in, attn_w_out=v_attn_w_out,
             attn_sinks=v_attn_sinks, lru_w_in=v_lru_w_in, lru_conv_w=v_lru_conv_w, lru_conv_b=v_lru_conv_b,
             lru_w_a=v_lru_w_a, lru_b_a=v_lru_b_a, lru_w_x=v_lru_w_x, lru_b_x=v_lru_b_x, lru_lambda=v_lru_lambda,
             lru_w_out=v_lru_w_out)

    h0 = x[0]
    target = loss_target[0]
    T, D = h0.shape
    depth = norm_pre.shape[0]
    n_attn = attn_w_in.shape[0]
    Q = attn_w_out.shape[1] * N_DEV
    KV = Q // GROUP
    LW = lru_w_out.shape[1] * N_DEV
    nh = LW // LRU_BLOCK

    n_lru = lru_w_in.shape[0]
    big_names = [n for n, _ in BIG]
    small_names = [n for n, _ in SMALL]
    small_shapes = [W[n].shape for n in small_names]
    repl_shapes = [W[n].shape for n in REPL]

    flat = lambda a: a.reshape(-1)
    def layer_shards(layer):
        j = layer // 2
        names = ("attn_w_in", "attn_w_out") if layer % 2 == 0 else ("lru_w_in", "lru_w_a", "lru_w_x", "lru_w_out")
        axes = dict(lru_w_a=1, lru_w_x=1, lru_w_in=BY_COLUMNS)
        return [W[n][j].astype(BF16) for n in names], [axes.get(n, 0) for n in names]

    def layer_weights(layer, gathered):
        if layer % 2 == 0:
            g_in, g_out = gathered
            w_in = jnp.moveaxis(g_in, 0, 1).reshape(D, -1)
            return dict(w_in=w_in, w_out=None if g_out is None else g_out.reshape(Q, D))
        g_in, g_wa, g_wx, g_out = gathered
        return dict(w_in=g_in, w_a=g_wa.reshape(nh, LRU_BLOCK, LRU_BLOCK),
                    w_x=g_wx.reshape(nh, LRU_BLOCK, LRU_BLOCK), w_out=g_out.reshape(LW, D))

    arrs0, _ = layer_shards(0)
    (u,), first = _rms_fwd(h0, norm_pre[0:1], name="rms_fwd",
                           comm=_gather_comm([arrs0[0], _pack([flat(W[n]) for n in small_names], SUBLANES, F32)], [0, 0]))
    weights = {0: layer_weights(0, (first[0], None))}
    full = {}
    for (n, ax), seg in zip(SMALL, _unpack(first[-1], small_shapes)):
        full[n] = _to_full(seg, ax)
    cw_f = full["lru_conv_w"]
    cb_f = full["lru_conv_b"][:, None, :]
    ba_f = full["lru_b_a"].reshape(-1, 1, LW)
    bx_f = full["lru_b_x"].reshape(-1, 1, LW)
    lam_f = full["lru_lambda"][:, None, :]

    h = h0
    saved = []
    for layer in range(depth):
        j = layer // 2
        wl = weights[layer]
        nxt = _gather_comm(*layer_shards(layer + 1)) if layer + 1 < depth else None
        if layer % 2 == 0:
            if wl["w_out"] is None:
                proj, got = _matmul(u, wl["w_in"], out_dtype=BF16, comm=_gather_comm([arrs0[1]], [0]), name="attn_in")
                wl["w_out"] = got[0].reshape(Q, D)
            else:
                proj = _matmul(u, wl["w_in"], out_dtype=BF16, name="attn_in")
            (mix, ypre), got = _attn_fwd(proj, attn_sinks[j], comm=nxt, name="attn_fwd")
            y = _matmul(ypre, wl["w_out"], name="attn_out")
        else:
            proj = _matmul(u, wl["w_in"], out_dtype=BF16, name="lru_in")
            (mix, ypre), got = _lru_fwd(proj, wl["w_a"], wl["w_x"], ba_f[j], bx_f[j], cw_f[j], cb_f[j], lam_f[j],
                                        comm=nxt, name="lru_fwd")
            y = _matmul(ypre, wl["w_out"], name="lru_out")
        if nxt is not None:
            weights[layer + 1] = layer_weights(layer + 1, got)
        saved.append((h, u, proj, mix, ypre, y))
        if layer + 1 < depth:
            h, u = _post_fwd(y, norm_post[layer:layer + 1], h, norm_pre[layer + 1:layer + 2], name="post_fwd")
        else:
            dh, loss_part = _post_loss(y, norm_post[layer:layer + 1], h, target, name="post_loss")

    g_pre = [None] * depth
    g_post = [None] * depth
    small_vec = [n for n in small_names] + ["attn_sinks"]
    grads = {n: [None] * W[n].shape[0] for n in small_vec}
    recv = {}

    def carried(keyed):
        return _exchange_comm([item for _, item in keyed]) if keyed else None

    def landed(keyed, got):
        for (key, _), r in zip(keyed, got):
            recv[key] = r

    pending = []
    for layer in reversed(range(depth)):
        j = layer // 2
        h_in, u, proj, mix, ypre, y = saved[layer]
        wl = weights[layer]
        if layer == depth - 1:
            dy, g_post[layer] = _rms_bwd(y, norm_post[layer:layer + 1], dh, None, out_dtype=BF16, name="post_bwd")
        if layer % 2 == 0:
            dyp = _matmul(dy, wl["w_out"], tb=True, out_dtype=BF16, name="attn_out_dx")
            dw_out = _matmul(ypre, dy, ta=True, out_dtype=BF16, name="attn_out_dw")
            pending.append((("attn_w_out", j), (dw_out.reshape(N_DEV, Q // N_DEV, D), False)))
            (dqg, dkv, dsink), got = _attn_bwd(proj, mix, dyp, attn_sinks[j], comm=carried(pending), name="attn_bwd")
            landed(pending, got)
            grads["attn_sinks"][j] = dsink[:, 0]
            dproj = _fill_columns(dqg, dkv, Q // (2 * KV), name="attn_dproj")
            dw = _matmul(u, dproj, ta=True, out_dtype=BF16, name="attn_in_dw")
            pending = [(("attn_w_in", j), (jnp.moveaxis(dw.reshape(D, N_DEV, -1), 1, 0), False))]
            if layer == 0:
                du, got = _matmul(dproj, wl["w_in"], tb=True, comm=carried(pending), name="attn_in_dx")
                landed(pending, got)
                pending = []
            else:
                du = _matmul(dproj, wl["w_in"], tb=True, name="attn_in_dx")
        else:
            dyp = _matmul(dy, wl["w_out"], tb=True, out_dtype=BF16, name="lru_out_dx")
            dw_out = _matmul(ypre, dy, ta=True, out_dtype=BF16, name="lru_out_dw")
            pending.append((("lru_w_out", j), (dw_out.reshape(N_DEV, LW // N_DEV, D), False)))
            (dxb, dgt, dwa, dwx, dba, dbx, dcw, dcb, dlam), got = _lru_bwd(
                proj, mix, dyp, wl["w_a"], wl["w_x"], ba_f[j], bx_f[j], cw_f[j], cb_f[j], lam_f[j],
                comm=carried(pending), name="lru_bwd")
            landed(pending, got)
            grads["lru_b_a"][j], grads["lru_b_x"][j] = dba.reshape(nh, LRU_BLOCK), dbx.reshape(nh, LRU_BLOCK)
            grads["lru_conv_w"][j], grads["lru_conv_b"][j], grads["lru_lambda"][j] = dcw, dcb[0], dlam[0]
            dproj = _fill_columns(dxb, dgt, 1, name="lru_dproj")
            du = _matmul(dproj, wl["w_in"], tb=True, name="lru_in_dx")
            mine = [(("lru_w_a", j), (dwa, True)), (("lru_w_x", j), (dwx, True))]
            dw, got = _matmul(u, dproj, ta=True, out_dtype=BF16, by_owner=True, comm=carried(mine), name="lru_in_dw")
            landed(mine, got)
            pending = [(("lru_w_in", j), (dw, False))]
        if layer > 0:
            dh, dy, g_pre[layer], g_post[layer - 1] = _pre_post_bwd(
                h_in, norm_pre[layer:layer + 1], du, dh, saved[layer - 1][5], norm_post[layer - 1:layer], name="pre_post_bwd")
        else:
            dh, g_pre[layer] = _rms_bwd(h_in, norm_pre[layer:layer + 1], du, dh, out_dtype=F32, name="pre_bwd")

    gfull = {n: jnp.stack(g) for n, g in grads.items()}
    gfull["norm_pre"] = jnp.concatenate(g_pre, axis=0)
    gfull["norm_post"] = jnp.concatenate(g_post, axis=0)

    repl_part = [jnp.broadcast_to(gfull[n].reshape(1, -1), (N_DEV, gfull[n].size)) for n in REPL]
    loss_slot = jnp.broadcast_to(loss_part.reshape(1, 1), (N_DEV, 1))
    send_small = _pack([_to_shards(gfull[n], ax) for n, ax in SMALL] + repl_part + [loss_slot], SUBLANES, F32)
    last = pending + [(("small", 0), (send_small, False))]
    landed(last, _comm_call(carried(last), name="exchange_last"))

    zero1 = jnp.zeros((1,), F32)
    outs = {}
    for n in big_names:
        shp = W[n].shape
        as3 = lambda a: a.reshape((shp[0], -1, shp[-1]))
        parts = [recv[n, j].reshape((N_DEV, -1, shp[-1])) for j in range(shp[0])]
        res = _adamw(parts, as3(W[n]), as3(M[n]), as3(V[n]), name="adamw_" + n)
        for kind, a in zip(("grad", "delta", "new_m", "new_v"), res):
            outs[kind, n] = a.reshape(shp)
    res_small = _adamw([recv["small", 0]],
                       *[_pack([flat(S[n]) for n in small_names] + [flat(S[n]) for n in REPL] + [zero1], SUBLANES, F32)[None]
                         for S in (W, M, V)], name="adamw_small")
    for kind, rs in zip(("grad", "delta", "new_m", "new_v"), res_small):
        for n, a in zip(small_names + list(REPL) + ["loss"], _unpack(rs[0], small_shapes + repl_shapes + [(1,)])):
            outs[kind, n] = a
    loss = outs["grad", "loss"][0]
    result = [loss, dh[None]]
    for kind in ("grad", "delta", "new_m", "new_v"):
        result += [outs[kind, n] for n in ORDER]
    return tuple(result)
```

```python
import math
from typing import Callable, NamedTuple

import jax
import jax.numpy as jnp
from jax import lax
from jax.experimental import pallas as pl
from jax.experimental.pallas import tpu as pltpu

F32 = jnp.float32
BF16 = jnp.bfloat16

N_DEV = 8
HEAD_DIM = 64
GROUP = 8
WINDOW = 128
LRU_BLOCK = 256
CONV_W = 4
C_RG = 8.0
NORM_EPS = 1e-6
MASK_VALUE = -1e30

ADAM_LR = 0.001
ADAM_B1 = 0.9
ADAM_B2 = 0.999
ADAM_EPS = 1e-08
ADAM_WD = 0.01
ADAM_STEP = 10

ROW_BLOCK = 512
LANES = 128
SUBLANES = 8
PACK_W = 1024
VMEM_LIMIT = 56 * 1024 * 1024
MATMUL_VMEM = 36 * 1024 * 1024
MXU_DIM = 256
MATMUL_TM = 1024
MATMUL_TN = 1536
MATMUL_TN_WHOLE = 2048
ADAMW_VMEM = 24 * 1024 * 1024
MESH = pl.DeviceIdType.MESH


def _cparams(sem=None):
    return pltpu.CompilerParams(dimension_semantics=sem, vmem_limit_bytes=VMEM_LIMIT)


class Comm(NamedTuple):
    arrays: list
    out_shapes: list
    sems: list
    start: Callable
    mid: Callable
    finish: Callable


def _call(body, *, name, grid, in_specs, out_specs, out_shape, scratch_shapes, semantics, args, comm=None):
    if comm is None:
        res = pl.pallas_call(body, name=name, grid=grid, in_specs=in_specs, out_specs=out_specs, out_shape=out_shape,
                             scratch_shapes=scratch_shapes, compiler_params=_cparams(semantics))(*args)
        return list(res), []
    n_in, n_out, n_scr = len(in_specs), len(out_specs), len(scratch_shapes)
    ci, co = len(comm.arrays), len(comm.out_shapes)
    steps = math.prod(grid)

    def hosted(*refs):
        ins, cins = refs[:n_in], refs[n_in:n_in + ci]
        o0 = n_in + ci
        outs, couts = refs[o0:o0 + n_out], refs[o0 + n_out:o0 + n_out + co]
        s0 = o0 + n_out + co
        scr, sems = refs[s0:s0 + n_scr], refs[s0 + n_scr:]
        step = 0
        for ax, g in enumerate(grid):
            step = step * g + pl.program_id(ax)

        @pl.when(step == 0)
        def _():
            comm.start(cins, couts, *sems)

        body(*ins, *outs, *scr)

        @pl.when(step == (steps * 3) // 4)
        def _():
            comm.mid(cins, couts, *sems)

        @pl.when(step == steps - 1)
        def _():
            comm.finish(cins, couts, *sems)

    any_spec = pl.BlockSpec(memory_space=pl.ANY)
    res = pl.pallas_call(
        hosted, name=name, grid=grid,
        in_specs=list(in_specs) + [any_spec] * ci, out_specs=list(out_specs) + [any_spec] * co,
        out_shape=list(out_shape) + list(comm.out_shapes), scratch_shapes=list(scratch_shapes) + list(comm.sems),
        compiler_params=_cparams(("arbitrary",) * len(grid)),
    )(*args, *comm.arrays)
    return list(res[:n_out]), list(res[n_out:])


def _pick(n, target, quantum):
    best = None
    for t in range(quantum, min(n, target) + 1, quantum):
        if n % t == 0:
            best = t
    return n if best is None else best


def _sigmoid(x):
    return 1.0 / (1.0 + jnp.exp(-x))


def _dot(a, b):
    return lax.dot_general(a, b, (((1,), (0,)), ((), ())), preferred_element_type=F32)


def _dot_nt(a, b):
    return lax.dot_general(a, b, (((1,), (1,)), ((), ())), preferred_element_type=F32)


def _dot_tn(a, b):
    return lax.dot_general(a, b, (((0,), (0,)), ((), ())), preferred_element_type=F32)


def _matmul(a, b, *, ta=False, tb=False, out_dtype=F32, by_owner=False, comm=None, name):
    if ta:
        K, M = a.shape
    else:
        M, K = a.shape
    if tb:
        N, K2 = b.shape
    else:
        K2, N = b.shape
    assert K == K2, (a.shape, b.shape, ta, tb)
    osz = jnp.dtype(out_dtype).itemsize

    def plan(tn):
        def vmem_bytes(tm, tk):
            acc = 0 if tk == K else tm * tn * 4
            return 2 * (tm * tk * a.dtype.itemsize + tk * tn * b.dtype.itemsize + tm * tn * osz) + acc

        def deepest(tm):
            fits = [t for t in range(MXU_DIM, K + 1, MXU_DIM) if K % t == 0 and vmem_bytes(tm, t) <= MATMUL_VMEM]
            return max(fits or [_pick(K, 2 * MXU_DIM, MXU_DIM)])

        tm = _pick(M, MATMUL_TM, MXU_DIM)
        if deepest(tm) < K and tm % (2 * MXU_DIM) == 0 and deepest(tm // 2) == K:
            tm //= 2
        return tm, deepest(tm)

    tn = N // N_DEV if by_owner else _pick(N, MATMUL_TN, MXU_DIM)
    tm, tk = plan(tn)
    if not by_owner and not ta and N <= MATMUL_TN_WHOLE and plan(N)[1] == K:
        tn = N
        tm, tk = plan(tn)
    assert tn % LANES == 0
    nk = K // tk
    dot = {(False, False): _dot, (False, True): _dot_nt, (True, False): _dot_tn}[(ta, tb)]

    if nk == 1:
        def body(a_ref, b_ref, o_ref):
            o_ref[...] = dot(a_ref[...].astype(BF16), b_ref[...].astype(BF16)).astype(o_ref.dtype)
        scratch = []
    else:
        def body(a_ref, b_ref, o_ref, acc_ref):
            k = pl.program_id(2)

            @pl.when(k == 0)
            def _():
                acc_ref[...] = jnp.zeros_like(acc_ref)

            acc_ref[...] += dot(a_ref[...].astype(BF16), b_ref[...].astype(BF16))

            @pl.when(k == nk - 1)
            def _():
                o_ref[...] = acc_ref[...].astype(o_ref.dtype)
        scratch = [pltpu.VMEM((tm, tn), F32)]

    a_spec = pl.BlockSpec((tk, tm), lambda j, i, k: (k, i)) if ta else pl.BlockSpec((tm, tk), lambda j, i, k: (i, k))
    b_spec = pl.BlockSpec((tn, tk), lambda j, i, k: (j, k)) if tb else pl.BlockSpec((tk, tn), lambda j, i, k: (k, j))
    if by_owner:
        o_spec = pl.BlockSpec((None, tm, tn), lambda j, i, k: (j, i, 0))
        o_shape = jax.ShapeDtypeStruct((N_DEV, M, tn), out_dtype)
    else:
        o_spec = pl.BlockSpec((tm, tn), lambda j, i, k: (i, j))
        o_shape = jax.ShapeDtypeStruct((M, N), out_dtype)
    res, extra = _call(body, name=name, grid=(N // tn, M // tm, nk), in_specs=[a_spec, b_spec], out_specs=[o_spec],
                       out_shape=[o_shape], scratch_shapes=scratch, semantics=("parallel", "parallel", "arbitrary"),
                       args=(a, b), comm=comm)
    return res[0] if comm is None else (res[0], extra)


def _rms_fwd(h, g, *, comm=None, name):
    T, D = h.shape
    tm = _pick(T, ROW_BLOCK, SUBLANES)

    def body(h_ref, g_ref, u_ref):
        x = h_ref[...]
        r = lax.rsqrt(jnp.mean(x * x, axis=-1, keepdims=True) + NORM_EPS)
        u_ref[...] = ((x * r) * g_ref[...]).astype(u_ref.dtype)

    return _call(
        body, name=name, grid=(T // tm,),
        in_specs=[pl.BlockSpec((tm, D), lambda i: (i, 0)), pl.BlockSpec((1, D), lambda i: (0, 0))],
        out_specs=[pl.BlockSpec((tm, D), lambda i: (i, 0))],
        out_shape=[jax.ShapeDtypeStruct((T, D), BF16)],
        scratch_shapes=[], semantics=("parallel",), args=(h, g), comm=comm)


def _post_fwd(y, g, h, g_next, *, name):
    T, D = y.shape
    tm = _pick(T, ROW_BLOCK, SUBLANES)

    def body(y_ref, g_ref, h_ref, gn_ref, o_ref, u_ref):
        x = y_ref[...]
        r = lax.rsqrt(jnp.mean(x * x, axis=-1, keepdims=True) + NORM_EPS)
        ho = h_ref[...] + (x * r) * g_ref[...]
        o_ref[...] = ho
        rn = lax.rsqrt(jnp.mean(ho * ho, axis=-1, keepdims=True) + NORM_EPS)
        u_ref[...] = ((ho * rn) * gn_ref[...]).astype(u_ref.dtype)

    row = pl.BlockSpec((tm, D), lambda i: (i, 0))
    vec = pl.BlockSpec((1, D), lambda i: (0, 0))
    return pl.pallas_call(
        body, name=name, grid=(T // tm,),
        in_specs=[row, vec, row, vec],
        out_specs=[row, row],
        out_shape=[jax.ShapeDtypeStruct((T, D), F32), jax.ShapeDtypeStruct((T, D), BF16)],
        compiler_params=_cparams(("parallel",)),
    )(y, g, h, g_next)


def _rms_bwd(x, g, dz, res, *, out_dtype, name):
    T, D = x.shape
    tm = _pick(T, ROW_BLOCK, SUBLANES)
    has_res = res is not None

    def body(*refs):
        if has_res:
            x_ref, g_ref, dz_ref, res_ref, dx_ref, dg_ref = refs
        else:
            x_ref, g_ref, dz_ref, dx_ref, dg_ref = refs
        i = pl.program_id(0)

        @pl.when(i == 0)
        def _():
            dg_ref[...] = jnp.zeros_like(dg_ref)

        dx, dg = _norm_dx(x_ref[...], g_ref[...], dz_ref[...])
        dg_ref[...] += dg
        if has_res:
            dx = dx + res_ref[...]
        dx_ref[...] = dx.astype(dx_ref.dtype)

    row = pl.BlockSpec((tm, D), lambda i: (i, 0))
    vec = pl.BlockSpec((1, D), lambda i: (0, 0))
    ins = [x, g, dz] + ([res] if has_res else [])
    return pl.pallas_call(
        body, name=name, grid=(T // tm,),
        in_specs=[row, vec, row] + ([row] if has_res else []),
        out_specs=[row, vec],
        out_shape=[jax.ShapeDtypeStruct((T, D), out_dtype), jax.ShapeDtypeStruct((1, D), F32)],
        compiler_params=_cparams(("arbitrary",)),
    )(*ins)


def _norm_dx(x, g, dz):
    r = lax.rsqrt(jnp.mean(x * x, axis=-1, keepdims=True) + NORM_EPS)
    xhat = x * r
    dxh = dz * g
    dx = r * (dxh - xhat * jnp.mean(dxh * xhat, axis=-1, keepdims=True))
    return dx, jnp.sum(dz * xhat, axis=0, keepdims=True)


def _pre_post_bwd(h_in, g_pre, du, dh, y_below, g_post_below, *, name):
    T, D = h_in.shape
    tm = _pick(T, ROW_BLOCK, SUBLANES)

    def body(h_ref, gp_ref, du_ref, dh_ref, y_ref, gq_ref, dhn_ref, dy_ref, dgp_ref, dgq_ref):
        i = pl.program_id(0)

        @pl.when(i == 0)
        def _():
            dgp_ref[...] = jnp.zeros_like(dgp_ref)
            dgq_ref[...] = jnp.zeros_like(dgq_ref)

        dx, dgp = _norm_dx(h_ref[...], gp_ref[...], du_ref[...])
        dhn = dh_ref[...] + dx
        dhn_ref[...] = dhn
        dgp_ref[...] += dgp
        dy, dgq = _norm_dx(y_ref[...], gq_ref[...], dhn)
        dy_ref[...] = dy.astype(dy_ref.dtype)
        dgq_ref[...] += dgq

    row = pl.BlockSpec((tm, D), lambda i: (i, 0))
    vec = pl.BlockSpec((1, D), lambda i: (0, 0))
    return pl.pallas_call(
        body, name=name, grid=(T // tm,),
        in_specs=[row, vec, row, row, row, vec],
        out_specs=[row, row, vec, vec],
        out_shape=[jax.ShapeDtypeStruct((T, D), F32), jax.ShapeDtypeStruct((T, D), BF16),
                   jax.ShapeDtypeStruct((1, D), F32), jax.ShapeDtypeStruct((1, D), F32)],
        compiler_params=_cparams(("arbitrary",)),
    )(h_in, g_pre, du, dh, y_below, g_post_below)


def _post_loss(y, g, h, target, *, name):
    T, D = h.shape
    tm = _pick(T, ROW_BLOCK, SUBLANES)

    def body(y_ref, g_ref, h_ref, t_ref, dh_ref, l_ref):
        i = pl.program_id(0)

        @pl.when(i == 0)
        def _():
            l_ref[...] = jnp.zeros_like(l_ref)

        x = y_ref[...]
        r = lax.rsqrt(jnp.mean(x * x, axis=-1, keepdims=True) + NORM_EPS)
        e = (h_ref[...] + (x * r) * g_ref[...]) - t_ref[...]
        dh_ref[...] = e * (1.0 / D)
        row = jnp.sum(e * e, axis=-1, keepdims=True) * (0.5 / D)
        l_ref[...] += jnp.sum(row, axis=0, keepdims=True)

    row = pl.BlockSpec((tm, D), lambda i: (i, 0))
    return pl.pallas_call(
        body, name=name, grid=(T // tm,),
        in_specs=[row, pl.BlockSpec((1, D), lambda i: (0, 0)), row, row],
        out_specs=[row, pl.BlockSpec((1, 1), lambda i: (0, 0))],
        out_shape=[jax.ShapeDtypeStruct((T, D), F32), jax.ShapeDtypeStruct((1, 1), F32)],
        compiler_params=_cparams(("arbitrary",)),
    )(y, g, h, target)


def _attn_dims(P):
    Q = P * 4 // 9
    KV = Q // GROUP
    assert 2 * Q + 2 * KV == P and KV % LANES == 0 and Q % (2 * KV) == 0
    return Q, KV


def _attn_specs(Q, KV, nb):
    blk = WINDOW
    row = lambda i: jnp.minimum(i, nb - 1)
    q_spec = pl.BlockSpec((blk, Q), lambda i: (row(i), 0))
    kvc_spec = pl.BlockSpec((blk, 2 * KV), lambda i: (row(i), Q // (2 * KV)))
    kvp_spec = pl.BlockSpec((blk, 2 * KV), lambda i: (jnp.maximum(row(i) - 1, 0), Q // (2 * KV)))
    g_specs = [pl.BlockSpec((blk, 2 * KV), lambda i, b=b: (row(i), Q // (2 * KV) + 1 + b)) for b in range(Q // (2 * KV))]
    return q_spec, kvc_spec, kvp_spec, g_specs


def _stack_gate(g_refs, cols):
    width = g_refs[0].shape[1]
    parts = [g_refs[cs.start // width][:, cs.start % width:cs.start % width + LANES] for cs in cols]
    return jnp.concatenate(parts, axis=0).astype(F32)


PAIRS = GROUP // 2
STACK = PAIRS * WINDOW


def _band_bias():
    c = lax.broadcasted_iota(jnp.int32, (2, 2 * WINDOW, STACK), 1)
    r = lax.broadcasted_iota(jnp.int32, (2, 2 * WINDOW, STACK), 2) & (WINDOW - 1)
    first = lax.broadcasted_iota(jnp.int32, (2, 2 * WINDOW, STACK), 0) == 0
    allowed = (c > r) & (c <= r + WINDOW) & ((c >= WINDOW) | ~first)
    return jnp.where(allowed, 0.0, MASK_VALUE).astype(F32)


BIAS_SPEC = pl.BlockSpec((None, 2 * WINDOW, STACK), lambda i: (jnp.minimum(i, 1), 0, 0))


def _group_cols(kvh):
    c0 = kvh * GROUP * HEAD_DIM
    return [slice(c0 + j * LANES, c0 + (j + 1) * LANES) for j in range(PAIRS)]


def _stack(ref, cols, scale=None):
    x = jnp.concatenate([ref[:, cs] for cs in cols], axis=0).astype(F32)
    return x if scale is None else x * scale


def _group_sinks(sink_ref, kvh, half):
    return jnp.concatenate([jnp.full((1, WINDOW), sink_ref[kvh * GROUP + 2 * j + half], F32) for j in range(PAIRS)], axis=1)


def _pair_halves(x128, xt128, e):
    lo = lax.broadcasted_iota(jnp.int32, x128.shape, 1) < HEAD_DIM
    lo_t = lax.broadcasted_iota(jnp.int32, xt128.shape, 0) < HEAD_DIM
    if e == 0:
        x_lo, xt_lo = jnp.where(lo, x128, 0.0), jnp.where(lo_t, xt128, 0.0)
        x_hi, xt_hi = pltpu.roll(x_lo, HEAD_DIM, 1), pltpu.roll(xt_lo, HEAD_DIM, 0)
    else:
        x_hi, xt_hi = jnp.where(lo, 0.0, x128), jnp.where(lo_t, 0.0, xt128)
        x_lo, xt_lo = pltpu.roll(x_hi, HEAD_DIM, 1), pltpu.roll(xt_hi, HEAD_DIM, 0)
    return (x_lo.astype(BF16), x_hi.astype(BF16)), (xt_lo.astype(BF16), xt_hi.astype(BF16))


def _softmax_sink(st, bias, sink):
    st = st + bias
    m = jnp.maximum(jnp.max(st, axis=0, keepdims=True), sink)
    p = jnp.exp(st - m)
    es = jnp.exp(sink - m)
    inv = 1.0 / (jnp.sum(p, axis=0, keepdims=True) + es)
    return p * inv, es * inv


def _attn_fwd(proj, sinks, *, comm=None, name):
    T, P = proj.shape
    Q, KV = _attn_dims(P)
    nb = T // WINDOW
    npairs = KV // LANES
    scale = 1.0 / math.sqrt(HEAD_DIM)

    ng = Q // (2 * KV)

    def body(sink_ref, bias_ref, q_ref, kvc_ref, kvp_ref, *rest):
        g_refs, (out_ref, yp_ref) = rest[:ng], rest[ng:]
        allowed = bias_ref[...]
        for p in range(npairs):
            ks = slice(p * LANES, (p + 1) * LANES)
            vs = slice(KV + p * LANES, KV + (p + 1) * LANES)
            k128 = jnp.concatenate([kvp_ref[:, ks], kvc_ref[:, ks]], axis=0).astype(F32)
            v128 = jnp.concatenate([kvp_ref[:, vs], kvc_ref[:, vs]], axis=0).astype(F32)
            kt128, vt128 = k128.T, v128.T
            for e in range(2):
                kvh = 2 * p + e
                khalf, _ = _pair_halves(k128, kt128, e)
                _, vthalf = _pair_halves(v128, vt128, e)
                cols = _group_cols(kvh)
                q4 = _stack(q_ref, cols, scale).astype(BF16)
                ot = None
                for half in range(2):
                    st = _dot_nt(khalf[half], q4)
                    pn, _ = _softmax_sink(st, allowed, _group_sinks(sink_ref, kvh, half))
                    o = _dot(vthalf[half], pn.astype(BF16))
                    ot = o if ot is None else ot + o
                o4 = ot.T
                g4 = _stack_gate(g_refs, cols)
                y4 = (o4 * (g4 * _sigmoid(g4))).astype(BF16)
                for j, cs in enumerate(cols):
                    out_ref[:, cs] = o4[j * WINDOW:(j + 1) * WINDOW]
                    yp_ref[:, cs] = y4[j * WINDOW:(j + 1) * WINDOW]

    q_spec, kvc_spec, kvp_spec, g_specs = _attn_specs(Q, KV, nb)
    row = pl.BlockSpec((WINDOW, Q), lambda i: (i, 0))
    return _call(
        body, name=name, grid=(nb,),
        in_specs=[pl.BlockSpec(memory_space=pltpu.SMEM), BIAS_SPEC, q_spec, kvc_spec, kvp_spec] + g_specs,
        out_specs=[row, row],
        out_shape=[jax.ShapeDtypeStruct((T, Q), F32), jax.ShapeDtypeStruct((T, Q), BF16)],
        scratch_shapes=[], semantics=("parallel",), args=(sinks, _band_bias(), proj, proj, proj) + (proj,) * ng, comm=comm)


def _attn_bwd(proj, out, dyp, sinks, *, comm=None, name):
    T, P = proj.shape
    Q, KV = _attn_dims(P)
    nb = T // WINDOW
    npairs = KV // LANES
    H = Q // HEAD_DIM
    ng = Q // (2 * KV)
    scale = 1.0 / math.sqrt(HEAD_DIM)

    def body(sink_ref, bias_ref, q_ref, kvc_ref, kvp_ref, *rest):
        g_refs, (out_ref, dyp_ref, dqg_ref, dkv_ref, dsink_ref, carry_ref) = rest[:ng], rest[ng:]
        i = pl.program_id(0)

        @pl.when(i == 0)
        def _():
            carry_ref[...] = jnp.zeros_like(carry_ref)
            dsink_ref[...] = jnp.zeros_like(dsink_ref)

        @pl.when(i == nb)
        def _():
            dkv_ref[...] = carry_ref[...].astype(dkv_ref.dtype)

        @pl.when(i < nb)
        def _():
            allowed = bias_ref[...]
            lo = lax.broadcasted_iota(jnp.int32, (2 * WINDOW, LANES), 1) < HEAD_DIM
            sel_lane = lax.broadcasted_iota(jnp.int32, (SUBLANES, LANES), 1) < HEAD_DIM
            sels = (jnp.where(sel_lane, 1.0, 0.0).astype(BF16), jnp.where(sel_lane, 0.0, 1.0).astype(BF16))
            for p in range(npairs):
                ks = slice(p * LANES, (p + 1) * LANES)
                vs = slice(KV + p * LANES, KV + (p + 1) * LANES)
                k128 = jnp.concatenate([kvp_ref[:, ks], kvc_ref[:, ks]], axis=0).astype(F32)
                v128 = jnp.concatenate([kvp_ref[:, vs], kvc_ref[:, vs]], axis=0).astype(F32)
                kt128, vt128 = k128.T, v128.T
                dk_e, dv_e = [], []
                for e in range(2):
                    kvh = 2 * p + e
                    khalf, kthalf = _pair_halves(k128, kt128, e)
                    vhalf, _ = _pair_halves(v128, vt128, e)
                    cols = _group_cols(kvh)
                    q4 = _stack(q_ref, cols, scale).astype(BF16)
                    g4 = _stack_gate(g_refs, cols)
                    o4 = _stack(out_ref, cols)
                    dy4 = _stack(dyp_ref, cols)
                    sg = _sigmoid(g4)
                    do4 = dy4 * (g4 * sg)
                    dg4 = (dy4 * o4 * (sg * (1.0 + g4 * (1.0 - sg)))).astype(dqg_ref.dtype)
                    dod = do4 * o4
                    dod_hi = dod.astype(BF16)
                    dod_lo = (dod - dod_hi.astype(F32)).astype(BF16)
                    do4b = do4.astype(BF16)
                    dqt = None
                    dk_h, dv_h = [], []
                    for half in range(2):
                        delta = jnp.max(_dot_nt(sels[half], dod_hi) + _dot_nt(sels[half], dod_lo), axis=0, keepdims=True)
                        st = _dot_nt(khalf[half], q4)
                        pn, psink = _softmax_sink(st, allowed, _group_sinks(sink_ref, kvh, half))
                        dp = _dot_nt(vhalf[half], do4b)
                        ds = (pn * (dp - delta)).astype(BF16)
                        dq = _dot(kthalf[half], ds)
                        dqt = dq if dqt is None else dqt + dq
                        dk_h.append(_dot(ds, q4))
                        dv_h.append(_dot(pn.astype(BF16), do4b))
                        pd = psink * delta
                        for j in range(PAIRS):
                            n = kvh * GROUP + 2 * j + half
                            dsn = -jnp.sum(pd[:, j * WINDOW:(j + 1) * WINDOW], axis=1, keepdims=True)
                            dsink_ref[n:n + 1, :] += jnp.broadcast_to(dsn, (1, LANES))
                    dq4 = (dqt.T * scale).astype(dqg_ref.dtype)
                    for j, cs in enumerate(cols):
                        dqg_ref[:, cs] = dq4[j * WINDOW:(j + 1) * WINDOW]
                        dqg_ref[:, slice(Q + 2 * KV + cs.start, Q + 2 * KV + cs.stop)] = dg4[j * WINDOW:(j + 1) * WINDOW]
                    acc_k = jnp.where(lo, dk_h[0], dk_h[1])
                    acc_v = jnp.where(lo, dv_h[0], dv_h[1])
                    dk_e.append(acc_k + pltpu.roll(acc_k, HEAD_DIM, 1))
                    dv_e.append(acc_v + pltpu.roll(acc_v, HEAD_DIM, 1))
                for sl, de in ((ks, dk_e), (vs, dv_e)):
                    d128 = jnp.where(lo, de[0], de[1])
                    dkv_ref[:, sl] = (carry_ref[:, sl] + d128[:WINDOW]).astype(dkv_ref.dtype)
                    carry_ref[:, sl] = d128[WINDOW:]

    q_spec, kvc_spec, kvp_spec, g_specs = _attn_specs(Q, KV, nb)
    last = lambda i: (jnp.minimum(i, nb - 1), 0)
    row = pl.BlockSpec((WINDOW, Q), last)
    return _call(
        body, name=name, grid=(nb + 1,),
        in_specs=[pl.BlockSpec(memory_space=pltpu.SMEM), BIAS_SPEC, q_spec, kvc_spec, kvp_spec] + g_specs + [row, row],
        out_specs=[pl.BlockSpec((WINDOW, P), last),
                   pl.BlockSpec((WINDOW, 2 * KV), lambda i: (jnp.maximum(i - 1, 0), 0)),
                   pl.BlockSpec((H, LANES), lambda i: (0, 0))],
        out_shape=[jax.ShapeDtypeStruct((T, P), BF16), jax.ShapeDtypeStruct((T, 2 * KV), BF16),
                   jax.ShapeDtypeStruct((H, LANES), F32)],
        scratch_shapes=[pltpu.VMEM((WINDOW, 2 * KV), F32)],
        semantics=("arbitrary",), args=(sinks, _band_bias(), proj, proj, proj) + (proj,) * ng + (out, dyp), comm=comm)


def _fill_columns(full, part, col_block, *, name):
    T, w = part.shape
    tm = _pick(T, MATMUL_TM, SUBLANES * 2)

    def body(full_ref, part_ref, o_ref):
        del full_ref
        o_ref[...] = part_ref[...]

    return pl.pallas_call(
        body, name=name, grid=(T // tm,),
        in_specs=[pl.BlockSpec(memory_space=pl.ANY), pl.BlockSpec((tm, w), lambda i: (i, 0))],
        out_specs=pl.BlockSpec((tm, w), lambda i: (i, col_block)),
        out_shape=jax.ShapeDtypeStruct(full.shape, full.dtype),
        input_output_aliases={0: 0},
        compiler_params=_cparams(("parallel",)),
    )(full, part)


LRU_CHUNK = 2048


def _shift_down(x, halo8, s):
    if s == 0:
        return x
    row8 = lax.broadcasted_iota(jnp.int32, (SUBLANES, 1), 0)
    r = pltpu.roll(x, s, 0)
    top = jnp.where(row8 < s, pltpu.roll(halo8, s, 0), r[:SUBLANES])
    return jnp.concatenate([top, r[SUBLANES:]], axis=0)


def _shift_up(x, halo8, s):
    if s == 0:
        return x
    n = x.shape[0]
    row8 = lax.broadcasted_iota(jnp.int32, (SUBLANES, 1), 0)
    r = pltpu.roll(x, n - s, 0)
    bot = jnp.where(row8 >= SUBLANES - s, pltpu.roll(halo8, SUBLANES - s, 0), r[n - SUBLANES:])
    return jnp.concatenate([r[:n - SUBLANES], bot], axis=0)


def _scan_fwd(a, b, c0):
    n = a.shape[0]
    row = lax.broadcasted_iota(jnp.int32, (n, 1), 0) & (SUBLANES - 1)
    s = 1
    while s < SUBLANES:
        keep = row >= s
        ar = jnp.where(keep, pltpu.roll(a, s, 0), 1.0)
        br = jnp.where(keep, pltpu.roll(b, s, 0), 0.0)
        b = a * br + b
        a = a * ar
        s *= 2
    out, c = [], c0
    for i in range(n // SUBLANES):
        rows = slice(i * SUBLANES, (i + 1) * SUBLANES)
        h = a[rows] * c + b[rows]
        out.append(h)
        c = h[SUBLANES - 1:]
    return jnp.concatenate(out, axis=0)


def _scan_rev(al, b, c0):
    n = al.shape[0]
    row = lax.broadcasted_iota(jnp.int32, (n, 1), 0) & (SUBLANES - 1)
    s = 1
    while s < SUBLANES:
        keep = row < SUBLANES - s
        ar = jnp.where(keep, pltpu.roll(al, n - s, 0), 1.0)
        br = jnp.where(keep, pltpu.roll(b, n - s, 0), 0.0)
        b = b + al * br
        al = al * ar
        s *= 2
    out, c = [], c0
    for i in reversed(range(n // SUBLANES)):
        rows = slice(i * SUBLANES, (i + 1) * SUBLANES)
        l = b[rows] + al[rows] * c
        out.append(l)
        c = l[:1]
    return jnp.concatenate(out[::-1], axis=0)


def _log1p_pos(z):
    return jnp.where(z < 0.01, z * (1.0 - z * (0.5 - z * (1.0 / 3.0))), jnp.log(1.0 + z))


def _one_minus_sq(a, log_a):
    x = 2.0 * log_a
    series = -x * (1.0 + x * (0.5 + x * (1.0 / 6.0)))
    return jnp.where(x > -0.02, series, 1.0 - a * a)


def _softplus_neg(lam):
    return jnp.maximum(-lam, 0.0) + _log1p_pos(jnp.exp(-jnp.abs(lam)))


def _lru_gates(xb, halo, wa, wx, ba, bx, cw_ref, cb, lam, with_inverse=False):
    xs = [_shift_down(xb, halo, s) for s in range(CONV_W)]
    xc = cb + xs[3] * cw_ref[0:1, :] + xs[2] * cw_ref[1:2, :] + xs[1] * cw_ref[2:3, :] + xs[0] * cw_ref[3:4, :]
    xcb = xc.astype(BF16)
    r = _sigmoid(_dot(xcb, wa) + ba)
    ig = _sigmoid(_dot(xcb, wx) + bx)
    sp = _softplus_neg(lam)
    log_a = (-C_RG * r) * sp
    a = jnp.exp(log_a)
    z = _one_minus_sq(a, log_a)
    if not with_inverse:
        return xs, xc, xcb, r, ig, sp, a, jnp.sqrt(z), None
    rmult = lax.rsqrt(z)
    return xs, xc, xcb, r, ig, sp, a, z * rmult, rmult


def _tile_rows(dtype):
    return SUBLANES * 4 // jnp.dtype(dtype).itemsize


def _last_rows(ref):
    return ref[...].astype(F32)[ref.shape[0] - SUBLANES:]


def _lru_specs(nh, nt, tc, rev):
    tix = (lambda t: nt - 1 - t) if rev else (lambda t: t)
    chunk = lambda off: pl.BlockSpec((tc, LRU_BLOCK), lambda h, t: (tix(t), h + off))
    prev8 = lambda off, rows: pl.BlockSpec((rows, LRU_BLOCK),
                                           lambda h, t: (jnp.maximum(tix(t) * (tc // rows) - 1, 0), h + off))
    wblk = pl.BlockSpec((None, LRU_BLOCK, LRU_BLOCK), lambda h, t: (h, 0, 0))
    vec = pl.BlockSpec((1, LRU_BLOCK), lambda h, t: (0, h))
    cwb = pl.BlockSpec((CONV_W, LRU_BLOCK), lambda h, t: (0, h))
    return tix, chunk, prev8, wblk, vec, cwb


def _lru_fwd(proj, wa, wx, ba, bx, cw, cb, lam, *, comm=None, name):
    T, W2 = proj.shape
    W = W2 // 2
    nh = W // LRU_BLOCK
    tc = _pick(T, LRU_CHUNK, SUBLANES)
    nt = T // tc

    def body(xb_ref, xh_ref, gt_ref, wa_ref, wx_ref, ba_ref, bx_ref, cw_ref, cb_ref, lam_ref, hs_ref, yp_ref, carry_ref):
        t = pl.program_id(1)

        @pl.when(t == 0)
        def _():
            carry_ref[...] = jnp.zeros_like(carry_ref)

        halo = jnp.where(t > 0, _last_rows(xh_ref), 0.0)
        _, xc, _, _, ig, _, a, mult, _ = _lru_gates(xb_ref[...].astype(F32), halo, wa_ref[...], wx_ref[...], ba_ref[...],
                                                  bx_ref[...], cw_ref, cb_ref[...], lam_ref[...])
        hs = _scan_fwd(a, mult * (ig * xc), carry_ref[SUBLANES - 1:SUBLANES, :])
        hs_ref[...] = hs
        carry_ref[...] = hs[tc - SUBLANES:]
        g = gt_ref[...].astype(F32)
        yp_ref[...] = (hs * (g * _sigmoid(g))).astype(BF16)

    _, chunk, prev8, wblk, vec, cwb = _lru_specs(nh, nt, tc, False)
    return _call(
        body, name=name, grid=(nh, nt),
        in_specs=[chunk(0), prev8(0, _tile_rows(proj.dtype)), chunk(nh), wblk, wblk, vec, vec, cwb, vec, vec],
        out_specs=[chunk(0), chunk(0)],
        out_shape=[jax.ShapeDtypeStruct((T, W), F32), jax.ShapeDtypeStruct((T, W), BF16)],
        scratch_shapes=[pltpu.VMEM((SUBLANES, LRU_BLOCK), F32)],
        semantics=("parallel", "arbitrary"), args=(proj, proj, proj, wa, wx, ba, bx, cw, cb, lam), comm=comm)


def _lru_bwd(proj, hs, dyp, wa, wx, ba, bx, cw, cb, lam, *, comm=None, name):
    T, W2 = proj.shape
    W = W2 // 2
    nh = W // LRU_BLOCK
    tc = _pick(T, LRU_CHUNK, SUBLANES)
    nt = T // tc

    def body(xb_ref, xh_ref, gt_ref, hs_ref, hh_ref, dyp_ref, wa_ref, wx_ref, ba_ref, bx_ref, cw_ref, cb_ref, lam_ref,
             dx_ref, dg_ref, dwa_ref, dwx_ref, dba_ref, dbx_ref, dcw_ref, dcb_ref, dlam_ref,
             ca_ref, cl_ref, cx_ref):
        t = pl.program_id(1)
        first = t == nt - 1

        @pl.when(t == 0)
        def _():
            for ref in (ca_ref, cl_ref, cx_ref, dwa_ref, dwx_ref, dba_ref, dbx_ref, dcw_ref, dcb_ref, dlam_ref):
                ref[...] = jnp.zeros_like(ref)

        xb = xb_ref[...].astype(F32)
        halo = jnp.where(first, 0.0, _last_rows(xh_ref))
        wa = wa_ref[...]
        wx = wx_ref[...]
        lam = lam_ref[...]
        xs, xc, xcb, r, ig, sp, a, mult, rmult = _lru_gates(xb, halo, wa, wx, ba_ref[...], bx_ref[...], cw_ref, cb_ref[...], lam,
                                                            with_inverse=True)
        hsv = hs_ref[...]
        g = gt_ref[...].astype(F32)
        dy = dyp_ref[...].astype(F32)
        sg = _sigmoid(g)
        dg_ref[...] = (dy * hsv * (sg * (1.0 + g * (1.0 - sg)))).astype(dg_ref.dtype)
        dhs = dy * (g * sg)

        al = _shift_up(a, ca_ref[...], 1)
        lmb = _scan_rev(al, dhs, cl_ref[0:1, :])
        hprev = _shift_down(hsv, jnp.where(first, 0.0, _last_rows(hh_ref)), 1)
        da = lmb * hprev
        ixc = ig * xc
        dmult = lmb * ixc
        dlog_a = a * (da - dmult * a * rmult)
        dr = dlog_a * (-C_RG * sp)
        dlam_ref[...] += jnp.sum(dlog_a * r, axis=0, keepdims=True) * (C_RG * _sigmoid(-lam))
        dpa = dr * (r * (1.0 - r))
        dpx = (lmb * mult * xc) * (ig * (1.0 - ig))
        dpab = dpa.astype(BF16)
        dpxb = dpx.astype(BF16)
        dwa_ref[...] += _dot_tn(xcb, dpab)
        dwx_ref[...] += _dot_tn(xcb, dpxb)
        dba_ref[...] += jnp.sum(dpa, axis=0, keepdims=True)
        dbx_ref[...] += jnp.sum(dpx, axis=0, keepdims=True)
        dxc = lmb * mult * ig + _dot_nt(dpab, wa) + _dot_nt(dpxb, wx)
        dcb_ref[...] += jnp.sum(dxc, axis=0, keepdims=True)
        for s in range(CONV_W):
            dcw_ref[CONV_W - 1 - s:CONV_W - s, :] += jnp.sum(dxc * xs[s], axis=0, keepdims=True)
        cxv = cx_ref[...]
        dxb = dxc * cw_ref[3:4, :]
        for s in range(1, CONV_W):
            dxb = dxb + _shift_up(dxc, cxv, s) * cw_ref[3 - s:4 - s, :]
        dx_ref[...] = dxb.astype(dx_ref.dtype)
        ca_ref[...] = a[:SUBLANES]
        cl_ref[...] = lmb[:SUBLANES]
        cx_ref[...] = dxc[:SUBLANES]

    tix, chunk, prev8, wblk, vec, cwb = _lru_specs(nh, nt, tc, True)
    hchunk = pl.BlockSpec((tc, LRU_BLOCK), lambda h, t: (tix(t), h))
    carry = pltpu.VMEM((SUBLANES, LRU_BLOCK), F32)
    return _call(
        body, name=name, grid=(nh, nt),
        in_specs=[chunk(0), prev8(0, _tile_rows(proj.dtype)), chunk(nh), hchunk, prev8(0, _tile_rows(hs.dtype)), hchunk,
                  wblk, wblk, vec, vec, cwb, vec, vec],
        out_specs=[hchunk, hchunk, wblk, wblk, vec, vec, cwb, vec, vec],
        out_shape=[jax.ShapeDtypeStruct((T, W2), BF16), jax.ShapeDtypeStruct((T, W), BF16),
                   jax.ShapeDtypeStruct((nh, LRU_BLOCK, LRU_BLOCK), F32), jax.ShapeDtypeStruct((nh, LRU_BLOCK, LRU_BLOCK), F32),
                   jax.ShapeDtypeStruct((1, W), F32), jax.ShapeDtypeStruct((1, W), F32),
                   jax.ShapeDtypeStruct((CONV_W, W), F32), jax.ShapeDtypeStruct((1, W), F32), jax.ShapeDtypeStruct((1, W), F32)],
        scratch_shapes=[carry, carry, carry], semantics=("parallel", "arbitrary"),
        args=(proj, proj, proj, hs, hs, dyp, wa, wx, ba, bx, cw, cb, lam), comm=comm)


def _position():
    return lax.axis_index("x"), lax.axis_index("y"), lax.axis_index("c")


def _sems(n):
    return [pltpu.SemaphoreType.DMA((n, 7)), pltpu.SemaphoreType.DMA((n, 7)), pltpu.SemaphoreType.DMA((n,))]


BY_COLUMNS = -1


def _gather_comm(arrs, axes):
    n = len(arrs)

    def tools(ins, outs, send_sems, recv_sems, local_sems):
        x, y, c = _position()
        me, sibling = (x, y, c), (x, y, 1 - c)
        chips = [(1 - x, y), (x, 1 - y), (1 - x, 1 - y)]

        def slot(a, pos):
            if axes[a] == BY_COLUMNS:
                w = ins[a].shape[1]
                return outs[a].at[:, pl.ds(pl.multiple_of(pos * w, LANES), w)]
            return outs[a].at[(slice(None),) * axes[a] + (pos,)]

        def copy(a, k, block, to, src=None):
            px, py, pc = block
            rows = slot(a, 4 * px + 2 * py + pc)
            return pltpu.make_async_remote_copy(
                src_ref=rows if src is None else src, dst_ref=rows,
                send_sem=send_sems.at[a, k], recv_sem=recv_sems.at[a, k],
                device_id=to, device_id_type=MESH)

        own = lambda a: pltpu.make_async_copy(ins[a], slot(a, 4 * x + 2 * y + c), local_sems.at[a])
        first = lambda a: ([copy(a, 0, me, sibling, src=ins[a])]
                           + [copy(a, 1 + j, me, (*chip, c), src=ins[a]) for j, chip in enumerate(chips)])
        passed = lambda a: [copy(a, 4 + j, (*chip, c), sibling) for j, chip in enumerate(chips)]
        return me, sibling, chips, c, copy, own, first, passed

    def start(ins, outs, *sems):
        *_, own, first, _ = tools(ins, outs, *sems)
        for a in range(n):
            own(a).start()
            for cp in first(a):
                cp.start()

    def mid(ins, outs, *sems):
        me, _, chips, c, copy, _, _, passed = tools(ins, outs, *sems)
        for a in range(n):
            fwd = passed(a)
            for j, chip in enumerate(chips):
                copy(a, 1 + j, (*chip, c), me).wait_recv()
                fwd[j].start()

    def finish(ins, outs, *sems):
        me, sibling, chips, c, copy, own, first, passed = tools(ins, outs, *sems)
        for a in range(n):
            copy(a, 0, sibling, me).wait_recv()
            for j, chip in enumerate(chips):
                copy(a, 4 + j, (*chip, 1 - c), me).wait_recv()
        for a in range(n):
            for cp in first(a) + passed(a):
                cp.wait_send()
            own(a).wait()

    shapes = [jax.ShapeDtypeStruct((a.shape[0], N_DEV * a.shape[1]) if ax == BY_COLUMNS else
                                   a.shape[:ax] + (N_DEV,) + a.shape[ax:], a.dtype) for a, ax in zip(arrs, axes)]
    return Comm(list(arrs), shapes, _sems(n), start, mid, finish)


def _comm_call(comm, *, name):
    ci, co = len(comm.arrays), len(comm.out_shapes)

    def body(*refs):
        ins, outs, sems = refs[:ci], refs[ci:ci + co], refs[ci + co:]
        comm.start(ins, outs, *sems)
        comm.mid(ins, outs, *sems)
        comm.finish(ins, outs, *sems)

    any_spec = pl.BlockSpec(memory_space=pl.ANY)
    return pl.pallas_call(body, name=name, in_specs=[any_spec] * ci, out_specs=[any_spec] * co,
                          out_shape=comm.out_shapes, scratch_shapes=comm.sems)(*comm.arrays)


def _exchange_comm(items):
    n = len(items)

    def tools(ins, outs, send_sems, recv_sems, local_sems):
        x, y, c = _position()
        me = 4 * x + 2 * y + c

        def src(a, pos):
            if items[a][1]:
                rows = ins[a].shape[1] // N_DEV
                return ins[a].at[:, pl.ds(pl.multiple_of(pos * rows, rows), rows)]
            return ins[a].at[pos]

        copies = [pltpu.make_async_copy(src(a, me), outs[a].at[me], local_sems.at[a]) for a in range(n)]
        for k in range(1, N_DEV):
            px = x ^ ((k >> 2) & 1)
            py = y ^ ((k >> 1) & 1)
            pc = c ^ (k & 1)
            copies += [pltpu.make_async_remote_copy(
                src_ref=src(a, 4 * px + 2 * py + pc), dst_ref=outs[a].at[me],
                send_sem=send_sems.at[a, k - 1], recv_sem=recv_sems.at[a, k - 1],
                device_id=(px, py, pc), device_id_type=MESH) for a in range(n)]
        return copies

    def start(ins, outs, *sems):
        for cp in tools(ins, outs, *sems):
            cp.start()

    def mid(ins, outs, *sems):
        pass

    def finish(ins, outs, *sems):
        for cp in tools(ins, outs, *sems):
            cp.wait()

    shapes = []
    for arr, split in items:
        blk = (arr.shape[0], arr.shape[1] // N_DEV) + arr.shape[2:] if split else arr.shape[1:]
        shapes.append(jax.ShapeDtypeStruct((N_DEV,) + blk, arr.dtype))
    return Comm([arr for arr, _ in items], shapes, _sems(n), start, mid, finish)


def _adamw(parts, w, m, v, *, name):
    L, R, C = w.shape
    assert len(parts) == L
    row_bytes = 2 * (L * N_DEV * C * parts[0].dtype.itemsize + 7 * C * 4)
    tr = _pick(R, max(16, ADAMW_VMEM // row_bytes), 16)
    nr = R // tr
    c1 = 1.0 / (1.0 - ADAM_B1 ** ADAM_STEP)
    c2 = 1.0 / (1.0 - ADAM_B2 ** ADAM_STEP)

    def body(*refs):
        p_refs = refs[:L]
        w_ref, m_ref, v_ref, g_ref, d_ref, nm_ref, nv_ref = refs[L:]
        layer = pl.program_id(0)
        for idx, p_ref in enumerate(p_refs):
            @pl.when(layer == idx)
            def _():
                g = p_ref[0].astype(F32)
                for s in range(1, N_DEV):
                    g = g + p_ref[s].astype(F32)
                nm = ADAM_B1 * m_ref[...] + (1.0 - ADAM_B1) * g
                nv = ADAM_B2 * v_ref[...] + (1.0 - ADAM_B2) * (g * g)
                g_ref[...] = g
                nm_ref[...] = nm
                nv_ref[...] = nv
                d_ref[...] = -ADAM_LR * ((nm * c1) / (jnp.sqrt(nv * c2) + ADAM_EPS) + ADAM_WD * w_ref[...])

    def part_spec(idx):
        return pl.BlockSpec((N_DEV, tr, C),
                            lambda l, i: (0, jnp.where(l == idx, i, jnp.where(l < idx, 0, nr - 1)), 0))

    blk = pl.BlockSpec((None, tr, C), lambda l, i: (l, i, 0))
    return pl.pallas_call(
        body, name=name, grid=(L, nr),
        in_specs=[part_spec(idx) for idx in range(L)] + [blk, blk, blk],
        out_specs=[blk] * 4,
        out_shape=[jax.ShapeDtypeStruct((L, R, C), F32)] * 4,
        compiler_params=_cparams(("arbitrary", "arbitrary")),
    )(*parts, w, m, v)


def _pack(flat_parts, row_multiple, dtype):
    lead = flat_parts[0].shape[:-1]
    total = sum(p.shape[-1] for p in flat_parts)
    quantum = PACK_W * row_multiple
    padded = -(-total // quantum) * quantum
    parts = [p.astype(dtype) for p in flat_parts]
    if padded > total:
        parts.append(jnp.zeros(lead + (padded - total,), dtype))
    return jnp.concatenate(parts, axis=-1).reshape(lead + (padded // PACK_W, PACK_W))


def _unpack(buf, shapes):
    lead = buf.shape[:-2]
    flat = buf.reshape(lead + (-1,))
    out, off = [], 0
    for shp in shapes:
        n = math.prod(shp)
        out.append(flat[..., off:off + n].reshape(lead + tuple(shp)))
        off += n
    return out


def _to_full(seg, ax):
    shard = seg.shape[1:]
    full = shard[:ax] + (N_DEV * shard[ax],) + shard[ax + 1:]
    return jnp.moveaxis(seg, 0, ax).reshape(full)


def _to_shards(full, ax):
    shp = full.shape
    split = shp[:ax] + (N_DEV, shp[ax] // N_DEV) + shp[ax + 1:]
    return jnp.moveaxis(full.reshape(split), ax, 0).reshape(N_DEV, -1)


BIG = (("attn_w_in", 1), ("attn_w_out", 1), ("lru_w_in", 1), ("lru_w_a", 2), ("lru_w_x", 2), ("lru_w_out", 1))
SMALL = (("lru_conv_w", 2), ("lru_conv_b", 1), ("lru_b_a", 2), ("lru_b_x", 2), ("lru_lambda", 1))
REPL = ("norm_pre", "norm_post", "attn_sinks")
ORDER = ("norm_pre", "norm_post", "attn_w_in", "attn_w_out", "attn_sinks", "lru_w_in", "lru_conv_w", "lru_conv_b",
         "lru_w_a", "lru_b_a", "lru_w_x", "lru_b_x", "lru_lambda", "lru_w_out")


def kernel(x, norm_pre, norm_post, attn_w_in, attn_w_out, attn_sinks, lru_w_in, lru_conv_w, lru_conv_b, lru_w_a, lru_b_a, lru_w_x, lru_b_x, lru_lambda, lru_w_out, loss_target, m_norm_pre, m_norm_post, m_attn_w_in, m_attn_w_out, m_attn_sinks, m_lru_w_in, m_lru_conv_w, m_lru_conv_b, m_lru_w_a, m_lru_b_a, m_lru_w_x, m_lru_b_x, m_lru_lambda, m_lru_w_out, v_norm_pre, v_norm_post, v_attn_w_in, v_attn_w_out, v_attn_sinks, v_lru_w_in, v_lru_conv_w, v_lru_conv_b, v_lru_w_a, v_lru_b_a, v_lru_w_x, v_lru_b_x, v_lru_lambda, v_lru_w_out):
    W = dict(norm_pre=norm_pre, norm_post=norm_post, attn_w_in=attn_w_in, attn_w_out=attn_w_out, attn_sinks=attn_sinks,
             lru_w_in=lru_w_in, lru_conv_w=lru_conv_w, lru_conv_b=lru_conv_b, lru_w_a=lru_w_a, lru_b_a=lru_b_a,
             lru_w_x=lru_w_x, lru_b_x=lru_b_x, lru_lambda=lru_lambda, lru_w_out=lru_w_out)
    M = dict(norm_pre=m_norm_pre, norm_post=m_norm_post, attn_w_in=m_attn_w_in, attn_w_out=m_attn_w_out,
             attn_sinks=m_attn_sinks, lru_w_in=m_lru_w_in, lru_conv_w=m_lru_conv_w, lru_conv_b=m_lru_conv_b,
             lru_w_a=m_lru_w_a, lru_b_a=m_lru_b_a, lru_w_x=m_lru_w_x, lru_b_x=m_lru_b_x, lru_lambda=m_lru_lambda,
             lru_w_out=m_lru_w_out)
    V = dict(norm_pre=v_norm_pre, norm_post=v_norm_post, attn_w_in=v_attn_w_in, attn_w_out=v_attn_w_out,
             attn_sinks=v_attn_sinks, lru_w_in=v_lru_w_in, lru_conv_w=v_lru_conv_w, lru_conv_b=v_lru_conv_b,
             lru_w_a=v_lru_w_a, lru_b_a=v_lru_b_a, lru_w_x=v_lru_w_x, lru_b_x=v_lru_b_x, lru_lambda=v_lru_lambda,
             lru_w_out=v_lru_w_out)

    h0 = x[0]
    target = loss_target[0]
    T, D = h0.shape
    depth = norm_pre.shape[0]
    n_attn = attn_w_in.shape[0]
    Q = attn_w_out.shape[1] * N_DEV
    KV = Q // GROUP
    LW = lru_w_out.shape[1] * N_DEV
    nh = LW // LRU_BLOCK

    n_lru = lru_w_in.shape[0]
    big_names = [n for n, _ in BIG]
    small_names = [n for n, _ in SMALL]
    small_shapes = [W[n].shape for n in small_names]
    repl_shapes = [W[n].shape for n in REPL]

    flat = lambda a: a.reshape(-1)
    def layer_shards(layer):
        j = layer // 2
        names = ("attn_w_in", "attn_w_out") if layer % 2 == 0 else ("lru_w_in", "lru_w_a", "lru_w_x", "lru_w_out")
        axes = dict(lru_w_a=1, lru_w_x=1, lru_w_in=BY_COLUMNS)
        return [W[n][j].astype(BF16) for n in names], [axes.get(n, 0) for n in names]

    def layer_weights(layer, gathered):
        if layer % 2 == 0:
            g_in, g_out = gathered
            w_in = jnp.moveaxis(g_in, 0, 1).reshape(D, -1)
            return dict(w_in=w_in, w_out=None if g_out is None else g_out.reshape(Q, D))
        g_in, g_wa, g_wx, g_out = gathered
        return dict(w_in=g_in, w_a=g_wa.reshape(nh, LRU_BLOCK, LRU_BLOCK),
                    w_x=g_wx.reshape(nh, LRU_BLOCK, LRU_BLOCK), w_out=g_out.reshape(LW, D))

    arrs0, _ = layer_shards(0)
    (u,), first = _rms_fwd(h0, norm_pre[0:1], name="rms_fwd",
                           comm=_gather_comm([arrs0[0], _pack([flat(W[n]) for n in small_names], SUBLANES, F32)], [0, 0]))
    weights = {0: layer_weights(0, (first[0], None))}
    full = {}
    for (n, ax), seg in zip(SMALL, _unpack(first[-1], small_shapes)):
        full[n] = _to_full(seg, ax)
    cw_f = full["lru_conv_w"]
    cb_f = full["lru_conv_b"][:, None, :]
    ba_f = full["lru_b_a"].reshape(-1, 1, LW)
    bx_f = full["lru_b_x"].reshape(-1, 1, LW)
    lam_f = full["lru_lambda"][:, None, :]

    h = h0
    saved = []
    for layer in range(depth):
        j = layer // 2
        wl = weights[layer]
        nxt = _gather_comm(*layer_shards(layer + 1)) if layer + 1 < depth else None
        if layer % 2 == 0:
            if wl["w_out"] is None:
                proj, got = _matmul(u, wl["w_in"], out_dtype=BF16, comm=_gather_comm([arrs0[1]], [0]), name="attn_in")
                wl["w_out"] = got[0].reshape(Q, D)
            else:
                proj = _matmul(u, wl["w_in"], out_dtype=BF16, name="attn_in")
            (mix, ypre), got = _attn_fwd(proj, attn_sinks[j], comm=nxt, name="attn_fwd")
            y = _matmul(ypre, wl["w_out"], name="attn_out")
        else:
            proj = _matmul(u, wl["w_in"], out_dtype=BF16, name="lru_in")
            (mix, ypre), got = _lru_fwd(proj, wl["w_a"], wl["w_x"], ba_f[j], bx_f[j], cw_f[j], cb_f[j], lam_f[j],
                                        comm=nxt, name="lru_fwd")
            y = _matmul(ypre, wl["w_out"], name="lru_out")
        if nxt is not None:
            weights[layer + 1] = layer_weights(layer + 1, got)
        saved.append((h, u, proj, mix, ypre, y))
        if layer + 1 < depth:
            h, u = _post_fwd(y, norm_post[layer:layer + 1], h, norm_pre[layer + 1:layer + 2], name="post_fwd")
        else:
            dh, loss_part = _post_loss(y, norm_post[layer:layer + 1], h, target, name="post_loss")

    g_pre = [None] * depth
    g_post = [None] * depth
    small_vec = [n for n in small_names] + ["attn_sinks"]
    grads = {n: [None] * W[n].shape[0] for n in small_vec}
    recv = {}

    def carried(keyed):
        return _exchange_comm([item for _, item in keyed]) if keyed else None

    def landed(keyed, got):
        for (key, _), r in zip(keyed, got):
            recv[key] = r

    pending = []
    for layer in reversed(range(depth)):
        j = layer // 2
        h_in, u, proj, mix, ypre, y = saved[layer]
        wl = weights[layer]
        if layer == depth - 1:
            dy, g_post[layer] = _rms_bwd(y, norm_post[layer:layer + 1], dh, None, out_dtype=BF16, name="post_bwd")
        if layer % 2 == 0:
            dyp = _matmul(dy, wl["w_out"], tb=True, out_dtype=BF16, name="attn_out_dx")
            dw_out = _matmul(ypre, dy, ta=True, out_dtype=BF16, name="attn_out_dw")
            pending.append((("attn_w_out", j), (dw_out.reshape(N_DEV, Q // N_DEV, D), False)))
            (dqg, dkv, dsink), got = _attn_bwd(proj, mix, dyp, attn_sinks[j], comm=carried(pending), name="attn_bwd")
            landed(pending, got)
            grads["attn_sinks"][j] = dsink[:, 0]
            dproj = _fill_columns(dqg, dkv, Q // (2 * KV), name="attn_dproj")
            dw = _matmul(u, dproj, ta=True, out_dtype=BF16, name="attn_in_dw")
            pending = [(("attn_w_in", j), (jnp.moveaxis(dw.reshape(D, N_DEV, -1), 1, 0), False))]
            if layer == 0:
                du, got = _matmul(dproj, wl["w_in"], tb=True, comm=carried(pending), name="attn_in_dx")
                landed(pending, got)
                pending = []
            else:
                du = _matmul(dproj, wl["w_in"], tb=True, name="attn_in_dx")
        else:
            dyp = _matmul(dy, wl["w_out"], tb=True, out_dtype=BF16, name="lru_out_dx")
            dw_out = _matmul(ypre, dy, ta=True, out_dtype=BF16, name="lru_out_dw")
            pending.append((("lru_w_out", j), (dw_out.reshape(N_DEV, LW // N_DEV, D), False)))
            (dxb, dgt, dwa, dwx, dba, dbx, dcw, dcb, dlam), got = _lru_bwd(
                proj, mix, dyp, wl["w_a"], wl["w_x"], ba_f[j], bx_f[j], cw_f[j], cb_f[j], lam_f[j],
                comm=carried(pending), name="lru_bwd")
            landed(pending, got)
            grads["lru_b_a"][j], grads["lru_b_x"][j] = dba.reshape(nh, LRU_BLOCK), dbx.reshape(nh, LRU_BLOCK)
            grads["lru_conv_w"][j], grads["lru_conv_b"][j], grads["lru_lambda"][j] = dcw, dcb[0], dlam[0]
            dproj = _fill_columns(dxb, dgt, 1, name="lru_dproj")
            du = _matmul(dproj, wl["w_in"], tb=True, name="lru_in_dx")
            mine = [(("lru_w_a", j), (dwa, True)), (("lru_w_x", j), (dwx, True))]
            dw, got = _matmul(u, dproj, ta=True, out_dtype=BF16, by_owner=True, comm=carried(mine), name="lru_in_dw")
            landed(mine, got)
            pending = [(("lru_w_in", j), (dw, False))]
        if layer > 0:
            dh, dy, g_pre[layer], g_post[layer - 1] = _pre_post_bwd(
                h_in, norm_pre[layer:layer + 1], du, dh, saved[layer - 1][5], norm_post[layer - 1:layer], name="pre_post_bwd")
        else:
            dh, g_pre[layer] = _rms_bwd(h_in, norm_pre[layer:layer + 1], du, dh, out_dtype=F32, name="pre_bwd")

    gfull = {n: jnp.stack(g) for n, g in grads.items()}
    gfull["norm_pre"] = jnp.concatenate(g_pre, axis=0)
    gfull["norm_post"] = jnp.concatenate(g_post, axis=0)

    repl_part = [jnp.broadcast_to(gfull[n].reshape(1, -1), (N_DEV, gfull[n].size)) for n in REPL]
    loss_slot = jnp.broadcast_to(loss_part.reshape(1, 1), (N_DEV, 1))
    send_small = _pack([_to_shards(gfull[n], ax) for n, ax in SMALL] + repl_part + [loss_slot], SUBLANES, F32)
    last = pending + [(("small", 0), (send_small, False))]
    landed(last, _comm_call(carried(last), name="exchange_last"))

    zero1 = jnp.zeros((1,), F32)
    outs = {}
    for n in big_names:
        shp = W[n].shape
        as3 = lambda a: a.reshape((shp[0], -1, shp[-1]))
        parts = [recv[n, j].reshape((N_DEV, -1, shp[-1])) for j in range(shp[0])]
        res = _adamw(parts, as3(W[n]), as3(M[n]), as3(V[n]), name="adamw_" + n)
        for kind, a in zip(("grad", "delta", "new_m", "new_v"), res):
            outs[kind, n] = a.reshape(shp)
    res_small = _adamw([recv["small", 0]],
                       *[_pack([flat(S[n]) for n in small_names] + [flat(S[n]) for n in REPL] + [zero1], SUBLANES, F32)[None]
                         for S in (W, M, V)], name="adamw_small")
    for kind, rs in zip(("grad", "delta", "new_m", "new_v"), res_small):
        for n, a in zip(small_names + list(REPL) + ["loss"], _unpack(rs[0], small_shapes + repl_shapes + [(1,)])):
            outs[kind, n] = a
    loss = outs["grad", "loss"][0]
    result = [loss, dh[None]]
    for kind in ("grad", "delta", "new_m", "new_v"):
        result += [outs[kind, n] for n in ORDER]
    return tuple(result)
```
